```python
import jax, jax.numpy as jnp
from jax import lax
import numpy as np

D_MODEL = 1024
BATCH = 8
SEQ = 2048
DEPTH = 1
DEC_BATCH = 128
DEC_SEQ = 8
PAST_LEN = 16384
PAGE_SIZE = 128

MIX_WIDTH = D_MODEL
CONV_WIDTH = MIX_WIDTH // 2
CONV_GROUPS = 8
CONV_K = 3
M_WIDTH = MIX_WIDTH - CONV_WIDTH
M_HEADS = 4
HEAD_DIM = M_WIDTH // M_HEADS
N_META = 16
CHUNK = 128
EPS = 1e-6
PROJ_SPLITS = [CONV_WIDTH, CONV_WIDTH, CONV_WIDTH, CONV_WIDTH,
               M_WIDTH, M_WIDTH, M_WIDTH, M_WIDTH, M_WIDTH,
               M_HEADS, M_HEADS]
PROJ_WIDTH = sum(PROJ_SPLITS)

kernel_name = 'hymba_conv_mlstm_decode_step'


def rmsnorm(x, g):
    xf = x.astype(jnp.float32)
    y = xf * lax.rsqrt(jnp.mean(xf * xf, axis=-1, keepdims=True) + EPS)
    return (y * g.astype(jnp.float32)).astype(x.dtype)


def short_conv(u, buf, w):
    T = u.shape[1]
    ucat = jnp.concatenate([buf.astype(u.dtype), u], axis=1)
    y = w[0] * ucat[:, 0:T]
    for j in range(1, CONV_K):
        y = y + w[j] * ucat[:, j:j + T]
    return y, ucat[:, -(CONV_K - 1):]


def mlstm_chunk(q, k, v, ig, lf, c_prev, n_prev, m_prev):
    L = q.shape[2]
    b = jnp.cumsum(lf, axis=-1)
    causal = jnp.tril(jnp.ones((L, L), dtype=bool))
    log_d = jnp.where(causal, b[..., :, None] - b[..., None, :] + ig[..., None, :], -jnp.inf)
    m_inter = b + m_prev[..., None]
    m_t = jnp.maximum(m_inter, jnp.max(log_d, axis=-1))
    d = jnp.exp(log_d - m_t[..., None])
    inter = jnp.exp(m_inter - m_t)
    s = jnp.einsum('bhjd,bhsd->bhjs', q, k) * d
    num = jnp.einsum('bhjs,bhsv->bhjv', s, v) + inter[..., None] * jnp.einsum('bhjd,bhdv->bhjv', q, c_prev)
    den = jnp.sum(s, axis=-1) + inter * jnp.einsum('bhjd,bhd->bhj', q, n_prev)
    h = num / jnp.maximum(jnp.abs(den), jnp.exp(-m_t))[..., None]
    m_new = m_t[..., -1]
    w_last = jnp.exp(log_d[..., -1, :] - m_new[..., None])
    decay = inter[..., -1]
    c_new = decay[..., None, None] * c_prev + jnp.einsum('bhs,bhsd,bhsv->bhdv', w_last, k, v)
    n_new = decay[..., None] * n_prev + jnp.einsum('bhs,bhsd->bhd', w_last, k)
    return h, (c_new, n_new, m_new)


def mlstm_seq(q, k, v, ig, lf, state):
    B, H, T, D = q.shape
    nf = T // CHUNK
    hs = []
    if nf > 0:
        def to_chunks(a):
            a = a[:, :, :nf * CHUNK]
            a = a.reshape(a.shape[:2] + (nf, CHUNK) + a.shape[3:])
            return jnp.moveaxis(a, 2, 0)

        def body(carry, xs):
            qc, kc, vc, ic, fc = xs
            h, new = mlstm_chunk(qc, kc, vc, ic, fc, *carry)
            return new, h

        state, hc = lax.scan(body, state, tuple(to_chunks(a) for a in (q, k, v, ig, lf)))
        hs.append(jnp.moveaxis(hc, 0, 2).reshape(B, H, nf * CHUNK, D))
    if T % CHUNK:
        s0 = nf * CHUNK
        h, state = mlstm_chunk(q[:, :, s0:], k[:, :, s0:], v[:, :, s0:], ig[:, :, s0:], lf[:, :, s0:], *state)
        hs.append(h)
    h = hs[0] if len(hs) == 1 else jnp.concatenate(hs, axis=2)
    return h, state


def mixer_layer(x, conv_buf, c0, n0, m0, lead, norm_gain, w_in, conv_w, b_igate, b_fgate, mh_norm_gain, w_out):
    B, T, _ = x.shape
    xn = rmsnorm(x, norm_gain)
    proj = jnp.einsum('btd,dp->btp', xn, w_in)
    gb, gc, hc, zc, q, k, v, o, zm, ig, fg = jnp.split(proj, np.cumsum(PROJ_SPLITS)[:-1].tolist(), axis=-1)
    yc, conv_new = short_conv(gc * hc, conv_buf, conv_w)
    y_conv = gb * yc * jax.nn.silu(zc)
    def heads(a):
        return a.astype(jnp.float32).reshape(B, T, M_HEADS, HEAD_DIM).transpose(0, 2, 1, 3)
    qh, kh, vh = heads(q), heads(k) * (HEAD_DIM ** -0.5), heads(v)
    igh = (ig + b_igate).astype(jnp.float32).transpose(0, 2, 1)
    lfh = jax.nn.log_sigmoid((fg + b_fgate).astype(jnp.float32)).transpose(0, 2, 1)
    state = (c0.astype(jnp.float32), n0.astype(jnp.float32), m0.astype(jnp.float32))
    if lead > 0:
        h_lead, state = mlstm_chunk(qh[:, :, :lead], kh[:, :, :lead], vh[:, :, :lead],
                                    igh[:, :, :lead], lfh[:, :, :lead], *state)
        h_rest, state = mlstm_seq(qh[:, :, lead:], kh[:, :, lead:], vh[:, :, lead:],
                                  igh[:, :, lead:], lfh[:, :, lead:], state)
        h = jnp.concatenate([h_lead, h_rest], axis=2)
    else:
        h, state = mlstm_seq(qh, kh, vh, igh, lfh, state)
    h = h * lax.rsqrt(jnp.mean(h * h, axis=-1, keepdims=True) + EPS)
    h = h.transpose(0, 2, 1, 3).reshape(B, T, M_WIDTH) * mh_norm_gain.astype(jnp.float32)
    y_m = (h.astype(x.dtype) * jax.nn.sigmoid(o)) * jax.nn.silu(zm)
    out = jnp.einsum('btw,wd->btd', jnp.concatenate([y_conv, y_m], axis=-1), w_out)
    c_new, n_new, m_new = state
    return x + out, (conv_new, c_new.astype(x.dtype), n_new.astype(x.dtype), m_new.astype(x.dtype))


def setup_inputs(seed: int = 0) -> dict:
    key = jax.random.key(seed)
    ks = jax.random.split(key, 16)
    f32 = jnp.float32
    return {
        'x_prompt': jax.random.normal(ks[0], (BATCH, SEQ, D_MODEL), f32),
        'x_sample': jax.random.normal(ks[1], (DEC_BATCH, DEC_SEQ, D_MODEL), f32),
        'state_conv': jax.random.normal(ks[2], (DEC_BATCH, CONV_K - 1, CONV_WIDTH), f32),
        'state_mlstm_c': 0.5 * jax.random.normal(ks[3], (DEC_BATCH, M_HEADS, HEAD_DIM, HEAD_DIM), f32),
        'state_mlstm_n': 0.5 * jax.random.normal(ks[4], (DEC_BATCH, M_HEADS, HEAD_DIM), f32),
        'state_mlstm_m': jax.random.normal(ks[5], (DEC_BATCH, M_HEADS), f32),
        'meta_tokens': jax.random.normal(ks[6], (N_META, D_MODEL), f32),
        'norm_gain': 1.0 + 0.05 * jax.random.normal(ks[7], (D_MODEL,), f32),
        'w_in': jax.random.normal(ks[8], (D_MODEL, PROJ_WIDTH), f32) * D_MODEL ** -0.5,
        'conv_w': jax.random.normal(ks[9], (CONV_K, CONV_WIDTH), f32) * CONV_K ** -0.5,
        'b_igate': 0.1 * jax.random.normal(ks[10], (M_HEADS,), f32),
        'b_fgate': jnp.linspace(3.0, 6.0, M_HEADS, dtype=f32) + 0.1 * jax.random.normal(ks[11], (M_HEADS,), f32),
        'mh_norm_gain': 1.0 + 0.05 * jax.random.normal(ks[12], (M_WIDTH,), f32),
        'w_out': jax.random.normal(ks[13], (MIX_WIDTH, D_MODEL), f32) * MIX_WIDTH ** -0.5,
        'final_norm_gain': 1.0 + 0.05 * jax.random.normal(ks[14], (D_MODEL,), f32),
    }


def reference(x_prompt, x_sample, state_conv, state_mlstm_c, state_mlstm_n, state_mlstm_m,
              meta_tokens, norm_gain, w_in, conv_w, b_igate, b_fgate, mh_norm_gain, w_out, final_norm_gain):
    weights = (norm_gain, w_in, conv_w, b_igate, b_fgate, mh_norm_gain, w_out)
    Bp = x_prompt.shape[0]
    meta = jnp.broadcast_to(meta_tokens.astype(x_prompt.dtype)[None], (Bp, N_META, D_MODEL))
    xp = jnp.concatenate([meta, x_prompt], axis=1)
    conv0 = jnp.zeros((Bp, CONV_K - 1, CONV_WIDTH), x_prompt.dtype)
    c0 = jnp.zeros((Bp, M_HEADS, HEAD_DIM, HEAD_DIM), jnp.float32)
    n0 = jnp.zeros((Bp, M_HEADS, HEAD_DIM), jnp.float32)
    m0 = jnp.zeros((Bp, M_HEADS), jnp.float32)
    xs = x_sample
    cs, ns, ms, bs = state_mlstm_c, state_mlstm_n, state_mlstm_m, state_conv
    for _ in range(DEPTH):
        xp, (conv0, c0, n0, m0) = mixer_layer(xp, conv0, c0, n0, m0, N_META, *weights)
        xs, (bs, cs, ns, ms) = mixer_layer(xs, bs, cs, ns, ms, 0, *weights)
    y_prompt = rmsnorm(xp[:, N_META:], final_norm_gain)
    y_sample = rmsnorm(xs, final_norm_gain)
    return (y_prompt, y_sample, conv0, c0, n0, m0, bs, cs, ns, ms)
```

```python
import functools

import jax
import jax.numpy as jnp
from jax import lax
from jax.experimental import pallas as pl
from jax.experimental.pallas import tpu as pltpu

D_MODEL = 1024
CONV_WIDTH = 512
M_WIDTH = 512
M_HEADS = 4
HEAD_DIM = 128
N_META = 16
CONV_K = 3
EPS = 1e-6
MAIN_WIDTH = 4 * CONV_WIDTH + 5 * M_WIDTH
GATE_WIDTH = 256
LANES = 128
SUBLANES = 8
CHUNK = 128
DEC_SEQ = 8
SEQ_PER_STEP = CHUNK // DEC_SEQ
NEG_INF = float("-inf")
K_SCALE = HEAD_DIM ** -0.5

OFF_GB, OFF_GC, OFF_HC, OFF_ZC = 0, 512, 1024, 1536
OFF_Q, OFF_K, OFF_V, OFF_O, OFF_ZM = 2048, 2560, 3072, 3584, 4096

_HI = lax.Precision.HIGHEST
_BF = jnp.bfloat16
_F32 = jnp.float32


def _dot(a, b):
    return jnp.dot(a, b, preferred_element_type=_F32)


def _dot_exact(a, b):
    return jnp.dot(a, b, precision=_HI, preferred_element_type=_F32)


def _dot_nt(a, b):
    return lax.dot_general(a, b, (((1,), (1,)), ((), ())), preferred_element_type=_F32)


def _rms_rows(x, gain_row):
    return x * lax.rsqrt(jnp.mean(x * x, axis=-1, keepdims=True) + EPS) * gain_row


def _sigmoid(x):
    return 1.0 / (1.0 + jnp.exp(-x))


def _silu(x):
    return x * _sigmoid(x)


def _cummax_rows(x, pos, period):
    s = 1
    while s < period:
        shifted = pltpu.roll(x, s, 0)
        x = jnp.maximum(x, jnp.where(pos >= s, shifted, NEG_INF))
        s *= 2
    return x


def _gate_columns(g_raw, big_row, bfg_row):
    ig = g_raw[:, :LANES] + big_row
    lf = jax.nn.log_sigmoid(g_raw[:, LANES:] + bfg_row)
    return ig, lf


def _head_output(num, den, emt_b, gain_row, o, zm):
    hh = num / jnp.maximum(jnp.abs(den), emt_b)
    hn = hh * lax.rsqrt(jnp.mean(hh * hh, axis=-1, keepdims=True) + EPS) * gain_row
    return hn * _sigmoid(o) * _silu(zm)


def _conv_out(gb, zc, u, u_m1, u_m2, cw_ref):
    yc = cw_ref[0:1, :] * u_m2 + cw_ref[1:2, :] * u_m1 + cw_ref[2:3, :] * u
    return gb * yc * _silu(zc)


def _prompt_body(x_ref, wm_ref, wg_ref, wo_ref, ng_ref, fgain_ref, mhg_ref, cw_ref, big_ref, bfg_ref,
                 caug0_ref, m0_ref, ut0_ref,
                 y_ref, caug_out_ref, m_out_ref, ut_out_ref,
                 p_scr, g_scr, ycat_scr, ubuf, caug, m_scr, *, tb, n_pad):
    i = pl.program_id(1)
    last = pl.num_programs(1) - 1

    @pl.when(i == 0)
    def _():
        caug[...] = caug0_ref[...]
        m_scr[...] = m0_ref[...]
        ubuf[0:SUBLANES, :] = ut0_ref[...]

    x = x_ref[0]
    xn = _rms_rows(x, ng_ref[...]).astype(_BF)
    p_scr[...] = _dot(xn, wm_ref[...])
    g_scr[...] = _dot(xn, wg_ref[...])

    u = p_scr[:, OFF_GC:OFF_GC + CONV_WIDTH] * p_scr[:, OFF_HC:OFF_HC + CONV_WIDTH]
    ubuf[SUBLANES:SUBLANES + tb, :] = u
    u_m1 = ubuf[SUBLANES - 1:SUBLANES - 1 + tb, :]
    u_m2 = ubuf[SUBLANES - 2:SUBLANES - 2 + tb, :]
    y_conv = _conv_out(p_scr[:, OFF_GB:OFF_GB + CONV_WIDTH], p_scr[:, OFF_ZC:OFF_ZC + CONV_WIDTH],
                       u, u_m1, u_m2, cw_ref)
    ycat_scr[:, 0:CONV_WIDTH] = y_conv.astype(_BF)
    ubuf[0:SUBLANES, :] = ubuf[tb:tb + SUBLANES, :]

    row = lax.broadcasted_iota(jnp.int32, (CHUNK, CHUNK), 0)
    col = lax.broadcasted_iota(jnp.int32, (CHUNK, CHUNK), 1)
    causal = col <= row
    tril = causal.astype(_F32)
    ones_blk = jnp.ones((CHUNK, LANES), _BF)

    def chunk_body(c, carry):
        r0 = pl.multiple_of(c * CHUNK, CHUNK)
        rows = pl.ds(r0, CHUNK)
        ig, lf = _gate_columns(g_scr[rows, :], big_ref[...], bfg_ref[...])
        if n_pad:
            ig = jnp.where(row >= n_pad, ig, NEG_INF)
            lf = jnp.where(row >= n_pad, lf, 0.0)
        b = _dot_exact(tril, lf)
        a = ig - b
        m_prev = m_scr[...]
        big_m = jnp.maximum(m_prev, _cummax_rows(a, row, CHUNK))
        m_t = b + big_m
        inter = jnp.exp(m_prev - big_m)
        emt = jnp.exp(-m_t)
        m_last = big_m[CHUNK - 1:CHUNK, :]
        w = jnp.exp(a - m_last)
        decay = jnp.exp(m_prev - m_last)
        m_scr[...] = m_t[CHUNK - 1:CHUNK, :]
        a_t = a.T

        for h in range(M_HEADS):
            hs = h * HEAD_DIM
            q = p_scr[rows, OFF_Q + hs:OFF_Q + hs + HEAD_DIM]
            k = p_scr[rows, OFF_K + hs:OFF_K + hs + HEAD_DIM] * K_SCALE
            v = p_scr[rows, OFF_V + hs:OFF_V + hs + HEAD_DIM]
            q_bf = q.astype(_BF)
            v_bf = v.astype(_BF)
            a_b = jnp.broadcast_to(a_t[h:h + 1, :], (CHUNK, CHUNK))
            m_b = jnp.broadcast_to(big_m[:, h:h + 1], (CHUNK, CHUNK))
            d = jnp.exp(jnp.where(causal, a_b - m_b, NEG_INF))
            s = _dot_nt(q_bf, k.astype(_BF)) * d
            sv = _dot(s.astype(_BF), jnp.concatenate([v_bf, ones_blk], axis=1))
            qc = _dot(q_bf, caug[h].astype(_BF))
            tot = sv + jnp.broadcast_to(inter[:, h:h + 1], (CHUNK, 2 * LANES)) * qc
            emt_b = jnp.broadcast_to(emt[:, h:h + 1], (CHUNK, LANES))
            ym = _head_output(tot[:, :LANES], tot[:, LANES:], emt_b, mhg_ref[:, hs:hs + HEAD_DIM],
                              p_scr[rows, OFF_O + hs:OFF_O + hs + HEAD_DIM],
                              p_scr[rows, OFF_ZM + hs:OFF_ZM + hs + HEAD_DIM])
            ycat_scr[rows, CONV_WIDTH + hs:CONV_WIDTH + hs + HEAD_DIM] = ym.astype(_BF)
            w_b = jnp.broadcast_to(w[:, h:h + 1], (CHUNK, LANES))
            wv = jnp.concatenate([(w_b * v).astype(_BF), w_b.astype(_BF)], axis=1)
            upd = _dot(k.T.astype(_BF), wv)
            decay_b = jnp.broadcast_to(decay[:, h:h + 1], (HEAD_DIM, 2 * LANES))
            caug[h] = decay_b * caug[h] + upd
        return carry

    lax.fori_loop(0, tb // CHUNK, chunk_body, 0)

    out = _dot(ycat_scr[...], wo_ref[...]) + x
    y_ref[0] = _rms_rows(out, fgain_ref[...])

    @pl.when(i == last)
    def _():
        caug_out_ref[0] = caug[...]
        m_out_ref[0] = m_scr[...]
        ut_out_ref[0] = ubuf[0:SUBLANES, :]


def _const_spec(shape):
    nd = len(shape)
    return pl.BlockSpec(shape, lambda b, i: (0,) * nd)


def _prompt_call(x, weights, caug0, m0, ut0, *, tb, n_pad, vmem_mb):
    wm, wg, wo, ng, fgain, mhg, cw, big, bfg = weights
    nb, t, _ = x.shape
    nt = t // tb
    body = functools.partial(_prompt_body, tb=tb, n_pad=n_pad)
    in_specs = [
        pl.BlockSpec((1, tb, D_MODEL), lambda b, i: (b, i, 0)),
        _const_spec(wm.shape), _const_spec(wg.shape), _const_spec(wo.shape),
        _const_spec(ng.shape), _const_spec(fgain.shape), _const_spec(mhg.shape),
        _const_spec(cw.shape), _const_spec(big.shape), _const_spec(bfg.shape),
        _const_spec(caug0.shape), _const_spec(m0.shape), _const_spec(ut0.shape),
    ]
    out_shape = (
        jax.ShapeDtypeStruct((nb, t, D_MODEL), _F32),
        jax.ShapeDtypeStruct((nb, M_HEADS, HEAD_DIM, 2 * LANES), _F32),
        jax.ShapeDtypeStruct((nb, 1, LANES), _F32),
        jax.ShapeDtypeStruct((nb, SUBLANES, CONV_WIDTH), _F32),
    )
    out_specs = (
        pl.BlockSpec((1, tb, D_MODEL), lambda b, i: (b, i, 0)),
        pl.BlockSpec((1, M_HEADS, HEAD_DIM, 2 * LANES), lambda b, i: (b, 0, 0, 0)),
        pl.BlockSpec((1, 1, LANES), lambda b, i: (b, 0, 0)),
        pl.BlockSpec((1, SUBLANES, CONV_WIDTH), lambda b, i: (b, 0, 0)),
    )
    scratch = [
        pltpu.VMEM((tb, MAIN_WIDTH), _F32),
        pltpu.VMEM((tb, GATE_WIDTH), _F32),
        pltpu.VMEM((tb, D_MODEL), _BF),
        pltpu.VMEM((tb + 2 * SUBLANES, CONV_WIDTH), _F32),
        pltpu.VMEM((M_HEADS, HEAD_DIM, 2 * LANES), _F32),
        pltpu.VMEM((1, LANES), _F32),
    ]
    return pl.pallas_call(
        body,
        grid=(nb, nt),
        in_specs=in_specs,
        out_specs=out_specs,
        out_shape=out_shape,
        scratch_shapes=scratch,
        compiler_params=pltpu.CompilerParams(
            dimension_semantics=("arbitrary", "arbitrary"),
            vmem_limit_bytes=vmem_mb * 1024 * 1024),
        name="prompt_layer" if n_pad == 0 else "meta_layer",
    )(x, wm, wg, wo, ng, fgain, mhg, cw, big, bfg, caug0, m0, ut0)


def _sample_body(x_ref, wm_ref, wg_ref, wo_ref, ng_ref, fgain_ref, mhg_ref, cw_ref, big_ref, bfg_ref,
                 sconv_ref, mrows_ref, nrows_ref, n_ref, c_ref,
                 y_ref, u_ref, c_out_ref, n_out_ref, m_out_ref,
                 p_scr, dec_scr, ycat_scr):
    x = x_ref[...]
    xn = _rms_rows(x, ng_ref[...]).astype(_BF)
    p_scr[...] = _dot(xn, wm_ref[...])
    g_raw = _dot(xn, wg_ref[...])

    row = lax.broadcasted_iota(jnp.int32, (CHUNK, CHUNK), 0)
    col = lax.broadcasted_iota(jnp.int32, (CHUNK, CHUNK), 1)
    pos = row % DEC_SEQ
    grp = row // DEC_SEQ
    same_seq = grp == (col // DEC_SEQ)
    causal = same_seq & (col <= row)
    tril = causal.astype(_F32)
    to_last = (same_seq & (col % DEC_SEQ == DEC_SEQ - 1)).astype(_F32)
    grow = lax.broadcasted_iota(jnp.int32, (SEQ_PER_STEP, CHUNK), 0)
    gcol = lax.broadcasted_iota(jnp.int32, (SEQ_PER_STEP, CHUNK), 1)
    pick_last = (gcol == grow * DEC_SEQ + (DEC_SEQ - 1)).astype(_F32)
    seq_sum = (gcol // DEC_SEQ == grow).astype(_BF)
    ones_blk = jnp.ones((CHUNK, LANES), _BF)

    u = p_scr[:, OFF_GC:OFF_GC + CONV_WIDTH] * p_scr[:, OFF_HC:OFF_HC + CONV_WIDTH]
    u_ref[...] = u
    sc = sconv_ref[...]
    posw = lax.broadcasted_iota(jnp.int32, (CHUNK, CONV_WIDTH), 0) % DEC_SEQ
    u_m1 = jnp.where(posw == 0, pltpu.roll(sc, CHUNK - 1, 0), pltpu.roll(u, 1, 0))
    u_m2 = jnp.where(posw < 2, sc, pltpu.roll(u, 2, 0))
    y_conv = _conv_out(p_scr[:, OFF_GB:OFF_GB + CONV_WIDTH], p_scr[:, OFF_ZC:OFF_ZC + CONV_WIDTH],
                       u, u_m1, u_m2, cw_ref)
    ycat_scr[:, 0:CONV_WIDTH] = y_conv.astype(_BF)

    ig, lf = _gate_columns(g_raw, big_ref[...], bfg_ref[...])
    b = _dot_exact(tril, lf)
    a = ig - b
    m_prev = mrows_ref[...]
    big_m = jnp.maximum(m_prev, _cummax_rows(a, pos, DEC_SEQ))
    m_t = b + big_m
    inter = jnp.exp(m_prev - big_m)
    emt = jnp.exp(-m_t)
    m_last = _dot_exact(to_last, big_m)
    w = jnp.exp(a - m_last)
    decay = jnp.exp(m_prev - m_last)
    dec_scr[...] = decay
    m_out_ref[...] = _dot_exact(pick_last, m_t)
    dec16 = _dot_exact(pick_last, decay)
    a_t = a.T

    for h in range(M_HEADS):
        hs = h * HEAD_DIM
        q = p_scr[:, OFF_Q + hs:OFF_Q + hs + HEAD_DIM]
        k = p_scr[:, OFF_K + hs:OFF_K + hs + HEAD_DIM] * K_SCALE
        v = p_scr[:, OFF_V + hs:OFF_V + hs + HEAD_DIM]
        q_bf = q.astype(_BF)
        a_b = jnp.broadcast_to(a_t[h:h + 1, :], (CHUNK, CHUNK))
        m_b = jnp.broadcast_to(big_m[:, h:h + 1], (CHUNK, CHUNK))
        d = jnp.exp(jnp.where(causal, a_b - m_b, NEG_INF))
        s = _dot_nt(q_bf, k.astype(_BF)) * d
        sv = _dot(s.astype(_BF), jnp.concatenate([v.astype(_BF), ones_blk], axis=1))
        w_b = jnp.broadcast_to(w[:, h:h + 1], (CHUNK, LANES))
        wv = w_b * v
        k_t_bf = k.T.astype(_BF)

        def seq_body(g, num_inter):
            in_seq = grp == g
            c_old = c_ref[g, h]
            num_inter = num_inter + _dot(jnp.where(in_seq, q, 0.0).astype(_BF), c_old.astype(_BF))
            upd = _dot(k_t_bf, jnp.where(in_seq, wv, 0.0).astype(_BF))
            dec_row = dec_scr[pl.ds(g * DEC_SEQ, 1), :]
            dec_b = jnp.broadcast_to(dec_row[:, h:h + 1], (HEAD_DIM, HEAD_DIM))
            c_out_ref[g, h] = dec_b * c_old + upd
            return num_inter

        num_inter = lax.fori_loop(0, SEQ_PER_STEP, seq_body, jnp.zeros((CHUNK, HEAD_DIM), _F32))
        den_inter = jnp.sum(q * nrows_ref[:, hs:hs + HEAD_DIM], axis=-1, keepdims=True)
        inter_b = jnp.broadcast_to(inter[:, h:h + 1], (CHUNK, LANES))
        num = sv[:, :LANES] + inter_b * num_inter
        den = sv[:, LANES:] + inter_b * den_inter
        emt_b = jnp.broadcast_to(emt[:, h:h + 1], (CHUNK, LANES))
        ym = _head_output(num, den, emt_b, mhg_ref[:, hs:hs + HEAD_DIM],
                          p_scr[:, OFF_O + hs:OFF_O + hs + HEAD_DIM],
                          p_scr[:, OFF_ZM + hs:OFF_ZM + hs + HEAD_DIM])
        ycat_scr[:, CONV_WIDTH + hs:CONV_WIDTH + hs + HEAD_DIM] = ym.astype(_BF)
        n_upd = _dot(seq_sum, (w_b * k).astype(_BF))
        dec16_b = jnp.broadcast_to(dec16[:, h:h + 1], (SEQ_PER_STEP, HEAD_DIM))
        n_out_ref[:, hs:hs + HEAD_DIM] = dec16_b * n_ref[:, hs:hs + HEAD_DIM] + n_upd

    out = _dot(ycat_scr[...], wo_ref[...]) + x
    y_ref[...] = _rms_rows(out, fgain_ref[...])


def _sample_call(xs, weights, sconv, mrows, nrows, n2d, c_state, *, vmem_mb):
    wm, wg, wo, ng, fgain, mhg, cw, big, bfg = weights
    rows = xs.shape[0]
    nseq = c_state.shape[0]
    steps = rows // CHUNK

    def const(shape):
        nd = len(shape)
        return pl.BlockSpec(shape, lambda i: (0,) * nd)

    def rows_spec(width):
        return pl.BlockSpec((CHUNK, width), lambda i: (i, 0))

    c_spec = pl.BlockSpec((SEQ_PER_STEP, M_HEADS, HEAD_DIM, HEAD_DIM), lambda i: (i, 0, 0, 0))
    seq_spec = pl.BlockSpec((SEQ_PER_STEP, M_WIDTH), lambda i: (i, 0))
    in_specs = [
        rows_spec(D_MODEL),
        const(wm.shape), const(wg.shape), const(wo.shape), const(ng.shape), const(fgain.shape),
        const(mhg.shape), const(cw.shape), const(big.shape), const(bfg.shape),
        rows_spec(CONV_WIDTH), rows_spec(LANES), rows_spec(M_WIDTH), seq_spec, c_spec,
    ]
    out_shape = (
        jax.ShapeDtypeStruct((rows, D_MODEL), _F32),
        jax.ShapeDtypeStruct((rows, CONV_WIDTH), _F32),
        jax.ShapeDtypeStruct(c_state.shape, _F32),
        jax.ShapeDtypeStruct((nseq, M_WIDTH), _F32),
        jax.ShapeDtypeStruct((nseq, LANES), _F32),
    )
    out_specs = (
        rows_spec(D_MODEL), rows_spec(CONV_WIDTH), c_spec, seq_spec,
        pl.BlockSpec((SEQ_PER_STEP, LANES), lambda i: (i, 0)),
    )
    scratch = [
        pltpu.VMEM((CHUNK, MAIN_WIDTH), _F32),
        pltpu.VMEM((CHUNK, LANES), _F32),
        pltpu.VMEM((CHUNK, D_MODEL), _BF),
    ]
    return pl.pallas_call(
        _sample_body,
        grid=(steps,),
        in_specs=in_specs,
        out_specs=out_specs,
        out_shape=out_shape,
        scratch_shapes=scratch,
        compiler_params=pltpu.CompilerParams(
            dimension_semantics=("arbitrary",),
            vmem_limit_bytes=vmem_mb * 1024 * 1024),
        name="sample_layer",
    )(xs, wm, wg, wo, ng, fgain, mhg, cw, big, bfg, sconv, mrows, nrows, n2d, c_state)


def _pad_lanes(v):
    return jnp.pad(v.astype(_F32), (0, LANES - v.shape[0])).reshape(1, LANES)


def kernel(x_prompt, x_sample, state_conv, state_mlstm_c, state_mlstm_n, state_mlstm_m, meta_tokens, norm_gain,
           w_in, conv_w, b_igate, b_fgate, mh_norm_gain, w_out, final_norm_gain):
    nb, seq, _ = x_prompt.shape
    nseq, dec_seq, _ = x_sample.shape
    assert dec_seq == DEC_SEQ and meta_tokens.shape[0] == N_META

    wm = w_in[:, :MAIN_WIDTH].astype(_BF)
    wg = jnp.zeros((D_MODEL, GATE_WIDTH), _F32)
    wg = wg.at[:, 0:M_HEADS].set(w_in[:, MAIN_WIDTH:MAIN_WIDTH + M_HEADS])
    wg = wg.at[:, LANES:LANES + M_HEADS].set(w_in[:, MAIN_WIDTH + M_HEADS:MAIN_WIDTH + 2 * M_HEADS])
    wg = wg.astype(_BF)
    weights = (wm, wg, w_out.astype(_BF),
               norm_gain.reshape(1, D_MODEL), final_norm_gain.reshape(1, D_MODEL),
               mh_norm_gain.reshape(1, M_WIDTH), conv_w, _pad_lanes(b_igate), _pad_lanes(b_fgate))

    x_meta = jnp.concatenate([jnp.zeros((CHUNK - N_META, D_MODEL), _F32), meta_tokens.astype(_F32)], axis=0)
    zero_c = jnp.zeros((M_HEADS, HEAD_DIM, 2 * LANES), _F32)
    zero_m = jnp.zeros((1, LANES), _F32)
    zero_u = jnp.zeros((SUBLANES, CONV_WIDTH), _F32)
    _, caug_meta, m_meta, ut_meta = _prompt_call(
        x_meta[None], weights, zero_c, zero_m, zero_u, tb=CHUNK, n_pad=CHUNK - N_META, vmem_mb=40)

    y_prompt, caug_p, m_p, ut_p = _prompt_call(
        x_prompt, weights, caug_meta[0], m_meta[0], ut_meta[0], tb=256, n_pad=0, vmem_mb=48)
    conv_p = ut_p[:, SUBLANES - (CONV_K - 1):, :]
    c_p = caug_p[..., :HEAD_DIM]
    n_p = caug_p[..., HEAD_DIM]
    m_pp = m_p[:, 0, :M_HEADS]

    rows = nseq * DEC_SEQ
    xs = x_sample.reshape(rows, D_MODEL)
    sconv = jnp.pad(state_conv, ((0, 0), (0, DEC_SEQ - (CONV_K - 1)), (0, 0))).reshape(rows, CONV_WIDTH)
    m_pad = jnp.pad(state_mlstm_m, ((0, 0), (0, LANES - M_HEADS)))
    mrows = jnp.repeat(m_pad, DEC_SEQ, axis=0)
    n2d = state_mlstm_n.reshape(nseq, M_WIDTH)
    nrows = jnp.repeat(n2d, DEC_SEQ, axis=0)
    ys, u_s, c_s, n_s, m_s = _sample_call(xs, weights, sconv, mrows, nrows, n2d, state_mlstm_c, vmem_mb=56)
    y_sample = ys.reshape(nseq, DEC_SEQ, D_MODEL)
    conv_s = u_s.reshape(nseq, DEC_SEQ, CONV_WIDTH)[:, DEC_SEQ - (CONV_K - 1):, :]
    n_s3 = n_s.reshape(nseq, M_HEADS, HEAD_DIM)
    m_ss = m_s[:, :M_HEADS]

    return (y_prompt, y_sample, conv_p, c_p, n_p, m_pp, conv_s, c_s, n_s3, m_ss)
```

```python
import functools

import jax
import jax.numpy as jnp
from jax import lax
from jax.experimental import pallas as pl
from jax.experimental.pallas import tpu as pltpu

D_MODEL = 1024
CONV_WIDTH = 512
M_WIDTH = 512
M_HEADS = 4
HEAD_DIM = 128
N_META = 16
CONV_K = 3
EPS = 1e-6
MAIN_WIDTH = 4 * CONV_WIDTH + 5 * M_WIDTH
GATE_WIDTH = 256
LANES = 128
MXU_COLS = 256
SUBLANES = 8
CHUNK = 128
DEC_SEQ = 8
SEQ_PER_STEP = CHUNK // DEC_SEQ
NEG_INF = float("-inf")
K_SCALE = HEAD_DIM ** -0.5

QKV_WIDTH = 3 * M_WIDTH
REST_WIDTH = 4 * CONV_WIDTH + 2 * M_WIDTH
IN_OFF_Q = 4 * CONV_WIDTH
IN_OFF_O = IN_OFF_Q + QKV_WIDTH
A_Q, A_K, A_V = 0, 512, 1024
B_GB, B_GC, B_HC, B_ZC, B_O, B_ZM = 0, 512, 1024, 1536, 2048, 2560

_HI = lax.Precision.HIGHEST
_BF = jnp.bfloat16
_F32 = jnp.float32


def _dot(a, b):
    return jnp.dot(a, b, preferred_element_type=_F32)


def _dot_exact(a, b):
    return jnp.dot(a, b, precision=_HI, preferred_element_type=_F32)


def _dot_nt(a, b):
    return lax.dot_general(a, b, (((1,), (1,)), ((), ())), preferred_element_type=_F32)


def _rms_rows(x, gain_row):
    return x * lax.rsqrt(jnp.mean(x * x, axis=-1, keepdims=True) + EPS) * gain_row


def _sigmoid(x):
    return 1.0 / (1.0 + jnp.exp(-x))


def _silu(x):
    return x * _sigmoid(x)


def _cummax_rows(x, pos, period):
    s = 1
    while s < period:
        shifted = pltpu.roll(x, s, 0)
        x = jnp.maximum(x, jnp.where(pos >= s, shifted, NEG_INF))
        s *= 2
    return x


def _gate_columns(g_raw, big_row, bfg_row):
    ig = g_raw[:, :LANES] + big_row
    lf = jax.nn.log_sigmoid(g_raw[:, LANES:] + bfg_row)
    return ig, lf


def _head_norm(num, den, emt_b, gain_row):
    hh = num / jnp.maximum(jnp.abs(den), emt_b)
    return hh * lax.rsqrt(jnp.mean(hh * hh, axis=-1, keepdims=True) + EPS) * gain_row


def _head_output(num, den, emt_b, gain_row, o, zm):
    return _head_norm(num, den, emt_b, gain_row) * _sigmoid(o) * _silu(zm)


def _conv_out(gb, zc, u, u_m1, u_m2, cw_ref):
    yc = cw_ref[0:1, :] * u_m2 + cw_ref[1:2, :] * u_m1 + cw_ref[2:3, :] * u
    return gb * yc * _silu(zc)


def _prompt_body(x_ref, wa_ref, wb_ref, wg_ref, wo_ref, ng_ref, fgain_ref, mhg_ref, cw_ref, big_ref, bfg_ref,
                 caug0_ref, m0_ref, ut0_ref,
                 y_ref, caug_out_ref, m_out_ref, ut_out_ref,
                 pa_scr, pb_scr, g_scr, hn_scr, ycat_scr, ubuf, caug, m_scr, *, tb, n_pad):
    i = pl.program_id(1)
    last = pl.num_programs(1) - 1

    @pl.when(i == 0)
    def _():
        caug[...] = caug0_ref[...]
        m_scr[...] = m0_ref[...]
        ubuf[0:SUBLANES, :] = ut0_ref[...]

    x = x_ref[0]
    xn = _rms_rows(x, ng_ref[...]).astype(_BF)
    pa_scr[...] = _dot(xn, wa_ref[...])
    g_scr[...] = _dot(xn, wg_ref[...])

    row = lax.broadcasted_iota(jnp.int32, (CHUNK, CHUNK), 0)
    col = lax.broadcasted_iota(jnp.int32, (CHUNK, CHUNK), 1)
    causal = col <= row
    tril = causal.astype(_F32)
    ones_blk = jnp.ones((CHUNK, LANES), _BF)

    n_chunks = tb // CHUNK
    b_tiles = REST_WIDTH // MXU_COLS

    def project_b_tile(t):
        cols = slice(t * MXU_COLS, (t + 1) * MXU_COLS)
        pb_scr[:, cols] = _dot(xn, wb_ref[:, cols])

    early_tiles = 4
    for t in range(early_tiles):
        project_b_tile(t)

    gates = []
    m_prev = m_scr[...]
    for c in range(n_chunks):
        rows = slice(c * CHUNK, (c + 1) * CHUNK)
        ig, lf = _gate_columns(g_scr[rows, :], big_ref[...], bfg_ref[...])
        if n_pad:
            ig = jnp.where(row >= n_pad, ig, NEG_INF)
            lf = jnp.where(row >= n_pad, lf, 0.0)
        b = _dot_exact(tril, lf)
        a = ig - b
        big_m = jnp.maximum(m_prev, _cummax_rows(a, row, CHUNK))
        m_t = b + big_m
        m_last = big_m[CHUNK - 1:CHUNK, :]
        gates.append(dict(a_t=a.T, big_m=big_m, inter=jnp.exp(m_prev - big_m), emt=jnp.exp(-m_t),
                          w=jnp.exp(a - m_last), decay=jnp.exp(m_prev - m_last)))
        m_prev = m_t[CHUNK - 1:CHUNK, :]
    m_scr[...] = m_prev

    def stage_a(c, h):
        rows = slice(c * CHUNK, (c + 1) * CHUNK)
        hs = h * HEAD_DIM
        g = gates[c]
        q_bf = pa_scr[rows, A_Q + hs:A_Q + hs + HEAD_DIM].astype(_BF)
        k = pa_scr[rows, A_K + hs:A_K + hs + HEAD_DIM] * K_SCALE
        a_b = jnp.broadcast_to(g["a_t"][h:h + 1, :], (CHUNK, CHUNK))
        m_b = jnp.broadcast_to(g["big_m"][:, h:h + 1], (CHUNK, CHUNK))
        d = jnp.exp(jnp.where(causal, a_b - m_b, NEG_INF))
        return dict(qk=_dot_nt(q_bf, k.astype(_BF)), d=d, q_bf=q_bf, k=k)

    def stage_b(c, h, st):
        rows = slice(c * CHUNK, (c + 1) * CHUNK)
        hs = h * HEAD_DIM
        g = gates[c]
        v = pa_scr[rows, A_V + hs:A_V + hs + HEAD_DIM]
        s = st["qk"] * st["d"]
        sv = _dot(s.astype(_BF), jnp.concatenate([v.astype(_BF), ones_blk], axis=1))
        qc = _dot(st["q_bf"], caug[h].astype(_BF))
        tot = sv + jnp.broadcast_to(g["inter"][:, h:h + 1], (CHUNK, 2 * LANES)) * qc
        emt_b = jnp.broadcast_to(g["emt"][:, h:h + 1], (CHUNK, LANES))
        hn_scr[rows, hs:hs + HEAD_DIM] = _head_norm(tot[:, :LANES], tot[:, LANES:], emt_b,
                                                    mhg_ref[:, hs:hs + HEAD_DIM])
        w_b = jnp.broadcast_to(g["w"][:, h:h + 1], (CHUNK, LANES))
        wv = jnp.concatenate([(w_b * v).astype(_BF), w_b.astype(_BF)], axis=1)
        upd = _dot(st["k"].T.astype(_BF), wv)
        decay_b = jnp.broadcast_to(g["decay"][:, h:h + 1], (HEAD_DIM, 2 * LANES))
        caug[h] = decay_b * caug[h] + upd

    units = [(c, h) for c in range(n_chunks) for h in range(M_HEADS)]
    lag = min(2, len(units))
    pending = {}
    for t in range(max(len(units) + lag, b_tiles - early_tiles)):
        if t < len(units):
            pending[t] = stage_a(*units[t])
        if early_tiles + t < b_tiles:
            project_b_tile(early_tiles + t)
        if 0 <= t - lag < len(units):
            stage_b(*units[t - lag], pending.pop(t - lag))

    u = pb_scr[:, B_GC:B_GC + CONV_WIDTH] * pb_scr[:, B_HC:B_HC + CONV_WIDTH]
    ubuf[SUBLANES:SUBLANES + tb, :] = u
    u_m1 = ubuf[SUBLANES - 1:SUBLANES - 1 + tb, :]
    u_m2 = ubuf[SUBLANES - 2:SUBLANES - 2 + tb, :]
    y_conv = _conv_out(pb_scr[:, B_GB:B_GB + CONV_WIDTH], pb_scr[:, B_ZC:B_ZC + CONV_WIDTH],
                       u, u_m1, u_m2, cw_ref)
    ycat_scr[:, 0:CONV_WIDTH] = y_conv.astype(_BF)
    ubuf[0:SUBLANES, :] = ubuf[tb:tb + SUBLANES, :]

    y_m = hn_scr[...] * _sigmoid(pb_scr[:, B_O:B_O + M_WIDTH]) * _silu(pb_scr[:, B_ZM:B_ZM + M_WIDTH])
    ycat_scr[:, CONV_WIDTH:CONV_WIDTH + M_WIDTH] = y_m.astype(_BF)

    out = _dot(ycat_scr[...], wo_ref[...]) + x
    y_ref[0] = _rms_rows(out, fgain_ref[...])

    @pl.when(i == last)
    def _():
        caug_out_ref[0] = caug[...]
        m_out_ref[0] = m_scr[...]
        ut_out_ref[0] = ubuf[0:SUBLANES, :]


def _const_spec(shape):
    nd = len(shape)
    return pl.BlockSpec(shape, lambda b, i: (0,) * nd)


def _prompt_call(x, weights, caug0, m0, ut0, *, tb, n_pad, vmem_mb):
    wa, wb, wg, wo, ng, fgain, mhg, cw, big, bfg = weights
    nb, t, _ = x.shape
    nt = t // tb
    body = functools.partial(_prompt_body, tb=tb, n_pad=n_pad)
    in_specs = [
        pl.BlockSpec((1, tb, D_MODEL), lambda b, i: (b, i, 0)),
        _const_spec(wa.shape), _const_spec(wb.shape), _const_spec(wg.shape), _const_spec(wo.shape),
        _const_spec(ng.shape), _const_spec(fgain.shape), _const_spec(mhg.shape),
        _const_spec(cw.shape), _const_spec(big.shape), _const_spec(bfg.shape),
        _const_spec(caug0.shape), _const_spec(m0.shape), _const_spec(ut0.shape),
    ]
    out_shape = (
        jax.ShapeDtypeStruct((nb, t, D_MODEL), _F32),
        jax.ShapeDtypeStruct((nb, M_HEADS, HEAD_DIM, 2 * LANES), _F32),
        jax.ShapeDtypeStruct((nb, 1, LANES), _F32),
        jax.ShapeDtypeStruct((nb, SUBLANES, CONV_WIDTH), _F32),
    )
    out_specs = (
        pl.BlockSpec((1, tb, D_MODEL), lambda b, i: (b, i, 0)),
        pl.BlockSpec((1, M_HEADS, HEAD_DIM, 2 * LANES), lambda b, i: (b, 0, 0, 0)),
        pl.BlockSpec((1, 1, LANES), lambda b, i: (b, 0, 0)),
        pl.BlockSpec((1, SUBLANES, CONV_WIDTH), lambda b, i: (b, 0, 0)),
    )
    scratch = [
        pltpu.VMEM((tb, QKV_WIDTH), _F32),
        pltpu.VMEM((tb, REST_WIDTH), _F32),
        pltpu.VMEM((tb, GATE_WIDTH), _F32),
        pltpu.VMEM((tb, M_WIDTH), _F32),
        pltpu.VMEM((tb, D_MODEL), _BF),
        pltpu.VMEM((tb + 2 * SUBLANES, CONV_WIDTH), _F32),
        pltpu.VMEM((M_HEADS, HEAD_DIM, 2 * LANES), _F32),
        pltpu.VMEM((1, LANES), _F32),
    ]
    return pl.pallas_call(
        body,
        grid=(nb, nt),
        in_specs=in_specs,
        out_specs=out_specs,
        out_shape=out_shape,
        scratch_shapes=scratch,
        compiler_params=pltpu.CompilerParams(
            dimension_semantics=("arbitrary", "arbitrary"),
            vmem_limit_bytes=vmem_mb * 1024 * 1024),
        name="prompt_layer" if n_pad == 0 else "meta_layer",
    )(x, wa, wb, wg, wo, ng, fgain, mhg, cw, big, bfg, caug0, m0, ut0)


def _sample_body(x_ref, wa_ref, wb_ref, wg_ref, wo_ref, ng_ref, fgain_ref, mhg_ref, cw_ref, big_ref, bfg_ref,
                 sconv_ref, mrows_ref, nrows_ref, n_ref, c_ref,
                 y_ref, u_ref, c_out_ref, n_out_ref, m_out_ref,
                 pa_scr, pb_scr, ni_scr, ycat_scr):
    x = x_ref[...]
    xn = _rms_rows(x, ng_ref[...]).astype(_BF)
    pa_scr[...] = _dot(xn, wa_ref[...])
    g_raw = _dot(xn, wg_ref[...])
    pb_scr[...] = _dot(xn, wb_ref[...])

    row = lax.broadcasted_iota(jnp.int32, (CHUNK, CHUNK), 0)
    col = lax.broadcasted_iota(jnp.int32, (CHUNK, CHUNK), 1)
    pos = row % DEC_SEQ
    grp = row // DEC_SEQ
    same_seq = grp == (col // DEC_SEQ)
    causal = same_seq & (col <= row)
    tril = causal.astype(_F32)
    to_last = (same_seq & (col % DEC_SEQ == DEC_SEQ - 1)).astype(_F32)
    grow = lax.broadcasted_iota(jnp.int32, (SEQ_PER_STEP, CHUNK), 0)
    gcol = lax.broadcasted_iota(jnp.int32, (SEQ_PER_STEP, CHUNK), 1)
    pick_last = (gcol == grow * DEC_SEQ + (DEC_SEQ - 1)).astype(_F32)
    seq_sum = (gcol // DEC_SEQ == grow).astype(_BF)
    ones_blk = jnp.ones((CHUNK, LANES), _BF)

    u = pb_scr[:, B_GC:B_GC + CONV_WIDTH] * pb_scr[:, B_HC:B_HC + CONV_WIDTH]
    u_ref[...] = u
    sc = sconv_ref[...]
    posw = lax.broadcasted_iota(jnp.int32, (CHUNK, CONV_WIDTH), 0) % DEC_SEQ
    u_m1 = jnp.where(posw == 0, pltpu.roll(sc, CHUNK - 1, 0), pltpu.roll(u, 1, 0))
    u_m2 = jnp.where(posw < 2, sc, pltpu.roll(u, 2, 0))
    y_conv = _conv_out(pb_scr[:, B_GB:B_GB + CONV_WIDTH], pb_scr[:, B_ZC:B_ZC + CONV_WIDTH],
                       u, u_m1, u_m2, cw_ref)
    ycat_scr[:, 0:CONV_WIDTH] = y_conv.astype(_BF)

    ig, lf = _gate_columns(g_raw, big_ref[...], bfg_ref[...])
    b = _dot_exact(tril, lf)
    a = ig - b
    m_prev = mrows_ref[...]
    big_m = jnp.maximum(m_prev, _cummax_rows(a, pos, DEC_SEQ))
    m_t = b + big_m
    inter = jnp.exp(m_prev - big_m)
    emt = jnp.exp(-m_t)
    m_last = _dot_exact(to_last, big_m)
    w = jnp.exp(a - m_last)
    decay = jnp.exp(m_prev - m_last)
    m_out_ref[...] = _dot_exact(pick_last, m_t)
    dec16 = _dot_exact(pick_last, decay)
    a_t = a.T

    for h in range(M_HEADS):
        hs = h * HEAD_DIM
        q = pa_scr[:, A_Q + hs:A_Q + hs + HEAD_DIM]
        k = pa_scr[:, A_K + hs:A_K + hs + HEAD_DIM] * K_SCALE
        v = pa_scr[:, A_V + hs:A_V + hs + HEAD_DIM]
        q_bf = q.astype(_BF)
        a_b = jnp.broadcast_to(a_t[h:h + 1, :], (CHUNK, CHUNK))
        m_b = jnp.broadcast_to(big_m[:, h:h + 1], (CHUNK, CHUNK))
        d = jnp.exp(jnp.where(causal, a_b - m_b, NEG_INF))
        s = _dot_nt(q_bf, k.astype(_BF)) * d
        sv = _dot(s.astype(_BF), jnp.concatenate([v.astype(_BF), ones_blk], axis=1))
        w_b = jnp.broadcast_to(w[:, h:h + 1], (CHUNK, LANES))
        wv = w_b * v
        k_t_bf = k.T.astype(_BF)
        dec_b = jnp.broadcast_to(decay[:, h:h + 1], (CHUNK, HEAD_DIM))

        for g in range(SEQ_PER_STEP):
            seq_rows = slice(g * DEC_SEQ, (g + 1) * DEC_SEQ)
            c_old = c_ref[g, h]
            ni_scr[seq_rows, hs:hs + HEAD_DIM] = _dot(q[seq_rows, :].astype(_BF), c_old.astype(_BF))
            upd = _dot(k_t_bf, jnp.where(grp == g, wv, 0.0).astype(_BF))
            dec_g = jnp.broadcast_to(dec_b[g * DEC_SEQ:g * DEC_SEQ + 1, :], (HEAD_DIM, HEAD_DIM))
            c_out_ref[g, h] = dec_g * c_old + upd

        num_inter = ni_scr[:, hs:hs + HEAD_DIM]
        den_inter = jnp.sum(q * nrows_ref[:, hs:hs + HEAD_DIM], axis=-1, keepdims=True)
        inter_b = jnp.broadcast_to(inter[:, h:h + 1], (CHUNK, LANES))
        num = sv[:, :LANES] + inter_b * num_inter
        den = sv[:, LANES:] + inter_b * den_inter
        emt_b = jnp.broadcast_to(emt[:, h:h + 1], (CHUNK, LANES))
        ym = _head_output(num, den, emt_b, mhg_ref[:, hs:hs + HEAD_DIM],
                          pb_scr[:, B_O + hs:B_O + hs + HEAD_DIM],
                          pb_scr[:, B_ZM + hs:B_ZM + hs + HEAD_DIM])
        ycat_scr[:, CONV_WIDTH + hs:CONV_WIDTH + hs + HEAD_DIM] = ym.astype(_BF)
        n_upd = _dot(seq_sum, (w_b * k).astype(_BF))
        dec16_b = jnp.broadcast_to(dec16[:, h:h + 1], (SEQ_PER_STEP, HEAD_DIM))
        n_out_ref[:, hs:hs + HEAD_DIM] = dec16_b * n_ref[:, hs:hs + HEAD_DIM] + n_upd

    out = _dot(ycat_scr[...], wo_ref[...]) + x
    y_ref[...] = _rms_rows(out, fgain_ref[...])


def _sample_call(xs, weights, sconv, mrows, nrows, n2d, c_state, *, vmem_mb):
    wa, wb, wg, wo, ng, fgain, mhg, cw, big, bfg = weights
    rows = xs.shape[0]
    nseq = c_state.shape[0]
    steps = rows // CHUNK

    def const(shape):
        nd = len(shape)
        return pl.BlockSpec(shape, lambda i: (0,) * nd)

    def rows_spec(width):
        return pl.BlockSpec((CHUNK, width), lambda i: (i, 0))

    c_spec = pl.BlockSpec((SEQ_PER_STEP, M_HEADS, HEAD_DIM, HEAD_DIM), lambda i: (i, 0, 0, 0))
    seq_spec = pl.BlockSpec((SEQ_PER_STEP, M_WIDTH), lambda i: (i, 0))
    in_specs = [
        rows_spec(D_MODEL),
        const(wa.shape), const(wb.shape), const(wg.shape), const(wo.shape), const(ng.shape), const(fgain.shape),
        const(mhg.shape), const(cw.shape), const(big.shape), const(bfg.shape),
        rows_spec(CONV_WIDTH), rows_spec(LANES), rows_spec(M_WIDTH), seq_spec, c_spec,
    ]
    out_shape = (
        jax.ShapeDtypeStruct((rows, D_MODEL), _F32),
        jax.ShapeDtypeStruct((rows, CONV_WIDTH), _F32),
        jax.ShapeDtypeStruct(c_state.shape, _F32),
        jax.ShapeDtypeStruct((nseq, M_WIDTH), _F32),
        jax.ShapeDtypeStruct((nseq, LANES), _F32),
    )
    out_specs = (
        rows_spec(D_MODEL), rows_spec(CONV_WIDTH), c_spec, seq_spec,
        pl.BlockSpec((SEQ_PER_STEP, LANES), lambda i: (i, 0)),
    )
    scratch = [
        pltpu.VMEM((CHUNK, QKV_WIDTH), _F32),
        pltpu.VMEM((CHUNK, REST_WIDTH), _F32),
        pltpu.VMEM((CHUNK, M_WIDTH), _F32),
        pltpu.VMEM((CHUNK, D_MODEL), _BF),
    ]
    return pl.pallas_call(
        _sample_body,
        grid=(steps,),
        in_specs=in_specs,
        out_specs=out_specs,
        out_shape=out_shape,
        scratch_shapes=scratch,
        compiler_params=pltpu.CompilerParams(
            dimension_semantics=("arbitrary",),
            vmem_limit_bytes=vmem_mb * 1024 * 1024),
        name="sample_layer",
    )(xs, wa, wb, wg, wo, ng, fgain, mhg, cw, big, bfg, sconv, mrows, nrows, n2d, c_state)


def _pad_lanes(v):
    return jnp.pad(v.astype(_F32), (0, LANES - v.shape[0])).reshape(1, LANES)


def kernel(x_prompt, x_sample, state_conv, state_mlstm_c, state_mlstm_n, state_mlstm_m, meta_tokens, norm_gain,
           w_in, conv_w, b_igate, b_fgate, mh_norm_gain, w_out, final_norm_gain):
    nb, seq, _ = x_prompt.shape
    nseq, dec_seq, _ = x_sample.shape
    assert dec_seq == DEC_SEQ and meta_tokens.shape[0] == N_META

    wa = w_in[:, IN_OFF_Q:IN_OFF_O].astype(_BF)
    wb = jnp.concatenate([w_in[:, :IN_OFF_Q], w_in[:, IN_OFF_O:MAIN_WIDTH]], axis=1).astype(_BF)
    gate_pad = jnp.zeros((D_MODEL, LANES - M_HEADS), _F32)
    wg = jnp.concatenate([w_in[:, MAIN_WIDTH:MAIN_WIDTH + M_HEADS], gate_pad,
                          w_in[:, MAIN_WIDTH + M_HEADS:MAIN_WIDTH + 2 * M_HEADS], gate_pad], axis=1).astype(_BF)
    weights = (wa, wb, wg, w_out.astype(_BF),
               norm_gain.reshape(1, D_MODEL), final_norm_gain.reshape(1, D_MODEL),
               mh_norm_gain.reshape(1, M_WIDTH), conv_w, _pad_lanes(b_igate), _pad_lanes(b_fgate))

    x_meta = jnp.concatenate([jnp.zeros((CHUNK - N_META, D_MODEL), _F32), meta_tokens.astype(_F32)], axis=0)
    zero_c = jnp.zeros((M_HEADS, HEAD_DIM, 2 * LANES), _F32)
    zero_m = jnp.zeros((1, LANES), _F32)
    zero_u = jnp.zeros((SUBLANES, CONV_WIDTH), _F32)
    _, caug_meta, m_meta, ut_meta = _prompt_call(
        x_meta[None], weights, zero_c, zero_m, zero_u, tb=CHUNK, n_pad=CHUNK - N_META, vmem_mb=40)

    y_prompt, caug_p, m_p, ut_p = _prompt_call(
        x_prompt, weights, caug_meta[0], m_meta[0], ut_meta[0], tb=512, n_pad=0, vmem_mb=56)
    conv_p = ut_p[:, SUBLANES - (CONV_K - 1):, :]
    c_p = caug_p[..., :HEAD_DIM]
    n_p = caug_p[..., HEAD_DIM]
    m_pp = m_p[:, 0, :M_HEADS]

    rows = nseq * DEC_SEQ
    xs = x_sample.reshape(rows, D_MODEL)
    sconv = jnp.pad(state_conv, ((0, 0), (0, DEC_SEQ - (CONV_K - 1)), (0, 0))).reshape(rows, CONV_WIDTH)
    m_pad = jnp.pad(state_mlstm_m, ((0, 0), (0, LANES - M_HEADS)))
    mrows = jnp.repeat(m_pad, DEC_SEQ, axis=0)
    n2d = state_mlstm_n.reshape(nseq, M_WIDTH)
    nrows = jnp.repeat(n2d, DEC_SEQ, axis=0)
    ys, u_s, c_s, n_s, m_s = _sample_call(xs, weights, sconv, mrows, nrows, n2d, state_mlstm_c, vmem_mb=56)
    y_sample = ys.reshape(nseq, DEC_SEQ, D_MODEL)
    conv_s = u_s.reshape(nseq, DEC_SEQ, CONV_WIDTH)[:, DEC_SEQ - (CONV_K - 1):, :]
    n_s3 = n_s.reshape(nseq, M_HEADS, HEAD_DIM)
    m_ss = m_s[:, :M_HEADS]

    return (y_prompt, y_sample, conv_p, c_p, n_p, m_pp, conv_s, c_s, n_s3, m_ss)
```

```python
import functools

import jax
import jax.numpy as jnp
from jax import lax
from jax.experimental import pallas as pl
from jax.experimental.pallas import tpu as pltpu

D_MODEL = 1024
CONV_WIDTH = 512
M_WIDTH = 512
M_HEADS = 4
HEAD_DIM = 128
N_META = 16
CONV_K = 3
EPS = 1e-6
MAIN_WIDTH = 4 * CONV_WIDTH + 5 * M_WIDTH
LANES = 128
SUBLANES = 8
MXU_COLS = 256
N_TILES = MAIN_WIDTH // MXU_COLS
CHUNK = 128
DEC_SEQ = 8
SEQ_PER_STEP = CHUNK // DEC_SEQ
NEG_INF = float("-inf")
K_SCALE = HEAD_DIM ** -0.5

OFF_GB, OFF_GC, OFF_HC, OFF_ZC = 0, 512, 1024, 1536
OFF_Q, OFF_K, OFF_V, OFF_O, OFF_ZM = 2048, 2560, 3072, 3584, 4096
QKV_TILES = tuple(range(OFF_Q // MXU_COLS, OFF_O // MXU_COLS))
REST_TILES = tuple(t for t in range(N_TILES) if t not in QKV_TILES)

_HI = lax.Precision.HIGHEST
_BF = jnp.bfloat16
_F32 = jnp.float32


def _dot(a, b):
    return jnp.dot(a, b, preferred_element_type=_F32)


def _dot_exact(a, b):
    return jnp.dot(a, b, precision=_HI, preferred_element_type=_F32)


def _dot_nt(a, b):
    return lax.dot_general(a, b, (((1,), (1,)), ((), ())), preferred_element_type=_F32)


def _rms_rows(x, gain_row):
    return x * lax.rsqrt(jnp.mean(x * x, axis=-1, keepdims=True) + EPS) * gain_row


def _sigmoid(x):
    return 1.0 / (1.0 + jnp.exp(-x))


def _silu(x):
    return x * _sigmoid(x)


def _tile_cols(t):
    return slice(t * MXU_COLS, (t + 1) * MXU_COLS)


def _cummax_rows(x, pos, period):
    s = 1
    while s < period:
        shifted = pltpu.roll(x, s, 0)
        x = jnp.maximum(x, jnp.where(pos >= s, shifted, NEG_INF))
        s *= 2
    return x


def _gate_columns(g_raw, big_row, bfg_row):
    ig = g_raw + big_row
    lf = jax.nn.log_sigmoid(pltpu.roll(g_raw, LANES - M_HEADS, 1) + bfg_row)
    return ig, lf


def _head_norm(num, den, emt_b, gain_row):
    hh = num / jnp.maximum(jnp.abs(den), emt_b)
    return hh * lax.rsqrt(jnp.mean(hh * hh, axis=-1, keepdims=True) + EPS) * gain_row


def _conv_out(gb, zc, u, u_m1, u_m2, cw_ref):
    yc = cw_ref[0:1, :] * u_m2 + cw_ref[1:2, :] * u_m1 + cw_ref[2:3, :] * u
    return gb * yc * _silu(zc)


def _gated_heads(hn, p_scr):
    return hn * _sigmoid(p_scr[:, OFF_O:OFF_O + M_WIDTH]) * _silu(p_scr[:, OFF_ZM:OFF_ZM + M_WIDTH])


def _sequence_block(xn, project_tile, p_scr, g_scr, hn_scr, ycat_scr, ubuf, caug, m_scr,
                    wg_ref, mhg_ref, cw_ref, big_ref, bfg_ref, *, tb, n_pad):
    g_scr[...] = _dot_nt(xn, wg_ref[...].astype(_BF))
    late_tiles = []
    if project_tile is not None:
        for t in QKV_TILES:
            project_tile(t)
        early = 4
        for t in REST_TILES[:early]:
            project_tile(t)
        late_tiles = list(REST_TILES[early:])

    row = lax.broadcasted_iota(jnp.int32, (CHUNK, CHUNK), 0)
    col = lax.broadcasted_iota(jnp.int32, (CHUNK, CHUNK), 1)
    causal = col <= row
    tril = causal.astype(_F32)
    ones_blk = jnp.ones((CHUNK, LANES), _BF)

    n_chunks = tb // CHUNK
    gates = []
    m_prev = m_scr[...]
    for c in range(n_chunks):
        rows = slice(c * CHUNK, (c + 1) * CHUNK)
        ig, lf = _gate_columns(g_scr[rows, :], big_ref[...], bfg_ref[...])
        if n_pad:
            ig = jnp.where(row >= n_pad, ig, NEG_INF)
            lf = jnp.where(row >= n_pad, lf, 0.0)
        b = _dot_exact(tril, lf)
        a = ig - b
        big_m = jnp.maximum(m_prev, _cummax_rows(a, row, CHUNK))
        m_t = b + big_m
        m_last = big_m[CHUNK - 1:CHUNK, :]
        gates.append(dict(a_t=a.T, big_m=big_m, inter=jnp.exp(m_prev - big_m), emt=jnp.exp(-m_t),
                          w=jnp.exp(a - m_last), decay=jnp.exp(m_prev - m_last)))
        m_prev = m_t[CHUNK - 1:CHUNK, :]
    m_scr[...] = m_prev

    def stage_a(c, h):
        rows = slice(c * CHUNK, (c + 1) * CHUNK)
        hs = h * HEAD_DIM
        g = gates[c]
        q_bf = p_scr[rows, OFF_Q + hs:OFF_Q + hs + HEAD_DIM].astype(_BF)
        k = p_scr[rows, OFF_K + hs:OFF_K + hs + HEAD_DIM] * K_SCALE
        a_b = jnp.broadcast_to(g["a_t"][h:h + 1, :], (CHUNK, CHUNK))
        m_b = jnp.broadcast_to(g["big_m"][:, h:h + 1], (CHUNK, CHUNK))
        d = jnp.exp(jnp.where(causal, a_b - m_b, NEG_INF))
        return dict(qk=_dot_nt(q_bf, k.astype(_BF)), d=d, q_bf=q_bf, k=k)

    def stage_b(c, h, st):
        rows = slice(c * CHUNK, (c + 1) * CHUNK)
        hs = h * HEAD_DIM
        g = gates[c]
        v = p_scr[rows, OFF_V + hs:OFF_V + hs + HEAD_DIM]
        s = st["qk"] * st["d"]
        sv = _dot(s.astype(_BF), jnp.concatenate([v.astype(_BF), ones_blk], axis=1))
        qc = _dot(st["q_bf"], caug[h].astype(_BF))
        tot = sv + jnp.broadcast_to(g["inter"][:, h:h + 1], (CHUNK, 2 * LANES)) * qc
        emt_b = jnp.broadcast_to(g["emt"][:, h:h + 1], (CHUNK, LANES))
        hn_scr[rows, hs:hs + HEAD_DIM] = _head_norm(tot[:, :LANES], tot[:, LANES:], emt_b,
                                                    mhg_ref[:, hs:hs + HEAD_DIM])
        w_b = jnp.broadcast_to(g["w"][:, h:h + 1], (CHUNK, LANES))
        wv = jnp.concatenate([(w_b * v).astype(_BF), w_b.astype(_BF)], axis=1)
        upd = _dot(st["k"].T.astype(_BF), wv)
        decay_b = jnp.broadcast_to(g["decay"][:, h:h + 1], (HEAD_DIM, 2 * LANES))
        caug[h] = decay_b * caug[h] + upd

    units = [(c, h) for c in range(n_chunks) for h in range(M_HEADS)]
    lag = min(2, len(units))
    pending = {}
    for t in range(max(len(units) + lag, len(late_tiles))):
        if t < len(units):
            pending[t] = stage_a(*units[t])
        if t < len(late_tiles):
            project_tile(late_tiles[t])
        if 0 <= t - lag < len(units):
            stage_b(*units[t - lag], pending.pop(t - lag))

    u = p_scr[:, OFF_GC:OFF_GC + CONV_WIDTH] * p_scr[:, OFF_HC:OFF_HC + CONV_WIDTH]
    ubuf[SUBLANES:SUBLANES + tb, :] = u
    u_m1 = ubuf[SUBLANES - 1:SUBLANES - 1 + tb, :]
    u_m2 = ubuf[SUBLANES - 2:SUBLANES - 2 + tb, :]
    y_conv = _conv_out(p_scr[:, OFF_GB:OFF_GB + CONV_WIDTH], p_scr[:, OFF_ZC:OFF_ZC + CONV_WIDTH],
                       u, u_m1, u_m2, cw_ref)
    ycat_scr[:, 0:CONV_WIDTH] = y_conv.astype(_BF)
    ubuf[0:SUBLANES, :] = ubuf[tb:tb + SUBLANES, :]
    ycat_scr[:, CONV_WIDTH:CONV_WIDTH + M_WIDTH] = _gated_heads(hn_scr[...], p_scr).astype(_BF)


def _sequence_scratch(tb):
    return [
        pltpu.VMEM((tb, MAIN_WIDTH), _F32),
        pltpu.VMEM((tb, LANES), _F32),
        pltpu.VMEM((tb, M_WIDTH), _F32),
        pltpu.VMEM((tb, D_MODEL), _BF),
        pltpu.VMEM((tb + 2 * SUBLANES, CONV_WIDTH), _F32),
        pltpu.VMEM((M_HEADS, HEAD_DIM, 2 * LANES), _F32),
        pltpu.VMEM((1, LANES), _F32),
    ]


def _prep_body(wt_ref, x_ref, wg_ref, ng_ref, mhg_ref, cw_ref, big_ref, bfg_ref,
               w3_ref, caug_out_ref, m_out_ref, ut_out_ref,
               xn_scr, p3_scr, p_scr, g_scr, hn_scr, ycat_scr, ubuf, caug, m_scr):
    t = pl.program_id(0)

    @pl.when(t == 0)
    def _():
        xn_scr[...] = _rms_rows(x_ref[...], ng_ref[...]).astype(_BF)

    tile = wt_ref[...].T.astype(_BF)
    w3_ref[0] = tile
    p3_scr[t] = _dot(xn_scr[...], tile)

    @pl.when(t == N_TILES - 1)
    def _():
        for j in range(N_TILES):
            p_scr[:, _tile_cols(j)] = p3_scr[j]
        caug[...] = jnp.zeros_like(caug)
        m_scr[...] = jnp.zeros_like(m_scr)
        ubuf[0:SUBLANES, :] = jnp.zeros((SUBLANES, CONV_WIDTH), _F32)
        _sequence_block(xn_scr[...], None, p_scr, g_scr, hn_scr, ycat_scr, ubuf, caug, m_scr,
                        wg_ref, mhg_ref, cw_ref, big_ref, bfg_ref, tb=CHUNK, n_pad=CHUNK - N_META)
        caug_out_ref[...] = caug[...]
        m_out_ref[...] = m_scr[...]
        ut_out_ref[...] = ubuf[0:SUBLANES, :]


def _prep_call(w_t, x_meta, small):
    wg, ng, mhg, cw, big, bfg = small

    def const(shape):
        nd = len(shape)
        return pl.BlockSpec(shape, lambda t: (0,) * nd)

    in_specs = [
        pl.BlockSpec((MXU_COLS, D_MODEL), lambda t: (t, 0)),
        const(x_meta.shape), const(wg.shape), const(ng.shape), const(mhg.shape), const(cw.shape),
        const(big.shape), const(bfg.shape),
    ]
    out_shape = (
        jax.ShapeDtypeStruct((N_TILES, D_MODEL, MXU_COLS), _BF),
        jax.ShapeDtypeStruct((M_HEADS, HEAD_DIM, 2 * LANES), _F32),
        jax.ShapeDtypeStruct((1, LANES), _F32),
        jax.ShapeDtypeStruct((SUBLANES, CONV_WIDTH), _F32),
    )
    out_specs = (
        pl.BlockSpec((1, D_MODEL, MXU_COLS), lambda t: (t, 0, 0)),
        const(out_shape[1].shape), const(out_shape[2].shape), const(out_shape[3].shape),
    )
    scratch = [
        pltpu.VMEM((CHUNK, D_MODEL), _BF),
        pltpu.VMEM((N_TILES, CHUNK, MXU_COLS), _F32),
    ] + _sequence_scratch(CHUNK)
    return pl.pallas_call(
        _prep_body,
        grid=(N_TILES,),
        in_specs=in_specs,
        out_specs=out_specs,
        out_shape=out_shape,
        scratch_shapes=scratch,
        compiler_params=pltpu.CompilerParams(
            dimension_semantics=("arbitrary",), vmem_limit_bytes=32 * 1024 * 1024),
        name="prep_meta",
    )(w_t, x_meta, wg, ng, mhg, cw, big, bfg)


def _prompt_body(x_ref, w3_ref, wg_ref, wo_ref, ng_ref, fgain_ref, mhg_ref, cw_ref, big_ref, bfg_ref,
                 caug0_ref, m0_ref, ut0_ref,
                 y_ref, c_out_ref, n_out_ref, m_out_ref, conv_out_ref,
                 p_scr, g_scr, hn_scr, ycat_scr, ubuf, caug, m_scr, *, tb):
    i = pl.program_id(1)
    last = pl.num_programs(1) - 1

    @pl.when(i == 0)
    def _():
        caug[...] = caug0_ref[...]
        m_scr[...] = m0_ref[...]
        ubuf[0:SUBLANES, :] = ut0_ref[...]

    x = x_ref[0]
    xn = _rms_rows(x, ng_ref[...]).astype(_BF)

    def project_tile(t):
        p_scr[:, _tile_cols(t)] = _dot(xn, w3_ref[t])

    _sequence_block(xn, project_tile, p_scr, g_scr, hn_scr, ycat_scr, ubuf, caug, m_scr,
                    wg_ref, mhg_ref, cw_ref, big_ref, bfg_ref, tb=tb, n_pad=0)

    out = _dot(ycat_scr[...], wo_ref[...]) + x
    y_ref[0] = _rms_rows(out, fgain_ref[...])

    @pl.when(i == last)
    def _():
        m_out_ref[0] = m_scr[...]
        conv_out_ref[0] = ubuf[SUBLANES - (CONV_K - 1):SUBLANES, :]
        for h in range(M_HEADS):
            c_out_ref[0, h] = caug[h, :, 0:HEAD_DIM]
            n_out_ref[0, h:h + 1, :] = caug[h, :, LANES:2 * LANES].T[0:1, :]


def _prompt_call(x, w3, wo, small, fgain, caug0, m0, ut0, *, tb):
    wg, ng, mhg, cw, big, bfg = small
    nb, t, _ = x.shape
    nt = t // tb

    def const(shape):
        nd = len(shape)
        return pl.BlockSpec(shape, lambda b, i: (0,) * nd)

    in_specs = [
        pl.BlockSpec((1, tb, D_MODEL), lambda b, i: (b, i, 0)),
        const(w3.shape), const(wg.shape), const(wo.shape), const(ng.shape), const(fgain.shape),
        const(mhg.shape), const(cw.shape), const(big.shape), const(bfg.shape),
        const(caug0.shape), const(m0.shape), const(ut0.shape),
    ]
    out_shape = (
        jax.ShapeDtypeStruct((nb, t, D_MODEL), _F32),
        jax.ShapeDtypeStruct((nb, M_HEADS, HEAD_DIM, HEAD_DIM), _F32),
        jax.ShapeDtypeStruct((nb, M_HEADS, HEAD_DIM), _F32),
        jax.ShapeDtypeStruct((nb, 1, LANES), _F32),
        jax.ShapeDtypeStruct((nb, CONV_K - 1, CONV_WIDTH), _F32),
    )
    out_specs = (
        pl.BlockSpec((1, tb, D_MODEL), lambda b, i: (b, i, 0)),
        pl.BlockSpec((1, M_HEADS, HEAD_DIM, HEAD_DIM), lambda b, i: (b, 0, 0, 0)),
        pl.BlockSpec((1, M_HEADS, HEAD_DIM), lambda b, i: (b, 0, 0)),
        pl.BlockSpec((1, 1, LANES), lambda b, i: (b, 0, 0)),
        pl.BlockSpec((1, CONV_K - 1, CONV_WIDTH), lambda b, i: (b, 0, 0)),
    )
    return pl.pallas_call(
        functools.partial(_prompt_body, tb=tb),
        grid=(nb, nt),
        in_specs=in_specs,
        out_specs=out_specs,
        out_shape=out_shape,
        scratch_shapes=_sequence_scratch(tb),
        compiler_params=pltpu.CompilerParams(
            dimension_semantics=("arbitrary", "arbitrary"), vmem_limit_bytes=56 * 1024 * 1024),
        name="prompt_layer",
    )(x, w3, wg, wo, ng, fgain, mhg, cw, big, bfg, caug0, m0, ut0)


def _sample_body(x_ref, w3_ref, wg_ref, wo_ref, ng_ref, fgain_ref, mhg_ref, cw_ref, big_ref, bfg_ref,
                 sconv_ref, m_ref, n_ref, c_ref,
                 y_ref, conv_out_ref, c_out_ref, n_out_ref, m_out_ref,
                 p_scr, ni_scr, sc_scr, ycat_scr):
    @pl.when(pl.program_id(0) == 0)
    def _():
        sc_scr[...] = jnp.zeros_like(sc_scr)

    x = x_ref[...]
    xn = _rms_rows(x, ng_ref[...]).astype(_BF)
    for t in range(N_TILES):
        p_scr[:, _tile_cols(t)] = _dot(xn, w3_ref[t])
    g_raw = _dot_nt(xn, wg_ref[...].astype(_BF))

    row = lax.broadcasted_iota(jnp.int32, (CHUNK, CHUNK), 0)
    col = lax.broadcasted_iota(jnp.int32, (CHUNK, CHUNK), 1)
    pos = row % DEC_SEQ
    grp = row // DEC_SEQ
    same_seq = grp == (col // DEC_SEQ)
    causal = same_seq & (col <= row)
    tril = causal.astype(_F32)
    to_last = (same_seq & (col % DEC_SEQ == DEC_SEQ - 1)).astype(_F32)
    grow = lax.broadcasted_iota(jnp.int32, (SEQ_PER_STEP, CHUNK), 0)
    gcol = lax.broadcasted_iota(jnp.int32, (SEQ_PER_STEP, CHUNK), 1)
    pick_last = (gcol == grow * DEC_SEQ + (DEC_SEQ - 1)).astype(_F32)
    seq_sum = (gcol // DEC_SEQ == grow).astype(_BF)
    erow = lax.broadcasted_iota(jnp.int32, (CHUNK, SEQ_PER_STEP), 0)
    ecol = lax.broadcasted_iota(jnp.int32, (CHUNK, SEQ_PER_STEP), 1)
    expand = (erow // DEC_SEQ == ecol).astype(_F32)
    ones_blk = jnp.ones((CHUNK, LANES), _BF)

    u = p_scr[:, OFF_GC:OFF_GC + CONV_WIDTH] * p_scr[:, OFF_HC:OFF_HC + CONV_WIDTH]
    for g in range(SEQ_PER_STEP):
        sc_scr[g * DEC_SEQ:g * DEC_SEQ + CONV_K - 1, :] = sconv_ref[g]
        conv_out_ref[g] = u[(g + 1) * DEC_SEQ - (CONV_K - 1):(g + 1) * DEC_SEQ, :]
    sc = sc_scr[...]
    posw = lax.broadcasted_iota(jnp.int32, (CHUNK, CONV_WIDTH), 0) % DEC_SEQ
    u_m1 = jnp.where(posw == 0, pltpu.roll(sc, CHUNK - 1, 0), pltpu.roll(u, 1, 0))
    u_m2 = jnp.where(posw < 2, sc, pltpu.roll(u, 2, 0))
    y_conv = _conv_out(p_scr[:, OFF_GB:OFF_GB + CONV_WIDTH], p_scr[:, OFF_ZC:OFF_ZC + CONV_WIDTH],
                       u, u_m1, u_m2, cw_ref)
    ycat_scr[:, 0:CONV_WIDTH] = y_conv.astype(_BF)

    ig, lf = _gate_columns(g_raw, big_ref[...], bfg_ref[...])
    b = _dot_exact(tril, lf)
    a = ig - b
    m_prev = _dot_exact(expand, m_ref[...])
    big_m = jnp.maximum(m_prev, _cummax_rows(a, pos, DEC_SEQ))
    m_t = b + big_m
    inter = jnp.exp(m_prev - big_m)
    emt = jnp.exp(-m_t)
    m_last = _dot_exact(to_last, big_m)
    w = jnp.exp(a - m_last)
    decay = jnp.exp(m_prev - m_last)
    m_out_ref[...] = _dot_exact(pick_last, m_t)
    dec16 = _dot_exact(pick_last, decay)
    a_t = a.T

    for h in range(M_HEADS):
        hs = h * HEAD_DIM
        q = p_scr[:, OFF_Q + hs:OFF_Q + hs + HEAD_DIM]
        k = p_scr[:, OFF_K + hs:OFF_K + hs + HEAD_DIM] * K_SCALE
        v = p_scr[:, OFF_V + hs:OFF_V + hs + HEAD_DIM]
        q_bf = q.astype(_BF)
        a_b = jnp.broadcast_to(a_t[h:h + 1, :], (CHUNK, CHUNK))
        m_b = jnp.broadcast_to(big_m[:, h:h + 1], (CHUNK, CHUNK))
        d = jnp.exp(jnp.where(causal, a_b - m_b, NEG_INF))
        s = _dot_nt(q_bf, k.astype(_BF)) * d
        sv = _dot(s.astype(_BF), jnp.concatenate([v.astype(_BF), ones_blk], axis=1))
        w_b = jnp.broadcast_to(w[:, h:h + 1], (CHUNK, LANES))
        wv = w_b * v
        k_t_bf = k.T.astype(_BF)
        dec_b = jnp.broadcast_to(decay[:, h:h + 1], (CHUNK, HEAD_DIM))

        for g in range(SEQ_PER_STEP):
            seq_rows = slice(g * DEC_SEQ, (g + 1) * DEC_SEQ)
            c_old = c_ref[g, h]
            ni_scr[seq_rows, hs:hs + HEAD_DIM] = _dot(q[seq_rows, :].astype(_BF), c_old.astype(_BF))
            upd = _dot(k_t_bf, jnp.where(grp == g, wv, 0.0).astype(_BF))
            dec_g = jnp.broadcast_to(dec_b[g * DEC_SEQ:g * DEC_SEQ + 1, :], (HEAD_DIM, HEAD_DIM))
            c_out_ref[g, h] = dec_g * c_old + upd

        n_old = n_ref[:, h, :]
        num_inter = ni_scr[:, hs:hs + HEAD_DIM]
        den_inter = jnp.sum(q * _dot_exact(expand, n_old), axis=-1, keepdims=True)
        inter_b = jnp.broadcast_to(inter[:, h:h + 1], (CHUNK, LANES))
        num = sv[:, :LANES] + inter_b * num_inter
        den = sv[:, LANES:] + inter_b * den_inter
        emt_b = jnp.broadcast_to(emt[:, h:h + 1], (CHUNK, LANES))
        ni_scr[:, hs:hs + HEAD_DIM] = _head_norm(num, den, emt_b, mhg_ref[:, hs:hs + HEAD_DIM])
        n_upd = _dot(seq_sum, (w_b * k).astype(_BF))
        dec16_b = jnp.broadcast_to(dec16[:, h:h + 1], (SEQ_PER_STEP, HEAD_DIM))
        n_out_ref[:, h, :] = dec16_b * n_old + n_upd

    ycat_scr[:, CONV_WIDTH:CONV_WIDTH + M_WIDTH] = _gated_heads(ni_scr[...], p_scr).astype(_BF)
    out = _dot(ycat_scr[...], wo_ref[...]) + x
    y_ref[...] = _rms_rows(out, fgain_ref[...])


def _sample_call(xs, w3, wo, small, fgain, state_conv, m_pad, n_state, c_state):
    wg, ng, mhg, cw, big, bfg = small
    rows = xs.shape[0]
    nseq = c_state.shape[0]
    steps = rows // CHUNK

    def const(shape):
        nd = len(shape)
        return pl.BlockSpec(shape, lambda i: (0,) * nd)

    def seq_spec(*tail):
        nd = len(tail)
        return pl.BlockSpec((SEQ_PER_STEP,) + tail, lambda i: (i,) + (0,) * nd)

    x_spec = pl.BlockSpec((CHUNK, D_MODEL), lambda i: (i, 0))
    conv_spec = seq_spec(CONV_K - 1, CONV_WIDTH)
    c_spec = seq_spec(M_HEADS, HEAD_DIM, HEAD_DIM)
    n_spec = seq_spec(M_HEADS, HEAD_DIM)
    m_spec = seq_spec(LANES)
    in_specs = [
        x_spec,
        const(w3.shape), const(wg.shape), const(wo.shape), const(ng.shape), const(fgain.shape),
        const(mhg.shape), const(cw.shape), const(big.shape), const(bfg.shape),
        conv_spec, m_spec, n_spec, c_spec,
    ]
    out_shape = (
        jax.ShapeDtypeStruct((rows, D_MODEL), _F32),
        jax.ShapeDtypeStruct(state_conv.shape, _F32),
        jax.ShapeDtypeStruct(c_state.shape, _F32),
        jax.ShapeDtypeStruct(n_state.shape, _F32),
        jax.ShapeDtypeStruct((nseq, LANES), _F32),
    )
    out_specs = (x_spec, conv_spec, c_spec, n_spec, m_spec)
    scratch = [
        pltpu.VMEM((CHUNK, MAIN_WIDTH), _F32),
        pltpu.VMEM((CHUNK, M_WIDTH), _F32),
        pltpu.VMEM((CHUNK, CONV_WIDTH), _F32),
        pltpu.VMEM((CHUNK, D_MODEL), _BF),
    ]
    return pl.pallas_call(
        _sample_body,
        grid=(steps,),
        in_specs=in_specs,
        out_specs=out_specs,
        out_shape=out_shape,
        scratch_shapes=scratch,
        compiler_params=pltpu.CompilerParams(
            dimension_semantics=("arbitrary",), vmem_limit_bytes=56 * 1024 * 1024),
        name="sample_layer",
    )(xs, w3, wg, wo, ng, fgain, mhg, cw, big, bfg, state_conv, m_pad, n_state, c_state)


def _pad_lanes(v):
    return jnp.pad(v.astype(_F32), (0, LANES - v.shape[0])).reshape(1, LANES)


def kernel(x_prompt, x_sample, state_conv, state_mlstm_c, state_mlstm_n, state_mlstm_m, meta_tokens, norm_gain,
           w_in, conv_w, b_igate, b_fgate, mh_norm_gain, w_out, final_norm_gain):
    nseq, dec_seq, _ = x_sample.shape
    assert dec_seq == DEC_SEQ and meta_tokens.shape[0] == N_META

    w_t = w_in.T
    wg = jnp.pad(w_t[MAIN_WIDTH:], ((0, LANES - 2 * M_HEADS), (0, 0)))
    small = (wg, norm_gain.reshape(1, D_MODEL), mh_norm_gain.reshape(1, M_WIDTH), conv_w,
             _pad_lanes(b_igate), _pad_lanes(b_fgate))
    fgain = final_norm_gain.reshape(1, D_MODEL)
    wo = w_out.astype(_BF)

    x_meta = jnp.concatenate([jnp.zeros((CHUNK - N_META, D_MODEL), _F32), meta_tokens.astype(_F32)], axis=0)
    w3, caug_meta, m_meta, ut_meta = _prep_call(w_t, x_meta, small)

    y_prompt, c_p, n_p, m_p, conv_p = _prompt_call(
        x_prompt, w3, wo, small, fgain, caug_meta, m_meta, ut_meta, tb=512)
    m_pp = m_p[:, 0, :M_HEADS]

    xs = x_sample.reshape(nseq * DEC_SEQ, D_MODEL)
    m_pad = jnp.pad(state_mlstm_m, ((0, 0), (0, LANES - M_HEADS)))
    ys, conv_s, c_s, n_s, m_s = _sample_call(xs, w3, wo, small, fgain, state_conv, m_pad, state_mlstm_n,
                                             state_mlstm_c)
    y_sample = ys.reshape(nseq, DEC_SEQ, D_MODEL)
    m_ss = m_s[:, :M_HEADS]

    return (y_prompt, y_sample, conv_p, c_p, n_p, m_pp, conv_s, c_s, n_s, m_ss)
```

```python
import functools

import jax
import jax.numpy as jnp
from jax import lax
from jax.experimental import pallas as pl
from jax.experimental.pallas import tpu as pltpu

D_MODEL = 1024
CONV_WIDTH = 512
M_WIDTH = 512
M_HEADS = 4
HEAD_DIM = 128
N_META = 16
CONV_K = 3
EPS = 1e-6
MAIN_WIDTH = 4 * CONV_WIDTH + 5 * M_WIDTH
LANES = 128
SUBLANES = 8
MXU_COLS = 256
N_TILES = MAIN_WIDTH // MXU_COLS
CHUNK = 128
DEC_SEQ = 8
SEQ_PER_STEP = CHUNK // DEC_SEQ
NEG_INF = float("-inf")
K_SCALE = HEAD_DIM ** -0.5

OFF_GB, OFF_GC, OFF_HC, OFF_ZC = 0, 512, 1024, 1536
OFF_Q, OFF_K, OFF_V, OFF_O, OFF_ZM = 2048, 2560, 3072, 3584, 4096
QKV_TILES = tuple(range(OFF_Q // MXU_COLS, OFF_O // MXU_COLS))
REST_TILES = tuple(t for t in range(N_TILES) if t not in QKV_TILES)

_HI = lax.Precision.HIGHEST
_BF = jnp.bfloat16
_F32 = jnp.float32


def _dot(a, b):
    return jnp.dot(a, b, preferred_element_type=_F32)


def _dot_exact(a, b):
    return jnp.dot(a, b, precision=_HI, preferred_element_type=_F32)


def _dot_nt(a, b):
    return lax.dot_general(a, b, (((1,), (1,)), ((), ())), preferred_element_type=_F32)


def _rms_rows(x, gain_row):
    return x * lax.rsqrt(jnp.mean(x * x, axis=-1, keepdims=True) + EPS) * gain_row


def _sigmoid(x):
    return 1.0 / (1.0 + jnp.exp(-x))


def _silu(x):
    return x * _sigmoid(x)


def _tile_cols(t):
    return slice(t * MXU_COLS, (t + 1) * MXU_COLS)


def _cummax_rows(x, pos, period):
    s = 1
    while s < period:
        shifted = pltpu.roll(x, s, 0)
        x = jnp.maximum(x, jnp.where(pos >= s, shifted, NEG_INF))
        s *= 2
    return x


def _cumsum_rows(x, pos, period):
    s = 1
    while s < period:
        x = x + jnp.where(pos >= s, pltpu.roll(x, s, 0), 0.0)
        s *= 2
    return x


def _gate_columns(g_raw, big_row, bfg_row):
    ig = g_raw + big_row
    lf = jax.nn.log_sigmoid(pltpu.roll(g_raw, LANES - M_HEADS, 1) + bfg_row)
    return ig, lf


def _head_norm(num, den, emt_b, gain_row):
    hh = num / jnp.maximum(jnp.abs(den), emt_b)
    return hh * lax.rsqrt(jnp.mean(hh * hh, axis=-1, keepdims=True) + EPS) * gain_row


def _conv_out(gb, zc, u, u_m1, u_m2, cw_ref):
    yc = cw_ref[0:1, :] * u_m2 + cw_ref[1:2, :] * u_m1 + cw_ref[2:3, :] * u
    return gb * yc * _silu(zc)


def _gated_heads(hn, p_scr):
    return hn * _sigmoid(p_scr[:, OFF_O:OFF_O + M_WIDTH]) * _silu(p_scr[:, OFF_ZM:OFF_ZM + M_WIDTH])


def _sequence_block(xn, project_tile, p_scr, g_scr, hn_scr, ycat_scr, ubuf, caug, m_scr,
                    wg_ref, mhg_ref, cw_ref, big_ref, bfg_ref, *, tb, n_pad):
    g_scr[...] = _dot(xn, wg_ref[...])
    late_tiles = []
    if project_tile is not None:
        for t in QKV_TILES:
            project_tile(t)
        early = 4
        for t in REST_TILES[:early]:
            project_tile(t)
        late_tiles = list(REST_TILES[early:])

    row = lax.broadcasted_iota(jnp.int32, (CHUNK, CHUNK), 0)
    col = lax.broadcasted_iota(jnp.int32, (CHUNK, CHUNK), 1)
    causal = col <= row
    ones_blk = jnp.ones((CHUNK, LANES), _BF)

    n_chunks = tb // CHUNK
    gates = []
    m_prev = m_scr[...]
    for c in range(n_chunks):
        rows = slice(c * CHUNK, (c + 1) * CHUNK)
        ig, lf = _gate_columns(g_scr[rows, :], big_ref[...], bfg_ref[...])
        if n_pad:
            ig = jnp.where(row >= n_pad, ig, NEG_INF)
            lf = jnp.where(row >= n_pad, lf, 0.0)
        b = _cumsum_rows(lf, row, CHUNK)
        a = ig - b
        big_m = jnp.maximum(m_prev, _cummax_rows(a, row, CHUNK))
        m_t = b + big_m
        m_last = big_m[CHUNK - 1:CHUNK, :]
        gates.append(dict(a_t=a.T, big_m=big_m, inter=jnp.exp(m_prev - big_m), emt=jnp.exp(-m_t),
                          w=jnp.exp(a - m_last), decay=jnp.exp(m_prev - m_last)))
        m_prev = m_t[CHUNK - 1:CHUNK, :]
    m_scr[...] = m_prev

    def stage_a(c, h):
        rows = slice(c * CHUNK, (c + 1) * CHUNK)
        hs = h * HEAD_DIM
        g = gates[c]
        q = p_scr[rows, OFF_Q + hs:OFF_Q + hs + HEAD_DIM]
        k = p_scr[rows, OFF_K + hs:OFF_K + hs + HEAD_DIM] * K_SCALE
        a_b = jnp.broadcast_to(g["a_t"][h:h + 1, :], (CHUNK, CHUNK))
        m_b = jnp.broadcast_to(g["big_m"][:, h:h + 1], (CHUNK, CHUNK))
        d = jnp.exp(jnp.where(causal, a_b - m_b, NEG_INF))
        q_inter = (jnp.broadcast_to(g["inter"][:, h:h + 1], (CHUNK, HEAD_DIM)) * q).astype(_BF)
        return dict(qk=_dot_nt(q.astype(_BF), k.astype(_BF)), d=d, q_inter=q_inter, k=k)

    def stage_b(c, h, st):
        rows = slice(c * CHUNK, (c + 1) * CHUNK)
        hs = h * HEAD_DIM
        g = gates[c]
        v = p_scr[rows, OFF_V + hs:OFF_V + hs + HEAD_DIM]
        s = st["qk"] * st["d"]
        lhs = jnp.concatenate([s.astype(_BF), st["q_inter"]], axis=1)
        rhs = jnp.concatenate([jnp.concatenate([v.astype(_BF), ones_blk], axis=1), caug[h].astype(_BF)], axis=0)
        tot = _dot(lhs, rhs)
        emt_b = jnp.broadcast_to(g["emt"][:, h:h + 1], (CHUNK, LANES))
        hn_scr[rows, hs:hs + HEAD_DIM] = _head_norm(tot[:, :LANES], tot[:, LANES:], emt_b,
                                                    mhg_ref[:, hs:hs + HEAD_DIM])
        w_b = jnp.broadcast_to(g["w"][:, h:h + 1], (CHUNK, LANES))
        wv = jnp.concatenate([(w_b * v).astype(_BF), w_b.astype(_BF)], axis=1)
        upd = _dot(st["k"].T.astype(_BF), wv)
        decay_b = jnp.broadcast_to(g["decay"][:, h:h + 1], (HEAD_DIM, 2 * LANES))
        caug[h] = decay_b * caug[h] + upd

    units = [(c, h) for c in range(n_chunks) for h in range(M_HEADS)]
    lag = min(2, len(units))
    pending = {}
    for t in range(max(len(units) + lag, len(late_tiles))):
        if t < len(units):
            pending[t] = stage_a(*units[t])
        if t < len(late_tiles):
            project_tile(late_tiles[t])
        if 0 <= t - lag < len(units):
            stage_b(*units[t - lag], pending.pop(t - lag))

    u = p_scr[:, OFF_GC:OFF_GC + CONV_WIDTH] * p_scr[:, OFF_HC:OFF_HC + CONV_WIDTH]
    ubuf[SUBLANES:SUBLANES + tb, :] = u
    u_m1 = ubuf[SUBLANES - 1:SUBLANES - 1 + tb, :]
    u_m2 = ubuf[SUBLANES - 2:SUBLANES - 2 + tb, :]
    y_conv = _conv_out(p_scr[:, OFF_GB:OFF_GB + CONV_WIDTH], p_scr[:, OFF_ZC:OFF_ZC + CONV_WIDTH],
                       u, u_m1, u_m2, cw_ref)
    ycat_scr[:, 0:CONV_WIDTH] = y_conv.astype(_BF)
    ubuf[0:SUBLANES, :] = ubuf[tb:tb + SUBLANES, :]
    ycat_scr[:, CONV_WIDTH:CONV_WIDTH + M_WIDTH] = _gated_heads(hn_scr[...], p_scr).astype(_BF)


def _sequence_scratch(tb):
    return [
        pltpu.VMEM((tb, MAIN_WIDTH), _F32),
        pltpu.VMEM((tb, LANES), _F32),
        pltpu.VMEM((tb, M_WIDTH), _F32),
        pltpu.VMEM((tb, D_MODEL), _BF),
        pltpu.VMEM((tb + 2 * SUBLANES, CONV_WIDTH), _F32),
        pltpu.VMEM((M_HEADS, HEAD_DIM, 2 * LANES), _F32),
        pltpu.VMEM((1, LANES), _F32),
    ]


def _prep_body(wt_ref, x_ref, wgt_ref, ng_ref, mhg_ref, cw_ref, big_ref, bfg_ref,
               w3_ref, wg_ref, caug_out_ref, m_out_ref, ut_out_ref,
               xn_scr, p3_scr, p_scr, g_scr, hn_scr, ycat_scr, ubuf, caug, m_scr):
    t = pl.program_id(0)

    @pl.when(t == 0)
    def _():
        xn_scr[...] = _rms_rows(x_ref[...], ng_ref[...]).astype(_BF)
        wg_ref[...] = wgt_ref[...].T.astype(_BF)

    tile = wt_ref[...].T.astype(_BF)
    w3_ref[0] = tile
    p3_scr[t] = _dot(xn_scr[...], tile)

    @pl.when(t == N_TILES - 1)
    def _():
        for j in range(N_TILES):
            p_scr[:, _tile_cols(j)] = p3_scr[j]
        caug[...] = jnp.zeros_like(caug)
        m_scr[...] = jnp.zeros_like(m_scr)
        ubuf[0:SUBLANES, :] = jnp.zeros((SUBLANES, CONV_WIDTH), _F32)
        _sequence_block(xn_scr[...], None, p_scr, g_scr, hn_scr, ycat_scr, ubuf, caug, m_scr,
                        wg_ref, mhg_ref, cw_ref, big_ref, bfg_ref, tb=CHUNK, n_pad=CHUNK - N_META)
        caug_out_ref[...] = caug[...]
        m_out_ref[...] = m_scr[...]
        ut_out_ref[...] = ubuf[0:SUBLANES, :]


def _prep_call(w_t, x_meta, wgt, small):
    ng, mhg, cw, big, bfg = small

    def const(shape):
        nd = len(shape)
        return pl.BlockSpec(shape, lambda t: (0,) * nd)

    in_specs = [
        pl.BlockSpec((MXU_COLS, D_MODEL), lambda t: (t, 0)),
        const(x_meta.shape), const(wgt.shape), const(ng.shape), const(mhg.shape), const(cw.shape),
        const(big.shape), const(bfg.shape),
    ]
    out_shape = (
        jax.ShapeDtypeStruct((N_TILES, D_MODEL, MXU_COLS), _BF),
        jax.ShapeDtypeStruct((D_MODEL, LANES), _BF),
        jax.ShapeDtypeStruct((M_HEADS, HEAD_DIM, 2 * LANES), _F32),
        jax.ShapeDtypeStruct((1, LANES), _F32),
        jax.ShapeDtypeStruct((SUBLANES, CONV_WIDTH), _F32),
    )
    out_specs = (pl.BlockSpec((1, D_MODEL, MXU_COLS), lambda t: (t, 0, 0)),) + tuple(
        const(s.shape) for s in out_shape[1:])
    scratch = [
        pltpu.VMEM((CHUNK, D_MODEL), _BF),
        pltpu.VMEM((N_TILES, CHUNK, MXU_COLS), _F32),
    ] + _sequence_scratch(CHUNK)
    return pl.pallas_call(
        _prep_body,
        grid=(N_TILES,),
        in_specs=in_specs,
        out_specs=out_specs,
        out_shape=out_shape,
        scratch_shapes=scratch,
        compiler_params=pltpu.CompilerParams(
            dimension_semantics=("arbitrary",), vmem_limit_bytes=32 * 1024 * 1024),
        name="prep_meta",
    )(w_t, x_meta, wgt, ng, mhg, cw, big, bfg)


def _prompt_body(x_ref, w3_ref, wg_ref, wo_ref, ng_ref, fgain_ref, mhg_ref, cw_ref, big_ref, bfg_ref,
                 caug0_ref, m0_ref, ut0_ref,
                 y_ref, c_out_ref, n_out_ref, m_out_ref, conv_out_ref,
                 p_scr, g_scr, hn_scr, ycat_scr, ubuf, caug, m_scr, *, tb):
    i = pl.program_id(1)
    last = pl.num_programs(1) - 1

    @pl.when(i == 0)
    def _():
        caug[...] = caug0_ref[...]
        m_scr[...] = m0_ref[...]
        ubuf[0:SUBLANES, :] = ut0_ref[...]

    x = x_ref[0]
    xn = _rms_rows(x, ng_ref[...]).astype(_BF)

    def project_tile(t):
        p_scr[:, _tile_cols(t)] = _dot(xn, w3_ref[t])

    _sequence_block(xn, project_tile, p_scr, g_scr, hn_scr, ycat_scr, ubuf, caug, m_scr,
                    wg_ref, mhg_ref, cw_ref, big_ref, bfg_ref, tb=tb, n_pad=0)

    out = _dot(ycat_scr[...], wo_ref[...]) + x
    y_ref[0] = _rms_rows(out, fgain_ref[...])

    @pl.when(i == last)
    def _():
        m_out_ref[0] = m_scr[...]
        conv_out_ref[0] = ubuf[SUBLANES - (CONV_K - 1):SUBLANES, :]
        for h in range(M_HEADS):
            c_out_ref[0, h] = caug[h, :, 0:HEAD_DIM]
            n_out_ref[0, h:h + 1, :] = caug[h, :, LANES:2 * LANES].T[0:1, :]


def _prompt_call(x, w3, wg, wo, small, fgain, caug0, m0, ut0, *, tb):
    ng, mhg, cw, big, bfg = small
    nb, t, _ = x.shape
    nt = t // tb

    def const(shape):
        nd = len(shape)
        return pl.BlockSpec(shape, lambda b, i: (0,) * nd)

    in_specs = [
        pl.BlockSpec((1, tb, D_MODEL), lambda b, i: (b, i, 0)),
        const(w3.shape), const(wg.shape), const(wo.shape), const(ng.shape), const(fgain.shape),
        const(mhg.shape), const(cw.shape), const(big.shape), const(bfg.shape),
        const(caug0.shape), const(m0.shape), const(ut0.shape),
    ]
    out_shape = (
        jax.ShapeDtypeStruct((nb, t, D_MODEL), _F32),
        jax.ShapeDtypeStruct((nb, M_HEADS, HEAD_DIM, HEAD_DIM), _F32),
        jax.ShapeDtypeStruct((nb, M_HEADS, HEAD_DIM), _F32),
        jax.ShapeDtypeStruct((nb, 1, LANES), _F32),
        jax.ShapeDtypeStruct((nb, CONV_K - 1, CONV_WIDTH), _F32),
    )
    out_specs = (
        pl.BlockSpec((1, tb, D_MODEL), lambda b, i: (b, i, 0)),
        pl.BlockSpec((1, M_HEADS, HEAD_DIM, HEAD_DIM), lambda b, i: (b, 0, 0, 0)),
        pl.BlockSpec((1, M_HEADS, HEAD_DIM), lambda b, i: (b, 0, 0)),
        pl.BlockSpec((1, 1, LANES), lambda b, i: (b, 0, 0)),
        pl.BlockSpec((1, CONV_K - 1, CONV_WIDTH), lambda b, i: (b, 0, 0)),
    )
    return pl.pallas_call(
        functools.partial(_prompt_body, tb=tb),
        grid=(nb, nt),
        in_specs=in_specs,
        out_specs=out_specs,
        out_shape=out_shape,
        scratch_shapes=_sequence_scratch(tb),
        compiler_params=pltpu.CompilerParams(
            dimension_semantics=("arbitrary", "arbitrary"), vmem_limit_bytes=56 * 1024 * 1024),
        name="prompt_layer",
    )(x, w3, wg, wo, ng, fgain, mhg, cw, big, bfg, caug0, m0, ut0)


def _sample_body(x_ref, w3_ref, wg_ref, wo_ref, ng_ref, fgain_ref, mhg_ref, cw_ref, big_ref, bfg_ref,
                 sconv_ref, m_ref, n_ref, c_ref,
                 y_ref, conv_out_ref, c_out_ref, n_out_ref, m_out_ref,
                 p_scr, ni_scr, sc_scr, ycat_scr):
    @pl.when(pl.program_id(0) == 0)
    def _():
        sc_scr[...] = jnp.zeros_like(sc_scr)

    x = x_ref[...]
    xn = _rms_rows(x, ng_ref[...]).astype(_BF)
    for t in range(N_TILES):
        p_scr[:, _tile_cols(t)] = _dot(xn, w3_ref[t])
    g_raw = _dot(xn, wg_ref[...])

    row = lax.broadcasted_iota(jnp.int32, (CHUNK, CHUNK), 0)
    col = lax.broadcasted_iota(jnp.int32, (CHUNK, CHUNK), 1)
    pos = row % DEC_SEQ
    grp = row // DEC_SEQ
    causal = (grp == (col // DEC_SEQ)) & (col <= row)
    grow = lax.broadcasted_iota(jnp.int32, (SEQ_PER_STEP, CHUNK), 0)
    gcol = lax.broadcasted_iota(jnp.int32, (SEQ_PER_STEP, CHUNK), 1)
    pick_last = (gcol == grow * DEC_SEQ + (DEC_SEQ - 1)).astype(_F32)
    seq_sum = (gcol // DEC_SEQ == grow).astype(_BF)
    erow = lax.broadcasted_iota(jnp.int32, (CHUNK, SEQ_PER_STEP), 0)
    ecol = lax.broadcasted_iota(jnp.int32, (CHUNK, SEQ_PER_STEP), 1)
    own_seq = erow // DEC_SEQ == ecol
    ones_blk = jnp.ones((CHUNK, LANES), _BF)

    def rows_from_seqs(v16):
        return jnp.broadcast_to(v16[:, None, :], (SEQ_PER_STEP, DEC_SEQ, LANES)).reshape(CHUNK, LANES)

    def last_row_of_seq(v):
        v3 = v.reshape(SEQ_PER_STEP, DEC_SEQ, LANES)
        return jnp.broadcast_to(v3[:, DEC_SEQ - 1:DEC_SEQ, :], v3.shape).reshape(CHUNK, LANES)

    u = p_scr[:, OFF_GC:OFF_GC + CONV_WIDTH] * p_scr[:, OFF_HC:OFF_HC + CONV_WIDTH]
    for g in range(SEQ_PER_STEP):
        sc_scr[g * DEC_SEQ:g * DEC_SEQ + CONV_K - 1, :] = sconv_ref[g]
        conv_out_ref[g] = u[(g + 1) * DEC_SEQ - (CONV_K - 1):(g + 1) * DEC_SEQ, :]
    sc = sc_scr[...]
    posw = lax.broadcasted_iota(jnp.int32, (CHUNK, CONV_WIDTH), 0) % DEC_SEQ
    u_m1 = jnp.where(posw == 0, pltpu.roll(sc, CHUNK - 1, 0), pltpu.roll(u, 1, 0))
    u_m2 = jnp.where(posw < 2, sc, pltpu.roll(u, 2, 0))
    y_conv = _conv_out(p_scr[:, OFF_GB:OFF_GB + CONV_WIDTH], p_scr[:, OFF_ZC:OFF_ZC + CONV_WIDTH],
                       u, u_m1, u_m2, cw_ref)
    ycat_scr[:, 0:CONV_WIDTH] = y_conv.astype(_BF)

    ig, lf = _gate_columns(g_raw, big_ref[...], bfg_ref[...])
    b = _cumsum_rows(lf, pos, DEC_SEQ)
    a = ig - b
    m_prev = rows_from_seqs(m_ref[...])
    big_m = jnp.maximum(m_prev, _cummax_rows(a, pos, DEC_SEQ))
    m_t = b + big_m
    inter = jnp.exp(m_prev - big_m)
    emt = jnp.exp(-m_t)
    m_last = last_row_of_seq(big_m)
    w = jnp.exp(a - m_last)
    decay = jnp.exp(m_prev - m_last)
    m_out_ref[...] = _dot_exact(pick_last, m_t)
    dec16 = _dot_exact(pick_last, decay)
    a_t = a.T

    heads = []
    for h in range(M_HEADS):
        hs = h * HEAD_DIM
        q = p_scr[:, OFF_Q + hs:OFF_Q + hs + HEAD_DIM]
        k = p_scr[:, OFF_K + hs:OFF_K + hs + HEAD_DIM] * K_SCALE
        v = p_scr[:, OFF_V + hs:OFF_V + hs + HEAD_DIM]
        q_bf = q.astype(_BF)
        a_b = jnp.broadcast_to(a_t[h:h + 1, :], (CHUNK, CHUNK))
        m_b = jnp.broadcast_to(big_m[:, h:h + 1], (CHUNK, CHUNK))
        d = jnp.exp(jnp.where(causal, a_b - m_b, NEG_INF))
        w_b = jnp.broadcast_to(w[:, h:h + 1], (CHUNK, LANES))
        heads.append(dict(q_bf=q_bf, k=k, v=v, d=d, w_b=w_b, qk=_dot_nt(q_bf, k.astype(_BF))))

    for h in range(M_HEADS):
        hs = h * HEAD_DIM
        st = heads[h]
        wv = st["w_b"] * st["v"]
        k_t_bf = st["k"].T.astype(_BF)
        dec_b = jnp.broadcast_to(decay[:, h:h + 1], (CHUNK, HEAD_DIM))

        for g in range(0, SEQ_PER_STEP, 2):
            pair_rows = slice(g * DEC_SEQ, (g + 2) * DEC_SEQ)
            c0, c1 = c_ref[g, h], c_ref[g + 1, h]
            qc = _dot(st["q_bf"][pair_rows, :], jnp.concatenate([c0.astype(_BF), c1.astype(_BF)], axis=1))
            ni_scr[g * DEC_SEQ:(g + 1) * DEC_SEQ, hs:hs + HEAD_DIM] = qc[0:DEC_SEQ, 0:HEAD_DIM]
            ni_scr[(g + 1) * DEC_SEQ:(g + 2) * DEC_SEQ, hs:hs + HEAD_DIM] = qc[DEC_SEQ:, HEAD_DIM:]
            wv_pair = jnp.concatenate([jnp.where(grp == g, wv, 0.0).astype(_BF),
                                       jnp.where(grp == g + 1, wv, 0.0).astype(_BF)], axis=1)
            upd = _dot(k_t_bf, wv_pair)
            dec0 = jnp.broadcast_to(dec_b[g * DEC_SEQ:g * DEC_SEQ + 1, :], (HEAD_DIM, HEAD_DIM))
            dec1 = jnp.broadcast_to(dec_b[(g + 1) * DEC_SEQ:(g + 1) * DEC_SEQ + 1, :], (HEAD_DIM, HEAD_DIM))
            c_out_ref[g, h] = dec0 * c0 + upd[:, 0:HEAD_DIM]
            c_out_ref[g + 1, h] = dec1 * c1 + upd[:, HEAD_DIM:]

        s = st["qk"] * st["d"]
        sv = _dot(s.astype(_BF), jnp.concatenate([st["v"].astype(_BF), ones_blk], axis=1))
        n_old = n_ref[:, h, :]
        num_inter = ni_scr[:, hs:hs + HEAD_DIM]
        qn = _dot_nt(st["q_bf"], n_old.astype(_BF))
        den_inter = jnp.sum(jnp.where(own_seq, qn, 0.0), axis=-1, keepdims=True)
        inter_b = jnp.broadcast_to(inter[:, h:h + 1], (CHUNK, LANES))
        num = sv[:, :LANES] + inter_b * num_inter
        den = sv[:, LANES:] + inter_b * den_inter
        emt_b = jnp.broadcast_to(emt[:, h:h + 1], (CHUNK, LANES))
        ni_scr[:, hs:hs + HEAD_DIM] = _head_norm(num, den, emt_b, mhg_ref[:, hs:hs + HEAD_DIM])
        n_upd = _dot(seq_sum, (st["w_b"] * st["k"]).astype(_BF))
        dec16_b = jnp.broadcast_to(dec16[:, h:h + 1], (SEQ_PER_STEP, HEAD_DIM))
        n_out_ref[:, h, :] = dec16_b * n_old + n_upd

    ycat_scr[:, CONV_WIDTH:CONV_WIDTH + M_WIDTH] = _gated_heads(ni_scr[...], p_scr).astype(_BF)
    out = _dot(ycat_scr[...], wo_ref[...]) + x
    y_ref[...] = _rms_rows(out, fgain_ref[...])


def _sample_call(xs, w3, wg, wo, small, fgain, state_conv, m_pad, n_state, c_state):
    ng, mhg, cw, big, bfg = small
    rows = xs.shape[0]
    nseq = c_state.shape[0]
    steps = rows // CHUNK

    def const(shape):
        nd = len(shape)
        return pl.BlockSpec(shape, lambda i: (0,) * nd)

    def seq_spec(*tail):
        nd = len(tail)
        return pl.BlockSpec((SEQ_PER_STEP,) + tail, lambda i: (i,) + (0,) * nd)

    x_spec = pl.BlockSpec((CHUNK, D_MODEL), lambda i: (i, 0))
    conv_spec = seq_spec(CONV_K - 1, CONV_WIDTH)
    c_spec = seq_spec(M_HEADS, HEAD_DIM, HEAD_DIM)
    n_spec = seq_spec(M_HEADS, HEAD_DIM)
    m_spec = seq_spec(LANES)
    in_specs = [
        x_spec,
        const(w3.shape), const(wg.shape), const(wo.shape), const(ng.shape), const(fgain.shape),
        const(mhg.shape), const(cw.shape), const(big.shape), const(bfg.shape),
        conv_spec, m_spec, n_spec, c_spec,
    ]
    out_shape = (
        jax.ShapeDtypeStruct((rows, D_MODEL), _F32),
        jax.ShapeDtypeStruct(state_conv.shape, _F32),
        jax.ShapeDtypeStruct(c_state.shape, _F32),
        jax.ShapeDtypeStruct(n_state.shape, _F32),
        jax.ShapeDtypeStruct((nseq, LANES), _F32),
    )
    out_specs = (x_spec, conv_spec, c_spec, n_spec, m_spec)
    scratch = [
        pltpu.VMEM((CHUNK, MAIN_WIDTH), _F32),
        pltpu.VMEM((CHUNK, M_WIDTH), _F32),
        pltpu.VMEM((CHUNK, CONV_WIDTH), _F32),
        pltpu.VMEM((CHUNK, D_MODEL), _BF),
    ]
    return pl.pallas_call(
        _sample_body,
        grid=(steps,),
        in_specs=in_specs,
        out_specs=out_specs,
        out_shape=out_shape,
        scratch_shapes=scratch,
        compiler_params=pltpu.CompilerParams(
            dimension_semantics=("arbitrary",), vmem_limit_bytes=56 * 1024 * 1024),
        name="sample_layer",
    )(xs, w3, wg, wo, ng, fgain, mhg, cw, big, bfg, state_conv, m_pad, n_state, c_state)


def _pad_lanes(v):
    return jnp.pad(v.astype(_F32), (0, LANES - v.shape[0])).reshape(1, LANES)


def kernel(x_prompt, x_sample, state_conv, state_mlstm_c, state_mlstm_n, state_mlstm_m, meta_tokens, norm_gain,
           w_in, conv_w, b_igate, b_fgate, mh_norm_gain, w_out, final_norm_gain):
    nseq, dec_seq, _ = x_sample.shape
    assert dec_seq == DEC_SEQ and meta_tokens.shape[0] == N_META

    w_t = w_in.T
    wgt = jnp.pad(w_t[MAIN_WIDTH:], ((0, LANES - 2 * M_HEADS), (0, 0)))
    small = (norm_gain.reshape(1, D_MODEL), mh_norm_gain.reshape(1, M_WIDTH), conv_w,
             _pad_lanes(b_igate), _pad_lanes(b_fgate))
    fgain = final_norm_gain.reshape(1, D_MODEL)
    wo = w_out.astype(_BF)

    x_meta = jnp.concatenate([jnp.zeros((CHUNK - N_META, D_MODEL), _F32), meta_tokens.astype(_F32)], axis=0)
    w3, wg, caug_meta, m_meta, ut_meta = _prep_call(w_t, x_meta, wgt, small)

    y_prompt, c_p, n_p, m_p, conv_p = _prompt_call(
        x_prompt, w3, wg, wo, small, fgain, caug_meta, m_meta, ut_meta, tb=512)
    m_pp = m_p[:, 0, :M_HEADS]

    xs = x_sample.reshape(nseq * DEC_SEQ, D_MODEL)
    m_pad = jnp.pad(state_mlstm_m, ((0, 0), (0, LANES - M_HEADS)))
    ys, conv_s, c_s, n_s, m_s = _sample_call(xs, w3, wg, wo, small, fgain, state_conv, m_pad, state_mlstm_n,
                                             state_mlstm_c)
    y_sample = ys.reshape(nseq, DEC_SEQ, D_MODEL)
    m_ss = m_s[:, :M_HEADS]

    return (y_prompt, y_sample, conv_p, c_p, n_p, m_pp, conv_s, c_s, n_s, m_ss)
```

```python
import functools

import jax
import jax.numpy as jnp
from jax import lax
from jax.experimental import pallas as pl
from jax.experimental.pallas import tpu as pltpu

D_MODEL = 1024
CONV_WIDTH = 512
M_WIDTH = 512
M_HEADS = 4
HEAD_DIM = 128
N_META = 16
CONV_K = 3
EPS = 1e-6
MAIN_WIDTH = 4 * CONV_WIDTH + 5 * M_WIDTH
LANES = 128
SUBLANES = 8
MXU_COLS = 256
N_TILES = MAIN_WIDTH // MXU_COLS
CHUNK = 128
DEC_SEQ = 8
SEQ_PER_STEP = CHUNK // DEC_SEQ
NEG_INF = float("-inf")
K_SCALE = HEAD_DIM ** -0.5

OFF_GB, OFF_GC, OFF_HC, OFF_ZC = 0, 512, 1024, 1536
OFF_Q, OFF_K, OFF_V, OFF_O, OFF_ZM = 2048, 2560, 3072, 3584, 4096
QKV_TILES = tuple(range(OFF_Q // MXU_COLS, OFF_O // MXU_COLS))
REST_TILES = tuple(t for t in range(N_TILES) if t not in QKV_TILES)

_HI = lax.Precision.HIGHEST
_BF = jnp.bfloat16
_F32 = jnp.float32


def _dot(a, b):
    return jnp.dot(a, b, preferred_element_type=_F32)


def _dot_exact(a, b):
    return jnp.dot(a, b, precision=_HI, preferred_element_type=_F32)


def _dot_nt(a, b):
    return lax.dot_general(a, b, (((1,), (1,)), ((), ())), preferred_element_type=_F32)


def _rms_rows(x, gain_row):
    return x * lax.rsqrt(jnp.mean(x * x, axis=-1, keepdims=True) + EPS) * gain_row


def _sigmoid(x):
    return 1.0 / (1.0 + jnp.exp(-x))


def _silu(x):
    return x * _sigmoid(x)


def _tile_cols(t):
    return slice(t * MXU_COLS, (t + 1) * MXU_COLS)


def _cummax_rows(x, pos, period):
    s = 1
    while s < period:
        shifted = pltpu.roll(x, s, 0)
        x = jnp.maximum(x, jnp.where(pos >= s, shifted, NEG_INF))
        s *= 2
    return x


def _cumsum_rows(x, pos, period):
    s = 1
    while s < period:
        x = x + jnp.where(pos >= s, pltpu.roll(x, s, 0), 0.0)
        s *= 2
    return x


def _gate_columns(g_raw, big_row, bfg_row):
    ig = g_raw + big_row
    lf = jax.nn.log_sigmoid(pltpu.roll(g_raw, LANES - M_HEADS, 1) + bfg_row)
    return ig, lf


def _head_norm(num, den, emt_b, gain_row):
    hh = num / jnp.maximum(jnp.abs(den), emt_b)
    return hh * lax.rsqrt(jnp.mean(hh * hh, axis=-1, keepdims=True) + EPS) * gain_row


def _conv_out(gb, zc, u, u_m1, u_m2, cw_ref):
    yc = cw_ref[0:1, :] * u_m2 + cw_ref[1:2, :] * u_m1 + cw_ref[2:3, :] * u
    return gb * yc * _silu(zc)


def _gated_heads(hn, p_scr):
    return hn * _sigmoid(p_scr[:, OFF_O:OFF_O + M_WIDTH]) * _silu(p_scr[:, OFF_ZM:OFF_ZM + M_WIDTH])


def _sequence_block(xn, project_tile, p_scr, g_scr, hn_scr, ycat_scr, ubuf, caug, m_scr,
                    wg_ref, mhg_ref, cw_ref, big_ref, bfg_ref, *, tb, n_pad):
    g_scr[...] = _dot(xn, wg_ref[...])
    late_tiles = []
    if project_tile is not None:
        for t in QKV_TILES:
            project_tile(t)
        early = 4
        for t in REST_TILES[:early]:
            project_tile(t)
        late_tiles = list(REST_TILES[early:])

    row = lax.broadcasted_iota(jnp.int32, (CHUNK, CHUNK), 0)
    col = lax.broadcasted_iota(jnp.int32, (CHUNK, CHUNK), 1)
    causal = col <= row
    ones_blk = jnp.ones((CHUNK, LANES), _BF)

    n_chunks = tb // CHUNK
    gates = []
    m_prev = m_scr[...]
    for c in range(n_chunks):
        rows = slice(c * CHUNK, (c + 1) * CHUNK)
        ig, lf = _gate_columns(g_scr[rows, :], big_ref[...], bfg_ref[...])
        if n_pad:
            ig = jnp.where(row >= n_pad, ig, NEG_INF)
            lf = jnp.where(row >= n_pad, lf, 0.0)
        b = _cumsum_rows(lf, row, CHUNK)
        a = ig - b
        big_m = jnp.maximum(m_prev, _cummax_rows(a, row, CHUNK))
        m_t = b + big_m
        m_last = big_m[CHUNK - 1:CHUNK, :]
        gates.append(dict(a_t=a.T, big_m=big_m, inter=jnp.exp(m_prev - big_m), emt=jnp.exp(-m_t),
                          w=jnp.exp(a - m_last), decay=jnp.exp(m_prev - m_last)))
        m_prev = m_t[CHUNK - 1:CHUNK, :]
    m_scr[...] = m_prev

    def stage_a(c, h):
        rows = slice(c * CHUNK, (c + 1) * CHUNK)
        hs = h * HEAD_DIM
        g = gates[c]
        q = p_scr[rows, OFF_Q + hs:OFF_Q + hs + HEAD_DIM]
        k = p_scr[rows, OFF_K + hs:OFF_K + hs + HEAD_DIM] * K_SCALE
        a_b = jnp.broadcast_to(g["a_t"][h:h + 1, :], (CHUNK, CHUNK))
        m_b = jnp.broadcast_to(g["big_m"][:, h:h + 1], (CHUNK, CHUNK))
        d = jnp.exp(jnp.where(causal, a_b - m_b, NEG_INF))
        q_inter = (jnp.broadcast_to(g["inter"][:, h:h + 1], (CHUNK, HEAD_DIM)) * q).astype(_BF)
        return dict(qk=_dot_nt(q.astype(_BF), k.astype(_BF)), d=d, q_inter=q_inter, k=k)

    def stage_b(c, h, st):
        rows = slice(c * CHUNK, (c + 1) * CHUNK)
        hs = h * HEAD_DIM
        g = gates[c]
        v = p_scr[rows, OFF_V + hs:OFF_V + hs + HEAD_DIM]
        s = st["qk"] * st["d"]
        lhs = jnp.concatenate([s.astype(_BF), st["q_inter"]], axis=1)
        rhs = jnp.concatenate([jnp.concatenate([v.astype(_BF), ones_blk], axis=1), caug[h].astype(_BF)], axis=0)
        tot = _dot(lhs, rhs)
        emt_b = jnp.broadcast_to(g["emt"][:, h:h + 1], (CHUNK, LANES))
        hn_scr[rows, hs:hs + HEAD_DIM] = _head_norm(tot[:, :LANES], tot[:, LANES:], emt_b,
                                                    mhg_ref[:, hs:hs + HEAD_DIM])
        w_b = jnp.broadcast_to(g["w"][:, h:h + 1], (CHUNK, LANES))
        wv = jnp.concatenate([(w_b * v).astype(_BF), w_b.astype(_BF)], axis=1)
        upd = _dot(st["k"].T.astype(_BF), wv)
        decay_b = jnp.broadcast_to(g["decay"][:, h:h + 1], (HEAD_DIM, 2 * LANES))
        caug[h] = decay_b * caug[h] + upd

    units = [(c, h) for c in range(n_chunks) for h in range(M_HEADS)]
    lag = min(2, len(units))
    pending = {}
    for t in range(max(len(units) + lag, len(late_tiles))):
        if t < len(units):
            pending[t] = stage_a(*units[t])
        if t < len(late_tiles):
            project_tile(late_tiles[t])
        if 0 <= t - lag < len(units):
            stage_b(*units[t - lag], pending.pop(t - lag))

    u = p_scr[:, OFF_GC:OFF_GC + CONV_WIDTH] * p_scr[:, OFF_HC:OFF_HC + CONV_WIDTH]
    ubuf[SUBLANES:SUBLANES + tb, :] = u
    u_m1 = ubuf[SUBLANES - 1:SUBLANES - 1 + tb, :]
    u_m2 = ubuf[SUBLANES - 2:SUBLANES - 2 + tb, :]
    y_conv = _conv_out(p_scr[:, OFF_GB:OFF_GB + CONV_WIDTH], p_scr[:, OFF_ZC:OFF_ZC + CONV_WIDTH],
                       u, u_m1, u_m2, cw_ref)
    ycat_scr[:, 0:CONV_WIDTH] = y_conv.astype(_BF)
    ubuf[0:SUBLANES, :] = ubuf[tb:tb + SUBLANES, :]
    ycat_scr[:, CONV_WIDTH:CONV_WIDTH + M_WIDTH] = _gated_heads(hn_scr[...], p_scr).astype(_BF)


def _sequence_scratch(tb):
    return [
        pltpu.VMEM((tb, MAIN_WIDTH), _F32),
        pltpu.VMEM((tb, LANES), _F32),
        pltpu.VMEM((tb, M_WIDTH), _F32),
        pltpu.VMEM((tb, D_MODEL), _BF),
        pltpu.VMEM((tb + 2 * SUBLANES, CONV_WIDTH), _F32),
        pltpu.VMEM((M_HEADS, HEAD_DIM, 2 * LANES), _F32),
        pltpu.VMEM((1, LANES), _F32),
    ]


def _prep_body(wt_ref, x_ref, wgt_ref, ng_ref, mhg_ref, cw_ref, big_ref, bfg_ref,
               w3_ref, wg_ref, caug_out_ref, m_out_ref, ut_out_ref,
               xn_scr, p3_scr, p_scr, g_scr, hn_scr, ycat_scr, ubuf, caug, m_scr):
    t = pl.program_id(0)

    @pl.when(t == 0)
    def _():
        xn_scr[...] = _rms_rows(x_ref[...], ng_ref[...]).astype(_BF)
        wg_ref[...] = wgt_ref[...].T.astype(_BF)

    tile = wt_ref[...].T.astype(_BF)
    w3_ref[0] = tile
    p3_scr[t] = _dot(xn_scr[...], tile)

    @pl.when(t == N_TILES - 1)
    def _():
        for j in range(N_TILES):
            p_scr[:, _tile_cols(j)] = p3_scr[j]
        caug[...] = jnp.zeros_like(caug)
        m_scr[...] = jnp.zeros_like(m_scr)
        ubuf[0:SUBLANES, :] = jnp.zeros((SUBLANES, CONV_WIDTH), _F32)
        _sequence_block(xn_scr[...], None, p_scr, g_scr, hn_scr, ycat_scr, ubuf, caug, m_scr,
                        wg_ref, mhg_ref, cw_ref, big_ref, bfg_ref, tb=CHUNK, n_pad=CHUNK - N_META)
        caug_out_ref[...] = caug[...]
        m_out_ref[...] = m_scr[...]
        ut_out_ref[...] = ubuf[0:SUBLANES, :]


def _prep_call(w_t, x_meta, wgt, small):
    ng, mhg, cw, big, bfg = small

    def const(shape):
        nd = len(shape)
        return pl.BlockSpec(shape, lambda t: (0,) * nd)

    in_specs = [
        pl.BlockSpec((MXU_COLS, D_MODEL), lambda t: (t, 0)),
        const(x_meta.shape), const(wgt.shape), const(ng.shape), const(mhg.shape), const(cw.shape),
        const(big.shape), const(bfg.shape),
    ]
    out_shape = (
        jax.ShapeDtypeStruct((N_TILES, D_MODEL, MXU_COLS), _BF),
        jax.ShapeDtypeStruct((D_MODEL, LANES), _BF),
        jax.ShapeDtypeStruct((M_HEADS, HEAD_DIM, 2 * LANES), _F32),
        jax.ShapeDtypeStruct((1, LANES), _F32),
        jax.ShapeDtypeStruct((SUBLANES, CONV_WIDTH), _F32),
    )
    out_specs = (pl.BlockSpec((1, D_MODEL, MXU_COLS), lambda t: (t, 0, 0)),) + tuple(
        const(s.shape) for s in out_shape[1:])
    scratch = [
        pltpu.VMEM((CHUNK, D_MODEL), _BF),
        pltpu.VMEM((N_TILES, CHUNK, MXU_COLS), _F32),
    ] + _sequence_scratch(CHUNK)
    return pl.pallas_call(
        _prep_body,
        grid=(N_TILES,),
        in_specs=in_specs,
        out_specs=out_specs,
        out_shape=out_shape,
        scratch_shapes=scratch,
        compiler_params=pltpu.CompilerParams(
            dimension_semantics=("arbitrary",), vmem_limit_bytes=32 * 1024 * 1024),
        name="prep_meta",
    )(w_t, x_meta, wgt, ng, mhg, cw, big, bfg)


def _prompt_body(x_ref, w3_ref, wg_ref, wo_ref, ng_ref, fgain_ref, mhg_ref, cw_ref, big_ref, bfg_ref,
                 caug0_ref, m0_ref, ut0_ref,
                 y_ref, c_out_ref, n_out_ref, m_out_ref, conv_out_ref,
                 p_scr, g_scr, hn_scr, ycat_scr, ubuf, caug, m_scr, *, tb):
    i = pl.program_id(1)
    last = pl.num_programs(1) - 1

    @pl.when(i == 0)
    def _():
        caug[...] = caug0_ref[...]
        m_scr[...] = m0_ref[...]
        ubuf[0:SUBLANES, :] = ut0_ref[...]

    x = x_ref[0]
    xn = _rms_rows(x, ng_ref[...]).astype(_BF)

    def project_tile(t):
        p_scr[:, _tile_cols(t)] = _dot(xn, w3_ref[t])

    _sequence_block(xn, project_tile, p_scr, g_scr, hn_scr, ycat_scr, ubuf, caug, m_scr,
                    wg_ref, mhg_ref, cw_ref, big_ref, bfg_ref, tb=tb, n_pad=0)

    out = _dot(ycat_scr[...], wo_ref[...]) + x
    y_ref[0] = _rms_rows(out, fgain_ref[...])

    @pl.when(i == last)
    def _():
        m_out_ref[0] = m_scr[...]
        conv_out_ref[0] = ubuf[SUBLANES - (CONV_K - 1):SUBLANES, :]
        for h in range(M_HEADS):
            c_out_ref[0, h] = caug[h, :, 0:HEAD_DIM]
            n_out_ref[0, h:h + 1, :] = caug[h, :, LANES:2 * LANES].T[0:1, :]


def _prompt_call(x, w3, wg, wo, small, fgain, caug0, m0, ut0, *, tb):
    ng, mhg, cw, big, bfg = small
    nb, t, _ = x.shape
    nt = t // tb

    def const(shape):
        nd = len(shape)
        return pl.BlockSpec(shape, lambda b, i: (0,) * nd)

    in_specs = [
        pl.BlockSpec((1, tb, D_MODEL), lambda b, i: (b, i, 0)),
        const(w3.shape), const(wg.shape), const(wo.shape), const(ng.shape), const(fgain.shape),
        const(mhg.shape), const(cw.shape), const(big.shape), const(bfg.shape),
        const(caug0.shape), const(m0.shape), const(ut0.shape),
    ]
    out_shape = (
        jax.ShapeDtypeStruct((nb, t, D_MODEL), _F32),
        jax.ShapeDtypeStruct((nb, M_HEADS, HEAD_DIM, HEAD_DIM), _F32),
        jax.ShapeDtypeStruct((nb, M_HEADS, HEAD_DIM), _F32),
        jax.ShapeDtypeStruct((nb, 1, LANES), _F32),
        jax.ShapeDtypeStruct((nb, CONV_K - 1, CONV_WIDTH), _F32),
    )
    out_specs = (
        pl.BlockSpec((1, tb, D_MODEL), lambda b, i: (b, i, 0)),
        pl.BlockSpec((1, M_HEADS, HEAD_DIM, HEAD_DIM), lambda b, i: (b, 0, 0, 0)),
        pl.BlockSpec((1, M_HEADS, HEAD_DIM), lambda b, i: (b, 0, 0)),
        pl.BlockSpec((1, 1, LANES), lambda b, i: (b, 0, 0)),
        pl.BlockSpec((1, CONV_K - 1, CONV_WIDTH), lambda b, i: (b, 0, 0)),
    )
    return pl.pallas_call(
        functools.partial(_prompt_body, tb=tb),
        grid=(nb, nt),
        in_specs=in_specs,
        out_specs=out_specs,
        out_shape=out_shape,
        scratch_shapes=_sequence_scratch(tb),
        compiler_params=pltpu.CompilerParams(
            dimension_semantics=("arbitrary", "arbitrary"), vmem_limit_bytes=60 * 1024 * 1024),
        name="prompt_layer",
    )(x, w3, wg, wo, ng, fgain, mhg, cw, big, bfg, caug0, m0, ut0)


def _sample_body(x_ref, w3_ref, wg_ref, wo_ref, ng_ref, fgain_ref, mhg_ref, cw_ref, big_ref, bfg_ref,
                 sconv_ref, m_ref, n_ref, c_ref,
                 y_ref, conv_out_ref, c_out_ref, n_out_ref, m_out_ref,
                 p_scr, ni_scr, sc_scr, ycat_scr):
    @pl.when(pl.program_id(0) == 0)
    def _():
        sc_scr[...] = jnp.zeros_like(sc_scr)

    x = x_ref[...]
    xn = _rms_rows(x, ng_ref[...]).astype(_BF)
    for t in range(N_TILES):
        p_scr[:, _tile_cols(t)] = _dot(xn, w3_ref[t])
    g_raw = _dot(xn, wg_ref[...])

    row = lax.broadcasted_iota(jnp.int32, (CHUNK, CHUNK), 0)
    col = lax.broadcasted_iota(jnp.int32, (CHUNK, CHUNK), 1)
    pos = row % DEC_SEQ
    grp = row // DEC_SEQ
    causal = (grp == (col // DEC_SEQ)) & (col <= row)
    grow = lax.broadcasted_iota(jnp.int32, (SEQ_PER_STEP, CHUNK), 0)
    gcol = lax.broadcasted_iota(jnp.int32, (SEQ_PER_STEP, CHUNK), 1)
    pick_last = (gcol == grow * DEC_SEQ + (DEC_SEQ - 1)).astype(_F32)
    seq_sum = (gcol // DEC_SEQ == grow).astype(_BF)
    erow = lax.broadcasted_iota(jnp.int32, (CHUNK, SEQ_PER_STEP), 0)
    ecol = lax.broadcasted_iota(jnp.int32, (CHUNK, SEQ_PER_STEP), 1)
    own_seq = erow // DEC_SEQ == ecol
    ones_blk = jnp.ones((CHUNK, LANES), _BF)

    def rows_from_seqs(v16):
        return jnp.broadcast_to(v16[:, None, :], (SEQ_PER_STEP, DEC_SEQ, LANES)).reshape(CHUNK, LANES)

    def last_row_of_seq(v):
        v3 = v.reshape(SEQ_PER_STEP, DEC_SEQ, LANES)
        return jnp.broadcast_to(v3[:, DEC_SEQ - 1:DEC_SEQ, :], v3.shape).reshape(CHUNK, LANES)

    u = p_scr[:, OFF_GC:OFF_GC + CONV_WIDTH] * p_scr[:, OFF_HC:OFF_HC + CONV_WIDTH]
    for g in range(SEQ_PER_STEP):
        sc_scr[g * DEC_SEQ:g * DEC_SEQ + CONV_K - 1, :] = sconv_ref[g]
        conv_out_ref[g] = u[(g + 1) * DEC_SEQ - (CONV_K - 1):(g + 1) * DEC_SEQ, :]
    sc = sc_scr[...]
    posw = lax.broadcasted_iota(jnp.int32, (CHUNK, CONV_WIDTH), 0) % DEC_SEQ
    u_m1 = jnp.where(posw == 0, pltpu.roll(sc, CHUNK - 1, 0), pltpu.roll(u, 1, 0))
    u_m2 = jnp.where(posw < 2, sc, pltpu.roll(u, 2, 0))
    y_conv = _conv_out(p_scr[:, OFF_GB:OFF_GB + CONV_WIDTH], p_scr[:, OFF_ZC:OFF_ZC + CONV_WIDTH],
                       u, u_m1, u_m2, cw_ref)
    ycat_scr[:, 0:CONV_WIDTH] = y_conv.astype(_BF)

    ig, lf = _gate_columns(g_raw, big_ref[...], bfg_ref[...])
    b = _cumsum_rows(lf, pos, DEC_SEQ)
    a = ig - b
    m_prev = rows_from_seqs(m_ref[...])
    big_m = jnp.maximum(m_prev, _cummax_rows(a, pos, DEC_SEQ))
    m_t = b + big_m
    inter = jnp.exp(m_prev - big_m)
    emt = jnp.exp(-m_t)
    m_last = last_row_of_seq(big_m)
    w = jnp.exp(a - m_last)
    decay = jnp.exp(m_prev - m_last)
    m_out_ref[...] = _dot_exact(pick_last, m_t)
    dec16 = _dot_exact(pick_last, decay)
    a_t = a.T

    heads = []
    for h in range(M_HEADS):
        hs = h * HEAD_DIM
        q = p_scr[:, OFF_Q + hs:OFF_Q + hs + HEAD_DIM]
        k = p_scr[:, OFF_K + hs:OFF_K + hs + HEAD_DIM] * K_SCALE
        v = p_scr[:, OFF_V + hs:OFF_V + hs + HEAD_DIM]
        q_bf = q.astype(_BF)
        a_b = jnp.broadcast_to(a_t[h:h + 1, :], (CHUNK, CHUNK))
        m_b = jnp.broadcast_to(big_m[:, h:h + 1], (CHUNK, CHUNK))
        d = jnp.exp(jnp.where(causal, a_b - m_b, NEG_INF))
        w_b = jnp.broadcast_to(w[:, h:h + 1], (CHUNK, LANES))
        heads.append(dict(q_bf=q_bf, k=k, v=v, d=d, w_b=w_b, qk=_dot_nt(q_bf, k.astype(_BF))))

    for h in range(M_HEADS):
        hs = h * HEAD_DIM
        st = heads[h]
        wv = st["w_b"] * st["v"]
        k_t_bf = st["k"].T.astype(_BF)
        dec_b = jnp.broadcast_to(decay[:, h:h + 1], (CHUNK, HEAD_DIM))

        for g in range(0, SEQ_PER_STEP, 2):
            pair_rows = slice(g * DEC_SEQ, (g + 2) * DEC_SEQ)
            c0, c1 = c_ref[g, h], c_ref[g + 1, h]
            qc = _dot(st["q_bf"][pair_rows, :], jnp.concatenate([c0.astype(_BF), c1.astype(_BF)], axis=1))
            ni_scr[g * DEC_SEQ:(g + 1) * DEC_SEQ, hs:hs + HEAD_DIM] = qc[0:DEC_SEQ, 0:HEAD_DIM]
            ni_scr[(g + 1) * DEC_SEQ:(g + 2) * DEC_SEQ, hs:hs + HEAD_DIM] = qc[DEC_SEQ:, HEAD_DIM:]
            wv_pair = jnp.concatenate([jnp.where(grp == g, wv, 0.0).astype(_BF),
                                       jnp.where(grp == g + 1, wv, 0.0).astype(_BF)], axis=1)
            upd = _dot(k_t_bf, wv_pair)
            dec0 = jnp.broadcast_to(dec_b[g * DEC_SEQ:g * DEC_SEQ + 1, :], (HEAD_DIM, HEAD_DIM))
            dec1 = jnp.broadcast_to(dec_b[(g + 1) * DEC_SEQ:(g + 1) * DEC_SEQ + 1, :], (HEAD_DIM, HEAD_DIM))
            c_out_ref[g, h] = dec0 * c0 + upd[:, 0:HEAD_DIM]
            c_out_ref[g + 1, h] = dec1 * c1 + upd[:, HEAD_DIM:]

        s = st["qk"] * st["d"]
        sv = _dot(s.astype(_BF), jnp.concatenate([st["v"].astype(_BF), ones_blk], axis=1))
        n_old = n_ref[:, h, :]
        num_inter = ni_scr[:, hs:hs + HEAD_DIM]
        qn = _dot_nt(st["q_bf"], n_old.astype(_BF))
        den_inter = jnp.sum(jnp.where(own_seq, qn, 0.0), axis=-1, keepdims=True)
        inter_b = jnp.broadcast_to(inter[:, h:h + 1], (CHUNK, LANES))
        num = sv[:, :LANES] + inter_b * num_inter
        den = sv[:, LANES:] + inter_b * den_inter
        emt_b = jnp.broadcast_to(emt[:, h:h + 1], (CHUNK, LANES))
        ni_scr[:, hs:hs + HEAD_DIM] = _head_norm(num, den, emt_b, mhg_ref[:, hs:hs + HEAD_DIM])
        n_upd = _dot(seq_sum, (st["w_b"] * st["k"]).astype(_BF))
        dec16_b = jnp.broadcast_to(dec16[:, h:h + 1], (SEQ_PER_STEP, HEAD_DIM))
        n_out_ref[:, h, :] = dec16_b * n_old + n_upd

    ycat_scr[:, CONV_WIDTH:CONV_WIDTH + M_WIDTH] = _gated_heads(ni_scr[...], p_scr).astype(_BF)
    out = _dot(ycat_scr[...], wo_ref[...]) + x
    y_ref[...] = _rms_rows(out, fgain_ref[...])


def _sample_call(xs, w3, wg, wo, small, fgain, state_conv, m_pad, n_state, c_state):
    ng, mhg, cw, big, bfg = small
    rows = xs.shape[0]
    nseq = c_state.shape[0]
    steps = rows // CHUNK

    def const(shape):
        nd = len(shape)
        return pl.BlockSpec(shape, lambda i: (0,) * nd)

    def seq_spec(*tail):
        nd = len(tail)
        return pl.BlockSpec((SEQ_PER_STEP,) + tail, lambda i: (i,) + (0,) * nd)

    x_spec = pl.BlockSpec((CHUNK, D_MODEL), lambda i: (i, 0))
    conv_spec = seq_spec(CONV_K - 1, CONV_WIDTH)
    c_spec = seq_spec(M_HEADS, HEAD_DIM, HEAD_DIM)
    n_spec = seq_spec(M_HEADS, HEAD_DIM)
    m_spec = seq_spec(LANES)
    in_specs = [
        x_spec,
        const(w3.shape), const(wg.shape), const(wo.shape), const(ng.shape), const(fgain.shape),
        const(mhg.shape), const(cw.shape), const(big.shape), const(bfg.shape),
        conv_spec, m_spec, n_spec, c_spec,
    ]
    out_shape = (
        jax.ShapeDtypeStruct((rows, D_MODEL), _F32),
        jax.ShapeDtypeStruct(state_conv.shape, _F32),
        jax.ShapeDtypeStruct(c_state.shape, _F32),
        jax.ShapeDtypeStruct(n_state.shape, _F32),
        jax.ShapeDtypeStruct((nseq, LANES), _F32),
    )
    out_specs = (x_spec, conv_spec, c_spec, n_spec, m_spec)
    scratch = [
        pltpu.VMEM((CHUNK, MAIN_WIDTH), _F32),
        pltpu.VMEM((CHUNK, M_WIDTH), _F32),
        pltpu.VMEM((CHUNK, CONV_WIDTH), _F32),
        pltpu.VMEM((CHUNK, D_MODEL), _BF),
    ]
    return pl.pallas_call(
        _sample_body,
        grid=(steps,),
        in_specs=in_specs,
        out_specs=out_specs,
        out_shape=out_shape,
        scratch_shapes=scratch,
        compiler_params=pltpu.CompilerParams(
            dimension_semantics=("arbitrary",), vmem_limit_bytes=56 * 1024 * 1024),
        name="sample_layer",
    )(xs, w3, wg, wo, ng, fgain, mhg, cw, big, bfg, state_conv, m_pad, n_state, c_state)


def _pad_lanes(v):
    return jnp.pad(v.astype(_F32), (0, LANES - v.shape[0])).reshape(1, LANES)


def kernel(x_prompt, x_sample, state_conv, state_mlstm_c, state_mlstm_n, state_mlstm_m, meta_tokens, norm_gain,
           w_in, conv_w, b_igate, b_fgate, mh_norm_gain, w_out, final_norm_gain):
    nseq, dec_seq, _ = x_sample.shape
    assert dec_seq == DEC_SEQ and meta_tokens.shape[0] == N_META

    w_t = w_in.T
    wgt = jnp.pad(w_t[MAIN_WIDTH:], ((0, LANES - 2 * M_HEADS), (0, 0)))
    small = (norm_gain.reshape(1, D_MODEL), mh_norm_gain.reshape(1, M_WIDTH), conv_w,
             _pad_lanes(b_igate), _pad_lanes(b_fgate))
    fgain = final_norm_gain.reshape(1, D_MODEL)
    wo = w_out.astype(_BF)

    x_meta = jnp.concatenate([jnp.zeros((CHUNK - N_META, D_MODEL), _F32), meta_tokens.astype(_F32)], axis=0)
    w3, wg, caug_meta, m_meta, ut_meta = _prep_call(w_t, x_meta, wgt, small)

    y_prompt, c_p, n_p, m_p, conv_p = _prompt_call(
        x_prompt, w3, wg, wo, small, fgain, caug_meta, m_meta, ut_meta, tb=1024)
    m_pp = m_p[:, 0, :M_HEADS]

    xs = x_sample.reshape(nseq * DEC_SEQ, D_MODEL)
    m_pad = jnp.pad(state_mlstm_m, ((0, 0), (0, LANES - M_HEADS)))
    ys, conv_s, c_s, n_s, m_s = _sample_call(xs, w3, wg, wo, small, fgain, state_conv, m_pad, state_mlstm_n,
                                             state_mlstm_c)
    y_sample = ys.reshape(nseq, DEC_SEQ, D_MODEL)
    m_ss = m_s[:, :M_HEADS]

    return (y_prompt, y_sample, conv_p, c_p, n_p, m_pp, conv_s, c_s, n_s, m_ss)
```

```python
import functools

import jax
import jax.numpy as jnp
from jax import lax
from jax.experimental import pallas as pl
from jax.experimental.pallas import tpu as pltpu

D_MODEL = 1024
CONV_WIDTH = 512
M_WIDTH = 512
M_HEADS = 4
HEAD_DIM = 128
N_META = 16
CONV_K = 3
EPS = 1e-6
MAIN_WIDTH = 4 * CONV_WIDTH + 5 * M_WIDTH
LANES = 128
SUBLANES = 8
MXU_COLS = 256
N_TILES = MAIN_WIDTH // MXU_COLS
CHUNK = 128
DEC_SEQ = 8
SEQ_PER_STEP = CHUNK // DEC_SEQ
NEG_INF = float("-inf")
K_SCALE = HEAD_DIM ** -0.5

OFF_GB, OFF_GC, OFF_HC, OFF_ZC = 0, 512, 1024, 1536
OFF_Q, OFF_K, OFF_V, OFF_O, OFF_ZM = 2048, 2560, 3072, 3584, 4096
QKV_TILES = tuple(range(OFF_Q // MXU_COLS, OFF_O // MXU_COLS))
REST_TILES = tuple(t for t in range(N_TILES) if t not in QKV_TILES)

_HI = lax.Precision.HIGHEST
_BF = jnp.bfloat16
_F32 = jnp.float32


def _dot(a, b):
    return jnp.dot(a, b, preferred_element_type=_F32)


def _dot_exact(a, b):
    return jnp.dot(a, b, precision=_HI, preferred_element_type=_F32)


def _dot_nt(a, b):
    return lax.dot_general(a, b, (((1,), (1,)), ((), ())), preferred_element_type=_F32)


def _rms_rows(x, gain_row):
    return x * lax.rsqrt(jnp.mean(x * x, axis=-1, keepdims=True) + EPS) * gain_row


def _sigmoid(x):
    return 1.0 / (1.0 + jnp.exp(-x))


def _silu(x):
    return x * _sigmoid(x)


def _tile_cols(t):
    return slice(t * MXU_COLS, (t + 1) * MXU_COLS)


def _cummax_rows(x, pos, period):
    s = 1
    while s < period:
        shifted = pltpu.roll(x, s, 0)
        x = jnp.maximum(x, jnp.where(pos >= s, shifted, NEG_INF))
        s *= 2
    return x


def _cumsum_rows(x, pos, period):
    s = 1
    while s < period:
        x = x + jnp.where(pos >= s, pltpu.roll(x, s, 0), 0.0)
        s *= 2
    return x


def _gate_columns(g_raw, big_row, bfg_row):
    ig = g_raw + big_row
    lf = jax.nn.log_sigmoid(pltpu.roll(g_raw, LANES - M_HEADS, 1) + bfg_row)
    return ig, lf


def _head_norm(num, den, emt_b, gain_row):
    hh = num / jnp.maximum(jnp.abs(den), emt_b)
    return hh * lax.rsqrt(jnp.mean(hh * hh, axis=-1, keepdims=True) + EPS) * gain_row


def _conv_out(gb, zc, u, u_m1, u_m2, cw_ref):
    yc = cw_ref[0:1, :] * u_m2 + cw_ref[1:2, :] * u_m1 + cw_ref[2:3, :] * u
    return gb * yc * _silu(zc)


def _gated_heads(hn, p_scr):
    return hn * _sigmoid(p_scr[:, OFF_O:OFF_O + M_WIDTH]) * _silu(p_scr[:, OFF_ZM:OFF_ZM + M_WIDTH])


def _sequence_block(xn, project_tile, emit_rows, p_scr, g_scr, hn_scr, ycat_scr, ubuf, caug, m_scr,
                    wg_ref, mhg_ref, cw_ref, big_ref, bfg_ref, *, tb, n_pad, side_work=None):
    side_work = side_work or {}
    g_scr[...] = _dot(xn, wg_ref[...])
    late_tiles = []
    if project_tile is not None:
        for t in QKV_TILES:
            project_tile(t)
        early = 2
        for t in REST_TILES[:early]:
            project_tile(t)
        late_tiles = list(REST_TILES[early:])

    row = lax.broadcasted_iota(jnp.int32, (CHUNK, CHUNK), 0)
    col = lax.broadcasted_iota(jnp.int32, (CHUNK, CHUNK), 1)
    causal = col <= row
    ones_blk = jnp.ones((CHUNK, LANES), _BF)

    n_chunks = tb // CHUNK
    gates = []
    m_prev = m_scr[...]
    for c in range(n_chunks):
        rows = slice(c * CHUNK, (c + 1) * CHUNK)
        ig, lf = _gate_columns(g_scr[rows, :], big_ref[...], bfg_ref[...])
        if n_pad:
            ig = jnp.where(row >= n_pad, ig, NEG_INF)
            lf = jnp.where(row >= n_pad, lf, 0.0)
        b = _cumsum_rows(lf, row, CHUNK)
        a = ig - b
        big_m = jnp.maximum(m_prev, _cummax_rows(a, row, CHUNK))
        m_t = b + big_m
        m_last = big_m[CHUNK - 1:CHUNK, :]
        gates.append(dict(a_t=a.T, big_m=big_m, inter=jnp.exp(m_prev - big_m), emt=jnp.exp(-m_t),
                          w=jnp.exp(a - m_last), decay=jnp.exp(m_prev - m_last)))
        m_prev = m_t[CHUNK - 1:CHUNK, :]
    m_scr[...] = m_prev

    def stage_a(c, h):
        rows = slice(c * CHUNK, (c + 1) * CHUNK)
        hs = h * HEAD_DIM
        g = gates[c]
        q = p_scr[rows, OFF_Q + hs:OFF_Q + hs + HEAD_DIM]
        k = p_scr[rows, OFF_K + hs:OFF_K + hs + HEAD_DIM] * K_SCALE
        a_b = jnp.broadcast_to(g["a_t"][h:h + 1, :], (CHUNK, CHUNK))
        m_b = jnp.broadcast_to(g["big_m"][:, h:h + 1], (CHUNK, CHUNK))
        d = jnp.exp(jnp.where(causal, a_b - m_b, NEG_INF))
        q_inter = (jnp.broadcast_to(g["inter"][:, h:h + 1], (CHUNK, HEAD_DIM)) * q).astype(_BF)
        return dict(qk=_dot_nt(q.astype(_BF), k.astype(_BF)), d=d, q_inter=q_inter, k=k)

    def stage_b(c, h, st):
        rows = slice(c * CHUNK, (c + 1) * CHUNK)
        hs = h * HEAD_DIM
        g = gates[c]
        v = p_scr[rows, OFF_V + hs:OFF_V + hs + HEAD_DIM]
        s = st["qk"] * st["d"]
        lhs = jnp.concatenate([s.astype(_BF), st["q_inter"]], axis=1)
        rhs = jnp.concatenate([jnp.concatenate([v.astype(_BF), ones_blk], axis=1), caug[h].astype(_BF)], axis=0)
        tot = _dot(lhs, rhs)
        emt_b = jnp.broadcast_to(g["emt"][:, h:h + 1], (CHUNK, LANES))
        hn_scr[rows, hs:hs + HEAD_DIM] = _head_norm(tot[:, :LANES], tot[:, LANES:], emt_b,
                                                    mhg_ref[:, hs:hs + HEAD_DIM])
        w_b = jnp.broadcast_to(g["w"][:, h:h + 1], (CHUNK, LANES))
        wv = jnp.concatenate([(w_b * v).astype(_BF), w_b.astype(_BF)], axis=1)
        upd = _dot(st["k"].T.astype(_BF), wv)
        decay_b = jnp.broadcast_to(g["decay"][:, h:h + 1], (HEAD_DIM, 2 * LANES))
        caug[h] = decay_b * caug[h] + upd

    def finish_rows(r0, r1):
        rr = slice(r0, r1)
        n = r1 - r0
        u = p_scr[rr, OFF_GC:OFF_GC + CONV_WIDTH] * p_scr[rr, OFF_HC:OFF_HC + CONV_WIDTH]
        ubuf[SUBLANES + r0:SUBLANES + r1, :] = u
        u_m1 = ubuf[SUBLANES - 1 + r0:SUBLANES - 1 + r0 + n, :]
        u_m2 = ubuf[SUBLANES - 2 + r0:SUBLANES - 2 + r0 + n, :]
        y_conv = _conv_out(p_scr[rr, OFF_GB:OFF_GB + CONV_WIDTH], p_scr[rr, OFF_ZC:OFF_ZC + CONV_WIDTH],
                           u, u_m1, u_m2, cw_ref)
        ycat_scr[rr, 0:CONV_WIDTH] = y_conv.astype(_BF)
        y_m = (hn_scr[rr, :] * _sigmoid(p_scr[rr, OFF_O:OFF_O + M_WIDTH])
               * _silu(p_scr[rr, OFF_ZM:OFF_ZM + M_WIDTH]))
        ycat_scr[rr, CONV_WIDTH:CONV_WIDTH + M_WIDTH] = y_m.astype(_BF)
        if emit_rows is not None:
            emit_rows(r0, r1)

    units = [(c, h) for c in range(n_chunks) for h in range(M_HEADS)]
    lag = min(3, len(units))
    n_groups = 1
    group_units = len(units) // n_groups
    assert len(late_tiles) <= group_units + lag
    pending = {}
    for t in range(len(units) + lag):
        if t < len(units):
            pending[t] = stage_a(*units[t])
        if t < len(late_tiles):
            project_tile(late_tiles[t])
        for fn in side_work.get(t, ()):
            fn()
        if t >= lag:
            stage_b(*units[t - lag], pending.pop(t - lag))
            done = t - lag + 1
            if done % group_units == 0:
                gi = done // group_units
                finish_rows((gi - 1) * tb // n_groups, gi * tb // n_groups)
    ubuf[0:SUBLANES, :] = ubuf[tb:tb + SUBLANES, :]


def _sequence_scratch(tb):
    return [
        pltpu.VMEM((tb, MAIN_WIDTH), _F32),
        pltpu.VMEM((tb, LANES), _F32),
        pltpu.VMEM((tb, M_WIDTH), _F32),
        pltpu.VMEM((tb, D_MODEL), _BF),
        pltpu.VMEM((tb + 2 * SUBLANES, CONV_WIDTH), _F32),
        pltpu.VMEM((M_HEADS, HEAD_DIM, 2 * LANES), _F32),
        pltpu.VMEM((1, LANES), _F32),
    ]


def _prep_body(wt_ref, x_ref, wgt_ref, ng_ref, mhg_ref, cw_ref, big_ref, bfg_ref,
               w3_ref, wg_ref, caug_out_ref, m_out_ref, ut_out_ref,
               xn_scr, p3_scr, p_scr, g_scr, hn_scr, ycat_scr, ubuf, caug, m_scr):
    t = pl.program_id(0)

    @pl.when(t == 0)
    def _():
        xn_scr[...] = _rms_rows(x_ref[...], ng_ref[...]).astype(_BF)
        wg_ref[...] = wgt_ref[...].T.astype(_BF)

    tile = wt_ref[...].T.astype(_BF)
    w3_ref[0] = tile
    p3_scr[t] = _dot(xn_scr[...], tile)

    @pl.when(t == N_TILES - 1)
    def _():
        for j in range(N_TILES):
            p_scr[:, _tile_cols(j)] = p3_scr[j]
        caug[...] = jnp.zeros_like(caug)
        m_scr[...] = jnp.zeros_like(m_scr)
        ubuf[0:SUBLANES, :] = jnp.zeros((SUBLANES, CONV_WIDTH), _F32)
        _sequence_block(xn_scr[...], None, None, p_scr, g_scr, hn_scr, ycat_scr, ubuf, caug, m_scr,
                        wg_ref, mhg_ref, cw_ref, big_ref, bfg_ref, tb=CHUNK, n_pad=CHUNK - N_META)
        caug_out_ref[...] = caug[...]
        m_out_ref[...] = m_scr[...]
        ut_out_ref[...] = ubuf[0:SUBLANES, :]


def _prep_call(w_t, x_meta, wgt, small):
    ng, mhg, cw, big, bfg = small

    def const(shape):
        nd = len(shape)
        return pl.BlockSpec(shape, lambda t: (0,) * nd)

    in_specs = [
        pl.BlockSpec((MXU_COLS, D_MODEL), lambda t: (t, 0)),
        const(x_meta.shape), const(wgt.shape), const(ng.shape), const(mhg.shape), const(cw.shape),
        const(big.shape), const(bfg.shape),
    ]
    out_shape = (
        jax.ShapeDtypeStruct((N_TILES, D_MODEL, MXU_COLS), _BF),
        jax.ShapeDtypeStruct((D_MODEL, LANES), _BF),
        jax.ShapeDtypeStruct((M_HEADS, HEAD_DIM, 2 * LANES), _F32),
        jax.ShapeDtypeStruct((1, LANES), _F32),
        jax.ShapeDtypeStruct((SUBLANES, CONV_WIDTH), _F32),
    )
    out_specs = (pl.BlockSpec((1, D_MODEL, MXU_COLS), lambda t: (t, 0, 0)),) + tuple(
        const(s.shape) for s in out_shape[1:])
    scratch = [
        pltpu.VMEM((CHUNK, D_MODEL), _BF),
        pltpu.VMEM((N_TILES, CHUNK, MXU_COLS), _F32),
    ] + _sequence_scratch(CHUNK)
    return pl.pallas_call(
        _prep_body,
        grid=(N_TILES,),
        in_specs=in_specs,
        out_specs=out_specs,
        out_shape=out_shape,
        scratch_shapes=scratch,
        compiler_params=pltpu.CompilerParams(
            dimension_semantics=("arbitrary",), vmem_limit_bytes=32 * 1024 * 1024),
        name="prep_meta",
    )(w_t, x_meta, wgt, ng, mhg, cw, big, bfg)


def _prompt_body(x_ref, xnext_ref, w3_ref, wg_ref, wo_ref, ng_ref, fgain_ref, mhg_ref, cw_ref, big_ref, bfg_ref,
                 caug0_ref, m0_ref, ut0_ref,
                 y_ref, c_out_ref, n_out_ref, m_out_ref, conv_out_ref,
                 xn_scr, out_scr, p_scr, g_scr, hn_scr, ycat_scr, ubuf, caug, m_scr, *, tb, nt, n_blocks):
    s = pl.program_id(0)
    i = s % nt
    live = s < n_blocks

    @pl.when(s == 0)
    def _():
        xn_scr[...] = _rms_rows(x_ref[0], ng_ref[...]).astype(_BF)
        out_scr[...] = jnp.zeros_like(out_scr)

    @pl.when(live & (i == 0))
    def _():
        caug[...] = caug0_ref[...]
        m_scr[...] = m0_ref[...]
        ubuf[0:SUBLANES, :] = ut0_ref[...]

    @pl.when(live)
    def _():
        xn = xn_scr[...]

        def project_tile(t):
            p_scr[:, _tile_cols(t)] = _dot(xn, w3_ref[t])

        def emit_rows(r0, r1):
            out_scr[r0:r1, :] = _dot(ycat_scr[r0:r1, :], wo_ref[...]) + x_ref[0, r0:r1, :]

        def norm_previous(r0, r1):
            def fn():
                y_ref[0, r0:r1, :] = _rms_rows(out_scr[r0:r1, :], fgain_ref[...])
            return fn

        def norm_next(r0, r1):
            def fn():
                xn_scr[r0:r1, :] = _rms_rows(xnext_ref[0, r0:r1, :], ng_ref[...]).astype(_BF)
            return fn

        q4 = tb // 4
        side_work = {2 + 2 * j: [norm_previous(j * q4, (j + 1) * q4)] for j in range(4)}
        side_work.update({10 + 2 * j: [norm_next(j * q4, (j + 1) * q4)] for j in range(4)})
        _sequence_block(xn, project_tile, emit_rows, p_scr, g_scr, hn_scr, ycat_scr, ubuf, caug, m_scr,
                        wg_ref, mhg_ref, cw_ref, big_ref, bfg_ref, tb=tb, n_pad=0, side_work=side_work)

    @pl.when(s == n_blocks)
    def _():
        y_ref[0] = _rms_rows(out_scr[...], fgain_ref[...])

    @pl.when(live & (i == nt - 1))
    def _():
        m_out_ref[0] = m_scr[...]
        conv_out_ref[0] = ubuf[SUBLANES - (CONV_K - 1):SUBLANES, :]
        for h in range(M_HEADS):
            c_out_ref[0, h] = caug[h, :, 0:HEAD_DIM]
            n_out_ref[0, h:h + 1, :] = caug[h, :, LANES:2 * LANES].T[0:1, :]


def _prompt_call(x, w3, wg, wo, small, fgain, caug0, m0, ut0, *, tb):
    ng, mhg, cw, big, bfg = small
    nb, t, _ = x.shape
    nt = t // tb
    n_blocks = nb * nt

    def const(shape):
        nd = len(shape)
        return pl.BlockSpec(shape, lambda s: (0,) * nd)

    def block_of(s):
        return jnp.minimum(s, n_blocks - 1)

    def rows_spec(step_to_block):
        return pl.BlockSpec((1, tb, D_MODEL), lambda s: (step_to_block(s) // nt, step_to_block(s) % nt, 0))

    def state_spec(*tail):
        nd = len(tail)
        return pl.BlockSpec((1,) + tail, lambda s: (block_of(s) // nt,) + (0,) * nd)

    in_specs = [
        rows_spec(block_of),
        rows_spec(lambda s: block_of(s + 1)),
        const(w3.shape), const(wg.shape), const(wo.shape), const(ng.shape), const(fgain.shape),
        const(mhg.shape), const(cw.shape), const(big.shape), const(bfg.shape),
        const(caug0.shape), const(m0.shape), const(ut0.shape),
    ]
    out_shape = (
        jax.ShapeDtypeStruct((nb, t, D_MODEL), _F32),
        jax.ShapeDtypeStruct((nb, M_HEADS, HEAD_DIM, HEAD_DIM), _F32),
        jax.ShapeDtypeStruct((nb, M_HEADS, HEAD_DIM), _F32),
        jax.ShapeDtypeStruct((nb, 1, LANES), _F32),
        jax.ShapeDtypeStruct((nb, CONV_K - 1, CONV_WIDTH), _F32),
    )
    out_specs = (
        rows_spec(lambda s: jnp.maximum(s - 1, 0)),
        state_spec(M_HEADS, HEAD_DIM, HEAD_DIM),
        state_spec(M_HEADS, HEAD_DIM),
        state_spec(1, LANES),
        state_spec(CONV_K - 1, CONV_WIDTH),
    )
    scratch = [
        pltpu.VMEM((tb, D_MODEL), _BF),
        pltpu.VMEM((tb, D_MODEL), _F32),
    ] + _sequence_scratch(tb)
    return pl.pallas_call(
        functools.partial(_prompt_body, tb=tb, nt=nt, n_blocks=n_blocks),
        grid=(n_blocks + 1,),
        in_specs=in_specs,
        out_specs=out_specs,
        out_shape=out_shape,
        scratch_shapes=scratch,
        compiler_params=pltpu.CompilerParams(
            dimension_semantics=("arbitrary",), vmem_limit_bytes=56 * 1024 * 1024),
        name="prompt_layer",
    )(x, x, w3, wg, wo, ng, fgain, mhg, cw, big, bfg, caug0, m0, ut0)


def _sample_body(x_ref, w3_ref, wg_ref, wo_ref, ng_ref, fgain_ref, mhg_ref, cw_ref, big_ref, bfg_ref,
                 sconv_ref, m_ref, n_ref, c_ref,
                 y_ref, conv_out_ref, c_out_ref, n_out_ref, m_out_ref,
                 p_scr, ni_scr, sc_scr, ycat_scr):
    @pl.when(pl.program_id(0) == 0)
    def _():
        sc_scr[...] = jnp.zeros_like(sc_scr)

    x = x_ref[...]
    xn = _rms_rows(x, ng_ref[...]).astype(_BF)
    for t in range(N_TILES):
        p_scr[:, _tile_cols(t)] = _dot(xn, w3_ref[t])
    g_raw = _dot(xn, wg_ref[...])

    row = lax.broadcasted_iota(jnp.int32, (CHUNK, CHUNK), 0)
    col = lax.broadcasted_iota(jnp.int32, (CHUNK, CHUNK), 1)
    pos = row % DEC_SEQ
    grp = row // DEC_SEQ
    causal = (grp == (col // DEC_SEQ)) & (col <= row)
    grow = lax.broadcasted_iota(jnp.int32, (SEQ_PER_STEP, CHUNK), 0)
    gcol = lax.broadcasted_iota(jnp.int32, (SEQ_PER_STEP, CHUNK), 1)
    pick_last = (gcol == grow * DEC_SEQ + (DEC_SEQ - 1)).astype(_F32)
    seq_sum = (gcol // DEC_SEQ == grow).astype(_BF)
    erow = lax.broadcasted_iota(jnp.int32, (CHUNK, SEQ_PER_STEP), 0)
    ecol = lax.broadcasted_iota(jnp.int32, (CHUNK, SEQ_PER_STEP), 1)
    own_seq = erow // DEC_SEQ == ecol
    ones_blk = jnp.ones((CHUNK, LANES), _BF)

    def rows_from_seqs(v16):
        return jnp.broadcast_to(v16[:, None, :], (SEQ_PER_STEP, DEC_SEQ, LANES)).reshape(CHUNK, LANES)

    def last_row_of_seq(v):
        v3 = v.reshape(SEQ_PER_STEP, DEC_SEQ, LANES)
        return jnp.broadcast_to(v3[:, DEC_SEQ - 1:DEC_SEQ, :], v3.shape).reshape(CHUNK, LANES)

    u = p_scr[:, OFF_GC:OFF_GC + CONV_WIDTH] * p_scr[:, OFF_HC:OFF_HC + CONV_WIDTH]
    for g in range(SEQ_PER_STEP):
        sc_scr[g * DEC_SEQ:g * DEC_SEQ + CONV_K - 1, :] = sconv_ref[g]
        conv_out_ref[g] = u[(g + 1) * DEC_SEQ - (CONV_K - 1):(g + 1) * DEC_SEQ, :]
    sc = sc_scr[...]
    posw = lax.broadcasted_iota(jnp.int32, (CHUNK, CONV_WIDTH), 0) % DEC_SEQ
    u_m1 = jnp.where(posw == 0, pltpu.roll(sc, CHUNK - 1, 0), pltpu.roll(u, 1, 0))
    u_m2 = jnp.where(posw < 2, sc, pltpu.roll(u, 2, 0))
    y_conv = _conv_out(p_scr[:, OFF_GB:OFF_GB + CONV_WIDTH], p_scr[:, OFF_ZC:OFF_ZC + CONV_WIDTH],
                       u, u_m1, u_m2, cw_ref)
    ycat_scr[:, 0:CONV_WIDTH] = y_conv.astype(_BF)

    ig, lf = _gate_columns(g_raw, big_ref[...], bfg_ref[...])
    b = _cumsum_rows(lf, pos, DEC_SEQ)
    a = ig - b
    m_prev = rows_from_seqs(m_ref[...])
    big_m = jnp.maximum(m_prev, _cummax_rows(a, pos, DEC_SEQ))
    m_t = b + big_m
    inter = jnp.exp(m_prev - big_m)
    emt = jnp.exp(-m_t)
    m_last = last_row_of_seq(big_m)
    w = jnp.exp(a - m_last)
    decay = jnp.exp(m_prev - m_last)
    m_out_ref[...] = _dot_exact(pick_last, m_t)
    dec16 = _dot_exact(pick_last, decay)
    a_t = a.T

    heads = []
    for h in range(M_HEADS):
        hs = h * HEAD_DIM
        q = p_scr[:, OFF_Q + hs:OFF_Q + hs + HEAD_DIM]
        k = p_scr[:, OFF_K + hs:OFF_K + hs + HEAD_DIM] * K_SCALE
        v = p_scr[:, OFF_V + hs:OFF_V + hs + HEAD_DIM]
        q_bf = q.astype(_BF)
        a_b = jnp.broadcast_to(a_t[h:h + 1, :], (CHUNK, CHUNK))
        m_b = jnp.broadcast_to(big_m[:, h:h + 1], (CHUNK, CHUNK))
        d = jnp.exp(jnp.where(causal, a_b - m_b, NEG_INF))
        w_b = jnp.broadcast_to(w[:, h:h + 1], (CHUNK, LANES))
        heads.append(dict(q_bf=q_bf, k=k, v=v, d=d, w_b=w_b, qk=_dot_nt(q_bf, k.astype(_BF))))

    for h in range(M_HEADS):
        hs = h * HEAD_DIM
        st = heads[h]
        wv = st["w_b"] * st["v"]
        k_t_bf = st["k"].T.astype(_BF)
        dec_b = jnp.broadcast_to(decay[:, h:h + 1], (CHUNK, HEAD_DIM))

        for g in range(0, SEQ_PER_STEP, 2):
            pair_rows = slice(g * DEC_SEQ, (g + 2) * DEC_SEQ)
            c0, c1 = c_ref[g, h], c_ref[g + 1, h]
            qc = _dot(st["q_bf"][pair_rows, :], jnp.concatenate([c0.astype(_BF), c1.astype(_BF)], axis=1))
            ni_scr[g * DEC_SEQ:(g + 1) * DEC_SEQ, hs:hs + HEAD_DIM] = qc[0:DEC_SEQ, 0:HEAD_DIM]
            ni_scr[(g + 1) * DEC_SEQ:(g + 2) * DEC_SEQ, hs:hs + HEAD_DIM] = qc[DEC_SEQ:, HEAD_DIM:]
            wv_pair = jnp.concatenate([jnp.where(grp == g, wv, 0.0).astype(_BF),
                                       jnp.where(grp == g + 1, wv, 0.0).astype(_BF)], axis=1)
            upd = _dot(k_t_bf, wv_pair)
            dec0 = jnp.broadcast_to(dec_b[g * DEC_SEQ:g * DEC_SEQ + 1, :], (HEAD_DIM, HEAD_DIM))
            dec1 = jnp.broadcast_to(dec_b[(g + 1) * DEC_SEQ:(g + 1) * DEC_SEQ + 1, :], (HEAD_DIM, HEAD_DIM))
            c_out_ref[g, h] = dec0 * c0 + upd[:, 0:HEAD_DIM]
            c_out_ref[g + 1, h] = dec1 * c1 + upd[:, HEAD_DIM:]

        s = st["qk"] * st["d"]
        sv = _dot(s.astype(_BF), jnp.concatenate([st["v"].astype(_BF), ones_blk], axis=1))
        n_old = n_ref[:, h, :]
        num_inter = ni_scr[:, hs:hs + HEAD_DIM]
        qn = _dot_nt(st["q_bf"], n_old.astype(_BF))
        den_inter = jnp.sum(jnp.where(own_seq, qn, 0.0), axis=-1, keepdims=True)
        inter_b = jnp.broadcast_to(inter[:, h:h + 1], (CHUNK, LANES))
        num = sv[:, :LANES] + inter_b * num_inter
        den = sv[:, LANES:] + inter_b * den_inter
        emt_b = jnp.broadcast_to(emt[:, h:h + 1], (CHUNK, LANES))
        ni_scr[:, hs:hs + HEAD_DIM] = _head_norm(num, den, emt_b, mhg_ref[:, hs:hs + HEAD_DIM])
        n_upd = _dot(seq_sum, (st["w_b"] * st["k"]).astype(_BF))
        dec16_b = jnp.broadcast_to(dec16[:, h:h + 1], (SEQ_PER_STEP, HEAD_DIM))
        n_out_ref[:, h, :] = dec16_b * n_old + n_upd

    ycat_scr[:, CONV_WIDTH:CONV_WIDTH + M_WIDTH] = _gated_heads(ni_scr[...], p_scr).astype(_BF)
    out = _dot(ycat_scr[...], wo_ref[...]) + x
    y_ref[...] = _rms_rows(out, fgain_ref[...])


def _sample_call(xs, w3, wg, wo, small, fgain, state_conv, m_pad, n_state, c_state):
    ng, mhg, cw, big, bfg = small
    rows = xs.shape[0]
    nseq = c_state.shape[0]
    steps = rows // CHUNK

    def const(shape):
        nd = len(shape)
        return pl.BlockSpec(shape, lambda i: (0,) * nd)

    def seq_spec(*tail):
        nd = len(tail)
        return pl.BlockSpec((SEQ_PER_STEP,) + tail, lambda i: (i,) + (0,) * nd)

    x_spec = pl.BlockSpec((CHUNK, D_MODEL), lambda i: (i, 0))
    conv_spec = seq_spec(CONV_K - 1, CONV_WIDTH)
    c_spec = seq_spec(M_HEADS, HEAD_DIM, HEAD_DIM)
    n_spec = seq_spec(M_HEADS, HEAD_DIM)
    m_spec = seq_spec(LANES)
    in_specs = [
        x_spec,
        const(w3.shape), const(wg.shape), const(wo.shape), const(ng.shape), const(fgain.shape),
        const(mhg.shape), const(cw.shape), const(big.shape), const(bfg.shape),
        conv_spec, m_spec, n_spec, c_spec,
    ]
    out_shape = (
        jax.ShapeDtypeStruct((rows, D_MODEL), _F32),
        jax.ShapeDtypeStruct(state_conv.shape, _F32),
        jax.ShapeDtypeStruct(c_state.shape, _F32),
        jax.ShapeDtypeStruct(n_state.shape, _F32),
        jax.ShapeDtypeStruct((nseq, LANES), _F32),
    )
    out_specs = (x_spec, conv_spec, c_spec, n_spec, m_spec)
    scratch = [
        pltpu.VMEM((CHUNK, MAIN_WIDTH), _F32),
        pltpu.VMEM((CHUNK, M_WIDTH), _F32),
        pltpu.VMEM((CHUNK, CONV_WIDTH), _F32),
        pltpu.VMEM((CHUNK, D_MODEL), _BF),
    ]
    return pl.pallas_call(
        _sample_body,
        grid=(steps,),
        in_specs=in_specs,
        out_specs=out_specs,
        out_shape=out_shape,
        scratch_shapes=scratch,
        compiler_params=pltpu.CompilerParams(
            dimension_semantics=("arbitrary",), vmem_limit_bytes=56 * 1024 * 1024),
        name="sample_layer",
    )(xs, w3, wg, wo, ng, fgain, mhg, cw, big, bfg, state_conv, m_pad, n_state, c_state)


def _pad_lanes(v):
    return jnp.pad(v.astype(_F32), (0, LANES - v.shape[0])).reshape(1, LANES)


def kernel(x_prompt, x_sample, state_conv, state_mlstm_c, state_mlstm_n, state_mlstm_m, meta_tokens, norm_gain,
           w_in, conv_w, b_igate, b_fgate, mh_norm_gain, w_out, final_norm_gain):
    nseq, dec_seq, _ = x_sample.shape
    assert dec_seq == DEC_SEQ and meta_tokens.shape[0] == N_META

    w_t = w_in.T
    wgt = jnp.pad(w_t[MAIN_WIDTH:], ((0, LANES - 2 * M_HEADS), (0, 0)))
    small = (norm_gain.reshape(1, D_MODEL), mh_norm_gain.reshape(1, M_WIDTH), conv_w,
             _pad_lanes(b_igate), _pad_lanes(b_fgate))
    fgain = final_norm_gain.reshape(1, D_MODEL)
    wo = w_out.astype(_BF)

    x_meta = jnp.concatenate([jnp.zeros((CHUNK - N_META, D_MODEL), _F32), meta_tokens.astype(_F32)], axis=0)
    w3, wg, caug_meta, m_meta, ut_meta = _prep_call(w_t, x_meta, wgt, small)

    y_prompt, c_p, n_p, m_p, conv_p = _prompt_call(
        x_prompt, w3, wg, wo, small, fgain, caug_meta, m_meta, ut_meta, tb=512)
    m_pp = m_p[:, 0, :M_HEADS]

    xs = x_sample.reshape(nseq * DEC_SEQ, D_MODEL)
    m_pad = jnp.pad(state_mlstm_m, ((0, 0), (0, LANES - M_HEADS)))
    ys, conv_s, c_s, n_s, m_s = _sample_call(xs, w3, wg, wo, small, fgain, state_conv, m_pad, state_mlstm_n,
                                             state_mlstm_c)
    y_sample = ys.reshape(nseq, DEC_SEQ, D_MODEL)
    m_ss = m_s[:, :M_HEADS]

    return (y_prompt, y_sample, conv_p, c_p, n_p, m_pp, conv_s, c_s, n_s, m_ss)
```

```python
import functools

import jax
import jax.numpy as jnp
from jax import lax
from jax.experimental import pallas as pl
from jax.experimental.pallas import tpu as pltpu

D_MODEL = 1024
CONV_WIDTH = 512
M_WIDTH = 512
M_HEADS = 4
HEAD_DIM = 128
N_META = 16
CONV_K = 3
EPS = 1e-6
MAIN_WIDTH = 4 * CONV_WIDTH + 5 * M_WIDTH
LANES = 128
SUBLANES = 8
BF16_TILE_ROWS = 2 * SUBLANES
MXU_COLS = 256
N_TILES = MAIN_WIDTH // MXU_COLS
PREP_TILES = 3
CHUNK = 128
DEC_SEQ = 8
SEQ_PER_STEP = CHUNK // DEC_SEQ
NEG_INF = float("-inf")
K_SCALE = HEAD_DIM ** -0.5

OFF_GB, OFF_GC, OFF_HC, OFF_ZC = 0, 512, 1024, 1536
OFF_Q, OFF_K, OFF_V, OFF_O, OFF_ZM = 2048, 2560, 3072, 3584, 4096
QKV_TILES = tuple(range(OFF_Q // MXU_COLS, OFF_O // MXU_COLS))
REST_TILES = tuple(t for t in range(N_TILES) if t not in QKV_TILES)

_HI = lax.Precision.HIGHEST
_BF = jnp.bfloat16
_F32 = jnp.float32


def _dot(a, b):
    return jnp.dot(a, b, preferred_element_type=_F32)


def _dot_exact(a, b):
    return jnp.dot(a, b, precision=_HI, preferred_element_type=_F32)


def _dot_nt(a, b):
    return lax.dot_general(a, b, (((1,), (1,)), ((), ())), preferred_element_type=_F32)


def _rms_rows(x, gain_row):
    return x * lax.rsqrt(jnp.mean(x * x, axis=-1, keepdims=True) + EPS) * gain_row


def _sigmoid(x):
    return 1.0 / (1.0 + jnp.exp(-x))


def _silu(x):
    return x * _sigmoid(x)


def _tile_cols(t):
    return slice(t * MXU_COLS, (t + 1) * MXU_COLS)


def _cummax_rows(x, pos, period):
    s = 1
    while s < period:
        shifted = pltpu.roll(x, s, 0)
        x = jnp.maximum(x, jnp.where(pos >= s, shifted, NEG_INF))
        s *= 2
    return x


def _cumsum_rows(x, pos, period):
    s = 1
    while s < period:
        x = x + jnp.where(pos >= s, pltpu.roll(x, s, 0), 0.0)
        s *= 2
    return x


def _gate_columns(g_raw, big_row, bfg_row):
    ig = g_raw + big_row
    lf = jax.nn.log_sigmoid(pltpu.roll(g_raw, LANES - M_HEADS, 1) + bfg_row)
    return ig, lf


def _head_norm(num, den, emt_b, gain_row):
    hh = num / jnp.maximum(jnp.abs(den), emt_b)
    return hh * lax.rsqrt(jnp.mean(hh * hh, axis=-1, keepdims=True) + EPS) * gain_row


def _conv_out(gb, zc, u, u_m1, u_m2, cw_ref):
    yc = cw_ref[0:1, :] * u_m2 + cw_ref[1:2, :] * u_m1 + cw_ref[2:3, :] * u
    return gb * yc * _silu(zc)


def _gated_heads(hn, p_scr):
    return hn * _sigmoid(p_scr[:, OFF_O:OFF_O + M_WIDTH]) * _silu(p_scr[:, OFF_ZM:OFF_ZM + M_WIDTH])


def _sequence_block(xn, project_tile, emit_rows, p_scr, g_scr, hn_scr, ycat_scr, ubuf, caug, m_scr,
                    wg_ref, mhg_ref, cw_ref, big_ref, bfg_ref, *, tb, n_pad, side_work=None):
    side_work = side_work or {}
    anchors = []
    g_scr[...] = _dot(xn, wg_ref[...])
    late_tiles = []
    if project_tile is not None:
        for t in QKV_TILES:
            project_tile(t)
        early = 2
        for t in REST_TILES[:early]:
            project_tile(t)
        late_tiles = list(REST_TILES[early:])

    row = lax.broadcasted_iota(jnp.int32, (CHUNK, CHUNK), 0)
    col = lax.broadcasted_iota(jnp.int32, (CHUNK, CHUNK), 1)
    causal = col <= row
    ones_blk = jnp.ones((CHUNK, LANES), _BF)

    n_chunks = tb // CHUNK
    gates = []
    m_prev = m_scr[...]
    for c in range(n_chunks):
        rows = slice(c * CHUNK, (c + 1) * CHUNK)
        ig, lf = _gate_columns(g_scr[rows, :], big_ref[...], bfg_ref[...])
        if n_pad:
            ig = jnp.where(row >= n_pad, ig, NEG_INF)
            lf = jnp.where(row >= n_pad, lf, 0.0)
        b = _cumsum_rows(lf, row, CHUNK)
        a = ig - b
        big_m = jnp.maximum(m_prev, _cummax_rows(a, row, CHUNK))
        m_t = b + big_m
        m_last = big_m[CHUNK - 1:CHUNK, :]
        gates.append(dict(a_t=a.T, big_m=big_m, inter=jnp.exp(m_prev - big_m), emt=jnp.exp(-m_t),
                          w=jnp.exp(a - m_last), decay=jnp.exp(m_prev - m_last)))
        m_prev = m_t[CHUNK - 1:CHUNK, :]
    m_scr[...] = m_prev

    def stage_a(c, h):
        rows = slice(c * CHUNK, (c + 1) * CHUNK)
        hs = h * HEAD_DIM
        g = gates[c]
        q = p_scr[rows, OFF_Q + hs:OFF_Q + hs + HEAD_DIM]
        k = p_scr[rows, OFF_K + hs:OFF_K + hs + HEAD_DIM] * K_SCALE
        a_b = jnp.broadcast_to(g["a_t"][h:h + 1, :], (CHUNK, CHUNK))
        m_b = jnp.broadcast_to(g["big_m"][:, h:h + 1], (CHUNK, CHUNK))
        d = jnp.exp(jnp.where(causal, a_b - m_b, NEG_INF))
        q_inter = (jnp.broadcast_to(g["inter"][:, h:h + 1], (CHUNK, HEAD_DIM)) * q).astype(_BF)
        return dict(qk=_dot_nt(q.astype(_BF), k.astype(_BF)), d=d, q_inter=q_inter, k=k)

    def stage_b(c, h, st):
        rows = slice(c * CHUNK, (c + 1) * CHUNK)
        hs = h * HEAD_DIM
        g = gates[c]
        v = p_scr[rows, OFF_V + hs:OFF_V + hs + HEAD_DIM]
        s = st["qk"] * st["d"]
        lhs = jnp.concatenate([s.astype(_BF), st["q_inter"]], axis=1)
        rhs = jnp.concatenate([jnp.concatenate([v.astype(_BF), ones_blk], axis=1), caug[h].astype(_BF)], axis=0)
        tot = _dot(lhs, rhs)
        emt_b = jnp.broadcast_to(g["emt"][:, h:h + 1], (CHUNK, LANES))
        hn_scr[rows, hs:hs + HEAD_DIM] = _head_norm(tot[:, :LANES], tot[:, LANES:], emt_b,
                                                    mhg_ref[:, hs:hs + HEAD_DIM])
        w_b = jnp.broadcast_to(g["w"][:, h:h + 1], (CHUNK, LANES))
        wv = jnp.concatenate([(w_b * v).astype(_BF), w_b.astype(_BF)], axis=1)
        upd = _dot(st["k"].T.astype(_BF), wv)
        decay_b = jnp.broadcast_to(g["decay"][:, h:h + 1], (HEAD_DIM, 2 * LANES))
        caug[h] = decay_b * caug[h] + upd

    def finish_rows(r0, r1):
        rr = slice(r0, r1)
        n = r1 - r0
        u = p_scr[rr, OFF_GC:OFF_GC + CONV_WIDTH] * p_scr[rr, OFF_HC:OFF_HC + CONV_WIDTH]
        ubuf[SUBLANES + r0:SUBLANES + r1, :] = u
        u_m1 = ubuf[SUBLANES - 1 + r0:SUBLANES - 1 + r0 + n, :]
        u_m2 = ubuf[SUBLANES - 2 + r0:SUBLANES - 2 + r0 + n, :]
        y_conv = _conv_out(p_scr[rr, OFF_GB:OFF_GB + CONV_WIDTH], p_scr[rr, OFF_ZC:OFF_ZC + CONV_WIDTH],
                           u, u_m1, u_m2, cw_ref)
        ycat_scr[rr, 0:CONV_WIDTH] = y_conv.astype(_BF)
        y_m = (hn_scr[rr, :] * _sigmoid(p_scr[rr, OFF_O:OFF_O + M_WIDTH])
               * _silu(p_scr[rr, OFF_ZM:OFF_ZM + M_WIDTH]))
        ycat_scr[rr, CONV_WIDTH:CONV_WIDTH + M_WIDTH] = y_m.astype(_BF)
        while anchors:
            tile = ycat_scr[0:BF16_TILE_ROWS, 0:LANES].astype(_F32) + anchors.pop()
            ycat_scr[0:BF16_TILE_ROWS, 0:LANES] = tile.astype(_BF)
        if emit_rows is not None:
            emit_rows(r0, r1)

    units = [(c, h) for c in range(n_chunks) for h in range(M_HEADS)]
    lag = min(3, len(units))
    n_groups = 1
    group_units = len(units) // n_groups
    assert len(late_tiles) <= group_units + lag
    pending = {}
    for t in range(len(units) + lag):
        if t < len(units):
            pending[t] = stage_a(*units[t])
        if t < len(late_tiles):
            project_tile(late_tiles[t])
        for fn in side_work.get(t, ()):
            zero = fn()
            if zero is not None:
                anchors.append(zero)
        if t >= lag:
            stage_b(*units[t - lag], pending.pop(t - lag))
            done = t - lag + 1
            if done % group_units == 0:
                gi = done // group_units
                finish_rows((gi - 1) * tb // n_groups, gi * tb // n_groups)
    ubuf[0:SUBLANES, :] = ubuf[tb:tb + SUBLANES, :]


def _sequence_scratch(tb):
    return [
        pltpu.VMEM((tb, MAIN_WIDTH), _F32),
        pltpu.VMEM((tb, LANES), _F32),
        pltpu.VMEM((tb, M_WIDTH), _F32),
        pltpu.VMEM((tb, D_MODEL), _BF),
        pltpu.VMEM((tb + 2 * SUBLANES, CONV_WIDTH), _F32),
        pltpu.VMEM((M_HEADS, HEAD_DIM, 2 * LANES), _F32),
        pltpu.VMEM((1, LANES), _F32),
    ]


def _prep_body(wt_ref, x_ref, wgt_ref, ng_ref, mhg_ref, cw_ref, big_ref, bfg_ref,
               w3_ref, wg_ref, caug_out_ref, m_out_ref, ut_out_ref,
               xn_scr, p3_scr, p_scr, g_scr, hn_scr, ycat_scr, ubuf, caug, m_scr):
    t = pl.program_id(0)

    @pl.when(t == 0)
    def _():
        xn_scr[...] = _rms_rows(x_ref[...], ng_ref[...]).astype(_BF)
        wg_ref[...] = wgt_ref[...].T.astype(_BF)

    for j in range(PREP_TILES):
        tile = wt_ref[j * MXU_COLS:(j + 1) * MXU_COLS, :].T.astype(_BF)
        w3_ref[j] = tile
        p3_scr[t * PREP_TILES + j] = _dot(xn_scr[...], tile)

    @pl.when(t == N_TILES // PREP_TILES - 1)
    def _():
        for j in range(N_TILES):
            p_scr[:, _tile_cols(j)] = p3_scr[j]
        caug[...] = jnp.zeros_like(caug)
        m_scr[...] = jnp.zeros_like(m_scr)
        ubuf[0:SUBLANES, :] = jnp.zeros((SUBLANES, CONV_WIDTH), _F32)
        _sequence_block(xn_scr[...], None, None, p_scr, g_scr, hn_scr, ycat_scr, ubuf, caug, m_scr,
                        wg_ref, mhg_ref, cw_ref, big_ref, bfg_ref, tb=CHUNK, n_pad=CHUNK - N_META)
        caug_out_ref[...] = caug[...]
        m_out_ref[...] = m_scr[...]
        ut_out_ref[...] = ubuf[0:SUBLANES, :]


def _prep_call(w_t, x_meta, wgt, small):
    ng, mhg, cw, big, bfg = small

    def const(shape):
        nd = len(shape)
        return pl.BlockSpec(shape, lambda t: (0,) * nd)

    in_specs = [
        pl.BlockSpec((PREP_TILES * MXU_COLS, D_MODEL), lambda t: (t, 0)),
        const(x_meta.shape), const(wgt.shape), const(ng.shape), const(mhg.shape), const(cw.shape),
        const(big.shape), const(bfg.shape),
    ]
    out_shape = (
        jax.ShapeDtypeStruct((N_TILES, D_MODEL, MXU_COLS), _BF),
        jax.ShapeDtypeStruct((D_MODEL, LANES), _BF),
        jax.ShapeDtypeStruct((M_HEADS, HEAD_DIM, 2 * LANES), _F32),
        jax.ShapeDtypeStruct((1, LANES), _F32),
        jax.ShapeDtypeStruct((SUBLANES, CONV_WIDTH), _F32),
    )
    out_specs = (pl.BlockSpec((PREP_TILES, D_MODEL, MXU_COLS), lambda t: (t, 0, 0)),) + tuple(
        const(s.shape) for s in out_shape[1:])
    scratch = [
        pltpu.VMEM((CHUNK, D_MODEL), _BF),
        pltpu.VMEM((N_TILES, CHUNK, MXU_COLS), _F32),
    ] + _sequence_scratch(CHUNK)
    return pl.pallas_call(
        _prep_body,
        grid=(N_TILES // PREP_TILES,),
        in_specs=in_specs,
        out_specs=out_specs,
        out_shape=out_shape,
        scratch_shapes=scratch,
        compiler_params=pltpu.CompilerParams(
            dimension_semantics=("arbitrary",), vmem_limit_bytes=56 * 1024 * 1024),
        name="prep_meta",
    )(w_t, x_meta, wgt, ng, mhg, cw, big, bfg)


def _prompt_body(x_ref, xnext_ref, w3_ref, wg_ref, wo_ref, ng_ref, fgain_ref, mhg_ref, cw_ref, big_ref, bfg_ref,
                 caug0_ref, m0_ref, ut0_ref,
                 y_ref, c_out_ref, n_out_ref, m_out_ref, conv_out_ref,
                 xn_scr, out_scr, p_scr, g_scr, hn_scr, ycat_scr, ubuf, caug, m_scr, *, tb, nt, n_blocks):
    s = pl.program_id(0)
    i = s % nt
    live = s < n_blocks

    @pl.when(s == 0)
    def _():
        xn_scr[...] = _rms_rows(x_ref[0], ng_ref[...]).astype(_BF)
        out_scr[...] = jnp.zeros_like(out_scr)

    @pl.when(live & (i == 0))
    def _():
        caug[...] = caug0_ref[...]
        m_scr[...] = m0_ref[...]
        ubuf[0:SUBLANES, :] = ut0_ref[...]

    @pl.when(live)
    def _():
        xn = xn_scr[...]

        def project_tile(t):
            p_scr[:, _tile_cols(t)] = _dot(xn, w3_ref[t])

        def emit_rows(r0, r1):
            out_scr[r0:r1, :] = _dot(ycat_scr[r0:r1, :], wo_ref[...]) + x_ref[0, r0:r1, :]

        def norm_previous(r0, r1):
            def fn():
                y_ref[0, r0:r1, :] = _rms_rows(out_scr[r0:r1, :], fgain_ref[...])
            return fn

        def norm_next(r0, r1):
            def fn():
                xn_next = _rms_rows(xnext_ref[0, r0:r1, :], ng_ref[...])
                xn_scr[r0:r1, :] = xn_next.astype(_BF)
                bits = pltpu.bitcast(xn_next[0:BF16_TILE_ROWS, 0:LANES], jnp.uint32)
                return ((bits >> 16) >> 16).astype(_F32)
            return fn

        q4 = tb // 4
        side_work = {2 + 2 * j: [norm_previous(j * q4, (j + 1) * q4)] for j in range(4)}
        side_work.update({10 + 2 * j: [norm_next(j * q4, (j + 1) * q4)] for j in range(4)})
        _sequence_block(xn, project_tile, emit_rows, p_scr, g_scr, hn_scr, ycat_scr, ubuf, caug, m_scr,
                        wg_ref, mhg_ref, cw_ref, big_ref, bfg_ref, tb=tb, n_pad=0, side_work=side_work)

    @pl.when(s == n_blocks)
    def _():
        y_ref[0] = _rms_rows(out_scr[...], fgain_ref[...])

    @pl.when(live & (i == nt - 1))
    def _():
        m_out_ref[0] = m_scr[...]
        conv_out_ref[0] = ubuf[SUBLANES - (CONV_K - 1):SUBLANES, :]
        for h in range(M_HEADS):
            c_out_ref[0, h] = caug[h, :, 0:HEAD_DIM]
            n_out_ref[0, h:h + 1, :] = caug[h, :, LANES:2 * LANES].T[0:1, :]


def _prompt_call(x, w3, wg, wo, small, fgain, caug0, m0, ut0, *, tb):
    ng, mhg, cw, big, bfg = small
    nb, t, _ = x.shape
    nt = t // tb
    n_blocks = nb * nt

    def const(shape):
        nd = len(shape)
        return pl.BlockSpec(shape, lambda s: (0,) * nd)

    def block_of(s):
        return jnp.minimum(s, n_blocks - 1)

    def rows_spec(step_to_block):
        return pl.BlockSpec((1, tb, D_MODEL), lambda s: (step_to_block(s) // nt, step_to_block(s) % nt, 0))

    def state_spec(*tail):
        nd = len(tail)
        return pl.BlockSpec((1,) + tail, lambda s: (block_of(s) // nt,) + (0,) * nd)

    in_specs = [
        rows_spec(block_of),
        rows_spec(lambda s: block_of(s + 1)),
        const(w3.shape), const(wg.shape), const(wo.shape), const(ng.shape), const(fgain.shape),
        const(mhg.shape), const(cw.shape), const(big.shape), const(bfg.shape),
        const(caug0.shape), const(m0.shape), const(ut0.shape),
    ]
    out_shape = (
        jax.ShapeDtypeStruct((nb, t, D_MODEL), _F32),
        jax.ShapeDtypeStruct((nb, M_HEADS, HEAD_DIM, HEAD_DIM), _F32),
        jax.ShapeDtypeStruct((nb, M_HEADS, HEAD_DIM), _F32),
        jax.ShapeDtypeStruct((nb, 1, LANES), _F32),
        jax.ShapeDtypeStruct((nb, CONV_K - 1, CONV_WIDTH), _F32),
    )
    out_specs = (
        rows_spec(lambda s: jnp.maximum(s - 1, 0)),
        state_spec(M_HEADS, HEAD_DIM, HEAD_DIM),
        state_spec(M_HEADS, HEAD_DIM),
        state_spec(1, LANES),
        state_spec(CONV_K - 1, CONV_WIDTH),
    )
    scratch = [
        pltpu.VMEM((tb, D_MODEL), _BF),
        pltpu.VMEM((tb, D_MODEL), _F32),
    ] + _sequence_scratch(tb)
    return pl.pallas_call(
        functools.partial(_prompt_body, tb=tb, nt=nt, n_blocks=n_blocks),
        grid=(n_blocks + 1,),
        in_specs=in_specs,
        out_specs=out_specs,
        out_shape=out_shape,
        scratch_shapes=scratch,
        compiler_params=pltpu.CompilerParams(
            dimension_semantics=("arbitrary",), vmem_limit_bytes=56 * 1024 * 1024),
        name="prompt_layer",
    )(x, x, w3, wg, wo, ng, fgain, mhg, cw, big, bfg, caug0, m0, ut0)


def _sample_body(x_ref, w3_ref, wg_ref, wo_ref, ng_ref, fgain_ref, mhg_ref, cw_ref, big_ref, bfg_ref,
                 sconv_ref, m_ref, n_ref, c_ref,
                 y_ref, conv_out_ref, c_out_ref, n_out_ref, m_out_ref,
                 p_scr, ni_scr, sc_scr, ycat_scr):
    @pl.when(pl.program_id(0) == 0)
    def _():
        sc_scr[...] = jnp.zeros_like(sc_scr)

    x = x_ref[...]
    xn = _rms_rows(x, ng_ref[...]).astype(_BF)
    g_raw = _dot(xn, wg_ref[...])
    for t in QKV_TILES:
        p_scr[:, _tile_cols(t)] = _dot(xn, w3_ref[t])

    row = lax.broadcasted_iota(jnp.int32, (CHUNK, CHUNK), 0)
    col = lax.broadcasted_iota(jnp.int32, (CHUNK, CHUNK), 1)
    pos = row % DEC_SEQ
    grp = row // DEC_SEQ
    causal = (grp == (col // DEC_SEQ)) & (col <= row)
    grow = lax.broadcasted_iota(jnp.int32, (SEQ_PER_STEP, CHUNK), 0)
    gcol = lax.broadcasted_iota(jnp.int32, (SEQ_PER_STEP, CHUNK), 1)
    pick_last = (gcol == grow * DEC_SEQ + (DEC_SEQ - 1)).astype(_F32)
    seq_sum = (gcol // DEC_SEQ == grow).astype(_BF)
    erow = lax.broadcasted_iota(jnp.int32, (CHUNK, SEQ_PER_STEP), 0)
    ecol = lax.broadcasted_iota(jnp.int32, (CHUNK, SEQ_PER_STEP), 1)
    own_seq = erow // DEC_SEQ == ecol
    ones_blk = jnp.ones((CHUNK, LANES), _BF)

    def rows_from_seqs(v16):
        return jnp.broadcast_to(v16[:, None, :], (SEQ_PER_STEP, DEC_SEQ, LANES)).reshape(CHUNK, LANES)

    def last_row_of_seq(v):
        v3 = v.reshape(SEQ_PER_STEP, DEC_SEQ, LANES)
        return jnp.broadcast_to(v3[:, DEC_SEQ - 1:DEC_SEQ, :], v3.shape).reshape(CHUNK, LANES)

    ig, lf = _gate_columns(g_raw, big_ref[...], bfg_ref[...])
    b = _cumsum_rows(lf, pos, DEC_SEQ)
    a = ig - b
    m_prev = rows_from_seqs(m_ref[...])
    big_m = jnp.maximum(m_prev, _cummax_rows(a, pos, DEC_SEQ))
    m_t = b + big_m
    inter = jnp.exp(m_prev - big_m)
    emt = jnp.exp(-m_t)
    m_last = last_row_of_seq(big_m)
    w = jnp.exp(a - m_last)
    decay = jnp.exp(m_prev - m_last)
    m_out_ref[...] = _dot_exact(pick_last, m_t)
    dec16 = _dot_exact(pick_last, decay)
    a_t = a.T

    for t in REST_TILES:
        p_scr[:, _tile_cols(t)] = _dot(xn, w3_ref[t])

    heads = []
    for h in range(M_HEADS):
        hs = h * HEAD_DIM
        q = p_scr[:, OFF_Q + hs:OFF_Q + hs + HEAD_DIM]
        k = p_scr[:, OFF_K + hs:OFF_K + hs + HEAD_DIM] * K_SCALE
        v = p_scr[:, OFF_V + hs:OFF_V + hs + HEAD_DIM]
        q_bf = q.astype(_BF)
        a_b = jnp.broadcast_to(a_t[h:h + 1, :], (CHUNK, CHUNK))
        m_b = jnp.broadcast_to(big_m[:, h:h + 1], (CHUNK, CHUNK))
        d = jnp.exp(jnp.where(causal, a_b - m_b, NEG_INF))
        w_b = jnp.broadcast_to(w[:, h:h + 1], (CHUNK, LANES))
        heads.append(dict(q_bf=q_bf, k=k, v=v, d=d, w_b=w_b, qk=_dot_nt(q_bf, k.astype(_BF))))

    u = p_scr[:, OFF_GC:OFF_GC + CONV_WIDTH] * p_scr[:, OFF_HC:OFF_HC + CONV_WIDTH]
    for g in range(SEQ_PER_STEP):
        sc_scr[g * DEC_SEQ:g * DEC_SEQ + CONV_K - 1, :] = sconv_ref[g]
        conv_out_ref[g] = u[(g + 1) * DEC_SEQ - (CONV_K - 1):(g + 1) * DEC_SEQ, :]
    sc = sc_scr[...]
    posw = lax.broadcasted_iota(jnp.int32, (CHUNK, CONV_WIDTH), 0) % DEC_SEQ
    u_m1 = jnp.where(posw == 0, pltpu.roll(sc, CHUNK - 1, 0), pltpu.roll(u, 1, 0))
    u_m2 = jnp.where(posw < 2, sc, pltpu.roll(u, 2, 0))
    y_conv = _conv_out(p_scr[:, OFF_GB:OFF_GB + CONV_WIDTH], p_scr[:, OFF_ZC:OFF_ZC + CONV_WIDTH],
                       u, u_m1, u_m2, cw_ref)
    ycat_scr[:, 0:CONV_WIDTH] = y_conv.astype(_BF)

    for h in range(M_HEADS):
        hs = h * HEAD_DIM
        st = heads[h]
        wv = st["w_b"] * st["v"]
        k_t_bf = st["k"].T.astype(_BF)
        dec_b = jnp.broadcast_to(decay[:, h:h + 1], (CHUNK, HEAD_DIM))

        for g in range(0, SEQ_PER_STEP, 2):
            pair_rows = slice(g * DEC_SEQ, (g + 2) * DEC_SEQ)
            c0, c1 = c_ref[g, h], c_ref[g + 1, h]
            qc = _dot(st["q_bf"][pair_rows, :], jnp.concatenate([c0.astype(_BF), c1.astype(_BF)], axis=1))
            ni_scr[g * DEC_SEQ:(g + 1) * DEC_SEQ, hs:hs + HEAD_DIM] = qc[0:DEC_SEQ, 0:HEAD_DIM]
            ni_scr[(g + 1) * DEC_SEQ:(g + 2) * DEC_SEQ, hs:hs + HEAD_DIM] = qc[DEC_SEQ:, HEAD_DIM:]
            wv_pair = jnp.concatenate([jnp.where(grp == g, wv, 0.0).astype(_BF),
                                       jnp.where(grp == g + 1, wv, 0.0).astype(_BF)], axis=1)
            upd = _dot(k_t_bf, wv_pair)
            dec0 = jnp.broadcast_to(dec_b[g * DEC_SEQ:g * DEC_SEQ + 1, :], (HEAD_DIM, HEAD_DIM))
            dec1 = jnp.broadcast_to(dec_b[(g + 1) * DEC_SEQ:(g + 1) * DEC_SEQ + 1, :], (HEAD_DIM, HEAD_DIM))
            c_out_ref[g, h] = dec0 * c0 + upd[:, 0:HEAD_DIM]
            c_out_ref[g + 1, h] = dec1 * c1 + upd[:, HEAD_DIM:]

        s = st["qk"] * st["d"]
        sv = _dot(s.astype(_BF), jnp.concatenate([st["v"].astype(_BF), ones_blk], axis=1))
        n_old = n_ref[:, h, :]
        num_inter = ni_scr[:, hs:hs + HEAD_DIM]
        qn = _dot_nt(st["q_bf"], n_old.astype(_BF))
        den_inter = jnp.sum(jnp.where(own_seq, qn, 0.0), axis=-1, keepdims=True)
        inter_b = jnp.broadcast_to(inter[:, h:h + 1], (CHUNK, LANES))
        num = sv[:, :LANES] + inter_b * num_inter
        den = sv[:, LANES:] + inter_b * den_inter
        emt_b = jnp.broadcast_to(emt[:, h:h + 1], (CHUNK, LANES))
        ni_scr[:, hs:hs + HEAD_DIM] = _head_norm(num, den, emt_b, mhg_ref[:, hs:hs + HEAD_DIM])
        n_upd = _dot(seq_sum, (st["w_b"] * st["k"]).astype(_BF))
        dec16_b = jnp.broadcast_to(dec16[:, h:h + 1], (SEQ_PER_STEP, HEAD_DIM))
        n_out_ref[:, h, :] = dec16_b * n_old + n_upd

    ycat_scr[:, CONV_WIDTH:CONV_WIDTH + M_WIDTH] = _gated_heads(ni_scr[...], p_scr).astype(_BF)
    out = _dot(ycat_scr[...], wo_ref[...]) + x
    y_ref[...] = _rms_rows(out, fgain_ref[...])


def _sample_call(xs, w3, wg, wo, small, fgain, state_conv, m_pad, n_state, c_state):
    ng, mhg, cw, big, bfg = small
    rows = xs.shape[0]
    nseq = c_state.shape[0]
    steps = rows // CHUNK

    def const(shape):
        nd = len(shape)
        return pl.BlockSpec(shape, lambda i: (0,) * nd)

    def seq_spec(*tail):
        nd = len(tail)
        return pl.BlockSpec((SEQ_PER_STEP,) + tail, lambda i: (i,) + (0,) * nd)

    x_spec = pl.BlockSpec((CHUNK, D_MODEL), lambda i: (i, 0))
    conv_spec = seq_spec(CONV_K - 1, CONV_WIDTH)
    c_spec = seq_spec(M_HEADS, HEAD_DIM, HEAD_DIM)
    n_spec = seq_spec(M_HEADS, HEAD_DIM)
    m_spec = seq_spec(LANES)
    in_specs = [
        x_spec,
        const(w3.shape), const(wg.shape), const(wo.shape), const(ng.shape), const(fgain.shape),
        const(mhg.shape), const(cw.shape), const(big.shape), const(bfg.shape),
        conv_spec, m_spec, n_spec, c_spec,
    ]
    out_shape = (
        jax.ShapeDtypeStruct((rows, D_MODEL), _F32),
        jax.ShapeDtypeStruct(state_conv.shape, _F32),
        jax.ShapeDtypeStruct(c_state.shape, _F32),
        jax.ShapeDtypeStruct(n_state.shape, _F32),
        jax.ShapeDtypeStruct((nseq, LANES), _F32),
    )
    out_specs = (x_spec, conv_spec, c_spec, n_spec, m_spec)
    scratch = [
        pltpu.VMEM((CHUNK, MAIN_WIDTH), _F32),
        pltpu.VMEM((CHUNK, M_WIDTH), _F32),
        pltpu.VMEM((CHUNK, CONV_WIDTH), _F32),
        pltpu.VMEM((CHUNK, D_MODEL), _BF),
    ]
    return pl.pallas_call(
        _sample_body,
        grid=(steps,),
        in_specs=in_specs,
        out_specs=out_specs,
        out_shape=out_shape,
        scratch_shapes=scratch,
        compiler_params=pltpu.CompilerParams(
            dimension_semantics=("arbitrary",), vmem_limit_bytes=56 * 1024 * 1024),
        name="sample_layer",
    )(xs, w3, wg, wo, ng, fgain, mhg, cw, big, bfg, state_conv, m_pad, n_state, c_state)


def _pad_lanes(v):
    return jnp.pad(v.astype(_F32), (0, LANES - v.shape[0])).reshape(1, LANES)


def kernel(x_prompt, x_sample, state_conv, state_mlstm_c, state_mlstm_n, state_mlstm_m, meta_tokens, norm_gain,
           w_in, conv_w, b_igate, b_fgate, mh_norm_gain, w_out, final_norm_gain):
    nseq, dec_seq, _ = x_sample.shape
    assert dec_seq == DEC_SEQ and meta_tokens.shape[0] == N_META

    w_t = w_in.T
    wgt = jnp.pad(w_t[MAIN_WIDTH:], ((0, LANES - 2 * M_HEADS), (0, 0)))
    small = (norm_gain.reshape(1, D_MODEL), mh_norm_gain.reshape(1, M_WIDTH), conv_w,
             _pad_lanes(b_igate), _pad_lanes(b_fgate))
    fgain = final_norm_gain.reshape(1, D_MODEL)
    wo = w_out.astype(_BF)

    x_meta = jnp.concatenate([jnp.zeros((CHUNK - N_META, D_MODEL), _F32), meta_tokens.astype(_F32)], axis=0)
    w3, wg, caug_meta, m_meta, ut_meta = _prep_call(w_t, x_meta, wgt, small)

    y_prompt, c_p, n_p, m_p, conv_p = _prompt_call(
        x_prompt, w3, wg, wo, small, fgain, caug_meta, m_meta, ut_meta, tb=512)
    m_pp = m_p[:, 0, :M_HEADS]

    xs = x_sample.reshape(nseq * DEC_SEQ, D_MODEL)
    m_pad = jnp.pad(state_mlstm_m, ((0, 0), (0, LANES - M_HEADS)))
    ys, conv_s, c_s, n_s, m_s = _sample_call(xs, w3, wg, wo, small, fgain, state_conv, m_pad, state_mlstm_n,
                                             state_mlstm_c)
    y_sample = ys.reshape(nseq, DEC_SEQ, D_MODEL)
    m_ss = m_s[:, :M_HEADS]

    return (y_prompt, y_sample, conv_p, c_p, n_p, m_pp, conv_s, c_s, n_s, m_ss)
```

```python
import functools

import jax
import jax.numpy as jnp
from jax import lax
from jax.experimental import pallas as pl
from jax.experimental.pallas import tpu as pltpu

D_MODEL = 1024
CONV_WIDTH = 512
M_WIDTH = 512
M_HEADS = 4
HEAD_DIM = 128
N_META = 16
CONV_K = 3
EPS = 1e-6
MAIN_WIDTH = 4 * CONV_WIDTH + 5 * M_WIDTH
LANES = 128
SUBLANES = 8
BF16_TILE_ROWS = 2 * SUBLANES
MXU_COLS = 256
N_TILES = MAIN_WIDTH // MXU_COLS
PREP_TILES = 3
CHUNK = 128
DEC_SEQ = 8
SEQ_PER_STEP = CHUNK // DEC_SEQ
NEG_INF = float("-inf")
K_SCALE = HEAD_DIM ** -0.5

OFF_GB, OFF_GC, OFF_HC, OFF_ZC = 0, 512, 1024, 1536
OFF_Q, OFF_K, OFF_V, OFF_O, OFF_ZM = 2048, 2560, 3072, 3584, 4096
QKV_TILES = tuple(range(OFF_Q // MXU_COLS, OFF_O // MXU_COLS))
REST_TILES = tuple(t for t in range(N_TILES) if t not in QKV_TILES)

_HI = lax.Precision.HIGHEST
_BF = jnp.bfloat16
_F32 = jnp.float32


def _dot(a, b):
    return jnp.dot(a, b, preferred_element_type=_F32)


def _dot_exact(a, b):
    return jnp.dot(a, b, precision=_HI, preferred_element_type=_F32)


def _dot_nt(a, b):
    return lax.dot_general(a, b, (((1,), (1,)), ((), ())), preferred_element_type=_F32)


def _rms_rows(x, gain_row):
    return x * lax.rsqrt(jnp.mean(x * x, axis=-1, keepdims=True) + EPS) * gain_row


def _sigmoid(x):
    return 1.0 / (1.0 + jnp.exp(-x))


def _silu(x):
    return x * _sigmoid(x)


def _tile_cols(t):
    return slice(t * MXU_COLS, (t + 1) * MXU_COLS)


def _cummax_rows(x, pos, period):
    s = 1
    while s < period:
        shifted = pltpu.roll(x, s, 0)
        x = jnp.maximum(x, jnp.where(pos >= s, shifted, NEG_INF))
        s *= 2
    return x


def _cumsum_rows(x, pos, period):
    s = 1
    while s < period:
        x = x + jnp.where(pos >= s, pltpu.roll(x, s, 0), 0.0)
        s *= 2
    return x


def _gate_columns(g_raw, big_row, bfg_row):
    ig = g_raw + big_row
    lf = jax.nn.log_sigmoid(pltpu.roll(g_raw, LANES - M_HEADS, 1) + bfg_row)
    return ig, lf


def _head_norm(num, den, emt_b, gain_row):
    hh = num / jnp.maximum(jnp.abs(den), emt_b)
    return hh * lax.rsqrt(jnp.mean(hh * hh, axis=-1, keepdims=True) + EPS) * gain_row


def _conv_out(gb, zc, u, u_m1, u_m2, cw_ref):
    yc = cw_ref[0:1, :] * u_m2 + cw_ref[1:2, :] * u_m1 + cw_ref[2:3, :] * u
    return gb * yc * _silu(zc)


def _gated_heads(hn, p_scr):
    return hn * _sigmoid(p_scr[:, OFF_O:OFF_O + M_WIDTH]) * _silu(p_scr[:, OFF_ZM:OFF_ZM + M_WIDTH])


def _sequence_block(xn, project_tile, emit_rows, p_scr, g_scr, hn_scr, ycat_scr, ubuf, caug, m_scr,
                    wg_ref, mhg_ref, cw_ref, big_ref, bfg_ref, *, tb, n_pad, side_work=None):
    side_work = side_work or {}
    anchors = []
    g_scr[...] = _dot(xn, wg_ref[...])
    late_tiles = []
    if project_tile is not None:
        for t in QKV_TILES:
            project_tile(t)
        early = 2
        for t in REST_TILES[:early]:
            project_tile(t)
        late_tiles = list(REST_TILES[early:])

    row = lax.broadcasted_iota(jnp.int32, (CHUNK, CHUNK), 0)
    col = lax.broadcasted_iota(jnp.int32, (CHUNK, CHUNK), 1)
    causal = col <= row
    ones_blk = jnp.ones((CHUNK, LANES), _BF)

    n_chunks = tb // CHUNK
    gates = []
    m_prev = m_scr[...]
    for c in range(n_chunks):
        rows = slice(c * CHUNK, (c + 1) * CHUNK)
        ig, lf = _gate_columns(g_scr[rows, :], big_ref[...], bfg_ref[...])
        if n_pad:
            ig = jnp.where(row >= n_pad, ig, NEG_INF)
            lf = jnp.where(row >= n_pad, lf, 0.0)
        b = _cumsum_rows(lf, row, CHUNK)
        a = ig - b
        big_m = jnp.maximum(m_prev, _cummax_rows(a, row, CHUNK))
        m_t = b + big_m
        m_last = big_m[CHUNK - 1:CHUNK, :]
        gates.append(dict(a_t=a.T, big_m=big_m, inter=jnp.exp(m_prev - big_m), emt=jnp.exp(-m_t),
                          w=jnp.exp(a - m_last), decay=jnp.exp(m_prev - m_last)))
        m_prev = m_t[CHUNK - 1:CHUNK, :]
    m_scr[...] = m_prev

    def stage_a(c, h):
        rows = slice(c * CHUNK, (c + 1) * CHUNK)
        hs = h * HEAD_DIM
        g = gates[c]
        q = p_scr[rows, OFF_Q + hs:OFF_Q + hs + HEAD_DIM]
        k = p_scr[rows, OFF_K + hs:OFF_K + hs + HEAD_DIM] * K_SCALE
        a_b = jnp.broadcast_to(g["a_t"][h:h + 1, :], (CHUNK, CHUNK))
        m_b = jnp.broadcast_to(g["big_m"][:, h:h + 1], (CHUNK, CHUNK))
        d = jnp.exp(jnp.where(causal, a_b - m_b, NEG_INF))
        q_inter = (jnp.broadcast_to(g["inter"][:, h:h + 1], (CHUNK, HEAD_DIM)) * q).astype(_BF)
        return dict(qk=_dot_nt(q.astype(_BF), k.astype(_BF)), d=d, q_inter=q_inter, k=k)

    def stage_b(c, h, st):
        rows = slice(c * CHUNK, (c + 1) * CHUNK)
        hs = h * HEAD_DIM
        g = gates[c]
        v = p_scr[rows, OFF_V + hs:OFF_V + hs + HEAD_DIM]
        s = st["qk"] * st["d"]
        lhs = jnp.concatenate([s.astype(_BF), st["q_inter"]], axis=1)
        rhs = jnp.concatenate([jnp.concatenate([v.astype(_BF), ones_blk], axis=1), caug[h].astype(_BF)], axis=0)
        tot = _dot(lhs, rhs)
        emt_b = jnp.broadcast_to(g["emt"][:, h:h + 1], (CHUNK, LANES))
        hn_scr[rows, hs:hs + HEAD_DIM] = _head_norm(tot[:, :LANES], tot[:, LANES:], emt_b,
                                                    mhg_ref[:, hs:hs + HEAD_DIM])
        w_b = jnp.broadcast_to(g["w"][:, h:h + 1], (CHUNK, LANES))
        wv = jnp.concatenate([(w_b * v).astype(_BF), w_b.astype(_BF)], axis=1)
        upd = _dot(st["k"].T.astype(_BF), wv)
        decay_b = jnp.broadcast_to(g["decay"][:, h:h + 1], (HEAD_DIM, 2 * LANES))
        caug[h] = decay_b * caug[h] + upd

    def finish_rows(r0, r1):
        rr = slice(r0, r1)
        n = r1 - r0
        u = p_scr[rr, OFF_GC:OFF_GC + CONV_WIDTH] * p_scr[rr, OFF_HC:OFF_HC + CONV_WIDTH]
        ubuf[SUBLANES + r0:SUBLANES + r1, :] = u
        u_m1 = ubuf[SUBLANES - 1 + r0:SUBLANES - 1 + r0 + n, :]
        u_m2 = ubuf[SUBLANES - 2 + r0:SUBLANES - 2 + r0 + n, :]
        y_conv = _conv_out(p_scr[rr, OFF_GB:OFF_GB + CONV_WIDTH], p_scr[rr, OFF_ZC:OFF_ZC + CONV_WIDTH],
                           u, u_m1, u_m2, cw_ref)
        ycat_scr[rr, 0:CONV_WIDTH] = y_conv.astype(_BF)
        y_m = (hn_scr[rr, :] * _sigmoid(p_scr[rr, OFF_O:OFF_O + M_WIDTH])
               * _silu(p_scr[rr, OFF_ZM:OFF_ZM + M_WIDTH]))
        ycat_scr[rr, CONV_WIDTH:CONV_WIDTH + M_WIDTH] = y_m.astype(_BF)
        while anchors:
            tile = ycat_scr[0:BF16_TILE_ROWS, 0:LANES].astype(_F32) + anchors.pop()
            ycat_scr[0:BF16_TILE_ROWS, 0:LANES] = tile.astype(_BF)
        if emit_rows is not None:
            emit_rows(r0, r1)

    units = [(c, h) for c in range(n_chunks) for h in range(M_HEADS)]
    lag = min(3, len(units))
    n_groups = 1
    group_units = len(units) // n_groups
    assert len(late_tiles) <= group_units + lag
    pending = {}
    for t in range(len(units) + lag):
        if t < len(units):
            pending[t] = stage_a(*units[t])
        if t < len(late_tiles):
            project_tile(late_tiles[t])
        for fn in side_work.get(t, ()):
            zero = fn()
            if zero is not None:
                anchors.append(zero)
        if t >= lag:
            stage_b(*units[t - lag], pending.pop(t - lag))
            done = t - lag + 1
            if done % group_units == 0:
                gi = done // group_units
                finish_rows((gi - 1) * tb // n_groups, gi * tb // n_groups)
    ubuf[0:SUBLANES, :] = ubuf[tb:tb + SUBLANES, :]


def _sequence_scratch(tb):
    return [
        pltpu.VMEM((tb, MAIN_WIDTH), _F32),
        pltpu.VMEM((tb, LANES), _F32),
        pltpu.VMEM((tb, M_WIDTH), _F32),
        pltpu.VMEM((tb, D_MODEL), _BF),
        pltpu.VMEM((tb + 2 * SUBLANES, CONV_WIDTH), _F32),
        pltpu.VMEM((M_HEADS, HEAD_DIM, 2 * LANES), _F32),
        pltpu.VMEM((1, LANES), _F32),
    ]


def _prep_body(wt_ref, x_ref, wgt_ref, ng_ref, mhg_ref, cw_ref, big_ref, bfg_ref,
               w3_ref, wg_ref, caug_out_ref, m_out_ref, ut_out_ref,
               xn_scr, p3_scr, p_scr, g_scr, hn_scr, ycat_scr, ubuf, caug, m_scr):
    t = pl.program_id(0)

    @pl.when(t == 0)
    def _():
        xn_scr[...] = _rms_rows(x_ref[...], ng_ref[...]).astype(_BF)
        wg_ref[...] = wgt_ref[...].T.astype(_BF)

    for j in range(PREP_TILES):
        tile = wt_ref[j * MXU_COLS:(j + 1) * MXU_COLS, :].T.astype(_BF)
        w3_ref[j] = tile
        p3_scr[t * PREP_TILES + j] = _dot(xn_scr[...], tile)

    @pl.when(t == N_TILES // PREP_TILES - 1)
    def _():
        for j in range(N_TILES):
            p_scr[:, _tile_cols(j)] = p3_scr[j]
        caug[...] = jnp.zeros_like(caug)
        m_scr[...] = jnp.zeros_like(m_scr)
        ubuf[0:SUBLANES, :] = jnp.zeros((SUBLANES, CONV_WIDTH), _F32)
        _sequence_block(xn_scr[...], None, None, p_scr, g_scr, hn_scr, ycat_scr, ubuf, caug, m_scr,
                        wg_ref, mhg_ref, cw_ref, big_ref, bfg_ref, tb=CHUNK, n_pad=CHUNK - N_META)
        caug_out_ref[...] = caug[...]
        m_out_ref[...] = m_scr[...]
        ut_out_ref[...] = ubuf[0:SUBLANES, :]


def _prep_call(w_t, x_meta, wgt, small):
    ng, mhg, cw, big, bfg = small

    def const(shape):
        nd = len(shape)
        return pl.BlockSpec(shape, lambda t: (0,) * nd)

    in_specs = [
        pl.BlockSpec((PREP_TILES * MXU_COLS, D_MODEL), lambda t: (t, 0)),
        const(x_meta.shape), const(wgt.shape), const(ng.shape), const(mhg.shape), const(cw.shape),
        const(big.shape), const(bfg.shape),
    ]
    out_shape = (
        jax.ShapeDtypeStruct((N_TILES, D_MODEL, MXU_COLS), _BF),
        jax.ShapeDtypeStruct((D_MODEL, LANES), _BF),
        jax.ShapeDtypeStruct((M_HEADS, HEAD_DIM, 2 * LANES), _F32),
        jax.ShapeDtypeStruct((1, LANES), _F32),
        jax.ShapeDtypeStruct((SUBLANES, CONV_WIDTH), _F32),
    )
    out_specs = (pl.BlockSpec((PREP_TILES, D_MODEL, MXU_COLS), lambda t: (t, 0, 0)),) + tuple(
        const(s.shape) for s in out_shape[1:])
    scratch = [
        pltpu.VMEM((CHUNK, D_MODEL), _BF),
        pltpu.VMEM((N_TILES, CHUNK, MXU_COLS), _F32),
    ] + _sequence_scratch(CHUNK)
    return pl.pallas_call(
        _prep_body,
        grid=(N_TILES // PREP_TILES,),
        in_specs=in_specs,
        out_specs=out_specs,
        out_shape=out_shape,
        scratch_shapes=scratch,
        compiler_params=pltpu.CompilerParams(
            dimension_semantics=("arbitrary",), vmem_limit_bytes=56 * 1024 * 1024),
        name="prep_meta",
    )(w_t, x_meta, wgt, ng, mhg, cw, big, bfg)


def _prompt_body(x_ref, xnext_ref, w3_ref, wg_ref, wo_ref, ng_ref, fgain_ref, mhg_ref, cw_ref, big_ref, bfg_ref,
                 caug0_ref, m0_ref, ut0_ref,
                 y_ref, c_out_ref, n_out_ref, m_out_ref, conv_out_ref,
                 xn_scr, out_scr, p_scr, g_scr, hn_scr, ycat_scr, ubuf, caug, m_scr, *, tb, nt, n_blocks):
    s = pl.program_id(0)
    i = s % nt
    live = s < n_blocks

    @pl.when(s == 0)
    def _():
        xn_scr[...] = _rms_rows(x_ref[0], ng_ref[...]).astype(_BF)
        out_scr[...] = jnp.zeros_like(out_scr)

    @pl.when(live & (i == 0))
    def _():
        caug[...] = caug0_ref[...]
        m_scr[...] = m0_ref[...]
        ubuf[0:SUBLANES, :] = ut0_ref[...]

    @pl.when(live)
    def _():
        xn = xn_scr[...]

        def project_tile(t):
            p_scr[:, _tile_cols(t)] = _dot(xn, w3_ref[t])

        def emit_rows(r0, r1):
            out_scr[r0:r1, :] = _dot(ycat_scr[r0:r1, :], wo_ref[...]) + x_ref[0, r0:r1, :]

        def norm_previous(r0, r1):
            def fn():
                y_ref[0, r0:r1, :] = _rms_rows(out_scr[r0:r1, :], fgain_ref[...])
            return fn

        def norm_next(r0, r1):
            def fn():
                xn_scr[r0:r1, :] = _rms_rows(xnext_ref[0, r0:r1, :], ng_ref[...]).astype(_BF)
            return fn

        q4 = tb // 4
        side_work = {2 + 2 * j: [norm_previous(j * q4, (j + 1) * q4)] for j in range(4)}
        side_work.update({10 + 2 * j: [norm_next(j * q4, (j + 1) * q4)] for j in range(4)})
        _sequence_block(xn, project_tile, emit_rows, p_scr, g_scr, hn_scr, ycat_scr, ubuf, caug, m_scr,
                        wg_ref, mhg_ref, cw_ref, big_ref, bfg_ref, tb=tb, n_pad=0, side_work=side_work)

    @pl.when(s == n_blocks)
    def _():
        y_ref[0] = _rms_rows(out_scr[...], fgain_ref[...])

    @pl.when(live & (i == nt - 1))
    def _():
        m_out_ref[0] = m_scr[...]
        conv_out_ref[0] = ubuf[SUBLANES - (CONV_K - 1):SUBLANES, :]
        for h in range(M_HEADS):
            c_out_ref[0, h] = caug[h, :, 0:HEAD_DIM]
            n_out_ref[0, h:h + 1, :] = caug[h, :, LANES:2 * LANES].T[0:1, :]


def _prompt_call(x, w3, wg, wo, small, fgain, caug0, m0, ut0, *, tb):
    ng, mhg, cw, big, bfg = small
    nb, t, _ = x.shape
    nt = t // tb
    n_blocks = nb * nt

    def const(shape):
        nd = len(shape)
        return pl.BlockSpec(shape, lambda s: (0,) * nd)

    def block_of(s):
        return jnp.minimum(s, n_blocks - 1)

    def rows_spec(step_to_block):
        return pl.BlockSpec((1, tb, D_MODEL), lambda s: (step_to_block(s) // nt, step_to_block(s) % nt, 0))

    def state_spec(*tail):
        nd = len(tail)
        return pl.BlockSpec((1,) + tail, lambda s: (block_of(s) // nt,) + (0,) * nd)

    in_specs = [
        rows_spec(block_of),
        rows_spec(lambda s: block_of(s + 1)),
        const(w3.shape), const(wg.shape), const(wo.shape), const(ng.shape), const(fgain.shape),
        const(mhg.shape), const(cw.shape), const(big.shape), const(bfg.shape),
        const(caug0.shape), const(m0.shape), const(ut0.shape),
    ]
    out_shape = (
        jax.ShapeDtypeStruct((nb, t, D_MODEL), _F32),
        jax.ShapeDtypeStruct((nb, M_HEADS, HEAD_DIM, HEAD_DIM), _F32),
        jax.ShapeDtypeStruct((nb, M_HEADS, HEAD_DIM), _F32),
        jax.ShapeDtypeStruct((nb, 1, LANES), _F32),
        jax.ShapeDtypeStruct((nb, CONV_K - 1, CONV_WIDTH), _F32),
    )
    out_specs = (
        rows_spec(lambda s: jnp.maximum(s - 1, 0)),
        state_spec(M_HEADS, HEAD_DIM, HEAD_DIM),
        state_spec(M_HEADS, HEAD_DIM),
        state_spec(1, LANES),
        state_spec(CONV_K - 1, CONV_WIDTH),
    )
    scratch = [
        pltpu.VMEM((tb, D_MODEL), _BF),
        pltpu.VMEM((tb, D_MODEL), _F32),
    ] + _sequence_scratch(tb)
    return pl.pallas_call(
        functools.partial(_prompt_body, tb=tb, nt=nt, n_blocks=n_blocks),
        grid=(n_blocks + 1,),
        in_specs=in_specs,
        out_specs=out_specs,
        out_shape=out_shape,
        scratch_shapes=scratch,
        compiler_params=pltpu.CompilerParams(
            dimension_semantics=("arbitrary",), vmem_limit_bytes=56 * 1024 * 1024),
        name="prompt_layer",
    )(x, x, w3, wg, wo, ng, fgain, mhg, cw, big, bfg, caug0, m0, ut0)


def _sample_body(x_ref, w3_ref, wg_ref, wo_ref, ng_ref, fgain_ref, mhg_ref, cw_ref, big_ref, bfg_ref,
                 sconv_ref, m_ref, n_ref, c_ref,
                 y_ref, conv_out_ref, c_out_ref, n_out_ref, m_out_ref,
                 p_scr, ni_scr, sc_scr, ycat_scr):
    @pl.when(pl.program_id(0) == 0)
    def _():
        sc_scr[...] = jnp.zeros_like(sc_scr)

    x = x_ref[...]
    xn = _rms_rows(x, ng_ref[...]).astype(_BF)
    g_raw = _dot(xn, wg_ref[...])
    for t in QKV_TILES:
        p_scr[:, _tile_cols(t)] = _dot(xn, w3_ref[t])

    row = lax.broadcasted_iota(jnp.int32, (CHUNK, CHUNK), 0)
    col = lax.broadcasted_iota(jnp.int32, (CHUNK, CHUNK), 1)
    pos = row % DEC_SEQ
    grp = row // DEC_SEQ
    causal = (grp == (col // DEC_SEQ)) & (col <= row)
    grow = lax.broadcasted_iota(jnp.int32, (SEQ_PER_STEP, CHUNK), 0)
    gcol = lax.broadcasted_iota(jnp.int32, (SEQ_PER_STEP, CHUNK), 1)
    pick_last = (gcol == grow * DEC_SEQ + (DEC_SEQ - 1)).astype(_F32)
    seq_sum = (gcol // DEC_SEQ == grow).astype(_BF)
    erow = lax.broadcasted_iota(jnp.int32, (CHUNK, SEQ_PER_STEP), 0)
    ecol = lax.broadcasted_iota(jnp.int32, (CHUNK, SEQ_PER_STEP), 1)
    own_seq = erow // DEC_SEQ == ecol
    ones_blk = jnp.ones((CHUNK, LANES), _BF)

    def rows_from_seqs(v16):
        return jnp.broadcast_to(v16[:, None, :], (SEQ_PER_STEP, DEC_SEQ, LANES)).reshape(CHUNK, LANES)

    def last_row_of_seq(v):
        v3 = v.reshape(SEQ_PER_STEP, DEC_SEQ, LANES)
        return jnp.broadcast_to(v3[:, DEC_SEQ - 1:DEC_SEQ, :], v3.shape).reshape(CHUNK, LANES)

    ig, lf = _gate_columns(g_raw, big_ref[...], bfg_ref[...])
    b = _cumsum_rows(lf, pos, DEC_SEQ)
    a = ig - b
    m_prev = rows_from_seqs(m_ref[...])
    big_m = jnp.maximum(m_prev, _cummax_rows(a, pos, DEC_SEQ))
    m_t = b + big_m
    inter = jnp.exp(m_prev - big_m)
    emt = jnp.exp(-m_t)
    m_last = last_row_of_seq(big_m)
    w = jnp.exp(a - m_last)
    decay = jnp.exp(m_prev - m_last)
    m_out_ref[...] = _dot_exact(pick_last, m_t)
    dec16 = _dot_exact(pick_last, decay)
    a_t = a.T

    for t in REST_TILES:
        p_scr[:, _tile_cols(t)] = _dot(xn, w3_ref[t])

    heads = []
    for h in range(M_HEADS):
        hs = h * HEAD_DIM
        q = p_scr[:, OFF_Q + hs:OFF_Q + hs + HEAD_DIM]
        k = p_scr[:, OFF_K + hs:OFF_K + hs + HEAD_DIM] * K_SCALE
        v = p_scr[:, OFF_V + hs:OFF_V + hs + HEAD_DIM]
        q_bf = q.astype(_BF)
        a_b = jnp.broadcast_to(a_t[h:h + 1, :], (CHUNK, CHUNK))
        m_b = jnp.broadcast_to(big_m[:, h:h + 1], (CHUNK, CHUNK))
        d = jnp.exp(jnp.where(causal, a_b - m_b, NEG_INF))
        w_b = jnp.broadcast_to(w[:, h:h + 1], (CHUNK, LANES))
        heads.append(dict(q_bf=q_bf, k=k, v=v, d=d, w_b=w_b, qk=_dot_nt(q_bf, k.astype(_BF))))

    u = p_scr[:, OFF_GC:OFF_GC + CONV_WIDTH] * p_scr[:, OFF_HC:OFF_HC + CONV_WIDTH]
    for g in range(SEQ_PER_STEP):
        sc_scr[g * DEC_SEQ:g * DEC_SEQ + CONV_K - 1, :] = sconv_ref[g]
        conv_out_ref[g] = u[(g + 1) * DEC_SEQ - (CONV_K - 1):(g + 1) * DEC_SEQ, :]
    sc = sc_scr[...]
    posw = lax.broadcasted_iota(jnp.int32, (CHUNK, CONV_WIDTH), 0) % DEC_SEQ
    u_m1 = jnp.where(posw == 0, pltpu.roll(sc, CHUNK - 1, 0), pltpu.roll(u, 1, 0))
    u_m2 = jnp.where(posw < 2, sc, pltpu.roll(u, 2, 0))
    y_conv = _conv_out(p_scr[:, OFF_GB:OFF_GB + CONV_WIDTH], p_scr[:, OFF_ZC:OFF_ZC + CONV_WIDTH],
                       u, u_m1, u_m2, cw_ref)
    ycat_scr[:, 0:CONV_WIDTH] = y_conv.astype(_BF)

    for h in range(M_HEADS):
        hs = h * HEAD_DIM
        st = heads[h]
        wv = st["w_b"] * st["v"]
        k_t_bf = st["k"].T.astype(_BF)
        dec_b = jnp.broadcast_to(decay[:, h:h + 1], (CHUNK, HEAD_DIM))

        for g in range(0, SEQ_PER_STEP, 2):
            pair_rows = slice(g * DEC_SEQ, (g + 2) * DEC_SEQ)
            c0, c1 = c_ref[g, h], c_ref[g + 1, h]
            qc = _dot(st["q_bf"][pair_rows, :], jnp.concatenate([c0.astype(_BF), c1.astype(_BF)], axis=1))
            ni_scr[g * DEC_SEQ:(g + 1) * DEC_SEQ, hs:hs + HEAD_DIM] = qc[0:DEC_SEQ, 0:HEAD_DIM]
            ni_scr[(g + 1) * DEC_SEQ:(g + 2) * DEC_SEQ, hs:hs + HEAD_DIM] = qc[DEC_SEQ:, HEAD_DIM:]
            wv_pair = jnp.concatenate([jnp.where(grp == g, wv, 0.0).astype(_BF),
                                       jnp.where(grp == g + 1, wv, 0.0).astype(_BF)], axis=1)
            upd = _dot(k_t_bf, wv_pair)
            dec0 = jnp.broadcast_to(dec_b[g * DEC_SEQ:g * DEC_SEQ + 1, :], (HEAD_DIM, HEAD_DIM))
            dec1 = jnp.broadcast_to(dec_b[(g + 1) * DEC_SEQ:(g + 1) * DEC_SEQ + 1, :], (HEAD_DIM, HEAD_DIM))
            c_out_ref[g, h] = dec0 * c0 + upd[:, 0:HEAD_DIM]
            c_out_ref[g + 1, h] = dec1 * c1 + upd[:, HEAD_DIM:]

        s = st["qk"] * st["d"]
        sv = _dot(s.astype(_BF), jnp.concatenate([st["v"].astype(_BF), ones_blk], axis=1))
        n_old = n_ref[:, h, :]
        num_inter = ni_scr[:, hs:hs + HEAD_DIM]
        qn = _dot_nt(st["q_bf"], n_old.astype(_BF))
        den_inter = jnp.sum(jnp.where(own_seq, qn, 0.0), axis=-1, keepdims=True)
        inter_b = jnp.broadcast_to(inter[:, h:h + 1], (CHUNK, LANES))
        num = sv[:, :LANES] + inter_b * num_inter
        den = sv[:, LANES:] + inter_b * den_inter
        emt_b = jnp.broadcast_to(emt[:, h:h + 1], (CHUNK, LANES))
        ni_scr[:, hs:hs + HEAD_DIM] = _head_norm(num, den, emt_b, mhg_ref[:, hs:hs + HEAD_DIM])
        n_upd = _dot(seq_sum, (st["w_b"] * st["k"]).astype(_BF))
        dec16_b = jnp.broadcast_to(dec16[:, h:h + 1], (SEQ_PER_STEP, HEAD_DIM))
        n_out_ref[:, h, :] = dec16_b * n_old + n_upd

    ycat_scr[:, CONV_WIDTH:CONV_WIDTH + M_WIDTH] = _gated_heads(ni_scr[...], p_scr).astype(_BF)
    out = _dot(ycat_scr[...], wo_ref[...]) + x
    y_ref[...] = _rms_rows(out, fgain_ref[...])


def _sample_call(xs, w3, wg, wo, small, fgain, state_conv, m_pad, n_state, c_state):
    ng, mhg, cw, big, bfg = small
    rows = xs.shape[0]
    nseq = c_state.shape[0]
    steps = rows // CHUNK

    def const(shape):
        nd = len(shape)
        return pl.BlockSpec(shape, lambda i: (0,) * nd)

    def seq_spec(*tail):
        nd = len(tail)
        return pl.BlockSpec((SEQ_PER_STEP,) + tail, lambda i: (i,) + (0,) * nd)

    x_spec = pl.BlockSpec((CHUNK, D_MODEL), lambda i: (i, 0))
    conv_spec = seq_spec(CONV_K - 1, CONV_WIDTH)
    c_spec = seq_spec(M_HEADS, HEAD_DIM, HEAD_DIM)
    n_spec = seq_spec(M_HEADS, HEAD_DIM)
    m_spec = seq_spec(LANES)
    in_specs = [
        x_spec,
        const(w3.shape), const(wg.shape), const(wo.shape), const(ng.shape), const(fgain.shape),
        const(mhg.shape), const(cw.shape), const(big.shape), const(bfg.shape),
        conv_spec, m_spec, n_spec, c_spec,
    ]
    out_shape = (
        jax.ShapeDtypeStruct((rows, D_MODEL), _F32),
        jax.ShapeDtypeStruct(state_conv.shape, _F32),
        jax.ShapeDtypeStruct(c_state.shape, _F32),
        jax.ShapeDtypeStruct(n_state.shape, _F32),
        jax.ShapeDtypeStruct((nseq, LANES), _F32),
    )
    out_specs = (x_spec, conv_spec, c_spec, n_spec, m_spec)
    scratch = [
        pltpu.VMEM((CHUNK, MAIN_WIDTH), _F32),
        pltpu.VMEM((CHUNK, M_WIDTH), _F32),
        pltpu.VMEM((CHUNK, CONV_WIDTH), _F32),
        pltpu.VMEM((CHUNK, D_MODEL), _BF),
    ]
    return pl.pallas_call(
        _sample_body,
        grid=(steps,),
        in_specs=in_specs,
        out_specs=out_specs,
        out_shape=out_shape,
        scratch_shapes=scratch,
        compiler_params=pltpu.CompilerParams(
            dimension_semantics=("arbitrary",), vmem_limit_bytes=56 * 1024 * 1024),
        name="sample_layer",
    )(xs, w3, wg, wo, ng, fgain, mhg, cw, big, bfg, state_conv, m_pad, n_state, c_state)


def _pad_lanes(v):
    return jnp.pad(v.astype(_F32), (0, LANES - v.shape[0])).reshape(1, LANES)


def kernel(x_prompt, x_sample, state_conv, state_mlstm_c, state_mlstm_n, state_mlstm_m, meta_tokens, norm_gain,
           w_in, conv_w, b_igate, b_fgate, mh_norm_gain, w_out, final_norm_gain):
    nseq, dec_seq, _ = x_sample.shape
    assert dec_seq == DEC_SEQ and meta_tokens.shape[0] == N_META

    w_t = w_in.T
    wgt = jnp.pad(w_t[MAIN_WIDTH:], ((0, LANES - 2 * M_HEADS), (0, 0)))
    small = (norm_gain.reshape(1, D_MODEL), mh_norm_gain.reshape(1, M_WIDTH), conv_w,
             _pad_lanes(b_igate), _pad_lanes(b_fgate))
    fgain = final_norm_gain.reshape(1, D_MODEL)
    wo = w_out.astype(_BF)

    x_meta = jnp.concatenate([jnp.zeros((CHUNK - N_META, D_MODEL), _F32), meta_tokens.astype(_F32)], axis=0)
    w3, wg, caug_meta, m_meta, ut_meta = _prep_call(w_t, x_meta, wgt, small)

    y_prompt, c_p, n_p, m_p, conv_p = _prompt_call(
        x_prompt, w3, wg, wo, small, fgain, caug_meta, m_meta, ut_meta, tb=512)
    m_pp = m_p[:, 0, :M_HEADS]

    xs = x_sample.reshape(nseq * DEC_SEQ, D_MODEL)
    m_pad = jnp.pad(state_mlstm_m, ((0, 0), (0, LANES - M_HEADS)))
    ys, conv_s, c_s, n_s, m_s = _sample_call(xs, w3, wg, wo, small, fgain, state_conv, m_pad, state_mlstm_n,
                                             state_mlstm_c)
    y_sample = ys.reshape(nseq, DEC_SEQ, D_MODEL)
    m_ss = m_s[:, :M_HEADS]

    return (y_prompt, y_sample, conv_p, c_p, n_p, m_pp, conv_s, c_s, n_s, m_ss)
```

```python
import functools

import jax
import jax.numpy as jnp
from jax import lax
from jax.experimental import pallas as pl
from jax.experimental.pallas import tpu as pltpu

D_MODEL = 1024
CONV_WIDTH = 512
M_WIDTH = 512
M_HEADS = 4
HEAD_DIM = 128
N_META = 16
CONV_K = 3
EPS = 1e-6
MAIN_WIDTH = 4 * CONV_WIDTH + 5 * M_WIDTH
LANES = 128
SUBLANES = 8
BF16_TILE_ROWS = 2 * SUBLANES
MXU_COLS = 256
N_TILES = MAIN_WIDTH // MXU_COLS
PREP_TILES = 3
CHUNK = 128
DEC_SEQ = 8
SEQ_PER_STEP = CHUNK // DEC_SEQ
NEG_INF = float("-inf")
K_SCALE = HEAD_DIM ** -0.5

OFF_GB, OFF_GC, OFF_HC, OFF_ZC = 0, 512, 1024, 1536
OFF_Q, OFF_K, OFF_V, OFF_O, OFF_ZM = 2048, 2560, 3072, 3584, 4096
QKV_TILES = tuple(range(OFF_Q // MXU_COLS, OFF_O // MXU_COLS))
REST_TILES = tuple(t for t in range(N_TILES) if t not in QKV_TILES)

_HI = lax.Precision.HIGHEST
_BF = jnp.bfloat16
_F32 = jnp.float32


def _dot(a, b):
    return jnp.dot(a, b, preferred_element_type=_F32)


def _dot_exact(a, b):
    return jnp.dot(a, b, precision=_HI, preferred_element_type=_F32)


def _dot_nt(a, b):
    return lax.dot_general(a, b, (((1,), (1,)), ((), ())), preferred_element_type=_F32)


def _rms_rows(x, gain_row):
    return x * lax.rsqrt(jnp.mean(x * x, axis=-1, keepdims=True) + EPS) * gain_row


def _sigmoid(x):
    return 1.0 / (1.0 + jnp.exp(-x))


def _silu(x):
    return x * _sigmoid(x)


def _tile_cols(t):
    return slice(t * MXU_COLS, (t + 1) * MXU_COLS)


def _cummax_rows(x, pos, period):
    s = 1
    while s < period:
        shifted = pltpu.roll(x, s, 0)
        x = jnp.maximum(x, jnp.where(pos >= s, shifted, NEG_INF))
        s *= 2
    return x


def _cumsum_rows(x, pos, period):
    s = 1
    while s < period:
        x = x + jnp.where(pos >= s, pltpu.roll(x, s, 0), 0.0)
        s *= 2
    return x


def _gate_columns(g_raw, big_row, bfg_row):
    ig = g_raw + big_row
    lf = jax.nn.log_sigmoid(pltpu.roll(g_raw, LANES - M_HEADS, 1) + bfg_row)
    return ig, lf


def _head_norm(num, den, emt_b, gain_row):
    hh = num / jnp.maximum(jnp.abs(den), emt_b)
    return hh * lax.rsqrt(jnp.mean(hh * hh, axis=-1, keepdims=True) + EPS) * gain_row


def _conv_out(gb, zc, u, u_m1, u_m2, cw_ref):
    yc = cw_ref[0:1, :] * u_m2 + cw_ref[1:2, :] * u_m1 + cw_ref[2:3, :] * u
    return gb * yc * _silu(zc)


def _gated_heads(hn, p_scr):
    return hn * _sigmoid(p_scr[:, OFF_O:OFF_O + M_WIDTH]) * _silu(p_scr[:, OFF_ZM:OFF_ZM + M_WIDTH])


def _sequence_block(xn, project_tile, emit_rows, p_scr, g_scr, hn_scr, ycat_scr, ubuf, caug, m_scr,
                    wg_ref, mhg_ref, cw_ref, big_ref, bfg_ref, *, tb, n_pad, side_work=None):
    side_work = side_work or {}
    anchors = []
    g_scr[...] = _dot(xn, wg_ref[...])
    late_tiles = []
    if project_tile is not None:
        for t in QKV_TILES:
            project_tile(t)
        early = 2
        for t in REST_TILES[:early]:
            project_tile(t)
        late_tiles = list(REST_TILES[early:])

    row = lax.broadcasted_iota(jnp.int32, (CHUNK, CHUNK), 0)
    col = lax.broadcasted_iota(jnp.int32, (CHUNK, CHUNK), 1)
    causal = col <= row
    ones_blk = jnp.ones((CHUNK, LANES), _BF)

    n_chunks = tb // CHUNK
    gates = []
    m_prev = m_scr[...]
    for c in range(n_chunks):
        rows = slice(c * CHUNK, (c + 1) * CHUNK)
        ig, lf = _gate_columns(g_scr[rows, :], big_ref[...], bfg_ref[...])
        if n_pad:
            ig = jnp.where(row >= n_pad, ig, NEG_INF)
            lf = jnp.where(row >= n_pad, lf, 0.0)
        b = _cumsum_rows(lf, row, CHUNK)
        a = ig - b
        big_m = jnp.maximum(m_prev, _cummax_rows(a, row, CHUNK))
        m_t = b + big_m
        m_last = big_m[CHUNK - 1:CHUNK, :]
        gates.append(dict(a_t=a.T, big_m=big_m, inter=jnp.exp(m_prev - big_m), emt=jnp.exp(-m_t),
                          w=jnp.exp(a - m_last), decay=jnp.exp(m_prev - m_last)))
        m_prev = m_t[CHUNK - 1:CHUNK, :]
    m_scr[...] = m_prev

    def stage_a(c, h):
        rows = slice(c * CHUNK, (c + 1) * CHUNK)
        hs = h * HEAD_DIM
        g = gates[c]
        q = p_scr[rows, OFF_Q + hs:OFF_Q + hs + HEAD_DIM]
        k = p_scr[rows, OFF_K + hs:OFF_K + hs + HEAD_DIM] * K_SCALE
        a_b = jnp.broadcast_to(g["a_t"][h:h + 1, :], (CHUNK, CHUNK))
        m_b = jnp.broadcast_to(g["big_m"][:, h:h + 1], (CHUNK, CHUNK))
        d = jnp.exp(jnp.where(causal, a_b - m_b, NEG_INF))
        q_inter = (jnp.broadcast_to(g["inter"][:, h:h + 1], (CHUNK, HEAD_DIM)) * q).astype(_BF)
        return dict(qk=_dot_nt(q.astype(_BF), k.astype(_BF)), d=d, q_inter=q_inter, k=k)

    def stage_b(c, h, st):
        rows = slice(c * CHUNK, (c + 1) * CHUNK)
        hs = h * HEAD_DIM
        g = gates[c]
        v = p_scr[rows, OFF_V + hs:OFF_V + hs + HEAD_DIM]
        s = st["qk"] * st["d"]
        lhs = jnp.concatenate([s.astype(_BF), st["q_inter"]], axis=1)
        rhs = jnp.concatenate([jnp.concatenate([v.astype(_BF), ones_blk], axis=1), caug[h].astype(_BF)], axis=0)
        tot = _dot(lhs, rhs)
        emt_b = jnp.broadcast_to(g["emt"][:, h:h + 1], (CHUNK, LANES))
        hn_scr[rows, hs:hs + HEAD_DIM] = _head_norm(tot[:, :LANES], tot[:, LANES:], emt_b,
                                                    mhg_ref[:, hs:hs + HEAD_DIM])
        w_b = jnp.broadcast_to(g["w"][:, h:h + 1], (CHUNK, LANES))
        wv = jnp.concatenate([(w_b * v).astype(_BF), w_b.astype(_BF)], axis=1)
        upd = _dot(st["k"].T.astype(_BF), wv)
        decay_b = jnp.broadcast_to(g["decay"][:, h:h + 1], (HEAD_DIM, 2 * LANES))
        caug[h] = decay_b * caug[h] + upd

    def finish_rows(r0, r1):
        rr = slice(r0, r1)
        n = r1 - r0
        u = p_scr[rr, OFF_GC:OFF_GC + CONV_WIDTH] * p_scr[rr, OFF_HC:OFF_HC + CONV_WIDTH]
        ubuf[SUBLANES + r0:SUBLANES + r1, :] = u
        u_m1 = ubuf[SUBLANES - 1 + r0:SUBLANES - 1 + r0 + n, :]
        u_m2 = ubuf[SUBLANES - 2 + r0:SUBLANES - 2 + r0 + n, :]
        y_conv = _conv_out(p_scr[rr, OFF_GB:OFF_GB + CONV_WIDTH], p_scr[rr, OFF_ZC:OFF_ZC + CONV_WIDTH],
                           u, u_m1, u_m2, cw_ref)
        ycat_scr[rr, 0:CONV_WIDTH] = y_conv.astype(_BF)
        y_m = (hn_scr[rr, :] * _sigmoid(p_scr[rr, OFF_O:OFF_O + M_WIDTH])
               * _silu(p_scr[rr, OFF_ZM:OFF_ZM + M_WIDTH]))
        ycat_scr[rr, CONV_WIDTH:CONV_WIDTH + M_WIDTH] = y_m.astype(_BF)
        while anchors:
            tile = ycat_scr[0:BF16_TILE_ROWS, 0:LANES].astype(_F32) + anchors.pop()
            ycat_scr[0:BF16_TILE_ROWS, 0:LANES] = tile.astype(_BF)
        if emit_rows is not None:
            emit_rows(r0, r1)

    units = [(c, h) for c in range(n_chunks) for h in range(M_HEADS)]
    lag = min(3, len(units))
    n_groups = 1
    group_units = len(units) // n_groups
    assert len(late_tiles) <= group_units + lag
    pending = {}
    for t in range(len(units) + lag):
        if t < len(units):
            pending[t] = stage_a(*units[t])
        if t < len(late_tiles):
            project_tile(late_tiles[t])
        for fn in side_work.get(t, ()):
            zero = fn()
            if zero is not None:
                anchors.append(zero)
        if t >= lag:
            stage_b(*units[t - lag], pending.pop(t - lag))
            done = t - lag + 1
            if done % group_units == 0:
                gi = done // group_units
                finish_rows((gi - 1) * tb // n_groups, gi * tb // n_groups)
    ubuf[0:SUBLANES, :] = ubuf[tb:tb + SUBLANES, :]


def _sequence_scratch(tb):
    return [
        pltpu.VMEM((tb, MAIN_WIDTH), _F32),
        pltpu.VMEM((tb, LANES), _F32),
        pltpu.VMEM((tb, M_WIDTH), _F32),
        pltpu.VMEM((tb, D_MODEL), _BF),
        pltpu.VMEM((tb + 2 * SUBLANES, CONV_WIDTH), _F32),
        pltpu.VMEM((M_HEADS, HEAD_DIM, 2 * LANES), _F32),
        pltpu.VMEM((1, LANES), _F32),
    ]


def _prep_body(wt_ref, x_ref, wgt_ref, ng_ref, mhg_ref, cw_ref, big_ref, bfg_ref,
               w3_ref, wg_ref, caug_out_ref, m_out_ref, ut_out_ref,
               xn_scr, p3_scr, p_scr, g_scr, hn_scr, ycat_scr, ubuf, caug, m_scr):
    t = pl.program_id(0)

    @pl.when(t == 0)
    def _():
        xn_scr[...] = jnp.zeros_like(xn_scr)
        xn_scr[CHUNK - N_META:CHUNK, :] = _rms_rows(x_ref[...], ng_ref[...]).astype(_BF)
        gate_rows = jnp.concatenate(
            [wgt_ref[...], jnp.zeros((LANES - 2 * M_HEADS, D_MODEL), _F32)], axis=0)
        wg_ref[...] = gate_rows.T.astype(_BF)

    for j in range(PREP_TILES):
        tile = wt_ref[j * MXU_COLS:(j + 1) * MXU_COLS, :].T.astype(_BF)
        w3_ref[j] = tile
        p3_scr[t * PREP_TILES + j] = _dot(xn_scr[...], tile)

    @pl.when(t == N_TILES // PREP_TILES - 1)
    def _():
        for j in range(N_TILES):
            p_scr[:, _tile_cols(j)] = p3_scr[j]
        caug[...] = jnp.zeros_like(caug)
        m_scr[...] = jnp.zeros_like(m_scr)
        ubuf[0:SUBLANES, :] = jnp.zeros((SUBLANES, CONV_WIDTH), _F32)
        _sequence_block(xn_scr[...], None, None, p_scr, g_scr, hn_scr, ycat_scr, ubuf, caug, m_scr,
                        wg_ref, mhg_ref, cw_ref, big_ref, bfg_ref, tb=CHUNK, n_pad=CHUNK - N_META)
        caug_out_ref[...] = caug[...]
        m_out_ref[...] = m_scr[...]
        ut_out_ref[...] = ubuf[0:SUBLANES, :]


def _prep_call(w_t, meta_tokens, small):
    ng, mhg, cw, big, bfg = small
    n_gates = 2 * M_HEADS

    def const(shape):
        nd = len(shape)
        return pl.BlockSpec(shape, lambda t: (0,) * nd)

    in_specs = [
        pl.BlockSpec((PREP_TILES * MXU_COLS, D_MODEL), lambda t: (t, 0)),
        const(meta_tokens.shape),
        pl.BlockSpec((n_gates, D_MODEL), lambda t: (MAIN_WIDTH // n_gates, 0)),
        const(ng.shape), const(mhg.shape), const(cw.shape),
        const(big.shape), const(bfg.shape),
    ]
    out_shape = (
        jax.ShapeDtypeStruct((N_TILES, D_MODEL, MXU_COLS), _BF),
        jax.ShapeDtypeStruct((D_MODEL, LANES), _BF),
        jax.ShapeDtypeStruct((M_HEADS, HEAD_DIM, 2 * LANES), _F32),
        jax.ShapeDtypeStruct((1, LANES), _F32),
        jax.ShapeDtypeStruct((SUBLANES, CONV_WIDTH), _F32),
    )
    out_specs = (pl.BlockSpec((PREP_TILES, D_MODEL, MXU_COLS), lambda t: (t, 0, 0)),) + tuple(
        const(s.shape) for s in out_shape[1:])
    scratch = [
        pltpu.VMEM((CHUNK, D_MODEL), _BF),
        pltpu.VMEM((N_TILES, CHUNK, MXU_COLS), _F32),
    ] + _sequence_scratch(CHUNK)
    return pl.pallas_call(
        _prep_body,
        grid=(N_TILES // PREP_TILES,),
        in_specs=in_specs,
        out_specs=out_specs,
        out_shape=out_shape,
        scratch_shapes=scratch,
        compiler_params=pltpu.CompilerParams(
            dimension_semantics=("arbitrary",), vmem_limit_bytes=56 * 1024 * 1024),
        name="prep_meta",
    )(w_t, meta_tokens, w_t, ng, mhg, cw, big, bfg)


def _prompt_body(x_ref, xnext_ref, w3_ref, wg_ref, wo_ref, ng_ref, fgain_ref, mhg_ref, cw_ref, big_ref, bfg_ref,
                 caug0_ref, m0_ref, ut0_ref,
                 y_ref, c_out_ref, n_out_ref, m_out_ref, conv_out_ref,
                 xn_scr, out_scr, p_scr, g_scr, hn_scr, ycat_scr, ubuf, caug, m_scr, *, tb, nt, n_blocks):
    s = pl.program_id(0)
    i = s % nt
    live = s < n_blocks

    @pl.when(s == 0)
    def _():
        xn_scr[...] = _rms_rows(x_ref[0], ng_ref[...]).astype(_BF)
        out_scr[...] = jnp.zeros_like(out_scr)

    @pl.when(live & (i == 0))
    def _():
        caug[...] = caug0_ref[...]
        m_scr[...] = m0_ref[...]
        ubuf[0:SUBLANES, :] = ut0_ref[...]

    @pl.when(live)
    def _():
        xn = xn_scr[...]

        def project_tile(t):
            p_scr[:, _tile_cols(t)] = _dot(xn, w3_ref[t])

        def emit_rows(r0, r1):
            out_scr[r0:r1, :] = _dot(ycat_scr[r0:r1, :], wo_ref[...]) + x_ref[0, r0:r1, :]

        def norm_previous(r0, r1):
            def fn():
                y_ref[0, r0:r1, :] = _rms_rows(out_scr[r0:r1, :], fgain_ref[...])
            return fn

        def norm_next(r0, r1):
            def fn():
                xn_scr[r0:r1, :] = _rms_rows(xnext_ref[0, r0:r1, :], ng_ref[...]).astype(_BF)
            return fn

        q4 = tb // 4
        side_work = {2 + 2 * j: [norm_previous(j * q4, (j + 1) * q4)] for j in range(4)}
        side_work.update({10 + 2 * j: [norm_next(j * q4, (j + 1) * q4)] for j in range(4)})
        _sequence_block(xn, project_tile, emit_rows, p_scr, g_scr, hn_scr, ycat_scr, ubuf, caug, m_scr,
                        wg_ref, mhg_ref, cw_ref, big_ref, bfg_ref, tb=tb, n_pad=0, side_work=side_work)

    @pl.when(s == n_blocks)
    def _():
        y_ref[0] = _rms_rows(out_scr[...], fgain_ref[...])

    @pl.when(live & (i == nt - 1))
    def _():
        m_out_ref[0] = m_scr[...]
        conv_out_ref[0] = ubuf[SUBLANES - (CONV_K - 1):SUBLANES, :]
        for h in range(M_HEADS):
            c_out_ref[0, h] = caug[h, :, 0:HEAD_DIM]
            n_out_ref[0, h:h + 1, :] = caug[h, :, LANES:2 * LANES].T[0:1, :]


def _prompt_call(x, w3, wg, wo, small, fgain, caug0, m0, ut0, *, tb):
    ng, mhg, cw, big, bfg = small
    nb, t, _ = x.shape
    nt = t // tb
    n_blocks = nb * nt

    def const(shape):
        nd = len(shape)
        return pl.BlockSpec(shape, lambda s: (0,) * nd)

    def block_of(s):
        return jnp.minimum(s, n_blocks - 1)

    def rows_spec(step_to_block):
        return pl.BlockSpec((1, tb, D_MODEL), lambda s: (step_to_block(s) // nt, step_to_block(s) % nt, 0))

    def state_spec(*tail):
        nd = len(tail)
        return pl.BlockSpec((1,) + tail, lambda s: (block_of(s) // nt,) + (0,) * nd)

    in_specs = [
        rows_spec(block_of),
        rows_spec(lambda s: block_of(s + 1)),
        const(w3.shape), const(wg.shape), const(wo.shape), const(ng.shape), const(fgain.shape),
        const(mhg.shape), const(cw.shape), const(big.shape), const(bfg.shape),
        const(caug0.shape), const(m0.shape), const(ut0.shape),
    ]
    out_shape = (
        jax.ShapeDtypeStruct((nb, t, D_MODEL), _F32),
        jax.ShapeDtypeStruct((nb, M_HEADS, HEAD_DIM, HEAD_DIM), _F32),
        jax.ShapeDtypeStruct((nb, M_HEADS, HEAD_DIM), _F32),
        jax.ShapeDtypeStruct((nb, 1, LANES), _F32),
        jax.ShapeDtypeStruct((nb, CONV_K - 1, CONV_WIDTH), _F32),
    )
    out_specs = (
        rows_spec(lambda s: jnp.maximum(s - 1, 0)),
        state_spec(M_HEADS, HEAD_DIM, HEAD_DIM),
        state_spec(M_HEADS, HEAD_DIM),
        state_spec(1, LANES),
        state_spec(CONV_K - 1, CONV_WIDTH),
    )
    scratch = [
        pltpu.VMEM((tb, D_MODEL), _BF),
        pltpu.VMEM((tb, D_MODEL), _F32),
    ] + _sequence_scratch(tb)
    return pl.pallas_call(
        functools.partial(_prompt_body, tb=tb, nt=nt, n_blocks=n_blocks),
        grid=(n_blocks + 1,),
        in_specs=in_specs,
        out_specs=out_specs,
        out_shape=out_shape,
        scratch_shapes=scratch,
        compiler_params=pltpu.CompilerParams(
            dimension_semantics=("arbitrary",), vmem_limit_bytes=56 * 1024 * 1024),
        name="prompt_layer",
    )(x, x, w3, wg, wo, ng, fgain, mhg, cw, big, bfg, caug0, m0, ut0)


def _sample_body(x_ref, xnext_ref, w3_ref, wg_ref, wo_ref, ng_ref, fgain_ref, mhg_ref, cw_ref, big_ref, bfg_ref,
                 sconv_ref, m_ref, n_ref, c_ref,
                 y_ref, conv_out_ref, c_out_ref, n_out_ref, m_out_ref,
                 xn_scr, out_scr, p_scr, ni_scr, sc_scr, ycat_scr, *, steps):
    i = pl.program_id(0)

    @pl.when(i == 0)
    def _():
        sc_scr[...] = jnp.zeros_like(sc_scr)
        xn_scr[...] = _rms_rows(x_ref[...], ng_ref[...]).astype(_BF)
        out_scr[...] = jnp.zeros_like(out_scr)

    @pl.when(i < steps)
    def _():
        _sample_step(x_ref, xnext_ref, w3_ref, wg_ref, wo_ref, ng_ref, fgain_ref, mhg_ref, cw_ref, big_ref,
                     bfg_ref, sconv_ref, m_ref, n_ref, c_ref, y_ref, conv_out_ref, c_out_ref, n_out_ref,
                     m_out_ref, xn_scr, out_scr, p_scr, ni_scr, sc_scr, ycat_scr)

    @pl.when(i == steps)
    def _():
        y_ref[...] = _rms_rows(out_scr[...], fgain_ref[...])


def _sample_step(x_ref, xnext_ref, w3_ref, wg_ref, wo_ref, ng_ref, fgain_ref, mhg_ref, cw_ref, big_ref, bfg_ref,
                 sconv_ref, m_ref, n_ref, c_ref,
                 y_ref, conv_out_ref, c_out_ref, n_out_ref, m_out_ref,
                 xn_scr, out_scr, p_scr, ni_scr, sc_scr, ycat_scr):
    xn = xn_scr[...]
    g_raw = _dot(xn, wg_ref[...])
    for t in QKV_TILES:
        p_scr[:, _tile_cols(t)] = _dot(xn, w3_ref[t])
    y_ref[...] = _rms_rows(out_scr[...], fgain_ref[...])

    row = lax.broadcasted_iota(jnp.int32, (CHUNK, CHUNK), 0)
    col = lax.broadcasted_iota(jnp.int32, (CHUNK, CHUNK), 1)
    pos = row % DEC_SEQ
    grp = row // DEC_SEQ
    causal = (grp == (col // DEC_SEQ)) & (col <= row)
    grow = lax.broadcasted_iota(jnp.int32, (SEQ_PER_STEP, CHUNK), 0)
    gcol = lax.broadcasted_iota(jnp.int32, (SEQ_PER_STEP, CHUNK), 1)
    pick_last = (gcol == grow * DEC_SEQ + (DEC_SEQ - 1)).astype(_F32)
    seq_sum = (gcol // DEC_SEQ == grow).astype(_BF)
    erow = lax.broadcasted_iota(jnp.int32, (CHUNK, SEQ_PER_STEP), 0)
    ecol = lax.broadcasted_iota(jnp.int32, (CHUNK, SEQ_PER_STEP), 1)
    own_seq = erow // DEC_SEQ == ecol
    ones_blk = jnp.ones((CHUNK, LANES), _BF)

    def rows_from_seqs(v16):
        return jnp.broadcast_to(v16[:, None, :], (SEQ_PER_STEP, DEC_SEQ, LANES)).reshape(CHUNK, LANES)

    def last_row_of_seq(v):
        v3 = v.reshape(SEQ_PER_STEP, DEC_SEQ, LANES)
        return jnp.broadcast_to(v3[:, DEC_SEQ - 1:DEC_SEQ, :], v3.shape).reshape(CHUNK, LANES)

    ig, lf = _gate_columns(g_raw, big_ref[...], bfg_ref[...])
    b = _cumsum_rows(lf, pos, DEC_SEQ)
    a = ig - b
    m_prev = rows_from_seqs(m_ref[...])
    big_m = jnp.maximum(m_prev, _cummax_rows(a, pos, DEC_SEQ))
    m_t = b + big_m
    inter = jnp.exp(m_prev - big_m)
    emt = jnp.exp(-m_t)
    m_last = last_row_of_seq(big_m)
    w = jnp.exp(a - m_last)
    decay = jnp.exp(m_prev - m_last)
    m_out_ref[...] = _dot_exact(pick_last, m_t)
    dec16 = _dot_exact(pick_last, decay)
    a_t = a.T

    for t in REST_TILES:
        p_scr[:, _tile_cols(t)] = _dot(xn, w3_ref[t])
    xn_scr[...] = _rms_rows(xnext_ref[...], ng_ref[...]).astype(_BF)

    heads = []
    for h in range(M_HEADS):
        hs = h * HEAD_DIM
        q = p_scr[:, OFF_Q + hs:OFF_Q + hs + HEAD_DIM]
        k = p_scr[:, OFF_K + hs:OFF_K + hs + HEAD_DIM] * K_SCALE
        v = p_scr[:, OFF_V + hs:OFF_V + hs + HEAD_DIM]
        q_bf = q.astype(_BF)
        a_b = jnp.broadcast_to(a_t[h:h + 1, :], (CHUNK, CHUNK))
        m_b = jnp.broadcast_to(big_m[:, h:h + 1], (CHUNK, CHUNK))
        d = jnp.exp(jnp.where(causal, a_b - m_b, NEG_INF))
        w_b = jnp.broadcast_to(w[:, h:h + 1], (CHUNK, LANES))
        heads.append(dict(q_bf=q_bf, k=k, v=v, d=d, w_b=w_b, qk=_dot_nt(q_bf, k.astype(_BF))))

    u = p_scr[:, OFF_GC:OFF_GC + CONV_WIDTH] * p_scr[:, OFF_HC:OFF_HC + CONV_WIDTH]
    for g in range(SEQ_PER_STEP):
        sc_scr[g * DEC_SEQ:g * DEC_SEQ + CONV_K - 1, :] = sconv_ref[g]
        conv_out_ref[g] = u[(g + 1) * DEC_SEQ - (CONV_K - 1):(g + 1) * DEC_SEQ, :]
    sc = sc_scr[...]
    posw = lax.broadcasted_iota(jnp.int32, (CHUNK, CONV_WIDTH), 0) % DEC_SEQ
    u_m1 = jnp.where(posw == 0, pltpu.roll(sc, CHUNK - 1, 0), pltpu.roll(u, 1, 0))
    u_m2 = jnp.where(posw < 2, sc, pltpu.roll(u, 2, 0))
    y_conv = _conv_out(p_scr[:, OFF_GB:OFF_GB + CONV_WIDTH], p_scr[:, OFF_ZC:OFF_ZC + CONV_WIDTH],
                       u, u_m1, u_m2, cw_ref)
    ycat_scr[:, 0:CONV_WIDTH] = y_conv.astype(_BF)

    for h in range(M_HEADS):
        hs = h * HEAD_DIM
        st = heads[h]
        wv = st["w_b"] * st["v"]
        k_t_bf = st["k"].T.astype(_BF)
        dec_b = jnp.broadcast_to(decay[:, h:h + 1], (CHUNK, HEAD_DIM))

        for g in range(0, SEQ_PER_STEP, 2):
            pair_rows = slice(g * DEC_SEQ, (g + 2) * DEC_SEQ)
            c0, c1 = c_ref[g, h], c_ref[g + 1, h]
            qc = _dot(st["q_bf"][pair_rows, :], jnp.concatenate([c0.astype(_BF), c1.astype(_BF)], axis=1))
            ni_scr[g * DEC_SEQ:(g + 1) * DEC_SEQ, hs:hs + HEAD_DIM] = qc[0:DEC_SEQ, 0:HEAD_DIM]
            ni_scr[(g + 1) * DEC_SEQ:(g + 2) * DEC_SEQ, hs:hs + HEAD_DIM] = qc[DEC_SEQ:, HEAD_DIM:]
            wv_pair = jnp.concatenate([jnp.where(grp == g, wv, 0.0).astype(_BF),
                                       jnp.where(grp == g + 1, wv, 0.0).astype(_BF)], axis=1)
            upd = _dot(k_t_bf, wv_pair)
            dec0 = jnp.broadcast_to(dec_b[g * DEC_SEQ:g * DEC_SEQ + 1, :], (HEAD_DIM, HEAD_DIM))
            dec1 = jnp.broadcast_to(dec_b[(g + 1) * DEC_SEQ:(g + 1) * DEC_SEQ + 1, :], (HEAD_DIM, HEAD_DIM))
            c_out_ref[g, h] = dec0 * c0 + upd[:, 0:HEAD_DIM]
            c_out_ref[g + 1, h] = dec1 * c1 + upd[:, HEAD_DIM:]

        s = st["qk"] * st["d"]
        sv = _dot(s.astype(_BF), jnp.concatenate([st["v"].astype(_BF), ones_blk], axis=1))
        n_old = n_ref[:, h, :]
        num_inter = ni_scr[:, hs:hs + HEAD_DIM]
        qn = _dot_nt(st["q_bf"], n_old.astype(_BF))
        den_inter = jnp.sum(jnp.where(own_seq, qn, 0.0), axis=-1, keepdims=True)
        inter_b = jnp.broadcast_to(inter[:, h:h + 1], (CHUNK, LANES))
        num = sv[:, :LANES] + inter_b * num_inter
        den = sv[:, LANES:] + inter_b * den_inter
        emt_b = jnp.broadcast_to(emt[:, h:h + 1], (CHUNK, LANES))
        ni_scr[:, hs:hs + HEAD_DIM] = _head_norm(num, den, emt_b, mhg_ref[:, hs:hs + HEAD_DIM])
        n_upd = _dot(seq_sum, (st["w_b"] * st["k"]).astype(_BF))
        dec16_b = jnp.broadcast_to(dec16[:, h:h + 1], (SEQ_PER_STEP, HEAD_DIM))
        n_out_ref[:, h, :] = dec16_b * n_old + n_upd

    ycat_scr[:, CONV_WIDTH:CONV_WIDTH + M_WIDTH] = _gated_heads(ni_scr[...], p_scr).astype(_BF)
    out_scr[...] = _dot(ycat_scr[...], wo_ref[...]) + x_ref[...]


def _sample_call(xs, w3, wg, wo, small, fgain, state_conv, m_pad, n_state, c_state):
    ng, mhg, cw, big, bfg = small
    rows = xs.shape[0]
    nseq = c_state.shape[0]
    steps = rows // CHUNK

    def const(shape):
        nd = len(shape)
        return pl.BlockSpec(shape, lambda i: (0,) * nd)

    def block_of(i):
        return jnp.minimum(i, steps - 1)

    def seq_spec(*tail):
        nd = len(tail)
        return pl.BlockSpec((SEQ_PER_STEP,) + tail, lambda i: (block_of(i),) + (0,) * nd)

    def rows_spec(step_to_block):
        return pl.BlockSpec((CHUNK, D_MODEL), lambda i: (step_to_block(i), 0))

    conv_spec = seq_spec(CONV_K - 1, CONV_WIDTH)
    c_spec = seq_spec(M_HEADS, HEAD_DIM, HEAD_DIM)
    n_spec = seq_spec(M_HEADS, HEAD_DIM)
    m_spec = seq_spec(LANES)
    in_specs = [
        rows_spec(block_of), rows_spec(lambda i: block_of(i + 1)),
        const(w3.shape), const(wg.shape), const(wo.shape), const(ng.shape), const(fgain.shape),
        const(mhg.shape), const(cw.shape), const(big.shape), const(bfg.shape),
        conv_spec, m_spec, n_spec, c_spec,
    ]
    out_shape = (
        jax.ShapeDtypeStruct((rows, D_MODEL), _F32),
        jax.ShapeDtypeStruct(state_conv.shape, _F32),
        jax.ShapeDtypeStruct(c_state.shape, _F32),
        jax.ShapeDtypeStruct(n_state.shape, _F32),
        jax.ShapeDtypeStruct((nseq, LANES), _F32),
    )
    out_specs = (rows_spec(lambda i: jnp.maximum(i - 1, 0)), conv_spec, c_spec, n_spec, m_spec)
    scratch = [
        pltpu.VMEM((CHUNK, D_MODEL), _BF),
        pltpu.VMEM((CHUNK, D_MODEL), _F32),
        pltpu.VMEM((CHUNK, MAIN_WIDTH), _F32),
        pltpu.VMEM((CHUNK, M_WIDTH), _F32),
        pltpu.VMEM((CHUNK, CONV_WIDTH), _F32),
        pltpu.VMEM((CHUNK, D_MODEL), _BF),
    ]
    return pl.pallas_call(
        functools.partial(_sample_body, steps=steps),
        grid=(steps + 1,),
        in_specs=in_specs,
        out_specs=out_specs,
        out_shape=out_shape,
        scratch_shapes=scratch,
        compiler_params=pltpu.CompilerParams(
            dimension_semantics=("arbitrary",), vmem_limit_bytes=56 * 1024 * 1024),
        name="sample_layer",
    )(xs, xs, w3, wg, wo, ng, fgain, mhg, cw, big, bfg, state_conv, m_pad, n_state, c_state)


def _pad_lanes(v):
    return jnp.pad(v.astype(_F32), (0, LANES - v.shape[0])).reshape(1, LANES)


def kernel(x_prompt, x_sample, state_conv, state_mlstm_c, state_mlstm_n, state_mlstm_m, meta_tokens, norm_gain,
           w_in, conv_w, b_igate, b_fgate, mh_norm_gain, w_out, final_norm_gain):
    nseq, dec_seq, _ = x_sample.shape
    assert dec_seq == DEC_SEQ and meta_tokens.shape[0] == N_META

    w_t = w_in.T
    small = (norm_gain.reshape(1, D_MODEL), mh_norm_gain.reshape(1, M_WIDTH), conv_w,
             _pad_lanes(b_igate), _pad_lanes(b_fgate))
    fgain = final_norm_gain.reshape(1, D_MODEL)
    wo = w_out.astype(_BF)

    w3, wg, caug_meta, m_meta, ut_meta = _prep_call(w_t, meta_tokens.astype(_F32), small)

    y_prompt, c_p, n_p, m_p, conv_p = _prompt_call(
        x_prompt, w3, wg, wo, small, fgain, caug_meta, m_meta, ut_meta, tb=512)
    m_pp = m_p[:, 0, :M_HEADS]

    xs = x_sample.reshape(nseq * DEC_SEQ, D_MODEL)
    m_pad = jnp.pad(state_mlstm_m, ((0, 0), (0, LANES - M_HEADS)))
    ys, conv_s, c_s, n_s, m_s = _sample_call(xs, w3, wg, wo, small, fgain, state_conv, m_pad, state_mlstm_n,
                                             state_mlstm_c)
    y_sample = ys.reshape(nseq, DEC_SEQ, D_MODEL)
    m_ss = m_s[:, :M_HEADS]

    return (y_prompt, y_sample, conv_p, c_p, n_p, m_pp, conv_s, c_s, n_s, m_ss)
```

```python
import functools

import jax
import jax.numpy as jnp
from jax import lax
from jax.experimental import pallas as pl
from jax.experimental.pallas import tpu as pltpu

D_MODEL = 1024
CONV_WIDTH = 512
M_WIDTH = 512
M_HEADS = 4
HEAD_DIM = 128
N_META = 16
CONV_K = 3
EPS = 1e-6
MAIN_WIDTH = 4 * CONV_WIDTH + 5 * M_WIDTH
LANES = 128
SUBLANES = 8
BF16_TILE_ROWS = 2 * SUBLANES
MXU_COLS = 256
N_TILES = MAIN_WIDTH // MXU_COLS
PREP_TILES = 3
CHUNK = 128
DEC_SEQ = 8
SEQ_PER_STEP = CHUNK // DEC_SEQ
NEG_INF = float("-inf")
K_SCALE = HEAD_DIM ** -0.5

OFF_GB, OFF_GC, OFF_HC, OFF_ZC = 0, 512, 1024, 1536
OFF_Q, OFF_K, OFF_V, OFF_O, OFF_ZM = 2048, 2560, 3072, 3584, 4096
QKV_TILES = tuple(range(OFF_Q // MXU_COLS, OFF_O // MXU_COLS))
REST_TILES = tuple(t for t in range(N_TILES) if t not in QKV_TILES)

_SMEM_SPEC = pl.BlockSpec(memory_space=pltpu.SMEM)
_HI = lax.Precision.HIGHEST
_BF = jnp.bfloat16
_F32 = jnp.float32


def _dot(a, b):
    return jnp.dot(a, b, preferred_element_type=_F32)


def _dot_exact(a, b):
    return jnp.dot(a, b, precision=_HI, preferred_element_type=_F32)


def _dot_nt(a, b):
    return lax.dot_general(a, b, (((1,), (1,)), ((), ())), preferred_element_type=_F32)


def _rms_rows(x, gain_row):
    return x * lax.rsqrt(jnp.mean(x * x, axis=-1, keepdims=True) + EPS) * gain_row


def _sigmoid(x):
    return 1.0 / (1.0 + jnp.exp(-x))


def _silu(x):
    return x * _sigmoid(x)


def _tile_cols(t):
    return slice(t * MXU_COLS, (t + 1) * MXU_COLS)


def _cummax_rows(x, pos, period):
    s = 1
    while s < period:
        shifted = pltpu.roll(x, s, 0)
        x = jnp.maximum(x, jnp.where(pos >= s, shifted, NEG_INF))
        s *= 2
    return x


def _cumsum_rows(x, pos, period):
    s = 1
    while s < period:
        x = x + jnp.where(pos >= s, pltpu.roll(x, s, 0), 0.0)
        s *= 2
    return x


def _bias_row(b_ref):
    lane = lax.broadcasted_iota(jnp.int32, (1, LANES), 1)
    row = jnp.zeros((1, LANES), _F32)
    for h in range(M_HEADS):
        row = jnp.where(lane == h, b_ref[h], row)
    return row


def _gate_columns(g_raw, big_ref, bfg_ref):
    ig = g_raw + _bias_row(big_ref)
    lf = jax.nn.log_sigmoid(pltpu.roll(g_raw, LANES - M_HEADS, 1) + _bias_row(bfg_ref))
    return ig, lf


def _head_norm(num, den, emt_b, gain_row):
    hh = num / jnp.maximum(jnp.abs(den), emt_b)
    return hh * lax.rsqrt(jnp.mean(hh * hh, axis=-1, keepdims=True) + EPS) * gain_row


def _conv_out(gb, zc, u, u_m1, u_m2, cw_ref):
    yc = cw_ref[0:1, :] * u_m2 + cw_ref[1:2, :] * u_m1 + cw_ref[2:3, :] * u
    return gb * yc * _silu(zc)


def _gated_heads(hn, p_scr):
    return hn * _sigmoid(p_scr[:, OFF_O:OFF_O + M_WIDTH]) * _silu(p_scr[:, OFF_ZM:OFF_ZM + M_WIDTH])


def _sequence_block(xn, project_tile, emit_rows, p_scr, g_scr, hn_scr, ycat_scr, ubuf, caug, m_scr,
                    wg_ref, mhg_ref, cw_ref, big_ref, bfg_ref, *, tb, n_pad, side_work=None):
    side_work = side_work or {}
    anchors = []
    g_scr[...] = _dot(xn, wg_ref[...])
    for fn in side_work.get("first", ()):
        fn()
    late_tiles = []
    if project_tile is not None:
        for t in QKV_TILES:
            project_tile(t)
        early = 2
        for t in REST_TILES[:early]:
            project_tile(t)
        late_tiles = list(REST_TILES[early:])

    row = lax.broadcasted_iota(jnp.int32, (CHUNK, CHUNK), 0)
    col = lax.broadcasted_iota(jnp.int32, (CHUNK, CHUNK), 1)
    causal = col <= row
    ones_blk = jnp.ones((CHUNK, LANES), _BF)

    n_chunks = tb // CHUNK
    gates = []
    m_prev = m_scr[...]
    for c in range(n_chunks):
        rows = slice(c * CHUNK, (c + 1) * CHUNK)
        ig, lf = _gate_columns(g_scr[rows, :], big_ref, bfg_ref)
        if n_pad:
            ig = jnp.where(row >= n_pad, ig, NEG_INF)
            lf = jnp.where(row >= n_pad, lf, 0.0)
        b = _cumsum_rows(lf, row, CHUNK)
        a = ig - b
        big_m = jnp.maximum(m_prev, _cummax_rows(a, row, CHUNK))
        m_t = b + big_m
        m_last = big_m[CHUNK - 1:CHUNK, :]
        gates.append(dict(a_t=a.T, big_m=big_m, inter=jnp.exp(m_prev - big_m), emt=jnp.exp(-m_t),
                          w=jnp.exp(a - m_last), decay=jnp.exp(m_prev - m_last)))
        m_prev = m_t[CHUNK - 1:CHUNK, :]
    m_scr[...] = m_prev

    def stage_a(c, h):
        rows = slice(c * CHUNK, (c + 1) * CHUNK)
        hs = h * HEAD_DIM
        g = gates[c]
        q = p_scr[rows, OFF_Q + hs:OFF_Q + hs + HEAD_DIM]
        k = p_scr[rows, OFF_K + hs:OFF_K + hs + HEAD_DIM] * K_SCALE
        a_b = jnp.broadcast_to(g["a_t"][h:h + 1, :], (CHUNK, CHUNK))
        m_b = jnp.broadcast_to(g["big_m"][:, h:h + 1], (CHUNK, CHUNK))
        d = jnp.exp(jnp.where(causal, a_b - m_b, NEG_INF))
        q_inter = (jnp.broadcast_to(g["inter"][:, h:h + 1], (CHUNK, HEAD_DIM)) * q).astype(_BF)
        return dict(qk=_dot_nt(q.astype(_BF), k.astype(_BF)), d=d, q_inter=q_inter, k=k)

    def stage_b(c, h, st):
        rows = slice(c * CHUNK, (c + 1) * CHUNK)
        hs = h * HEAD_DIM
        g = gates[c]
        v = p_scr[rows, OFF_V + hs:OFF_V + hs + HEAD_DIM]
        s = st["qk"] * st["d"]
        lhs = jnp.concatenate([s.astype(_BF), st["q_inter"]], axis=1)
        rhs = jnp.concatenate([jnp.concatenate([v.astype(_BF), ones_blk], axis=1), caug[h].astype(_BF)], axis=0)
        tot = _dot(lhs, rhs)
        emt_b = jnp.broadcast_to(g["emt"][:, h:h + 1], (CHUNK, LANES))
        hn_scr[rows, hs:hs + HEAD_DIM] = _head_norm(tot[:, :LANES], tot[:, LANES:], emt_b,
                                                    mhg_ref[:, hs:hs + HEAD_DIM])
        w_b = jnp.broadcast_to(g["w"][:, h:h + 1], (CHUNK, LANES))
        wv = jnp.concatenate([(w_b * v).astype(_BF), w_b.astype(_BF)], axis=1)
        upd = _dot(st["k"].T.astype(_BF), wv)
        decay_b = jnp.broadcast_to(g["decay"][:, h:h + 1], (HEAD_DIM, 2 * LANES))
        caug[h] = decay_b * caug[h] + upd

    def finish_rows(r0, r1):
        rr = slice(r0, r1)
        n = r1 - r0
        u = p_scr[rr, OFF_GC:OFF_GC + CONV_WIDTH] * p_scr[rr, OFF_HC:OFF_HC + CONV_WIDTH]
        ubuf[SUBLANES + r0:SUBLANES + r1, :] = u
        u_m1 = ubuf[SUBLANES - 1 + r0:SUBLANES - 1 + r0 + n, :]
        u_m2 = ubuf[SUBLANES - 2 + r0:SUBLANES - 2 + r0 + n, :]
        y_conv = _conv_out(p_scr[rr, OFF_GB:OFF_GB + CONV_WIDTH], p_scr[rr, OFF_ZC:OFF_ZC + CONV_WIDTH],
                           u, u_m1, u_m2, cw_ref)
        ycat_scr[rr, 0:CONV_WIDTH] = y_conv.astype(_BF)
        y_m = (hn_scr[rr, :] * _sigmoid(p_scr[rr, OFF_O:OFF_O + M_WIDTH])
               * _silu(p_scr[rr, OFF_ZM:OFF_ZM + M_WIDTH]))
        ycat_scr[rr, CONV_WIDTH:CONV_WIDTH + M_WIDTH] = y_m.astype(_BF)
        while anchors:
            tile = ycat_scr[0:BF16_TILE_ROWS, 0:LANES].astype(_F32) + anchors.pop()
            ycat_scr[0:BF16_TILE_ROWS, 0:LANES] = tile.astype(_BF)
        if emit_rows is not None:
            emit_rows(r0, r1)

    units = [(c, h) for c in range(n_chunks) for h in range(M_HEADS)]
    lag = min(3, len(units))
    n_groups = 1
    group_units = len(units) // n_groups
    assert len(late_tiles) <= group_units + lag
    pending = {}
    for t in range(len(units) + lag):
        if t < len(units):
            pending[t] = stage_a(*units[t])
        if t < len(late_tiles):
            project_tile(late_tiles[t])
        for fn in side_work.get(t, ()):
            zero = fn()
            if zero is not None:
                anchors.append(zero)
        if t >= lag:
            stage_b(*units[t - lag], pending.pop(t - lag))
            done = t - lag + 1
            if done % group_units == 0:
                gi = done // group_units
                finish_rows((gi - 1) * tb // n_groups, gi * tb // n_groups)
    ubuf[0:SUBLANES, :] = ubuf[tb:tb + SUBLANES, :]


def _sequence_scratch(tb):
    return [
        pltpu.VMEM((tb, MAIN_WIDTH), _F32),
        pltpu.VMEM((tb, LANES), _F32),
        pltpu.VMEM((tb, M_WIDTH), _F32),
        pltpu.VMEM((tb, D_MODEL), _BF),
        pltpu.VMEM((tb + 2 * SUBLANES, CONV_WIDTH), _F32),
        pltpu.VMEM((M_HEADS, HEAD_DIM, 2 * LANES), _F32),
        pltpu.VMEM((1, LANES), _F32),
    ]


def _prep_body(wt_ref, x_ref, wgt_ref, ng_ref, mhg_ref, cw_ref, big_ref, bfg_ref,
               w3_ref, wg_ref, caug_out_ref, m_out_ref, ut_out_ref,
               xn_scr, p3_scr, p_scr, g_scr, hn_scr, ycat_scr, ubuf, caug, m_scr):
    t = pl.program_id(0)

    @pl.when(t == 0)
    def _():
        xn_scr[...] = jnp.zeros_like(xn_scr)
        xn_scr[CHUNK - N_META:CHUNK, :] = _rms_rows(x_ref[...], ng_ref[...]).astype(_BF)
        gate_rows = jnp.concatenate(
            [wgt_ref[...], jnp.zeros((LANES - 2 * M_HEADS, D_MODEL), _F32)], axis=0)
        wg_ref[...] = gate_rows.T.astype(_BF)

    for j in range(PREP_TILES):
        tile = wt_ref[j * MXU_COLS:(j + 1) * MXU_COLS, :].T.astype(_BF)
        w3_ref[j] = tile
        p3_scr[t * PREP_TILES + j] = _dot(xn_scr[...], tile)

    @pl.when(t == N_TILES // PREP_TILES - 1)
    def _():
        for j in range(N_TILES):
            p_scr[:, _tile_cols(j)] = p3_scr[j]
        caug[...] = jnp.zeros_like(caug)
        m_scr[...] = jnp.zeros_like(m_scr)
        ubuf[0:SUBLANES, :] = jnp.zeros((SUBLANES, CONV_WIDTH), _F32)
        _sequence_block(xn_scr[...], None, None, p_scr, g_scr, hn_scr, ycat_scr, ubuf, caug, m_scr,
                        wg_ref, mhg_ref, cw_ref, big_ref, bfg_ref, tb=CHUNK, n_pad=CHUNK - N_META)
        caug_out_ref[...] = caug[...]
        m_out_ref[...] = m_scr[...]
        ut_out_ref[...] = ubuf[0:SUBLANES, :]


def _prep_call(w_t, meta_tokens, small):
    ng, mhg, cw, big, bfg = small
    n_gates = 2 * M_HEADS

    def const(shape):
        nd = len(shape)
        return pl.BlockSpec(shape, lambda t: (0,) * nd)

    in_specs = [
        pl.BlockSpec((PREP_TILES * MXU_COLS, D_MODEL), lambda t: (t, 0)),
        const(meta_tokens.shape),
        pl.BlockSpec((n_gates, D_MODEL), lambda t: (MAIN_WIDTH // n_gates, 0)),
        const(ng.shape), const(mhg.shape), const(cw.shape),
        _SMEM_SPEC, _SMEM_SPEC,
    ]
    out_shape = (
        jax.ShapeDtypeStruct((N_TILES, D_MODEL, MXU_COLS), _BF),
        jax.ShapeDtypeStruct((D_MODEL, LANES), _BF),
        jax.ShapeDtypeStruct((M_HEADS, HEAD_DIM, 2 * LANES), _F32),
        jax.ShapeDtypeStruct((1, LANES), _F32),
        jax.ShapeDtypeStruct((SUBLANES, CONV_WIDTH), _F32),
    )
    out_specs = (pl.BlockSpec((PREP_TILES, D_MODEL, MXU_COLS), lambda t: (t, 0, 0)),) + tuple(
        const(s.shape) for s in out_shape[1:])
    scratch = [
        pltpu.VMEM((CHUNK, D_MODEL), _BF),
        pltpu.VMEM((N_TILES, CHUNK, MXU_COLS), _F32),
    ] + _sequence_scratch(CHUNK)
    return pl.pallas_call(
        _prep_body,
        grid=(N_TILES // PREP_TILES,),
        in_specs=in_specs,
        out_specs=out_specs,
        out_shape=out_shape,
        scratch_shapes=scratch,
        compiler_params=pltpu.CompilerParams(
            dimension_semantics=("arbitrary",), vmem_limit_bytes=56 * 1024 * 1024),
        name="prep_meta",
    )(w_t, meta_tokens, w_t, ng, mhg, cw, big, bfg)


def _prompt_body(x_ref, xnext_ref, w3_ref, wg_ref, wo_ref, ng_ref, fgain_ref, mhg_ref, cw_ref, big_ref, bfg_ref,
                 caug0_ref, m0_ref, ut0_ref,
                 y_ref, c_out_ref, n_out_ref, m_out_ref, conv_out_ref,
                 xn_scr, out_scr, p_scr, g_scr, hn_scr, ycat_scr, ubuf, caug, m_scr, *, tb, nt, n_blocks):
    s = pl.program_id(0)
    i = s % nt
    live = s < n_blocks

    @pl.when(s == 0)
    def _():
        xn_scr[...] = _rms_rows(x_ref[0], ng_ref[...]).astype(_BF)
        out_scr[...] = jnp.zeros_like(out_scr)

    @pl.when(live & (i == 0))
    def _():
        caug[...] = caug0_ref[...]
        m_scr[...] = m0_ref[...]
        ubuf[0:SUBLANES, :] = ut0_ref[...]

    @pl.when(live)
    def _():
        xn = xn_scr[...]

        def project_tile(t):
            p_scr[:, _tile_cols(t)] = _dot(xn, w3_ref[t])

        def emit_rows(r0, r1):
            out_scr[r0:r1, :] = _dot(ycat_scr[r0:r1, :], wo_ref[...]) + x_ref[0, r0:r1, :]

        def norm_previous(r0, r1):
            def fn():
                y_ref[0, r0:r1, :] = _rms_rows(out_scr[r0:r1, :], fgain_ref[...])
            return fn

        def norm_next(r0, r1):
            def fn():
                xn_scr[r0:r1, :] = _rms_rows(xnext_ref[0, r0:r1, :], ng_ref[...]).astype(_BF)
            return fn

        q4 = tb // 4
        side_work = {2 + 2 * j: [norm_previous(j * q4, (j + 1) * q4)] for j in range(4)}
        side_work.update({10 + 2 * j: [norm_next(j * q4, (j + 1) * q4)] for j in range(4)})
        _sequence_block(xn, project_tile, emit_rows, p_scr, g_scr, hn_scr, ycat_scr, ubuf, caug, m_scr,
                        wg_ref, mhg_ref, cw_ref, big_ref, bfg_ref, tb=tb, n_pad=0, side_work=side_work)

    @pl.when(s == n_blocks)
    def _():
        y_ref[0] = _rms_rows(out_scr[...], fgain_ref[...])

    @pl.when(live & (i == nt - 1))
    def _():
        m_out_ref[0] = m_scr[...]
        conv_out_ref[0] = ubuf[SUBLANES - (CONV_K - 1):SUBLANES, :]
        for h in range(M_HEADS):
            c_out_ref[0, h] = caug[h, :, 0:HEAD_DIM]
            n_out_ref[0, h:h + 1, :] = caug[h, :, LANES:2 * LANES].T[0:1, :]


def _prompt_call(x, w3, wg, wo, small, fgain, caug0, m0, ut0, *, tb):
    ng, mhg, cw, big, bfg = small
    nb, t, _ = x.shape
    nt = t // tb
    n_blocks = nb * nt

    def const(shape):
        nd = len(shape)
        return pl.BlockSpec(shape, lambda s: (0,) * nd)

    def block_of(s):
        return jnp.minimum(s, n_blocks - 1)

    def rows_spec(step_to_block):
        return pl.BlockSpec((1, tb, D_MODEL), lambda s: (step_to_block(s) // nt, step_to_block(s) % nt, 0))

    def state_spec(*tail):
        nd = len(tail)
        return pl.BlockSpec((1,) + tail, lambda s: (block_of(s) // nt,) + (0,) * nd)

    in_specs = [
        rows_spec(block_of),
        rows_spec(lambda s: block_of(s + 1)),
        const(w3.shape), const(wg.shape), const(wo.shape), const(ng.shape), const(fgain.shape),
        const(mhg.shape), const(cw.shape), _SMEM_SPEC, _SMEM_SPEC,
        const(caug0.shape), const(m0.shape), const(ut0.shape),
    ]
    out_shape = (
        jax.ShapeDtypeStruct((nb, t, D_MODEL), _F32),
        jax.ShapeDtypeStruct((nb, M_HEADS, HEAD_DIM, HEAD_DIM), _F32),
        jax.ShapeDtypeStruct((nb, M_HEADS, HEAD_DIM), _F32),
        jax.ShapeDtypeStruct((nb, 1, LANES), _F32),
        jax.ShapeDtypeStruct((nb, CONV_K - 1, CONV_WIDTH), _F32),
    )
    out_specs = (
        rows_spec(lambda s: jnp.maximum(s - 1, 0)),
        state_spec(M_HEADS, HEAD_DIM, HEAD_DIM),
        state_spec(M_HEADS, HEAD_DIM),
        state_spec(1, LANES),
        state_spec(CONV_K - 1, CONV_WIDTH),
    )
    scratch = [
        pltpu.VMEM((tb, D_MODEL), _BF),
        pltpu.VMEM((tb, D_MODEL), _F32),
    ] + _sequence_scratch(tb)
    return pl.pallas_call(
        functools.partial(_prompt_body, tb=tb, nt=nt, n_blocks=n_blocks),
        grid=(n_blocks + 1,),
        in_specs=in_specs,
        out_specs=out_specs,
        out_shape=out_shape,
        scratch_shapes=scratch,
        compiler_params=pltpu.CompilerParams(
            dimension_semantics=("arbitrary",), vmem_limit_bytes=56 * 1024 * 1024),
        name="prompt_layer",
    )(x, x, w3, wg, wo, ng, fgain, mhg, cw, big, bfg, caug0, m0, ut0)


def _sample_body(x_ref, w3_ref, wg_ref, wo_ref, ng_ref, fgain_ref, mhg_ref, cw_ref, big_ref, bfg_ref,
                 sconv_ref, m_ref, n_ref, c_ref,
                 y_ref, conv_out_ref, c_out_ref, n_out_ref, m_out_ref,
                 p_scr, ni_scr, sc_scr, ycat_scr):
    @pl.when(pl.program_id(0) == 0)
    def _():
        sc_scr[...] = jnp.zeros_like(sc_scr)

    x = x_ref[...]
    xn = _rms_rows(x, ng_ref[...]).astype(_BF)
    g_raw = _dot(xn, wg_ref[...])
    for t in QKV_TILES:
        p_scr[:, _tile_cols(t)] = _dot(xn, w3_ref[t])

    row = lax.broadcasted_iota(jnp.int32, (CHUNK, CHUNK), 0)
    col = lax.broadcasted_iota(jnp.int32, (CHUNK, CHUNK), 1)
    pos = row % DEC_SEQ
    grp = row // DEC_SEQ
    causal = (grp == (col // DEC_SEQ)) & (col <= row)
    grow = lax.broadcasted_iota(jnp.int32, (SEQ_PER_STEP, CHUNK), 0)
    gcol = lax.broadcasted_iota(jnp.int32, (SEQ_PER_STEP, CHUNK), 1)
    pick_last = (gcol == grow * DEC_SEQ + (DEC_SEQ - 1)).astype(_F32)
    seq_sum = (gcol // DEC_SEQ == grow).astype(_BF)
    erow = lax.broadcasted_iota(jnp.int32, (CHUNK, SEQ_PER_STEP), 0)
    ecol = lax.broadcasted_iota(jnp.int32, (CHUNK, SEQ_PER_STEP), 1)
    own_seq = erow // DEC_SEQ == ecol
    ones_blk = jnp.ones((CHUNK, LANES), _BF)

    def rows_from_seqs(v16):
        return jnp.broadcast_to(v16[:, None, :], (SEQ_PER_STEP, DEC_SEQ, LANES)).reshape(CHUNK, LANES)

    def last_row_of_seq(v):
        v3 = v.reshape(SEQ_PER_STEP, DEC_SEQ, LANES)
        return jnp.broadcast_to(v3[:, DEC_SEQ - 1:DEC_SEQ, :], v3.shape).reshape(CHUNK, LANES)

    ig, lf = _gate_columns(g_raw, big_ref, bfg_ref)
    b = _cumsum_rows(lf, pos, DEC_SEQ)
    a = ig - b
    m_prev = rows_from_seqs(m_ref[...])
    big_m = jnp.maximum(m_prev, _cummax_rows(a, pos, DEC_SEQ))
    m_t = b + big_m
    inter = jnp.exp(m_prev - big_m)
    emt = jnp.exp(-m_t)
    m_last = last_row_of_seq(big_m)
    w = jnp.exp(a - m_last)
    decay = jnp.exp(m_prev - m_last)
    m_out_ref[...] = _dot_exact(pick_last, m_t)
    dec16 = _dot_exact(pick_last, decay)
    a_t = a.T

    for t in REST_TILES:
        p_scr[:, _tile_cols(t)] = _dot(xn, w3_ref[t])

    heads = []
    for h in range(M_HEADS):
        hs = h * HEAD_DIM
        q = p_scr[:, OFF_Q + hs:OFF_Q + hs + HEAD_DIM]
        k = p_scr[:, OFF_K + hs:OFF_K + hs + HEAD_DIM] * K_SCALE
        v = p_scr[:, OFF_V + hs:OFF_V + hs + HEAD_DIM]
        q_bf = q.astype(_BF)
        a_b = jnp.broadcast_to(a_t[h:h + 1, :], (CHUNK, CHUNK))
        m_b = jnp.broadcast_to(big_m[:, h:h + 1], (CHUNK, CHUNK))
        d = jnp.exp(jnp.where(causal, a_b - m_b, NEG_INF))
        w_b = jnp.broadcast_to(w[:, h:h + 1], (CHUNK, LANES))
        heads.append(dict(q_bf=q_bf, k=k, v=v, d=d, w_b=w_b, qk=_dot_nt(q_bf, k.astype(_BF))))

    u = p_scr[:, OFF_GC:OFF_GC + CONV_WIDTH] * p_scr[:, OFF_HC:OFF_HC + CONV_WIDTH]
    for g in range(SEQ_PER_STEP):
        sc_scr[g * DEC_SEQ:g * DEC_SEQ + CONV_K - 1, :] = sconv_ref[g]
        conv_out_ref[g] = u[(g + 1) * DEC_SEQ - (CONV_K - 1):(g + 1) * DEC_SEQ, :]
    sc = sc_scr[...]
    posw = lax.broadcasted_iota(jnp.int32, (CHUNK, CONV_WIDTH), 0) % DEC_SEQ
    u_m1 = jnp.where(posw == 0, pltpu.roll(sc, CHUNK - 1, 0), pltpu.roll(u, 1, 0))
    u_m2 = jnp.where(posw < 2, sc, pltpu.roll(u, 2, 0))
    y_conv = _conv_out(p_scr[:, OFF_GB:OFF_GB + CONV_WIDTH], p_scr[:, OFF_ZC:OFF_ZC + CONV_WIDTH],
                       u, u_m1, u_m2, cw_ref)
    ycat_scr[:, 0:CONV_WIDTH] = y_conv.astype(_BF)

    for h in range(M_HEADS):
        hs = h * HEAD_DIM
        st = heads[h]
        wv = st["w_b"] * st["v"]
        k_t_bf = st["k"].T.astype(_BF)
        dec_b = jnp.broadcast_to(decay[:, h:h + 1], (CHUNK, HEAD_DIM))

        for g in range(0, SEQ_PER_STEP, 2):
            pair_rows = slice(g * DEC_SEQ, (g + 2) * DEC_SEQ)
            c0, c1 = c_ref[g, h], c_ref[g + 1, h]
            qc = _dot(st["q_bf"][pair_rows, :], jnp.concatenate([c0.astype(_BF), c1.astype(_BF)], axis=1))
            ni_scr[g * DEC_SEQ:(g + 1) * DEC_SEQ, hs:hs + HEAD_DIM] = qc[0:DEC_SEQ, 0:HEAD_DIM]
            ni_scr[(g + 1) * DEC_SEQ:(g + 2) * DEC_SEQ, hs:hs + HEAD_DIM] = qc[DEC_SEQ:, HEAD_DIM:]
            wv_pair = jnp.concatenate([jnp.where(grp == g, wv, 0.0).astype(_BF),
                                       jnp.where(grp == g + 1, wv, 0.0).astype(_BF)], axis=1)
            upd = _dot(k_t_bf, wv_pair)
            dec0 = jnp.broadcast_to(dec_b[g * DEC_SEQ:g * DEC_SEQ + 1, :], (HEAD_DIM, HEAD_DIM))
            dec1 = jnp.broadcast_to(dec_b[(g + 1) * DEC_SEQ:(g + 1) * DEC_SEQ + 1, :], (HEAD_DIM, HEAD_DIM))
            c_out_ref[g, h] = dec0 * c0 + upd[:, 0:HEAD_DIM]
            c_out_ref[g + 1, h] = dec1 * c1 + upd[:, HEAD_DIM:]

        s = st["qk"] * st["d"]
        sv = _dot(s.astype(_BF), jnp.concatenate([st["v"].astype(_BF), ones_blk], axis=1))
        n_old = n_ref[:, h, :]
        num_inter = ni_scr[:, hs:hs + HEAD_DIM]
        qn = _dot_nt(st["q_bf"], n_old.astype(_BF))
        den_inter = jnp.sum(jnp.where(own_seq, qn, 0.0), axis=-1, keepdims=True)
        inter_b = jnp.broadcast_to(inter[:, h:h + 1], (CHUNK, LANES))
        num = sv[:, :LANES] + inter_b * num_inter
        den = sv[:, LANES:] + inter_b * den_inter
        emt_b = jnp.broadcast_to(emt[:, h:h + 1], (CHUNK, LANES))
        ni_scr[:, hs:hs + HEAD_DIM] = _head_norm(num, den, emt_b, mhg_ref[:, hs:hs + HEAD_DIM])
        n_upd = _dot(seq_sum, (st["w_b"] * st["k"]).astype(_BF))
        dec16_b = jnp.broadcast_to(dec16[:, h:h + 1], (SEQ_PER_STEP, HEAD_DIM))
        n_out_ref[:, h, :] = dec16_b * n_old + n_upd

    ycat_scr[:, CONV_WIDTH:CONV_WIDTH + M_WIDTH] = _gated_heads(ni_scr[...], p_scr).astype(_BF)
    out = _dot(ycat_scr[...], wo_ref[...]) + x
    y_ref[...] = _rms_rows(out, fgain_ref[...])


def _sample_call(xs, w3, wg, wo, small, fgain, state_conv, m_pad, n_state, c_state):
    ng, mhg, cw, big, bfg = small
    rows = xs.shape[0]
    nseq = c_state.shape[0]
    steps = rows // CHUNK

    def const(shape):
        nd = len(shape)
        return pl.BlockSpec(shape, lambda i: (0,) * nd)

    def seq_spec(*tail):
        nd = len(tail)
        return pl.BlockSpec((SEQ_PER_STEP,) + tail, lambda i: (i,) + (0,) * nd)

    x_spec = pl.BlockSpec((CHUNK, D_MODEL), lambda i: (i, 0))
    conv_spec = seq_spec(CONV_K - 1, CONV_WIDTH)
    c_spec = seq_spec(M_HEADS, HEAD_DIM, HEAD_DIM)
    n_spec = seq_spec(M_HEADS, HEAD_DIM)
    m_spec = seq_spec(LANES)
    in_specs = [
        x_spec,
        const(w3.shape), const(wg.shape), const(wo.shape), const(ng.shape), const(fgain.shape),
        const(mhg.shape), const(cw.shape), _SMEM_SPEC, _SMEM_SPEC,
        conv_spec, m_spec, n_spec, c_spec,
    ]
    out_shape = (
        jax.ShapeDtypeStruct((rows, D_MODEL), _F32),
        jax.ShapeDtypeStruct(state_conv.shape, _F32),
        jax.ShapeDtypeStruct(c_state.shape, _F32),
        jax.ShapeDtypeStruct(n_state.shape, _F32),
        jax.ShapeDtypeStruct((nseq, LANES), _F32),
    )
    out_specs = (x_spec, conv_spec, c_spec, n_spec, m_spec)
    scratch = [
        pltpu.VMEM((CHUNK, MAIN_WIDTH), _F32),
        pltpu.VMEM((CHUNK, M_WIDTH), _F32),
        pltpu.VMEM((CHUNK, CONV_WIDTH), _F32),
        pltpu.VMEM((CHUNK, D_MODEL), _BF),
    ]
    return pl.pallas_call(
        _sample_body,
        grid=(steps,),
        in_specs=in_specs,
        out_specs=out_specs,
        out_shape=out_shape,
        scratch_shapes=scratch,
        compiler_params=pltpu.CompilerParams(
            dimension_semantics=("arbitrary",), vmem_limit_bytes=56 * 1024 * 1024),
        name="sample_layer",
    )(xs, w3, wg, wo, ng, fgain, mhg, cw, big, bfg, state_conv, m_pad, n_state, c_state)


def kernel(x_prompt, x_sample, state_conv, state_mlstm_c, state_mlstm_n, state_mlstm_m, meta_tokens, norm_gain,
           w_in, conv_w, b_igate, b_fgate, mh_norm_gain, w_out, final_norm_gain):
    nseq, dec_seq, _ = x_sample.shape
    assert dec_seq == DEC_SEQ and meta_tokens.shape[0] == N_META

    w_t = w_in.T
    small = (norm_gain.reshape(1, D_MODEL), mh_norm_gain.reshape(1, M_WIDTH), conv_w,
             b_igate.astype(_F32), b_fgate.astype(_F32))
    fgain = final_norm_gain.reshape(1, D_MODEL)
    wo = w_out.astype(_BF)

    w3, wg, caug_meta, m_meta, ut_meta = _prep_call(w_t, meta_tokens.astype(_F32), small)

    y_prompt, c_p, n_p, m_p, conv_p = _prompt_call(
        x_prompt, w3, wg, wo, small, fgain, caug_meta, m_meta, ut_meta, tb=512)
    m_pp = m_p[:, 0, :M_HEADS]

    xs = x_sample.reshape(nseq * DEC_SEQ, D_MODEL)
    m_pad = jnp.pad(state_mlstm_m, ((0, 0), (0, LANES - M_HEADS)))
    ys, conv_s, c_s, n_s, m_s = _sample_call(xs, w3, wg, wo, small, fgain, state_conv, m_pad, state_mlstm_n,
                                             state_mlstm_c)
    y_sample = ys.reshape(nseq, DEC_SEQ, D_MODEL)
    m_ss = m_s[:, :M_HEADS]

    return (y_prompt, y_sample, conv_p, c_p, n_p, m_pp, conv_s, c_s, n_s, m_ss)
```

```python
import functools

import jax
import jax.numpy as jnp
from jax import lax
from jax.experimental import pallas as pl
from jax.experimental.pallas import tpu as pltpu

D_MODEL = 1024
CONV_WIDTH = 512
M_WIDTH = 512
M_HEADS = 4
HEAD_DIM = 128
N_META = 16
CONV_K = 3
EPS = 1e-6
MAIN_WIDTH = 4 * CONV_WIDTH + 5 * M_WIDTH
LANES = 128
SUBLANES = 8
BF16_TILE_ROWS = 2 * SUBLANES
MXU_COLS = 256
N_TILES = MAIN_WIDTH // MXU_COLS
PREP_TILES = 3
CHUNK = 128
DEC_SEQ = 8
SEQ_PER_STEP = CHUNK // DEC_SEQ
PROMPT_ROWS = 512
LAYER_VMEM_BYTES = 62 * 1024 * 1024
NEG_INF = float("-inf")
K_SCALE = HEAD_DIM ** -0.5

OFF_GB, OFF_GC, OFF_HC, OFF_ZC = 0, 512, 1024, 1536
OFF_Q, OFF_K, OFF_V, OFF_O, OFF_ZM = 2048, 2560, 3072, 3584, 4096
QKV_TILES = tuple(range(OFF_Q // MXU_COLS, OFF_O // MXU_COLS))
REST_TILES = tuple(t for t in range(N_TILES) if t not in QKV_TILES)

_SMEM_SPEC = pl.BlockSpec(memory_space=pltpu.SMEM)
_HI = lax.Precision.HIGHEST
_BF = jnp.bfloat16
_F32 = jnp.float32


def _dot(a, b):
    return jnp.dot(a, b, preferred_element_type=_F32)


def _dot_exact(a, b):
    return jnp.dot(a, b, precision=_HI, preferred_element_type=_F32)


def _dot_nt(a, b):
    return lax.dot_general(a, b, (((1,), (1,)), ((), ())), preferred_element_type=_F32)


def _rms_rows(x, gain_row):
    return x * lax.rsqrt(jnp.mean(x * x, axis=-1, keepdims=True) + EPS) * gain_row


def _sigmoid(x):
    return 1.0 / (1.0 + jnp.exp(-x))


def _silu(x):
    return x * _sigmoid(x)


def _tile_cols(t):
    return slice(t * MXU_COLS, (t + 1) * MXU_COLS)


def _cummax_rows(x, pos, period):
    s = 1
    while s < period:
        shifted = pltpu.roll(x, s, 0)
        x = jnp.maximum(x, jnp.where(pos >= s, shifted, NEG_INF))
        s *= 2
    return x


def _cumsum_rows(x, pos, period):
    s = 1
    while s < period:
        x = x + jnp.where(pos >= s, pltpu.roll(x, s, 0), 0.0)
        s *= 2
    return x


def _bias_row(b_ref):
    lane = lax.broadcasted_iota(jnp.int32, (1, LANES), 1)
    row = jnp.zeros((1, LANES), _F32)
    for h in range(M_HEADS):
        row = jnp.where(lane == h, b_ref[h], row)
    return row


def _gate_columns(g_raw, big_ref, bfg_ref):
    ig = g_raw + _bias_row(big_ref)
    lf = jax.nn.log_sigmoid(pltpu.roll(g_raw, LANES - M_HEADS, 1) + _bias_row(bfg_ref))
    return ig, lf


def _head_norm(num, den, emt_b, gain_row):
    hh = num / jnp.maximum(jnp.abs(den), emt_b)
    return hh * lax.rsqrt(jnp.mean(hh * hh, axis=-1, keepdims=True) + EPS) * gain_row


def _conv_out(gb, zc, u, u_m1, u_m2, cw_ref):
    yc = cw_ref[0:1, :] * u_m2 + cw_ref[1:2, :] * u_m1 + cw_ref[2:3, :] * u
    return gb * yc * _silu(zc)


def _gated_heads(hn, p_scr):
    return hn * _sigmoid(p_scr[:, OFF_O:OFF_O + M_WIDTH]) * _silu(p_scr[:, OFF_ZM:OFF_ZM + M_WIDTH])


def _sequence_block(xn, project_tile, emit_rows, p_scr, g_scr, hn_scr, ycat_scr, ubuf, caug, m_scr,
                    wg_ref, mhg_ref, cw_ref, big_ref, bfg_ref, *, tb, n_pad, side_work=None):
    side_work = side_work or {}
    anchors = []
    g_scr[...] = _dot(xn, wg_ref[...])
    for fn in side_work.get("first", ()):
        fn()
    late_tiles = []
    if project_tile is not None:
        for t in QKV_TILES:
            project_tile(t)
        early = 2
        for t in REST_TILES[:early]:
            project_tile(t)
        late_tiles = list(REST_TILES[early:])

    row = lax.broadcasted_iota(jnp.int32, (CHUNK, CHUNK), 0)
    col = lax.broadcasted_iota(jnp.int32, (CHUNK, CHUNK), 1)
    causal = col <= row
    ones_blk = jnp.ones((CHUNK, LANES), _BF)

    n_chunks = tb // CHUNK
    gates = []
    m_prev = m_scr[...]
    for c in range(n_chunks):
        rows = slice(c * CHUNK, (c + 1) * CHUNK)
        ig, lf = _gate_columns(g_scr[rows, :], big_ref, bfg_ref)
        if n_pad:
            ig = jnp.where(row >= n_pad, ig, NEG_INF)
            lf = jnp.where(row >= n_pad, lf, 0.0)
        b = _cumsum_rows(lf, row, CHUNK)
        a = ig - b
        big_m = jnp.maximum(m_prev, _cummax_rows(a, row, CHUNK))
        m_t = b + big_m
        m_last = big_m[CHUNK - 1:CHUNK, :]
        gates.append(dict(a_t=a.T, big_m=big_m, inter=jnp.exp(m_prev - big_m), emt=jnp.exp(-m_t),
                          w=jnp.exp(a - m_last), decay=jnp.exp(m_prev - m_last)))
        m_prev = m_t[CHUNK - 1:CHUNK, :]
    m_scr[...] = m_prev

    def stage_a(c, h):
        rows = slice(c * CHUNK, (c + 1) * CHUNK)
        hs = h * HEAD_DIM
        g = gates[c]
        q = p_scr[rows, OFF_Q + hs:OFF_Q + hs + HEAD_DIM]
        k = p_scr[rows, OFF_K + hs:OFF_K + hs + HEAD_DIM] * K_SCALE
        a_b = jnp.broadcast_to(g["a_t"][h:h + 1, :], (CHUNK, CHUNK))
        m_b = jnp.broadcast_to(g["big_m"][:, h:h + 1], (CHUNK, CHUNK))
        d = jnp.exp(jnp.where(causal, a_b - m_b, NEG_INF))
        q_inter = (jnp.broadcast_to(g["inter"][:, h:h + 1], (CHUNK, HEAD_DIM)) * q).astype(_BF)
        return dict(qk=_dot_nt(q.astype(_BF), k.astype(_BF)), d=d, q_inter=q_inter, k=k)

    def stage_b(c, h, st):
        rows = slice(c * CHUNK, (c + 1) * CHUNK)
        hs = h * HEAD_DIM
        g = gates[c]
        v = p_scr[rows, OFF_V + hs:OFF_V + hs + HEAD_DIM]
        s = st["qk"] * st["d"]
        lhs = jnp.concatenate([s.astype(_BF), st["q_inter"]], axis=1)
        rhs = jnp.concatenate([jnp.concatenate([v.astype(_BF), ones_blk], axis=1), caug[h].astype(_BF)], axis=0)
        tot = _dot(lhs, rhs)
        emt_b = jnp.broadcast_to(g["emt"][:, h:h + 1], (CHUNK, LANES))
        hn_scr[rows, hs:hs + HEAD_DIM] = _head_norm(tot[:, :LANES], tot[:, LANES:], emt_b,
                                                    mhg_ref[:, hs:hs + HEAD_DIM])
        w_b = jnp.broadcast_to(g["w"][:, h:h + 1], (CHUNK, LANES))
        wv = jnp.concatenate([(w_b * v).astype(_BF), w_b.astype(_BF)], axis=1)
        upd = _dot(st["k"].T.astype(_BF), wv)
        decay_b = jnp.broadcast_to(g["decay"][:, h:h + 1], (HEAD_DIM, 2 * LANES))
        caug[h] = decay_b * caug[h] + upd

    def finish_rows(r0, r1):
        rr = slice(r0, r1)
        n = r1 - r0
        u = p_scr[rr, OFF_GC:OFF_GC + CONV_WIDTH] * p_scr[rr, OFF_HC:OFF_HC + CONV_WIDTH]
        ubuf[SUBLANES + r0:SUBLANES + r1, :] = u
        u_m1 = ubuf[SUBLANES - 1 + r0:SUBLANES - 1 + r0 + n, :]
        u_m2 = ubuf[SUBLANES - 2 + r0:SUBLANES - 2 + r0 + n, :]
        y_conv = _conv_out(p_scr[rr, OFF_GB:OFF_GB + CONV_WIDTH], p_scr[rr, OFF_ZC:OFF_ZC + CONV_WIDTH],
                           u, u_m1, u_m2, cw_ref)
        ycat_scr[rr, 0:CONV_WIDTH] = y_conv.astype(_BF)
        y_m = (hn_scr[rr, :] * _sigmoid(p_scr[rr, OFF_O:OFF_O + M_WIDTH])
               * _silu(p_scr[rr, OFF_ZM:OFF_ZM + M_WIDTH]))
        ycat_scr[rr, CONV_WIDTH:CONV_WIDTH + M_WIDTH] = y_m.astype(_BF)
        while anchors:
            tile = ycat_scr[0:BF16_TILE_ROWS, 0:LANES].astype(_F32) + anchors.pop()
            ycat_scr[0:BF16_TILE_ROWS, 0:LANES] = tile.astype(_BF)
        if emit_rows is not None:
            emit_rows(r0, r1)

    units = [(c, h) for c in range(n_chunks) for h in range(M_HEADS)]
    lag = min(3, len(units))
    n_groups = 1
    group_units = len(units) // n_groups
    assert len(late_tiles) <= group_units + lag
    pending = {}
    for t in range(len(units) + lag):
        if t < len(units):
            pending[t] = stage_a(*units[t])
        if t < len(late_tiles):
            project_tile(late_tiles[t])
        for fn in side_work.get(t, ()):
            zero = fn()
            if zero is not None:
                anchors.append(zero)
        if t >= lag:
            stage_b(*units[t - lag], pending.pop(t - lag))
            done = t - lag + 1
            if done % group_units == 0:
                gi = done // group_units
                finish_rows((gi - 1) * tb // n_groups, gi * tb // n_groups)
    ubuf[0:SUBLANES, :] = ubuf[tb:tb + SUBLANES, :]


def _sequence_scratch(tb):
    return [
        pltpu.VMEM((tb, MAIN_WIDTH), _F32),
        pltpu.VMEM((tb, LANES), _F32),
        pltpu.VMEM((tb, M_WIDTH), _F32),
        pltpu.VMEM((tb, D_MODEL), _BF),
        pltpu.VMEM((tb + 2 * SUBLANES, CONV_WIDTH), _F32),
        pltpu.VMEM((M_HEADS, HEAD_DIM, 2 * LANES), _F32),
        pltpu.VMEM((1, LANES), _F32),
    ]


def _prep_body(wt_ref, x_ref, wgt_ref, ng_ref, mhg_ref, cw_ref, big_ref, bfg_ref,
               w3_ref, wg_ref, caug_out_ref, m_out_ref, ut_out_ref,
               xn_scr, p3_scr, p_scr, g_scr, hn_scr, ycat_scr, ubuf, caug, m_scr):
    t = pl.program_id(0)

    @pl.when(t == 0)
    def _():
        xn_scr[...] = jnp.zeros_like(xn_scr)
        xn_scr[CHUNK - N_META:CHUNK, :] = _rms_rows(x_ref[...], ng_ref[...]).astype(_BF)
        gate_rows = jnp.concatenate(
            [wgt_ref[...], jnp.zeros((LANES - 2 * M_HEADS, D_MODEL), _F32)], axis=0)
        wg_ref[...] = gate_rows.T.astype(_BF)

    for j in range(PREP_TILES):
        tile = wt_ref[j * MXU_COLS:(j + 1) * MXU_COLS, :].T.astype(_BF)
        w3_ref[j] = tile
        p3_scr[t * PREP_TILES + j] = _dot(xn_scr[...], tile)

    @pl.when(t == N_TILES // PREP_TILES - 1)
    def _():
        for j in range(N_TILES):
            p_scr[:, _tile_cols(j)] = p3_scr[j]
        caug[...] = jnp.zeros_like(caug)
        m_scr[...] = jnp.zeros_like(m_scr)
        ubuf[0:SUBLANES, :] = jnp.zeros((SUBLANES, CONV_WIDTH), _F32)
        _sequence_block(xn_scr[...], None, None, p_scr, g_scr, hn_scr, ycat_scr, ubuf, caug, m_scr,
                        wg_ref, mhg_ref, cw_ref, big_ref, bfg_ref, tb=CHUNK, n_pad=CHUNK - N_META)
        caug_out_ref[...] = caug[...]
        m_out_ref[...] = m_scr[...]
        ut_out_ref[...] = ubuf[0:SUBLANES, :]


def _prep_call(w_t, meta_tokens, small):
    ng, mhg, cw, big, bfg = small
    n_gates = 2 * M_HEADS

    def const(shape):
        nd = len(shape)
        return pl.BlockSpec(shape, lambda t: (0,) * nd)

    in_specs = [
        pl.BlockSpec((PREP_TILES * MXU_COLS, D_MODEL), lambda t: (t, 0)),
        const(meta_tokens.shape),
        pl.BlockSpec((n_gates, D_MODEL), lambda t: (MAIN_WIDTH // n_gates, 0)),
        const(ng.shape), const(mhg.shape), const(cw.shape),
        _SMEM_SPEC, _SMEM_SPEC,
    ]
    out_shape = (
        jax.ShapeDtypeStruct((N_TILES, D_MODEL, MXU_COLS), _BF),
        jax.ShapeDtypeStruct((D_MODEL, LANES), _BF),
        jax.ShapeDtypeStruct((M_HEADS, HEAD_DIM, 2 * LANES), _F32),
        jax.ShapeDtypeStruct((1, LANES), _F32),
        jax.ShapeDtypeStruct((SUBLANES, CONV_WIDTH), _F32),
    )
    out_specs = (pl.BlockSpec((PREP_TILES, D_MODEL, MXU_COLS), lambda t: (t, 0, 0)),) + tuple(
        const(s.shape) for s in out_shape[1:])
    scratch = [
        pltpu.VMEM((CHUNK, D_MODEL), _BF),
        pltpu.VMEM((N_TILES, CHUNK, MXU_COLS), _F32),
    ] + _sequence_scratch(CHUNK)
    return pl.pallas_call(
        _prep_body,
        grid=(N_TILES // PREP_TILES,),
        in_specs=in_specs,
        out_specs=out_specs,
        out_shape=out_shape,
        scratch_shapes=scratch,
        compiler_params=pltpu.CompilerParams(
            dimension_semantics=("arbitrary",), vmem_limit_bytes=56 * 1024 * 1024),
        name="prep_meta",
    )(w_t, meta_tokens, w_t, ng, mhg, cw, big, bfg)


def _layer_body(x_ref, xnext_ref, w3_ref, wg_ref, wo_ref, ng_ref, fgain_ref, mhg_ref, cw_ref, big_ref, bfg_ref,
                caug0_ref, m0_ref, ut0_ref,
                xs_ref, sconv_ref, ms_ref, ns_ref, cs_ref,
                y_ref, c_out_ref, n_out_ref, m_out_ref, conv_out_ref,
                ys_ref, sconv_out_ref, cs_out_ref, ns_out_ref, ms_out_ref,
                xn_scr, out_scr, sc_scr, ps_scr, nis_scr, ycats_scr,
                p_scr, g_scr, hn_scr, ycat_scr, ubuf, caug, m_scr,
                *, tb, nt, n_blocks, n_sample_blocks):
    s = pl.program_id(0)
    i = s % nt
    live = s < n_blocks

    @pl.when(s == 0)
    def _():
        xn_scr[...] = _rms_rows(x_ref[0], ng_ref[...]).astype(_BF)
        out_scr[...] = jnp.zeros_like(out_scr)
        sc_scr[...] = jnp.zeros_like(sc_scr)

    @pl.when((i == 0) & (s < nt * n_sample_blocks))
    def _():
        _sample_block(xs_ref, w3_ref, wg_ref, wo_ref, ng_ref, fgain_ref, mhg_ref, cw_ref, big_ref, bfg_ref,
                      sconv_ref, ms_ref, ns_ref, cs_ref,
                      ys_ref, sconv_out_ref, cs_out_ref, ns_out_ref, ms_out_ref,
                      ps_scr, nis_scr, sc_scr, ycats_scr)

    @pl.when(live & (i == 0))
    def _():
        caug[...] = caug0_ref[...]
        m_scr[...] = m0_ref[...]
        ubuf[0:SUBLANES, :] = ut0_ref[...]

    @pl.when(live)
    def _():
        xn = xn_scr[...]

        def project_tile(t):
            p_scr[:, _tile_cols(t)] = _dot(xn, w3_ref[t])

        def emit_rows(r0, r1):
            out_scr[r0:r1, :] = _dot(ycat_scr[r0:r1, :], wo_ref[...]) + x_ref[0, r0:r1, :]

        def norm_previous(r0, r1):
            def fn():
                y_ref[0, r0:r1, :] = _rms_rows(out_scr[r0:r1, :], fgain_ref[...])
            return fn

        def norm_next(r0, r1):
            def fn():
                xn_scr[r0:r1, :] = _rms_rows(xnext_ref[0, r0:r1, :], ng_ref[...]).astype(_BF)
            return fn

        q4 = tb // 4
        side_work = {2 + 2 * j: [norm_previous(j * q4, (j + 1) * q4)] for j in range(4)}
        side_work.update({10 + 2 * j: [norm_next(j * q4, (j + 1) * q4)] for j in range(4)})
        _sequence_block(xn, project_tile, emit_rows, p_scr, g_scr, hn_scr, ycat_scr, ubuf, caug, m_scr,
                        wg_ref, mhg_ref, cw_ref, big_ref, bfg_ref, tb=tb, n_pad=0, side_work=side_work)

    @pl.when(s == n_blocks)
    def _():
        y_ref[0] = _rms_rows(out_scr[...], fgain_ref[...])

    @pl.when(live & (i == nt - 1))
    def _():
        m_out_ref[0] = m_scr[...]
        conv_out_ref[0] = ubuf[SUBLANES - (CONV_K - 1):SUBLANES, :]
        for h in range(M_HEADS):
            c_out_ref[0, h] = caug[h, :, 0:HEAD_DIM]
            n_out_ref[0, h:h + 1, :] = caug[h, :, LANES:2 * LANES].T[0:1, :]


def _layer_call(x, w3, wg, wo, small, fgain, caug0, m0, ut0, xs, state_conv, m_pad, n_state, c_state, *, tb):
    ng, mhg, cw, big, bfg = small
    nb, t, _ = x.shape
    nt = t // tb
    n_blocks = nb * nt
    rows = xs.shape[0]
    nseq = c_state.shape[0]
    n_sample_blocks = rows // CHUNK
    assert n_sample_blocks <= nb

    def const(shape):
        nd = len(shape)
        return pl.BlockSpec(shape, lambda s: (0,) * nd)

    def block_of(s):
        return jnp.minimum(s, n_blocks - 1)

    def rows_spec(step_to_block):
        return pl.BlockSpec((1, tb, D_MODEL), lambda s: (step_to_block(s) // nt, step_to_block(s) % nt, 0))

    def state_spec(*tail):
        nd = len(tail)
        return pl.BlockSpec((1,) + tail, lambda s: (block_of(s) // nt,) + (0,) * nd)

    def sample_spec(*block):
        nd = len(block) - 1
        return pl.BlockSpec(block, lambda s: (jnp.minimum(s // nt, n_sample_blocks - 1),) + (0,) * nd)

    xs_spec = sample_spec(CHUNK, D_MODEL)
    sconv_spec = sample_spec(SEQ_PER_STEP, CONV_K - 1, CONV_WIDTH)
    cs_spec = sample_spec(SEQ_PER_STEP, M_HEADS, HEAD_DIM, HEAD_DIM)
    ns_spec = sample_spec(SEQ_PER_STEP, M_HEADS, HEAD_DIM)
    ms_spec = sample_spec(SEQ_PER_STEP, LANES)
    in_specs = [
        rows_spec(block_of),
        rows_spec(lambda s: block_of(s + 1)),
        const(w3.shape), const(wg.shape), const(wo.shape), const(ng.shape), const(fgain.shape),
        const(mhg.shape), const(cw.shape), _SMEM_SPEC, _SMEM_SPEC,
        const(caug0.shape), const(m0.shape), const(ut0.shape),
        xs_spec, sconv_spec, ms_spec, ns_spec, cs_spec,
    ]
    out_shape = (
        jax.ShapeDtypeStruct((nb, t, D_MODEL), _F32),
        jax.ShapeDtypeStruct((nb, M_HEADS, HEAD_DIM, HEAD_DIM), _F32),
        jax.ShapeDtypeStruct((nb, M_HEADS, HEAD_DIM), _F32),
        jax.ShapeDtypeStruct((nb, 1, LANES), _F32),
        jax.ShapeDtypeStruct((nb, CONV_K - 1, CONV_WIDTH), _F32),
        jax.ShapeDtypeStruct((rows, D_MODEL), _F32),
        jax.ShapeDtypeStruct(state_conv.shape, _F32),
        jax.ShapeDtypeStruct(c_state.shape, _F32),
        jax.ShapeDtypeStruct(n_state.shape, _F32),
        jax.ShapeDtypeStruct((nseq, LANES), _F32),
    )
    out_specs = (
        rows_spec(lambda s: jnp.maximum(s - 1, 0)),
        state_spec(M_HEADS, HEAD_DIM, HEAD_DIM),
        state_spec(M_HEADS, HEAD_DIM),
        state_spec(1, LANES),
        state_spec(CONV_K - 1, CONV_WIDTH),
        xs_spec, sconv_spec, cs_spec, ns_spec, ms_spec,
    )
    scratch = [
        pltpu.VMEM((tb, D_MODEL), _BF),
        pltpu.VMEM((tb, D_MODEL), _F32),
        pltpu.VMEM((CHUNK, CONV_WIDTH), _F32),
        pltpu.VMEM((CHUNK, MAIN_WIDTH), _F32),
        pltpu.VMEM((CHUNK, M_WIDTH), _F32),
        pltpu.VMEM((CHUNK, D_MODEL), _BF),
    ] + _sequence_scratch(tb)
    return pl.pallas_call(
        functools.partial(_layer_body, tb=tb, nt=nt, n_blocks=n_blocks, n_sample_blocks=n_sample_blocks),
        grid=(n_blocks + 1,),
        in_specs=in_specs,
        out_specs=out_specs,
        out_shape=out_shape,
        scratch_shapes=scratch,
        compiler_params=pltpu.CompilerParams(
            dimension_semantics=("arbitrary",), vmem_limit_bytes=LAYER_VMEM_BYTES),
        name="layer",
    )(x, x, w3, wg, wo, ng, fgain, mhg, cw, big, bfg, caug0, m0, ut0, xs, state_conv, m_pad, n_state, c_state)


def _sample_block(x_ref, w3_ref, wg_ref, wo_ref, ng_ref, fgain_ref, mhg_ref, cw_ref, big_ref, bfg_ref,
                  sconv_ref, m_ref, n_ref, c_ref,
                  y_ref, conv_out_ref, c_out_ref, n_out_ref, m_out_ref,
                  p_scr, ni_scr, sc_scr, ycat_scr):
    x = x_ref[...]
    xn = _rms_rows(x, ng_ref[...]).astype(_BF)
    g_raw = _dot(xn, wg_ref[...])
    for t in QKV_TILES:
        p_scr[:, _tile_cols(t)] = _dot(xn, w3_ref[t])

    row = lax.broadcasted_iota(jnp.int32, (CHUNK, CHUNK), 0)
    col = lax.broadcasted_iota(jnp.int32, (CHUNK, CHUNK), 1)
    pos = row % DEC_SEQ
    grp = row // DEC_SEQ
    causal = (grp == (col // DEC_SEQ)) & (col <= row)
    grow = lax.broadcasted_iota(jnp.int32, (SEQ_PER_STEP, CHUNK), 0)
    gcol = lax.broadcasted_iota(jnp.int32, (SEQ_PER_STEP, CHUNK), 1)
    pick_last = (gcol == grow * DEC_SEQ + (DEC_SEQ - 1)).astype(_F32)
    seq_sum = (gcol // DEC_SEQ == grow).astype(_BF)
    erow = lax.broadcasted_iota(jnp.int32, (CHUNK, SEQ_PER_STEP), 0)
    ecol = lax.broadcasted_iota(jnp.int32, (CHUNK, SEQ_PER_STEP), 1)
    own_seq = erow // DEC_SEQ == ecol
    ones_blk = jnp.ones((CHUNK, LANES), _BF)

    def rows_from_seqs(v16):
        return jnp.broadcast_to(v16[:, None, :], (SEQ_PER_STEP, DEC_SEQ, LANES)).reshape(CHUNK, LANES)

    def last_row_of_seq(v):
        v3 = v.reshape(SEQ_PER_STEP, DEC_SEQ, LANES)
        return jnp.broadcast_to(v3[:, DEC_SEQ - 1:DEC_SEQ, :], v3.shape).reshape(CHUNK, LANES)

    ig, lf = _gate_columns(g_raw, big_ref, bfg_ref)
    b = _cumsum_rows(lf, pos, DEC_SEQ)
    a = ig - b
    m_prev = rows_from_seqs(m_ref[...])
    big_m = jnp.maximum(m_prev, _cummax_rows(a, pos, DEC_SEQ))
    m_t = b + big_m
    inter = jnp.exp(m_prev - big_m)
    emt = jnp.exp(-m_t)
    m_last = last_row_of_seq(big_m)
    w = jnp.exp(a - m_last)
    decay = jnp.exp(m_prev - m_last)
    m_out_ref[...] = _dot_exact(pick_last, m_t)
    dec16 = _dot_exact(pick_last, decay)
    a_t = a.T

    for t in REST_TILES:
        p_scr[:, _tile_cols(t)] = _dot(xn, w3_ref[t])

    heads = []
    for h in range(M_HEADS):
        hs = h * HEAD_DIM
        q = p_scr[:, OFF_Q + hs:OFF_Q + hs + HEAD_DIM]
        k = p_scr[:, OFF_K + hs:OFF_K + hs + HEAD_DIM] * K_SCALE
        v = p_scr[:, OFF_V + hs:OFF_V + hs + HEAD_DIM]
        q_bf = q.astype(_BF)
        a_b = jnp.broadcast_to(a_t[h:h + 1, :], (CHUNK, CHUNK))
        m_b = jnp.broadcast_to(big_m[:, h:h + 1], (CHUNK, CHUNK))
        d = jnp.exp(jnp.where(causal, a_b - m_b, NEG_INF))
        w_b = jnp.broadcast_to(w[:, h:h + 1], (CHUNK, LANES))
        heads.append(dict(q_bf=q_bf, k=k, v=v, d=d, w_b=w_b, qk=_dot_nt(q_bf, k.astype(_BF))))

    u = p_scr[:, OFF_GC:OFF_GC + CONV_WIDTH] * p_scr[:, OFF_HC:OFF_HC + CONV_WIDTH]
    for g in range(SEQ_PER_STEP):
        sc_scr[g * DEC_SEQ:g * DEC_SEQ + CONV_K - 1, :] = sconv_ref[g]
        conv_out_ref[g] = u[(g + 1) * DEC_SEQ - (CONV_K - 1):(g + 1) * DEC_SEQ, :]
    sc = sc_scr[...]
    posw = lax.broadcasted_iota(jnp.int32, (CHUNK, CONV_WIDTH), 0) % DEC_SEQ
    u_m1 = jnp.where(posw == 0, pltpu.roll(sc, CHUNK - 1, 0), pltpu.roll(u, 1, 0))
    u_m2 = jnp.where(posw < 2, sc, pltpu.roll(u, 2, 0))
    y_conv = _conv_out(p_scr[:, OFF_GB:OFF_GB + CONV_WIDTH], p_scr[:, OFF_ZC:OFF_ZC + CONV_WIDTH],
                       u, u_m1, u_m2, cw_ref)
    ycat_scr[:, 0:CONV_WIDTH] = y_conv.astype(_BF)

    for h in range(M_HEADS):
        hs = h * HEAD_DIM
        st = heads[h]
        wv = st["w_b"] * st["v"]
        k_t_bf = st["k"].T.astype(_BF)
        dec_b = jnp.broadcast_to(decay[:, h:h + 1], (CHUNK, HEAD_DIM))

        for g in range(0, SEQ_PER_STEP, 2):
            pair_rows = slice(g * DEC_SEQ, (g + 2) * DEC_SEQ)
            c0, c1 = c_ref[g, h], c_ref[g + 1, h]
            qc = _dot(st["q_bf"][pair_rows, :], jnp.concatenate([c0.astype(_BF), c1.astype(_BF)], axis=1))
            ni_scr[g * DEC_SEQ:(g + 1) * DEC_SEQ, hs:hs + HEAD_DIM] = qc[0:DEC_SEQ, 0:HEAD_DIM]
            ni_scr[(g + 1) * DEC_SEQ:(g + 2) * DEC_SEQ, hs:hs + HEAD_DIM] = qc[DEC_SEQ:, HEAD_DIM:]
            wv_pair = jnp.concatenate([jnp.where(grp == g, wv, 0.0).astype(_BF),
                                       jnp.where(grp == g + 1, wv, 0.0).astype(_BF)], axis=1)
            upd = _dot(k_t_bf, wv_pair)
            dec0 = jnp.broadcast_to(dec_b[g * DEC_SEQ:g * DEC_SEQ + 1, :], (HEAD_DIM, HEAD_DIM))
            dec1 = jnp.broadcast_to(dec_b[(g + 1) * DEC_SEQ:(g + 1) * DEC_SEQ + 1, :], (HEAD_DIM, HEAD_DIM))
            c_out_ref[g, h] = dec0 * c0 + upd[:, 0:HEAD_DIM]
            c_out_ref[g + 1, h] = dec1 * c1 + upd[:, HEAD_DIM:]

        s = st["qk"] * st["d"]
        sv = _dot(s.astype(_BF), jnp.concatenate([st["v"].astype(_BF), ones_blk], axis=1))
        n_old = n_ref[:, h, :]
        num_inter = ni_scr[:, hs:hs + HEAD_DIM]
        qn = _dot_nt(st["q_bf"], n_old.astype(_BF))
        den_inter = jnp.sum(jnp.where(own_seq, qn, 0.0), axis=-1, keepdims=True)
        inter_b = jnp.broadcast_to(inter[:, h:h + 1], (CHUNK, LANES))
        num = sv[:, :LANES] + inter_b * num_inter
        den = sv[:, LANES:] + inter_b * den_inter
        emt_b = jnp.broadcast_to(emt[:, h:h + 1], (CHUNK, LANES))
        ni_scr[:, hs:hs + HEAD_DIM] = _head_norm(num, den, emt_b, mhg_ref[:, hs:hs + HEAD_DIM])
        n_upd = _dot(seq_sum, (st["w_b"] * st["k"]).astype(_BF))
        dec16_b = jnp.broadcast_to(dec16[:, h:h + 1], (SEQ_PER_STEP, HEAD_DIM))
        n_out_ref[:, h, :] = dec16_b * n_old + n_upd

    ycat_scr[:, CONV_WIDTH:CONV_WIDTH + M_WIDTH] = _gated_heads(ni_scr[...], p_scr).astype(_BF)
    out = _dot(ycat_scr[...], wo_ref[...]) + x
    y_ref[...] = _rms_rows(out, fgain_ref[...])


def kernel(x_prompt, x_sample, state_conv, state_mlstm_c, state_mlstm_n, state_mlstm_m, meta_tokens, norm_gain,
           w_in, conv_w, b_igate, b_fgate, mh_norm_gain, w_out, final_norm_gain):
    nseq, dec_seq, _ = x_sample.shape
    assert dec_seq == DEC_SEQ and meta_tokens.shape[0] == N_META

    w_t = w_in.T
    small = (norm_gain.reshape(1, D_MODEL), mh_norm_gain.reshape(1, M_WIDTH), conv_w,
             b_igate.astype(_F32), b_fgate.astype(_F32))
    fgain = final_norm_gain.reshape(1, D_MODEL)
    wo = w_out.astype(_BF)

    w3, wg, caug_meta, m_meta, ut_meta = _prep_call(w_t, meta_tokens.astype(_F32), small)

    xs = x_sample.reshape(nseq * DEC_SEQ, D_MODEL)
    m_pad = jnp.pad(state_mlstm_m, ((0, 0), (0, LANES - M_HEADS)))
    y_prompt, c_p, n_p, m_p, conv_p, ys, conv_s, c_s, n_s, m_s = _layer_call(
        x_prompt, w3, wg, wo, small, fgain, caug_meta, m_meta, ut_meta,
        xs, state_conv, m_pad, state_mlstm_n, state_mlstm_c, tb=PROMPT_ROWS)
    m_pp = m_p[:, 0, :M_HEADS]
    y_sample = ys.reshape(nseq, DEC_SEQ, D_MODEL)
    m_ss = m_s[:, :M_HEADS]

    return (y_prompt, y_sample, conv_p, c_p, n_p, m_pp, conv_s, c_s, n_s, m_ss)
```

```python
import functools

import jax
import jax.numpy as jnp
from jax import lax
from jax.experimental import pallas as pl
from jax.experimental.pallas import tpu as pltpu

D_MODEL = 1024
CONV_WIDTH = 512
M_WIDTH = 512
M_HEADS = 4
HEAD_DIM = 128
N_META = 16
CONV_K = 3
EPS = 1e-6
MAIN_WIDTH = 4 * CONV_WIDTH + 5 * M_WIDTH
N_GATES = 2 * M_HEADS
LANES = 128
SUBLANES = 8
MXU_COLS = 256
N_TILES = MAIN_WIDTH // MXU_COLS
CHUNK = 128
DEC_SEQ = 8
SEQ_PER_STEP = CHUNK // DEC_SEQ
NEG_INF = float("-inf")
K_SCALE = HEAD_DIM ** -0.5

PROMPT_ROWS = 512
PREP_TILES = 3
VMEM_REQUEST_BYTES = 56 * 1024 * 1024
EARLY_TILES = 2
UNIT_LAG = 3

OFF_GB, OFF_GC, OFF_HC, OFF_ZC = 0, 512, 1024, 1536
OFF_Q, OFF_K, OFF_V, OFF_O, OFF_ZM = 2048, 2560, 3072, 3584, 4096
QKV_TILES = tuple(range(OFF_Q // MXU_COLS, OFF_O // MXU_COLS))
REST_TILES = tuple(t for t in range(N_TILES) if t not in QKV_TILES)

_SMEM_SPEC = pl.BlockSpec(memory_space=pltpu.SMEM)
_HI = lax.Precision.HIGHEST
_BF = jnp.bfloat16
_F32 = jnp.float32


def _dot(a, b):
    return jnp.dot(a, b, preferred_element_type=_F32)


def _dot_exact(a, b):
    return jnp.dot(a, b, precision=_HI, preferred_element_type=_F32)


def _dot_nt(a, b):
    return lax.dot_general(a, b, (((1,), (1,)), ((), ())), preferred_element_type=_F32)


def _rms_rows(x, gain_row):
    return x * lax.rsqrt(jnp.mean(x * x, axis=-1, keepdims=True) + EPS) * gain_row


def _sigmoid(x):
    return 1.0 / (1.0 + jnp.exp(-x))


def _silu(x):
    return x * _sigmoid(x)


def _tile_cols(t):
    return slice(t * MXU_COLS, (t + 1) * MXU_COLS)


def _cummax_rows(x, pos, period):
    s = 1
    while s < period:
        shifted = pltpu.roll(x, s, 0)
        x = jnp.maximum(x, jnp.where(pos >= s, shifted, NEG_INF))
        s *= 2
    return x


def _cumsum_rows(x, pos, period):
    s = 1
    while s < period:
        x = x + jnp.where(pos >= s, pltpu.roll(x, s, 0), 0.0)
        s *= 2
    return x


def _bias_row(b_ref):
    lane = lax.broadcasted_iota(jnp.int32, (1, LANES), 1)
    row = jnp.zeros((1, LANES), _F32)
    for h in range(M_HEADS):
        row = jnp.where(lane == h, b_ref[h], row)
    return row


def _gate_columns(g_raw, big_ref, bfg_ref):
    ig = g_raw + _bias_row(big_ref)
    lf = jax.nn.log_sigmoid(pltpu.roll(g_raw, LANES - M_HEADS, 1) + _bias_row(bfg_ref))
    return ig, lf


def _head_norm(num, den, emt_b, gain_row):
    hh = num / jnp.maximum(jnp.abs(den), emt_b)
    return hh * lax.rsqrt(jnp.mean(hh * hh, axis=-1, keepdims=True) + EPS) * gain_row


def _conv_out(gb, zc, u, u_m1, u_m2, cw_ref):
    yc = cw_ref[0:1, :] * u_m2 + cw_ref[1:2, :] * u_m1 + cw_ref[2:3, :] * u
    return gb * yc * _silu(zc)


def _gated_heads(hn, p_scr):
    return hn * _sigmoid(p_scr[:, OFF_O:OFF_O + M_WIDTH]) * _silu(p_scr[:, OFF_ZM:OFF_ZM + M_WIDTH])


def _sequence_block(xn, project_tile, p_scr, g_scr, hn_scr, ycat_scr, ubuf, caug, m_scr,
                    wg_ref, mhg_ref, cw_ref, big_ref, bfg_ref, *, tb, n_pad, side_work=None):
    side_work = side_work or {}
    g_scr[...] = _dot(xn, wg_ref[...])
    late_tiles = []
    if project_tile is not None:
        for t in QKV_TILES + REST_TILES[:EARLY_TILES]:
            project_tile(t)
        late_tiles = list(REST_TILES[EARLY_TILES:])

    row = lax.broadcasted_iota(jnp.int32, (CHUNK, CHUNK), 0)
    col = lax.broadcasted_iota(jnp.int32, (CHUNK, CHUNK), 1)
    causal = col <= row
    ones_blk = jnp.ones((CHUNK, LANES), _BF)

    n_chunks = tb // CHUNK
    gates = []
    m_prev = m_scr[...]
    for c in range(n_chunks):
        rows = slice(c * CHUNK, (c + 1) * CHUNK)
        ig, lf = _gate_columns(g_scr[rows, :], big_ref, bfg_ref)
        if n_pad:
            ig = jnp.where(row >= n_pad, ig, NEG_INF)
            lf = jnp.where(row >= n_pad, lf, 0.0)
        b = _cumsum_rows(lf, row, CHUNK)
        a = ig - b
        big_m = jnp.maximum(m_prev, _cummax_rows(a, row, CHUNK))
        m_t = b + big_m
        m_last = big_m[CHUNK - 1:CHUNK, :]
        gates.append(dict(a_t=a.T, big_m=big_m, inter=jnp.exp(m_prev - big_m), emt=jnp.exp(-m_t),
                          w=jnp.exp(a - m_last), decay=jnp.exp(m_prev - m_last)))
        m_prev = m_t[CHUNK - 1:CHUNK, :]
    m_scr[...] = m_prev

    def stage_a(c, h):
        rows = slice(c * CHUNK, (c + 1) * CHUNK)
        hs = h * HEAD_DIM
        g = gates[c]
        q2 = p_scr[rows, OFF_Q + hs:OFF_Q + hs + 2 * HEAD_DIM]
        k2 = p_scr[rows, OFF_K + hs:OFF_K + hs + 2 * HEAD_DIM] * K_SCALE
        k2_bf = k2.astype(_BF)
        zero = jnp.zeros((CHUNK, HEAD_DIM), _BF)
        k_diag = jnp.concatenate([jnp.concatenate([k2_bf[:, :HEAD_DIM], zero], axis=1),
                                  jnp.concatenate([zero, k2_bf[:, HEAD_DIM:]], axis=1)], axis=0)
        qk2 = _dot_nt(q2.astype(_BF), k_diag)
        out = []
        for j in range(2):
            cols = slice(j * HEAD_DIM, (j + 1) * HEAD_DIM)
            a_b = jnp.broadcast_to(g["a_t"][h + j:h + j + 1, :], (CHUNK, CHUNK))
            m_b = jnp.broadcast_to(g["big_m"][:, h + j:h + j + 1], (CHUNK, CHUNK))
            d = jnp.exp(jnp.where(causal, a_b - m_b, NEG_INF))
            inter_b = jnp.broadcast_to(g["inter"][:, h + j:h + j + 1], (CHUNK, HEAD_DIM))
            out.append(dict(qk=qk2[:, cols], d=d, q_inter=(inter_b * q2[:, cols]).astype(_BF), k=k2[:, cols]))
        return out

    def stage_b(c, h, st):
        rows = slice(c * CHUNK, (c + 1) * CHUNK)
        hs = h * HEAD_DIM
        g = gates[c]
        v = p_scr[rows, OFF_V + hs:OFF_V + hs + HEAD_DIM]
        s = st["qk"] * st["d"]
        lhs = jnp.concatenate([s.astype(_BF), st["q_inter"]], axis=1)
        rhs = jnp.concatenate([jnp.concatenate([v.astype(_BF), ones_blk], axis=1), caug[h].astype(_BF)], axis=0)
        tot = _dot(lhs, rhs)
        emt_b = jnp.broadcast_to(g["emt"][:, h:h + 1], (CHUNK, LANES))
        hn_scr[rows, hs:hs + HEAD_DIM] = _head_norm(tot[:, :LANES], tot[:, LANES:], emt_b,
                                                    mhg_ref[:, hs:hs + HEAD_DIM])
        w_b = jnp.broadcast_to(g["w"][:, h:h + 1], (CHUNK, LANES))
        wv = jnp.concatenate([(w_b * v).astype(_BF), w_b.astype(_BF)], axis=1)
        upd = _dot(st["k"].T.astype(_BF), wv)
        decay_b = jnp.broadcast_to(g["decay"][:, h:h + 1], (HEAD_DIM, 2 * LANES))
        caug[h] = decay_b * caug[h] + upd

    units = [(c, h) for c in range(n_chunks) for h in range(M_HEADS)]
    lag = min(UNIT_LAG, len(units))
    n_slots = len(units) + lag
    assert len(late_tiles) <= n_slots
    assert all(0 <= slot < n_slots for slot in side_work)
    pending = {}
    for t in range(n_slots):
        if t < len(units) and t % 2 == 0:
            pending[t], pending[t + 1] = stage_a(*units[t])
        if t < len(late_tiles):
            project_tile(late_tiles[t])
        for fn in side_work.get(t, ()):
            fn()
        if t >= lag:
            stage_b(*units[t - lag], pending.pop(t - lag))

    u = p_scr[:, OFF_GC:OFF_GC + CONV_WIDTH] * p_scr[:, OFF_HC:OFF_HC + CONV_WIDTH]
    ubuf[SUBLANES:SUBLANES + tb, :] = u
    u_m1 = ubuf[SUBLANES - 1:SUBLANES - 1 + tb, :]
    u_m2 = ubuf[SUBLANES - 2:SUBLANES - 2 + tb, :]
    y_conv = _conv_out(p_scr[:, OFF_GB:OFF_GB + CONV_WIDTH], p_scr[:, OFF_ZC:OFF_ZC + CONV_WIDTH],
                       u, u_m1, u_m2, cw_ref)
    ycat_scr[:, 0:CONV_WIDTH] = y_conv.astype(_BF)
    ycat_scr[:, CONV_WIDTH:CONV_WIDTH + M_WIDTH] = _gated_heads(hn_scr[...], p_scr).astype(_BF)
    ubuf[0:SUBLANES, :] = ubuf[tb:tb + SUBLANES, :]


def _sequence_scratch(tb):
    return [
        pltpu.VMEM((tb, MAIN_WIDTH), _F32),
        pltpu.VMEM((tb, LANES), _F32),
        pltpu.VMEM((tb, M_WIDTH), _F32),
        pltpu.VMEM((tb, D_MODEL), _BF),
        pltpu.VMEM((tb + 2 * SUBLANES, CONV_WIDTH), _F32),
        pltpu.VMEM((M_HEADS, HEAD_DIM, 2 * LANES), _F32),
        pltpu.VMEM((1, LANES), _F32),
    ]


def _prep_body(wt_ref, x_ref, wgt_ref, ng_ref, mhg_ref, cw_ref, big_ref, bfg_ref,
               w3_ref, wg_ref, caug_out_ref, m_out_ref, ut_out_ref,
               xn_scr, p3_scr, p_scr, g_scr, hn_scr, ycat_scr, ubuf, caug, m_scr):
    t = pl.program_id(0)

    @pl.when(t == 0)
    def _():
        xn_scr[...] = jnp.zeros_like(xn_scr)
        xn_scr[CHUNK - N_META:CHUNK, :] = _rms_rows(x_ref[...], ng_ref[...]).astype(_BF)
        gate_rows = jnp.concatenate([wgt_ref[...], jnp.zeros((LANES - N_GATES, D_MODEL), _F32)], axis=0)
        wg_ref[...] = gate_rows.T.astype(_BF)

    for j in range(PREP_TILES):
        tile = wt_ref[j * MXU_COLS:(j + 1) * MXU_COLS, :].T.astype(_BF)
        w3_ref[j] = tile
        p3_scr[t * PREP_TILES + j] = _dot(xn_scr[...], tile)

    @pl.when(t == N_TILES // PREP_TILES - 1)
    def _():
        for j in range(N_TILES):
            p_scr[:, _tile_cols(j)] = p3_scr[j]
        caug[...] = jnp.zeros_like(caug)
        m_scr[...] = jnp.zeros_like(m_scr)
        ubuf[0:SUBLANES, :] = jnp.zeros((SUBLANES, CONV_WIDTH), _F32)
        _sequence_block(xn_scr[...], None, p_scr, g_scr, hn_scr, ycat_scr, ubuf, caug, m_scr,
                        wg_ref, mhg_ref, cw_ref, big_ref, bfg_ref, tb=CHUNK, n_pad=CHUNK - N_META)
        caug_out_ref[...] = caug[...]
        m_out_ref[...] = m_scr[...]
        ut_out_ref[...] = ubuf[0:SUBLANES, :]


def _prep_call(w_t, meta_tokens, small):
    ng, mhg, cw, big, bfg = small

    def const(shape):
        nd = len(shape)
        return pl.BlockSpec(shape, lambda t: (0,) * nd)

    in_specs = [
        pl.BlockSpec((PREP_TILES * MXU_COLS, D_MODEL), lambda t: (t, 0)),
        const(meta_tokens.shape),
        pl.BlockSpec((N_GATES, D_MODEL), lambda t: (MAIN_WIDTH // N_GATES, 0)),
        const(ng.shape), const(mhg.shape), const(cw.shape),
        _SMEM_SPEC, _SMEM_SPEC,
    ]
    out_shape = (
        jax.ShapeDtypeStruct((N_TILES, D_MODEL, MXU_COLS), _BF),
        jax.ShapeDtypeStruct((D_MODEL, LANES), _BF),
        jax.ShapeDtypeStruct((M_HEADS, HEAD_DIM, 2 * LANES), _F32),
        jax.ShapeDtypeStruct((1, LANES), _F32),
        jax.ShapeDtypeStruct((SUBLANES, CONV_WIDTH), _F32),
    )
    out_specs = (pl.BlockSpec((PREP_TILES, D_MODEL, MXU_COLS), lambda t: (t, 0, 0)),) + tuple(
        const(s.shape) for s in out_shape[1:])
    scratch = [
        pltpu.VMEM((CHUNK, D_MODEL), _BF),
        pltpu.VMEM((N_TILES, CHUNK, MXU_COLS), _F32),
    ] + _sequence_scratch(CHUNK)
    return pl.pallas_call(
        _prep_body,
        grid=(N_TILES // PREP_TILES,),
        in_specs=in_specs,
        out_specs=out_specs,
        out_shape=out_shape,
        scratch_shapes=scratch,
        compiler_params=pltpu.CompilerParams(
            dimension_semantics=("arbitrary",), vmem_limit_bytes=VMEM_REQUEST_BYTES),
        name="prep_meta",
    )(w_t, meta_tokens, w_t, ng, mhg, cw, big, bfg)


def _prompt_body(x_ref, w3_ref, wg_ref, wo_ref, ng_ref, fgain_ref, mhg_ref, cw_ref, big_ref, bfg_ref,
                 caug0_ref, m0_ref, ut0_ref,
                 y_ref, c_out_ref, n_out_ref, m_out_ref, conv_out_ref,
                 out_scr, p_scr, g_scr, hn_scr, ycat_scr, ubuf, caug, m_scr, *, tb, nt, n_blocks):
    s = pl.program_id(0)
    i = s % nt
    live = s < n_blocks

    @pl.when(s == 0)
    def _():
        out_scr[...] = jnp.zeros_like(out_scr)

    @pl.when(live & (i == 0))
    def _():
        caug[...] = caug0_ref[...]
        m_scr[...] = m0_ref[...]
        ubuf[0:SUBLANES, :] = ut0_ref[...]

    @pl.when(live)
    def _():
        xn = _rms_rows(x_ref[0], ng_ref[...]).astype(_BF)

        def project_tile(t):
            p_scr[:, _tile_cols(t)] = _dot(xn, w3_ref[t])

        def norm_previous(r0, r1):
            def fn():
                y_ref[0, r0:r1, :] = _rms_rows(out_scr[r0:r1, :], fgain_ref[...])
            return fn

        q4 = tb // 4
        side_work = {2 + 2 * j: [norm_previous(j * q4, (j + 1) * q4)] for j in range(4)}
        _sequence_block(xn, project_tile, p_scr, g_scr, hn_scr, ycat_scr, ubuf, caug, m_scr,
                        wg_ref, mhg_ref, cw_ref, big_ref, bfg_ref, tb=tb, n_pad=0, side_work=side_work)
        out_scr[...] = _dot(ycat_scr[...], wo_ref[...]) + x_ref[0]

    @pl.when(s == n_blocks)
    def _():
        y_ref[0] = _rms_rows(out_scr[...], fgain_ref[...])

    @pl.when(live & (i == nt - 1))
    def _():
        m_out_ref[0] = m_scr[...]
        conv_out_ref[0] = ubuf[SUBLANES - (CONV_K - 1):SUBLANES, :]
        for h in range(M_HEADS):
            c_out_ref[0, h] = caug[h, :, 0:HEAD_DIM]
            n_out_ref[0, h:h + 1, :] = caug[h, :, LANES:2 * LANES].T[0:1, :]


def _prompt_call(x, w3, wg, wo, small, fgain, caug0, m0, ut0):
    ng, mhg, cw, big, bfg = small
    nb, t, _ = x.shape
    tb = PROMPT_ROWS
    nt = t // tb
    n_blocks = nb * nt

    def const(shape):
        nd = len(shape)
        return pl.BlockSpec(shape, lambda s: (0,) * nd)

    def block_of(s):
        return jnp.minimum(s, n_blocks - 1)

    def rows_spec(step_to_block):
        return pl.BlockSpec((1, tb, D_MODEL), lambda s: (step_to_block(s) // nt, step_to_block(s) % nt, 0))

    def state_spec(*tail):
        nd = len(tail)
        return pl.BlockSpec((1,) + tail, lambda s: (block_of(s) // nt,) + (0,) * nd)

    in_specs = [
        rows_spec(block_of),
        const(w3.shape), const(wg.shape), const(wo.shape), const(ng.shape), const(fgain.shape),
        const(mhg.shape), const(cw.shape), _SMEM_SPEC, _SMEM_SPEC,
        const(caug0.shape), const(m0.shape), const(ut0.shape),
    ]
    out_shape = (
        jax.ShapeDtypeStruct((nb, t, D_MODEL), _F32),
        jax.ShapeDtypeStruct((nb, M_HEADS, HEAD_DIM, HEAD_DIM), _F32),
        jax.ShapeDtypeStruct((nb, M_HEADS, HEAD_DIM), _F32),
        jax.ShapeDtypeStruct((nb, 1, LANES), _F32),
        jax.ShapeDtypeStruct((nb, CONV_K - 1, CONV_WIDTH), _F32),
    )
    out_specs = (
        rows_spec(lambda s: jnp.maximum(s - 1, 0)),
        state_spec(M_HEADS, HEAD_DIM, HEAD_DIM),
        state_spec(M_HEADS, HEAD_DIM),
        state_spec(1, LANES),
        state_spec(CONV_K - 1, CONV_WIDTH),
    )
    scratch = [
        pltpu.VMEM((tb, D_MODEL), _F32),
    ] + _sequence_scratch(tb)
    return pl.pallas_call(
        functools.partial(_prompt_body, tb=tb, nt=nt, n_blocks=n_blocks),
        grid=(n_blocks + 1,),
        in_specs=in_specs,
        out_specs=out_specs,
        out_shape=out_shape,
        scratch_shapes=scratch,
        compiler_params=pltpu.CompilerParams(
            dimension_semantics=("arbitrary",), vmem_limit_bytes=VMEM_REQUEST_BYTES),
        name="prompt_layer",
    )(x, w3, wg, wo, ng, fgain, mhg, cw, big, bfg, caug0, m0, ut0)


def _sample_body(x_ref, w3_ref, wg_ref, wo_ref, ng_ref, fgain_ref, mhg_ref, cw_ref, big_ref, bfg_ref,
                 sconv_ref, m_ref, n_ref, c_ref,
                 y_ref, conv_out_ref, c_out_ref, n_out_ref, m_out_ref,
                 p_scr, ni_scr, sc_scr, ycat_scr):
    @pl.when(pl.program_id(0) == 0)
    def _():
        sc_scr[...] = jnp.zeros_like(sc_scr)

    x = x_ref[...]
    xn = _rms_rows(x, ng_ref[...]).astype(_BF)
    g_raw = _dot(xn, wg_ref[...])
    for t in QKV_TILES:
        p_scr[:, _tile_cols(t)] = _dot(xn, w3_ref[t])

    row = lax.broadcasted_iota(jnp.int32, (CHUNK, CHUNK), 0)
    col = lax.broadcasted_iota(jnp.int32, (CHUNK, CHUNK), 1)
    pos = row % DEC_SEQ
    grp = row // DEC_SEQ
    causal = (grp == (col // DEC_SEQ)) & (col <= row)
    grow = lax.broadcasted_iota(jnp.int32, (SEQ_PER_STEP, CHUNK), 0)
    gcol = lax.broadcasted_iota(jnp.int32, (SEQ_PER_STEP, CHUNK), 1)
    pick_last = (gcol == grow * DEC_SEQ + (DEC_SEQ - 1)).astype(_F32)
    seq_sum = (gcol // DEC_SEQ == grow).astype(_BF)
    erow = lax.broadcasted_iota(jnp.int32, (CHUNK, SEQ_PER_STEP), 0)
    ecol = lax.broadcasted_iota(jnp.int32, (CHUNK, SEQ_PER_STEP), 1)
    own_seq = erow // DEC_SEQ == ecol
    ones_blk = jnp.ones((CHUNK, LANES), _BF)

    def rows_from_seqs(v16):
        return jnp.broadcast_to(v16[:, None, :], (SEQ_PER_STEP, DEC_SEQ, LANES)).reshape(CHUNK, LANES)

    def last_row_of_seq(v):
        v3 = v.reshape(SEQ_PER_STEP, DEC_SEQ, LANES)
        return jnp.broadcast_to(v3[:, DEC_SEQ - 1:DEC_SEQ, :], v3.shape).reshape(CHUNK, LANES)

    ig, lf = _gate_columns(g_raw, big_ref, bfg_ref)
    b = _cumsum_rows(lf, pos, DEC_SEQ)
    a = ig - b
    m_prev = rows_from_seqs(m_ref[...])
    big_m = jnp.maximum(m_prev, _cummax_rows(a, pos, DEC_SEQ))
    m_t = b + big_m
    inter = jnp.exp(m_prev - big_m)
    emt = jnp.exp(-m_t)
    m_last = last_row_of_seq(big_m)
    w = jnp.exp(a - m_last)
    decay = jnp.exp(m_prev - m_last)
    m_out_ref[...] = _dot_exact(pick_last, m_t)
    dec16 = _dot_exact(pick_last, decay)
    a_t = a.T

    for t in REST_TILES:
        p_scr[:, _tile_cols(t)] = _dot(xn, w3_ref[t])

    heads = []
    for h in range(M_HEADS):
        hs = h * HEAD_DIM
        q = p_scr[:, OFF_Q + hs:OFF_Q + hs + HEAD_DIM]
        k = p_scr[:, OFF_K + hs:OFF_K + hs + HEAD_DIM] * K_SCALE
        v = p_scr[:, OFF_V + hs:OFF_V + hs + HEAD_DIM]
        q_bf = q.astype(_BF)
        a_b = jnp.broadcast_to(a_t[h:h + 1, :], (CHUNK, CHUNK))
        m_b = jnp.broadcast_to(big_m[:, h:h + 1], (CHUNK, CHUNK))
        d = jnp.exp(jnp.where(causal, a_b - m_b, NEG_INF))
        w_b = jnp.broadcast_to(w[:, h:h + 1], (CHUNK, LANES))
        heads.append(dict(q_bf=q_bf, k=k, v=v, d=d, w_b=w_b, qk=_dot_nt(q_bf, k.astype(_BF))))

    u = p_scr[:, OFF_GC:OFF_GC + CONV_WIDTH] * p_scr[:, OFF_HC:OFF_HC + CONV_WIDTH]
    for g in range(SEQ_PER_STEP):
        sc_scr[g * DEC_SEQ:g * DEC_SEQ + CONV_K - 1, :] = sconv_ref[g]
        conv_out_ref[g] = u[(g + 1) * DEC_SEQ - (CONV_K - 1):(g + 1) * DEC_SEQ, :]
    sc = sc_scr[...]
    posw = lax.broadcasted_iota(jnp.int32, (CHUNK, CONV_WIDTH), 0) % DEC_SEQ
    u_m1 = jnp.where(posw == 0, pltpu.roll(sc, CHUNK - 1, 0), pltpu.roll(u, 1, 0))
    u_m2 = jnp.where(posw < 2, sc, pltpu.roll(u, 2, 0))
    y_conv = _conv_out(p_scr[:, OFF_GB:OFF_GB + CONV_WIDTH], p_scr[:, OFF_ZC:OFF_ZC + CONV_WIDTH],
                       u, u_m1, u_m2, cw_ref)
    ycat_scr[:, 0:CONV_WIDTH] = y_conv.astype(_BF)

    for h in range(M_HEADS):
        hs = h * HEAD_DIM
        st = heads[h]
        wv = st["w_b"] * st["v"]
        k_t_bf = st["k"].T.astype(_BF)
        dec_b = jnp.broadcast_to(decay[:, h:h + 1], (CHUNK, HEAD_DIM))

        for g in range(0, SEQ_PER_STEP, 2):
            pair_rows = slice(g * DEC_SEQ, (g + 2) * DEC_SEQ)
            c0, c1 = c_ref[g, h], c_ref[g + 1, h]
            qc = _dot(st["q_bf"][pair_rows, :], jnp.concatenate([c0.astype(_BF), c1.astype(_BF)], axis=1))
            ni_scr[g * DEC_SEQ:(g + 1) * DEC_SEQ, hs:hs + HEAD_DIM] = qc[0:DEC_SEQ, 0:HEAD_DIM]
            ni_scr[(g + 1) * DEC_SEQ:(g + 2) * DEC_SEQ, hs:hs + HEAD_DIM] = qc[DEC_SEQ:, HEAD_DIM:]
            wv_pair = jnp.concatenate([jnp.where(grp == g, wv, 0.0).astype(_BF),
                                       jnp.where(grp == g + 1, wv, 0.0).astype(_BF)], axis=1)
            upd = _dot(k_t_bf, wv_pair)
            dec0 = jnp.broadcast_to(dec_b[g * DEC_SEQ:g * DEC_SEQ + 1, :], (HEAD_DIM, HEAD_DIM))
            dec1 = jnp.broadcast_to(dec_b[(g + 1) * DEC_SEQ:(g + 1) * DEC_SEQ + 1, :], (HEAD_DIM, HEAD_DIM))
            c_out_ref[g, h] = dec0 * c0 + upd[:, 0:HEAD_DIM]
            c_out_ref[g + 1, h] = dec1 * c1 + upd[:, HEAD_DIM:]

        s = st["qk"] * st["d"]
        sv = _dot(s.astype(_BF), jnp.concatenate([st["v"].astype(_BF), ones_blk], axis=1))
        n_old = n_ref[:, h, :]
        num_inter = ni_scr[:, hs:hs + HEAD_DIM]
        qn = _dot_nt(st["q_bf"], n_old.astype(_BF))
        den_inter = jnp.sum(jnp.where(own_seq, qn, 0.0), axis=-1, keepdims=True)
        inter_b = jnp.broadcast_to(inter[:, h:h + 1], (CHUNK, LANES))
        num = sv[:, :LANES] + inter_b * num_inter
        den = sv[:, LANES:] + inter_b * den_inter
        emt_b = jnp.broadcast_to(emt[:, h:h + 1], (CHUNK, LANES))
        ni_scr[:, hs:hs + HEAD_DIM] = _head_norm(num, den, emt_b, mhg_ref[:, hs:hs + HEAD_DIM])
        n_upd = _dot(seq_sum, (st["w_b"] * st["k"]).astype(_BF))
        dec16_b = jnp.broadcast_to(dec16[:, h:h + 1], (SEQ_PER_STEP, HEAD_DIM))
        n_out_ref[:, h, :] = dec16_b * n_old + n_upd

    ycat_scr[:, CONV_WIDTH:CONV_WIDTH + M_WIDTH] = _gated_heads(ni_scr[...], p_scr).astype(_BF)
    out = _dot(ycat_scr[...], wo_ref[...]) + x
    y_ref[...] = _rms_rows(out, fgain_ref[...])


def _sample_call(xs, w3, wg, wo, small, fgain, state_conv, m_pad, n_state, c_state):
    ng, mhg, cw, big, bfg = small
    rows = xs.shape[0]
    nseq = c_state.shape[0]
    steps = rows // CHUNK

    def const(shape):
        nd = len(shape)
        return pl.BlockSpec(shape, lambda i: (0,) * nd)

    def seq_spec(*tail):
        nd = len(tail)
        return pl.BlockSpec((SEQ_PER_STEP,) + tail, lambda i: (i,) + (0,) * nd)

    x_spec = pl.BlockSpec((CHUNK, D_MODEL), lambda i: (i, 0))
    conv_spec = seq_spec(CONV_K - 1, CONV_WIDTH)
    c_spec = seq_spec(M_HEADS, HEAD_DIM, HEAD_DIM)
    n_spec = seq_spec(M_HEADS, HEAD_DIM)
    m_spec = seq_spec(LANES)
    in_specs = [
        x_spec,
        const(w3.shape), const(wg.shape), const(wo.shape), const(ng.shape), const(fgain.shape),
        const(mhg.shape), const(cw.shape), _SMEM_SPEC, _SMEM_SPEC,
        conv_spec, m_spec, n_spec, c_spec,
    ]
    out_shape = (
        jax.ShapeDtypeStruct((rows, D_MODEL), _F32),
        jax.ShapeDtypeStruct(state_conv.shape, _F32),
        jax.ShapeDtypeStruct(c_state.shape, _F32),
        jax.ShapeDtypeStruct(n_state.shape, _F32),
        jax.ShapeDtypeStruct((nseq, LANES), _F32),
    )
    out_specs = (x_spec, conv_spec, c_spec, n_spec, m_spec)
    scratch = [
        pltpu.VMEM((CHUNK, MAIN_WIDTH), _F32),
        pltpu.VMEM((CHUNK, M_WIDTH), _F32),
        pltpu.VMEM((CHUNK, CONV_WIDTH), _F32),
        pltpu.VMEM((CHUNK, D_MODEL), _BF),
    ]
    return pl.pallas_call(
        _sample_body,
        grid=(steps,),
        in_specs=in_specs,
        out_specs=out_specs,
        out_shape=out_shape,
        scratch_shapes=scratch,
        compiler_params=pltpu.CompilerParams(
            dimension_semantics=("arbitrary",), vmem_limit_bytes=VMEM_REQUEST_BYTES),
        name="sample_layer",
    )(xs, w3, wg, wo, ng, fgain, mhg, cw, big, bfg, state_conv, m_pad, n_state, c_state)


def kernel(x_prompt, x_sample, state_conv, state_mlstm_c, state_mlstm_n, state_mlstm_m, meta_tokens, norm_gain,
           w_in, conv_w, b_igate, b_fgate, mh_norm_gain, w_out, final_norm_gain):
    nseq, dec_seq, _ = x_sample.shape
    assert dec_seq == DEC_SEQ and meta_tokens.shape[0] == N_META
    assert w_in.shape == (D_MODEL, MAIN_WIDTH + N_GATES) and x_prompt.shape[1] % PROMPT_ROWS == 0
    assert (nseq * DEC_SEQ) % CHUNK == 0

    w_t = w_in.T
    small = (norm_gain.reshape(1, D_MODEL), mh_norm_gain.reshape(1, M_WIDTH), conv_w,
             b_igate.astype(_F32), b_fgate.astype(_F32))
    fgain = final_norm_gain.reshape(1, D_MODEL)
    wo = w_out.astype(_BF)

    w3, wg, caug_meta, m_meta, ut_meta = _prep_call(w_t, meta_tokens.astype(_F32), small)

    y_prompt, c_p, n_p, m_p, conv_p = _prompt_call(x_prompt, w3, wg, wo, small, fgain, caug_meta, m_meta, ut_meta)
    m_pp = m_p[:, 0, :M_HEADS]

    xs = x_sample.reshape(nseq * DEC_SEQ, D_MODEL)
    m_pad = jnp.pad(state_mlstm_m, ((0, 0), (0, LANES - M_HEADS)))
    ys, conv_s, c_s, n_s, m_s = _sample_call(xs, w3, wg, wo, small, fgain, state_conv, m_pad, state_mlstm_n,
                                             state_mlstm_c)
    y_sample = ys.reshape(nseq, DEC_SEQ, D_MODEL)
    m_ss = m_s[:, :M_HEADS]

    return (y_prompt, y_sample, conv_p, c_p, n_p, m_pp, conv_s, c_s, n_s, m_ss)
```

```python
import functools

import jax
import jax.numpy as jnp
from jax import lax
from jax.experimental import pallas as pl
from jax.experimental.pallas import tpu as pltpu

D_MODEL = 1024
CONV_WIDTH = 512
M_WIDTH = 512
M_HEADS = 4
HEAD_DIM = 128
N_META = 16
CONV_K = 3
EPS = 1e-6
MAIN_WIDTH = 4 * CONV_WIDTH + 5 * M_WIDTH
N_GATES = 2 * M_HEADS
LANES = 128
SUBLANES = 8
MXU_COLS = 256
N_TILES = MAIN_WIDTH // MXU_COLS
CHUNK = 128
DEC_SEQ = 8
SEQ_PER_STEP = CHUNK // DEC_SEQ
NEG_INF = float("-inf")
K_SCALE = HEAD_DIM ** -0.5

PROMPT_ROWS = 512
PREP_TILES = 3
VMEM_REQUEST_BYTES = 56 * 1024 * 1024
EARLY_TILES = 2
UNIT_LAG = 3

OFF_GB, OFF_GC, OFF_HC, OFF_ZC = 0, 512, 1024, 1536
OFF_Q, OFF_K, OFF_V, OFF_O, OFF_ZM = 2048, 2560, 3072, 3584, 4096
QKV_TILES = tuple(range(OFF_Q // MXU_COLS, OFF_O // MXU_COLS))
REST_TILES = tuple(t for t in range(N_TILES) if t not in QKV_TILES)

_SMEM_SPEC = pl.BlockSpec(memory_space=pltpu.SMEM)
_HI = lax.Precision.HIGHEST
_BF = jnp.bfloat16
_F32 = jnp.float32


def _dot(a, b):
    return jnp.dot(a, b, preferred_element_type=_F32)


def _dot_exact(a, b):
    return jnp.dot(a, b, precision=_HI, preferred_element_type=_F32)


def _dot_nt(a, b):
    return lax.dot_general(a, b, (((1,), (1,)), ((), ())), preferred_element_type=_F32)


def _rms_rows(x, gain_row):
    return x * lax.rsqrt(jnp.mean(x * x, axis=-1, keepdims=True) + EPS) * gain_row


def _sigmoid(x):
    return 1.0 / (1.0 + jnp.exp(-x))


def _silu(x):
    return x * _sigmoid(x)


def _tile_cols(t):
    return slice(t * MXU_COLS, (t + 1) * MXU_COLS)


def _cummax_rows(x, pos, period):
    s = 1
    while s < period:
        shifted = pltpu.roll(x, s, 0)
        x = jnp.maximum(x, jnp.where(pos >= s, shifted, NEG_INF))
        s *= 2
    return x


def _cumsum_rows(x, pos, period):
    s = 1
    while s < period:
        x = x + jnp.where(pos >= s, pltpu.roll(x, s, 0), 0.0)
        s *= 2
    return x


def _bias_row(b_ref):
    lane = lax.broadcasted_iota(jnp.int32, (1, LANES), 1)
    row = jnp.zeros((1, LANES), _F32)
    for h in range(M_HEADS):
        row = jnp.where(lane == h, b_ref[h], row)
    return row


def _gate_columns(g_raw, big_ref, bfg_ref):
    ig = g_raw + _bias_row(big_ref)
    lf = jax.nn.log_sigmoid(pltpu.roll(g_raw, LANES - M_HEADS, 1) + _bias_row(bfg_ref))
    return ig, lf


def _head_norm(num, den, emt_b, gain_row):
    hh = num / jnp.maximum(jnp.abs(den), emt_b)
    return hh * lax.rsqrt(jnp.mean(hh * hh, axis=-1, keepdims=True) + EPS) * gain_row


def _conv_out(gb, zc, u, u_m1, u_m2, cw_ref):
    yc = cw_ref[0:1, :] * u_m2 + cw_ref[1:2, :] * u_m1 + cw_ref[2:3, :] * u
    return gb * yc * _silu(zc)


def _gated_heads(hn, p_scr):
    return hn * _sigmoid(p_scr[:, OFF_O:OFF_O + M_WIDTH]) * _silu(p_scr[:, OFF_ZM:OFF_ZM + M_WIDTH])


def _sequence_block(xn, project_tile, p_scr, g_scr, hn_scr, ycat_scr, ubuf, caug, m_scr,
                    wg_ref, mhg_ref, cw_ref, big_ref, bfg_ref, *, tb, n_pad, side_work=None):
    side_work = side_work or {}
    g_scr[...] = _dot(xn, wg_ref[...])
    late_tiles = []
    if project_tile is not None:
        for t in QKV_TILES + REST_TILES[:EARLY_TILES]:
            project_tile(t)
        late_tiles = list(REST_TILES[EARLY_TILES:])

    row = lax.broadcasted_iota(jnp.int32, (CHUNK, CHUNK), 0)
    col = lax.broadcasted_iota(jnp.int32, (CHUNK, CHUNK), 1)
    causal = col <= row
    ones_blk = jnp.ones((CHUNK, LANES), _BF)

    n_chunks = tb // CHUNK
    gates = []
    m_prev = m_scr[...]
    for c in range(n_chunks):
        rows = slice(c * CHUNK, (c + 1) * CHUNK)
        ig, lf = _gate_columns(g_scr[rows, :], big_ref, bfg_ref)
        if n_pad:
            ig = jnp.where(row >= n_pad, ig, NEG_INF)
            lf = jnp.where(row >= n_pad, lf, 0.0)
        b = _cumsum_rows(lf, row, CHUNK)
        a = ig - b
        big_m = jnp.maximum(m_prev, _cummax_rows(a, row, CHUNK))
        m_t = b + big_m
        m_last = big_m[CHUNK - 1:CHUNK, :]
        gates.append(dict(a_t=a.T, big_m=big_m, inter=jnp.exp(m_prev - big_m), emt=jnp.exp(-m_t),
                          w=jnp.exp(a - m_last), decay=jnp.exp(m_prev - m_last)))
        m_prev = m_t[CHUNK - 1:CHUNK, :]
    m_scr[...] = m_prev

    def stage_a(c, h):
        rows = slice(c * CHUNK, (c + 1) * CHUNK)
        hs = h * HEAD_DIM
        g = gates[c]
        q2 = p_scr[rows, OFF_Q + hs:OFF_Q + hs + 2 * HEAD_DIM]
        k2 = p_scr[rows, OFF_K + hs:OFF_K + hs + 2 * HEAD_DIM] * K_SCALE
        k2_bf = k2.astype(_BF)
        zero = jnp.zeros((CHUNK, HEAD_DIM), _BF)
        k_diag = jnp.concatenate([jnp.concatenate([k2_bf[:, :HEAD_DIM], zero], axis=1),
                                  jnp.concatenate([zero, k2_bf[:, HEAD_DIM:]], axis=1)], axis=0)
        qk2 = _dot_nt(q2.astype(_BF), k_diag)
        out = []
        for j in range(2):
            cols = slice(j * HEAD_DIM, (j + 1) * HEAD_DIM)
            a_b = jnp.broadcast_to(g["a_t"][h + j:h + j + 1, :], (CHUNK, CHUNK))
            m_b = jnp.broadcast_to(g["big_m"][:, h + j:h + j + 1], (CHUNK, CHUNK))
            d = jnp.exp(jnp.where(causal, a_b - m_b, NEG_INF))
            inter_b = jnp.broadcast_to(g["inter"][:, h + j:h + j + 1], (CHUNK, HEAD_DIM))
            out.append(dict(qk=qk2[:, cols], d=d, q_inter=(inter_b * q2[:, cols]).astype(_BF), k=k2[:, cols]))
        return out

    def stage_b(c, h, st):
        rows = slice(c * CHUNK, (c + 1) * CHUNK)
        hs = h * HEAD_DIM
        g = gates[c]
        v = p_scr[rows, OFF_V + hs:OFF_V + hs + HEAD_DIM]
        s = st["qk"] * st["d"]
        lhs = jnp.concatenate([s.astype(_BF), st["q_inter"]], axis=1)
        rhs = jnp.concatenate([jnp.concatenate([v.astype(_BF), ones_blk], axis=1), caug[h].astype(_BF)], axis=0)
        tot = _dot(lhs, rhs)
        emt_b = jnp.broadcast_to(g["emt"][:, h:h + 1], (CHUNK, LANES))
        hn_scr[rows, hs:hs + HEAD_DIM] = _head_norm(tot[:, :LANES], tot[:, LANES:], emt_b,
                                                    mhg_ref[:, hs:hs + HEAD_DIM])
        w_b = jnp.broadcast_to(g["w"][:, h:h + 1], (CHUNK, LANES))
        wv = jnp.concatenate([(w_b * v).astype(_BF), w_b.astype(_BF)], axis=1)
        upd = _dot(st["k"].T.astype(_BF), wv)
        decay_b = jnp.broadcast_to(g["decay"][:, h:h + 1], (HEAD_DIM, 2 * LANES))
        caug[h] = decay_b * caug[h] + upd

    units = [(c, h) for c in range(n_chunks) for h in range(M_HEADS)]
    lag = min(UNIT_LAG, len(units))
    n_slots = len(units) + lag
    assert len(late_tiles) <= n_slots
    assert all(0 <= slot < n_slots for slot in side_work)
    pending = {}
    for t in range(n_slots):
        if t < len(units) and t % 2 == 0:
            pending[t], pending[t + 1] = stage_a(*units[t])
        if t < len(late_tiles):
            project_tile(late_tiles[t])
        for fn in side_work.get(t, ()):
            fn()
        if t >= lag:
            stage_b(*units[t - lag], pending.pop(t - lag))

    u = p_scr[:, OFF_GC:OFF_GC + CONV_WIDTH] * p_scr[:, OFF_HC:OFF_HC + CONV_WIDTH]
    ubuf[SUBLANES:SUBLANES + tb, :] = u
    u_m1 = ubuf[SUBLANES - 1:SUBLANES - 1 + tb, :]
    u_m2 = ubuf[SUBLANES - 2:SUBLANES - 2 + tb, :]
    y_conv = _conv_out(p_scr[:, OFF_GB:OFF_GB + CONV_WIDTH], p_scr[:, OFF_ZC:OFF_ZC + CONV_WIDTH],
                       u, u_m1, u_m2, cw_ref)
    ycat_scr[:, 0:CONV_WIDTH] = y_conv.astype(_BF)
    ycat_scr[:, CONV_WIDTH:CONV_WIDTH + M_WIDTH] = _gated_heads(hn_scr[...], p_scr).astype(_BF)
    ubuf[0:SUBLANES, :] = ubuf[tb:tb + SUBLANES, :]


def _sequence_scratch(tb):
    return [
        pltpu.VMEM((tb, MAIN_WIDTH), _F32),
        pltpu.VMEM((tb, LANES), _F32),
        pltpu.VMEM((tb, M_WIDTH), _F32),
        pltpu.VMEM((tb, D_MODEL), _BF),
        pltpu.VMEM((tb + 2 * SUBLANES, CONV_WIDTH), _F32),
        pltpu.VMEM((M_HEADS, HEAD_DIM, 2 * LANES), _F32),
        pltpu.VMEM((1, LANES), _F32),
    ]


def _front_body(wt_ref, x_ref, wgt_ref, wo_ref, ng_ref, fgain_ref, mhg_ref, cw_ref, big_ref, bfg_ref,
                xs_ref, sconv_ref, ms_ref, ns_ref, cs_ref,
                w3_ref, wg_ref, caug_out_ref, m_out_ref, ut_out_ref,
                ys_ref, sconv_out_ref, cs_out_ref, ns_out_ref, ms_out_ref,
                w3_scr, xn_scr, p3_scr, sc_scr, p_scr, g_scr, hn_scr, ycat_scr, ubuf, caug, m_scr):
    t = pl.program_id(0)
    n_prep = N_TILES // PREP_TILES

    @pl.when(t == 0)
    def _():
        xn_scr[...] = jnp.zeros_like(xn_scr)
        xn_scr[CHUNK - N_META:CHUNK, :] = _rms_rows(x_ref[...], ng_ref[...]).astype(_BF)
        gate_rows = jnp.concatenate([wgt_ref[...], jnp.zeros((LANES - N_GATES, D_MODEL), _F32)], axis=0)
        wg_ref[...] = gate_rows.T.astype(_BF)
        sc_scr[...] = jnp.zeros_like(sc_scr)

    @pl.when(t < n_prep)
    def _():
        for j in range(PREP_TILES):
            tile = wt_ref[j * MXU_COLS:(j + 1) * MXU_COLS, :].T.astype(_BF)
            w3_ref[j] = tile
            w3_scr[t * PREP_TILES + j] = tile
            p3_scr[t * PREP_TILES + j] = _dot(xn_scr[...], tile)

    @pl.when(t == n_prep - 1)
    def _():
        for j in range(N_TILES):
            p_scr[:, _tile_cols(j)] = p3_scr[j]
        caug[...] = jnp.zeros_like(caug)
        m_scr[...] = jnp.zeros_like(m_scr)
        ubuf[0:SUBLANES, :] = jnp.zeros((SUBLANES, CONV_WIDTH), _F32)
        _sequence_block(xn_scr[...], None, p_scr, g_scr, hn_scr, ycat_scr, ubuf, caug, m_scr,
                        wg_ref, mhg_ref, cw_ref, big_ref, bfg_ref, tb=CHUNK, n_pad=CHUNK - N_META)
        caug_out_ref[...] = caug[...]
        m_out_ref[...] = m_scr[...]
        ut_out_ref[...] = ubuf[0:SUBLANES, :]

    @pl.when(t >= n_prep)
    def _():
        _sample_block(xs_ref, w3_scr, wg_ref, wo_ref, ng_ref, fgain_ref, mhg_ref, cw_ref, big_ref, bfg_ref,
                      sconv_ref, ms_ref, ns_ref, cs_ref,
                      ys_ref, sconv_out_ref, cs_out_ref, ns_out_ref, ms_out_ref,
                      p_scr, hn_scr, sc_scr, ycat_scr)


def _front_call(w_t, meta_tokens, wo, small, fgain, xs, state_conv, m_pad, n_state, c_state):
    ng, mhg, cw, big, bfg = small
    n_prep = N_TILES // PREP_TILES
    rows = xs.shape[0]
    nseq = c_state.shape[0]
    n_sample = rows // CHUNK

    def const(shape):
        nd = len(shape)
        return pl.BlockSpec(shape, lambda t: (0,) * nd)

    def sample_spec(*block):
        nd = len(block) - 1
        return pl.BlockSpec(block, lambda t: (jnp.maximum(t - n_prep, 0),) + (0,) * nd)

    def prep_idx(t):
        return jnp.minimum(t, n_prep - 1)

    xs_spec = sample_spec(CHUNK, D_MODEL)
    sconv_spec = sample_spec(SEQ_PER_STEP, CONV_K - 1, CONV_WIDTH)
    cs_spec = sample_spec(SEQ_PER_STEP, M_HEADS, HEAD_DIM, HEAD_DIM)
    ns_spec = sample_spec(SEQ_PER_STEP, M_HEADS, HEAD_DIM)
    ms_spec = sample_spec(SEQ_PER_STEP, LANES)
    in_specs = [
        pl.BlockSpec((PREP_TILES * MXU_COLS, D_MODEL), lambda t: (prep_idx(t), 0)),
        const(meta_tokens.shape),
        pl.BlockSpec((N_GATES, D_MODEL), lambda t: (MAIN_WIDTH // N_GATES, 0)),
        const(wo.shape), const(ng.shape), const(fgain.shape), const(mhg.shape), const(cw.shape),
        _SMEM_SPEC, _SMEM_SPEC,
        xs_spec, sconv_spec, ms_spec, ns_spec, cs_spec,
    ]
    out_shape = (
        jax.ShapeDtypeStruct((N_TILES, D_MODEL, MXU_COLS), _BF),
        jax.ShapeDtypeStruct((D_MODEL, LANES), _BF),
        jax.ShapeDtypeStruct((M_HEADS, HEAD_DIM, 2 * LANES), _F32),
        jax.ShapeDtypeStruct((1, LANES), _F32),
        jax.ShapeDtypeStruct((SUBLANES, CONV_WIDTH), _F32),
        jax.ShapeDtypeStruct((rows, D_MODEL), _F32),
        jax.ShapeDtypeStruct(state_conv.shape, _F32),
        jax.ShapeDtypeStruct(c_state.shape, _F32),
        jax.ShapeDtypeStruct(n_state.shape, _F32),
        jax.ShapeDtypeStruct((nseq, LANES), _F32),
    )
    out_specs = (
        pl.BlockSpec((PREP_TILES, D_MODEL, MXU_COLS), lambda t: (prep_idx(t), 0, 0)),
        const(out_shape[1].shape), const(out_shape[2].shape), const(out_shape[3].shape), const(out_shape[4].shape),
        xs_spec, sconv_spec, cs_spec, ns_spec, ms_spec,
    )
    scratch = [
        pltpu.VMEM((N_TILES, D_MODEL, MXU_COLS), _BF),
        pltpu.VMEM((CHUNK, D_MODEL), _BF),
        pltpu.VMEM((N_TILES, CHUNK, MXU_COLS), _F32),
        pltpu.VMEM((CHUNK, CONV_WIDTH), _F32),
    ] + _sequence_scratch(CHUNK)
    return pl.pallas_call(
        _front_body,
        grid=(n_prep + n_sample,),
        in_specs=in_specs,
        out_specs=out_specs,
        out_shape=out_shape,
        scratch_shapes=scratch,
        compiler_params=pltpu.CompilerParams(
            dimension_semantics=("arbitrary",), vmem_limit_bytes=VMEM_REQUEST_BYTES),
        name="prep_meta_sample",
    )(w_t, meta_tokens, w_t, wo, ng, fgain, mhg, cw, big, bfg, xs, state_conv, m_pad, n_state, c_state)


def _prompt_body(x_ref, w3_ref, wg_ref, wo_ref, ng_ref, fgain_ref, mhg_ref, cw_ref, big_ref, bfg_ref,
                 caug0_ref, m0_ref, ut0_ref,
                 y_ref, c_out_ref, n_out_ref, m_out_ref, conv_out_ref,
                 out_scr, p_scr, g_scr, hn_scr, ycat_scr, ubuf, caug, m_scr, *, tb, nt, n_blocks):
    s = pl.program_id(0)
    i = s % nt
    live = s < n_blocks

    @pl.when(s == 0)
    def _():
        out_scr[...] = jnp.zeros_like(out_scr)

    @pl.when(live & (i == 0))
    def _():
        caug[...] = caug0_ref[...]
        m_scr[...] = m0_ref[...]
        ubuf[0:SUBLANES, :] = ut0_ref[...]

    @pl.when(live)
    def _():
        xn = _rms_rows(x_ref[0], ng_ref[...]).astype(_BF)

        def project_tile(t):
            p_scr[:, _tile_cols(t)] = _dot(xn, w3_ref[t])

        def norm_previous(r0, r1):
            def fn():
                y_ref[0, r0:r1, :] = _rms_rows(out_scr[r0:r1, :], fgain_ref[...])
            return fn

        q4 = tb // 4
        side_work = {2 + 2 * j: [norm_previous(j * q4, (j + 1) * q4)] for j in range(4)}
        _sequence_block(xn, project_tile, p_scr, g_scr, hn_scr, ycat_scr, ubuf, caug, m_scr,
                        wg_ref, mhg_ref, cw_ref, big_ref, bfg_ref, tb=tb, n_pad=0, side_work=side_work)
        out_scr[...] = _dot(ycat_scr[...], wo_ref[...]) + x_ref[0]

    @pl.when(s == n_blocks)
    def _():
        y_ref[0] = _rms_rows(out_scr[...], fgain_ref[...])

    @pl.when(live & (i == nt - 1))
    def _():
        m_out_ref[0] = m_scr[...]
        conv_out_ref[0] = ubuf[SUBLANES - (CONV_K - 1):SUBLANES, :]
        for h in range(M_HEADS):
            c_out_ref[0, h] = caug[h, :, 0:HEAD_DIM]
            n_out_ref[0, h:h + 1, :] = caug[h, :, LANES:2 * LANES].T[0:1, :]


def _prompt_call(x, w3, wg, wo, small, fgain, caug0, m0, ut0):
    ng, mhg, cw, big, bfg = small
    nb, t, _ = x.shape
    tb = PROMPT_ROWS
    nt = t // tb
    n_blocks = nb * nt

    def const(shape):
        nd = len(shape)
        return pl.BlockSpec(shape, lambda s: (0,) * nd)

    def block_of(s):
        return jnp.minimum(s, n_blocks - 1)

    def rows_spec(step_to_block):
        return pl.BlockSpec((1, tb, D_MODEL), lambda s: (step_to_block(s) // nt, step_to_block(s) % nt, 0))

    def state_spec(*tail):
        nd = len(tail)
        return pl.BlockSpec((1,) + tail, lambda s: (block_of(s) // nt,) + (0,) * nd)

    in_specs = [
        rows_spec(block_of),
        const(w3.shape), const(wg.shape), const(wo.shape), const(ng.shape), const(fgain.shape),
        const(mhg.shape), const(cw.shape), _SMEM_SPEC, _SMEM_SPEC,
        const(caug0.shape), const(m0.shape), const(ut0.shape),
    ]
    out_shape = (
        jax.ShapeDtypeStruct((nb, t, D_MODEL), _F32),
        jax.ShapeDtypeStruct((nb, M_HEADS, HEAD_DIM, HEAD_DIM), _F32),
        jax.ShapeDtypeStruct((nb, M_HEADS, HEAD_DIM), _F32),
        jax.ShapeDtypeStruct((nb, 1, LANES), _F32),
        jax.ShapeDtypeStruct((nb, CONV_K - 1, CONV_WIDTH), _F32),
    )
    out_specs = (
        rows_spec(lambda s: jnp.maximum(s - 1, 0)),
        state_spec(M_HEADS, HEAD_DIM, HEAD_DIM),
        state_spec(M_HEADS, HEAD_DIM),
        state_spec(1, LANES),
        state_spec(CONV_K - 1, CONV_WIDTH),
    )
    scratch = [
        pltpu.VMEM((tb, D_MODEL), _F32),
    ] + _sequence_scratch(tb)
    return pl.pallas_call(
        functools.partial(_prompt_body, tb=tb, nt=nt, n_blocks=n_blocks),
        grid=(n_blocks + 1,),
        in_specs=in_specs,
        out_specs=out_specs,
        out_shape=out_shape,
        scratch_shapes=scratch,
        compiler_params=pltpu.CompilerParams(
            dimension_semantics=("arbitrary",), vmem_limit_bytes=VMEM_REQUEST_BYTES),
        name="prompt_layer",
    )(x, w3, wg, wo, ng, fgain, mhg, cw, big, bfg, caug0, m0, ut0)


def _sample_block(x_ref, w3_ref, wg_ref, wo_ref, ng_ref, fgain_ref, mhg_ref, cw_ref, big_ref, bfg_ref,
                  sconv_ref, m_ref, n_ref, c_ref,
                  y_ref, conv_out_ref, c_out_ref, n_out_ref, m_out_ref,
                  p_scr, ni_scr, sc_scr, ycat_scr):
    x = x_ref[...]
    xn = _rms_rows(x, ng_ref[...]).astype(_BF)
    g_raw = _dot(xn, wg_ref[...])
    for t in QKV_TILES:
        p_scr[:, _tile_cols(t)] = _dot(xn, w3_ref[t])

    row = lax.broadcasted_iota(jnp.int32, (CHUNK, CHUNK), 0)
    col = lax.broadcasted_iota(jnp.int32, (CHUNK, CHUNK), 1)
    pos = row % DEC_SEQ
    grp = row // DEC_SEQ
    causal = (grp == (col // DEC_SEQ)) & (col <= row)
    grow = lax.broadcasted_iota(jnp.int32, (SEQ_PER_STEP, CHUNK), 0)
    gcol = lax.broadcasted_iota(jnp.int32, (SEQ_PER_STEP, CHUNK), 1)
    pick_last = (gcol == grow * DEC_SEQ + (DEC_SEQ - 1)).astype(_F32)
    seq_sum = (gcol // DEC_SEQ == grow).astype(_BF)
    erow = lax.broadcasted_iota(jnp.int32, (CHUNK, SEQ_PER_STEP), 0)
    ecol = lax.broadcasted_iota(jnp.int32, (CHUNK, SEQ_PER_STEP), 1)
    own_seq = erow // DEC_SEQ == ecol
    ones_blk = jnp.ones((CHUNK, LANES), _BF)

    def rows_from_seqs(v16):
        return jnp.broadcast_to(v16[:, None, :], (SEQ_PER_STEP, DEC_SEQ, LANES)).reshape(CHUNK, LANES)

    def last_row_of_seq(v):
        v3 = v.reshape(SEQ_PER_STEP, DEC_SEQ, LANES)
        return jnp.broadcast_to(v3[:, DEC_SEQ - 1:DEC_SEQ, :], v3.shape).reshape(CHUNK, LANES)

    ig, lf = _gate_columns(g_raw, big_ref, bfg_ref)
    b = _cumsum_rows(lf, pos, DEC_SEQ)
    a = ig - b
    m_prev = rows_from_seqs(m_ref[...])
    big_m = jnp.maximum(m_prev, _cummax_rows(a, pos, DEC_SEQ))
    m_t = b + big_m
    inter = jnp.exp(m_prev - big_m)
    emt = jnp.exp(-m_t)
    m_last = last_row_of_seq(big_m)
    w = jnp.exp(a - m_last)
    decay = jnp.exp(m_prev - m_last)
    m_out_ref[...] = _dot_exact(pick_last, m_t)
    dec16 = _dot_exact(pick_last, decay)
    a_t = a.T

    for t in REST_TILES:
        p_scr[:, _tile_cols(t)] = _dot(xn, w3_ref[t])

    heads = []
    for h in range(M_HEADS):
        hs = h * HEAD_DIM
        q = p_scr[:, OFF_Q + hs:OFF_Q + hs + HEAD_DIM]
        k = p_scr[:, OFF_K + hs:OFF_K + hs + HEAD_DIM] * K_SCALE
        v = p_scr[:, OFF_V + hs:OFF_V + hs + HEAD_DIM]
        q_bf = q.astype(_BF)
        a_b = jnp.broadcast_to(a_t[h:h + 1, :], (CHUNK, CHUNK))
        m_b = jnp.broadcast_to(big_m[:, h:h + 1], (CHUNK, CHUNK))
        d = jnp.exp(jnp.where(causal, a_b - m_b, NEG_INF))
        w_b = jnp.broadcast_to(w[:, h:h + 1], (CHUNK, LANES))
        heads.append(dict(q_bf=q_bf, k=k, v=v, d=d, w_b=w_b, qk=_dot_nt(q_bf, k.astype(_BF))))

    u = p_scr[:, OFF_GC:OFF_GC + CONV_WIDTH] * p_scr[:, OFF_HC:OFF_HC + CONV_WIDTH]
    for g in range(SEQ_PER_STEP):
        sc_scr[g * DEC_SEQ:g * DEC_SEQ + CONV_K - 1, :] = sconv_ref[g]
        conv_out_ref[g] = u[(g + 1) * DEC_SEQ - (CONV_K - 1):(g + 1) * DEC_SEQ, :]
    sc = sc_scr[...]
    posw = lax.broadcasted_iota(jnp.int32, (CHUNK, CONV_WIDTH), 0) % DEC_SEQ
    u_m1 = jnp.where(posw == 0, pltpu.roll(sc, CHUNK - 1, 0), pltpu.roll(u, 1, 0))
    u_m2 = jnp.where(posw < 2, sc, pltpu.roll(u, 2, 0))
    y_conv = _conv_out(p_scr[:, OFF_GB:OFF_GB + CONV_WIDTH], p_scr[:, OFF_ZC:OFF_ZC + CONV_WIDTH],
                       u, u_m1, u_m2, cw_ref)
    ycat_scr[:, 0:CONV_WIDTH] = y_conv.astype(_BF)

    for h in range(M_HEADS):
        hs = h * HEAD_DIM
        st = heads[h]
        wv = st["w_b"] * st["v"]
        k_t_bf = st["k"].T.astype(_BF)
        dec_b = jnp.broadcast_to(decay[:, h:h + 1], (CHUNK, HEAD_DIM))

        for g in range(0, SEQ_PER_STEP, 2):
            pair_rows = slice(g * DEC_SEQ, (g + 2) * DEC_SEQ)
            c0, c1 = c_ref[g, h], c_ref[g + 1, h]
            qc = _dot(st["q_bf"][pair_rows, :], jnp.concatenate([c0.astype(_BF), c1.astype(_BF)], axis=1))
            ni_scr[g * DEC_SEQ:(g + 1) * DEC_SEQ, hs:hs + HEAD_DIM] = qc[0:DEC_SEQ, 0:HEAD_DIM]
            ni_scr[(g + 1) * DEC_SEQ:(g + 2) * DEC_SEQ, hs:hs + HEAD_DIM] = qc[DEC_SEQ:, HEAD_DIM:]
            wv_pair = jnp.concatenate([jnp.where(grp == g, wv, 0.0).astype(_BF),
                                       jnp.where(grp == g + 1, wv, 0.0).astype(_BF)], axis=1)
            upd = _dot(k_t_bf, wv_pair)
            dec0 = jnp.broadcast_to(dec_b[g * DEC_SEQ:g * DEC_SEQ + 1, :], (HEAD_DIM, HEAD_DIM))
            dec1 = jnp.broadcast_to(dec_b[(g + 1) * DEC_SEQ:(g + 1) * DEC_SEQ + 1, :], (HEAD_DIM, HEAD_DIM))
            c_out_ref[g, h] = dec0 * c0 + upd[:, 0:HEAD_DIM]
            c_out_ref[g + 1, h] = dec1 * c1 + upd[:, HEAD_DIM:]

        s = st["qk"] * st["d"]
        sv = _dot(s.astype(_BF), jnp.concatenate([st["v"].astype(_BF), ones_blk], axis=1))
        n_old = n_ref[:, h, :]
        num_inter = ni_scr[:, hs:hs + HEAD_DIM]
        qn = _dot_nt(st["q_bf"], n_old.astype(_BF))
        den_inter = jnp.sum(jnp.where(own_seq, qn, 0.0), axis=-1, keepdims=True)
        inter_b = jnp.broadcast_to(inter[:, h:h + 1], (CHUNK, LANES))
        num = sv[:, :LANES] + inter_b * num_inter
        den = sv[:, LANES:] + inter_b * den_inter
        emt_b = jnp.broadcast_to(emt[:, h:h + 1], (CHUNK, LANES))
        ni_scr[:, hs:hs + HEAD_DIM] = _head_norm(num, den, emt_b, mhg_ref[:, hs:hs + HEAD_DIM])
        n_upd = _dot(seq_sum, (st["w_b"] * st["k"]).astype(_BF))
        dec16_b = jnp.broadcast_to(dec16[:, h:h + 1], (SEQ_PER_STEP, HEAD_DIM))
        n_out_ref[:, h, :] = dec16_b * n_old + n_upd

    ycat_scr[:, CONV_WIDTH:CONV_WIDTH + M_WIDTH] = _gated_heads(ni_scr[...], p_scr).astype(_BF)
    out = _dot(ycat_scr[...], wo_ref[...]) + x
    y_ref[...] = _rms_rows(out, fgain_ref[...])


def kernel(x_prompt, x_sample, state_conv, state_mlstm_c, state_mlstm_n, state_mlstm_m, meta_tokens, norm_gain,
           w_in, conv_w, b_igate, b_fgate, mh_norm_gain, w_out, final_norm_gain):
    nseq, dec_seq, _ = x_sample.shape
    assert dec_seq == DEC_SEQ and meta_tokens.shape[0] == N_META
    assert w_in.shape == (D_MODEL, MAIN_WIDTH + N_GATES) and x_prompt.shape[1] % PROMPT_ROWS == 0
    assert (nseq * DEC_SEQ) % CHUNK == 0

    w_t = w_in.T
    small = (norm_gain.reshape(1, D_MODEL), mh_norm_gain.reshape(1, M_WIDTH), conv_w,
             b_igate.astype(_F32), b_fgate.astype(_F32))
    fgain = final_norm_gain.reshape(1, D_MODEL)
    wo = w_out.astype(_BF)

    xs = x_sample.reshape(nseq * DEC_SEQ, D_MODEL)
    m_pad = jnp.pad(state_mlstm_m, ((0, 0), (0, LANES - M_HEADS)))
    w3, wg, caug_meta, m_meta, ut_meta, ys, conv_s, c_s, n_s, m_s = _front_call(
        w_t, meta_tokens.astype(_F32), wo, small, fgain, xs, state_conv, m_pad, state_mlstm_n, state_mlstm_c)

    y_prompt, c_p, n_p, m_p, conv_p = _prompt_call(x_prompt, w3, wg, wo, small, fgain, caug_meta, m_meta, ut_meta)
    m_pp = m_p[:, 0, :M_HEADS]
    y_sample = ys.reshape(nseq, DEC_SEQ, D_MODEL)
    m_ss = m_s[:, :M_HEADS]

    return (y_prompt, y_sample, conv_p, c_p, n_p, m_pp, conv_s, c_s, n_s, m_ss)
```

```python
import functools

import jax
import jax.numpy as jnp
from jax import lax
from jax.experimental import pallas as pl
from jax.experimental.pallas import tpu as pltpu

D_MODEL = 1024
CONV_WIDTH = 512
M_WIDTH = 512
M_HEADS = 4
HEAD_DIM = 128
N_META = 16
CONV_K = 3
EPS = 1e-6
MAIN_WIDTH = 4 * CONV_WIDTH + 5 * M_WIDTH
N_GATES = 2 * M_HEADS
LANES = 128
SUBLANES = 8
MXU_COLS = 256
N_TILES = MAIN_WIDTH // MXU_COLS
CHUNK = 128
DEC_SEQ = 8
SEQ_PER_STEP = CHUNK // DEC_SEQ
NEG_INF = float("-inf")
K_SCALE = HEAD_DIM ** -0.5

PROMPT_ROWS = 512
PREP_TILES = 3
VMEM_REQUEST_BYTES = 56 * 1024 * 1024
EARLY_TILES = 2
UNIT_LAG = 3

OFF_GB, OFF_GC, OFF_HC, OFF_ZC = 0, 512, 1024, 1536
OFF_Q, OFF_K, OFF_V, OFF_O, OFF_ZM = 2048, 2560, 3072, 3584, 4096
QKV_TILES = tuple(range(OFF_Q // MXU_COLS, OFF_O // MXU_COLS))
REST_TILES = tuple(t for t in range(N_TILES) if t not in QKV_TILES)

_SMEM_SPEC = pl.BlockSpec(memory_space=pltpu.SMEM)
_HI = lax.Precision.HIGHEST
_BF = jnp.bfloat16
_F32 = jnp.float32


def _dot(a, b):
    return jnp.dot(a, b, preferred_element_type=_F32)


def _dot_exact(a, b):
    return jnp.dot(a, b, precision=_HI, preferred_element_type=_F32)


def _dot_nt(a, b):
    return lax.dot_general(a, b, (((1,), (1,)), ((), ())), preferred_element_type=_F32)


def _rms_rows(x, gain_row):
    return x * lax.rsqrt(jnp.mean(x * x, axis=-1, keepdims=True) + EPS) * gain_row


def _sigmoid(x):
    return 1.0 / (1.0 + jnp.exp(-x))


def _silu(x):
    return x * _sigmoid(x)


def _tile_cols(t):
    return slice(t * MXU_COLS, (t + 1) * MXU_COLS)


def _cummax_rows(x, pos, period):
    s = 1
    while s < period:
        shifted = pltpu.roll(x, s, 0)
        x = jnp.maximum(x, jnp.where(pos >= s, shifted, NEG_INF))
        s *= 2
    return x


def _cumsum_rows(x, pos, period):
    s = 1
    while s < period:
        x = x + jnp.where(pos >= s, pltpu.roll(x, s, 0), 0.0)
        s *= 2
    return x


def _bias_row(b_ref):
    lane = lax.broadcasted_iota(jnp.int32, (1, LANES), 1)
    row = jnp.zeros((1, LANES), _F32)
    for h in range(M_HEADS):
        row = jnp.where(lane == h, b_ref[h], row)
    return row


def _gate_columns(g_raw, big_ref, bfg_ref):
    ig = g_raw + _bias_row(big_ref)
    lf = jax.nn.log_sigmoid(pltpu.roll(g_raw, LANES - M_HEADS, 1) + _bias_row(bfg_ref))
    return ig, lf


def _head_norm(num, den, emt_b, gain_row):
    hh = num / jnp.maximum(jnp.abs(den), emt_b)
    return hh * lax.rsqrt(jnp.mean(hh * hh, axis=-1, keepdims=True) + EPS) * gain_row


def _conv_out(gb, zc, u, u_m1, u_m2, cw_ref):
    yc = cw_ref[0:1, :] * u_m2 + cw_ref[1:2, :] * u_m1 + cw_ref[2:3, :] * u
    return gb * yc * _silu(zc)


def _gated_heads(hn, p_scr):
    return hn * _sigmoid(p_scr[:, OFF_O:OFF_O + M_WIDTH]) * _silu(p_scr[:, OFF_ZM:OFF_ZM + M_WIDTH])


def _sequence_block(xn, project_tile, p_scr, g_scr, hn_scr, ycat_scr, ubuf, caug, m_scr,
                    wg_ref, mhg_ref, cw_ref, big_ref, bfg_ref, *, tb, n_pad, side_work=None):
    side_work = side_work or {}
    g_scr[...] = _dot(xn, wg_ref[...])
    late_tiles = []
    if project_tile is not None:
        for t in QKV_TILES + REST_TILES[:EARLY_TILES]:
            project_tile(t)
        late_tiles = list(REST_TILES[EARLY_TILES:])

    row = lax.broadcasted_iota(jnp.int32, (CHUNK, CHUNK), 0)
    col = lax.broadcasted_iota(jnp.int32, (CHUNK, CHUNK), 1)
    causal = col <= row
    ones_blk = jnp.ones((CHUNK, LANES), _BF)

    n_chunks = tb // CHUNK
    gates = []
    m_prev = m_scr[...]
    for c in range(n_chunks):
        rows = slice(c * CHUNK, (c + 1) * CHUNK)
        ig, lf = _gate_columns(g_scr[rows, :], big_ref, bfg_ref)
        if n_pad:
            ig = jnp.where(row >= n_pad, ig, NEG_INF)
            lf = jnp.where(row >= n_pad, lf, 0.0)
        b = _cumsum_rows(lf, row, CHUNK)
        a = ig - b
        big_m = jnp.maximum(m_prev, _cummax_rows(a, row, CHUNK))
        m_t = b + big_m
        m_last = big_m[CHUNK - 1:CHUNK, :]
        gates.append(dict(a_t=a.T, big_m=big_m, inter=jnp.exp(m_prev - big_m), emt=jnp.exp(-m_t),
                          w=jnp.exp(a - m_last), decay=jnp.exp(m_prev - m_last)))
        m_prev = m_t[CHUNK - 1:CHUNK, :]
    m_scr[...] = m_prev

    def stage_a(c, h):
        rows = slice(c * CHUNK, (c + 1) * CHUNK)
        hs = h * HEAD_DIM
        g = gates[c]
        q2 = p_scr[rows, OFF_Q + hs:OFF_Q + hs + 2 * HEAD_DIM]
        k2 = p_scr[rows, OFF_K + hs:OFF_K + hs + 2 * HEAD_DIM] * K_SCALE
        k2_bf = k2.astype(_BF)
        zero = jnp.zeros((CHUNK, HEAD_DIM), _BF)
        k_diag = jnp.concatenate([jnp.concatenate([k2_bf[:, :HEAD_DIM], zero], axis=1),
                                  jnp.concatenate([zero, k2_bf[:, HEAD_DIM:]], axis=1)], axis=0)
        qk2 = _dot_nt(q2.astype(_BF), k_diag)
        out = []
        for j in range(2):
            cols = slice(j * HEAD_DIM, (j + 1) * HEAD_DIM)
            a_b = jnp.broadcast_to(g["a_t"][h + j:h + j + 1, :], (CHUNK, CHUNK))
            m_b = jnp.broadcast_to(g["big_m"][:, h + j:h + j + 1], (CHUNK, CHUNK))
            d = jnp.exp(jnp.where(causal, a_b - m_b, NEG_INF))
            inter_b = jnp.broadcast_to(g["inter"][:, h + j:h + j + 1], (CHUNK, HEAD_DIM))
            out.append(dict(qk=qk2[:, cols], d=d, q_inter=(inter_b * q2[:, cols]).astype(_BF), k=k2[:, cols]))
        return out

    def stage_b(c, h, st):
        rows = slice(c * CHUNK, (c + 1) * CHUNK)
        hs = h * HEAD_DIM
        g = gates[c]
        v = p_scr[rows, OFF_V + hs:OFF_V + hs + HEAD_DIM]
        s = st["qk"] * st["d"]
        lhs = jnp.concatenate([s.astype(_BF), st["q_inter"]], axis=1)
        rhs = jnp.concatenate([jnp.concatenate([v.astype(_BF), ones_blk], axis=1), caug[h].astype(_BF)], axis=0)
        tot = _dot(lhs, rhs)
        emt_b = jnp.broadcast_to(g["emt"][:, h:h + 1], (CHUNK, LANES))
        hn_scr[rows, hs:hs + HEAD_DIM] = _head_norm(tot[:, :LANES], tot[:, LANES:], emt_b,
                                                    mhg_ref[:, hs:hs + HEAD_DIM])
        w_b = jnp.broadcast_to(g["w"][:, h:h + 1], (CHUNK, LANES))
        wv = jnp.concatenate([(w_b * v).astype(_BF), w_b.astype(_BF)], axis=1)
        upd = _dot(st["k"].T.astype(_BF), wv)
        decay_b = jnp.broadcast_to(g["decay"][:, h:h + 1], (HEAD_DIM, 2 * LANES))
        caug[h] = decay_b * caug[h] + upd

    units = [(c, h) for c in range(n_chunks) for h in range(M_HEADS)]
    lag = min(UNIT_LAG, len(units))
    n_slots = len(units) + lag
    assert len(late_tiles) <= n_slots
    assert all(0 <= slot < n_slots for slot in side_work)
    pending = {}
    for t in range(n_slots):
        if t < len(units) and t % 2 == 0:
            pending[t], pending[t + 1] = stage_a(*units[t])
        if t < len(late_tiles):
            project_tile(late_tiles[t])
        for fn in side_work.get(t, ()):
            fn()
        if t >= lag:
            stage_b(*units[t - lag], pending.pop(t - lag))

    u = p_scr[:, OFF_GC:OFF_GC + CONV_WIDTH] * p_scr[:, OFF_HC:OFF_HC + CONV_WIDTH]
    ubuf[SUBLANES:SUBLANES + tb, :] = u
    u_m1 = ubuf[SUBLANES - 1:SUBLANES - 1 + tb, :]
    u_m2 = ubuf[SUBLANES - 2:SUBLANES - 2 + tb, :]
    y_conv = _conv_out(p_scr[:, OFF_GB:OFF_GB + CONV_WIDTH], p_scr[:, OFF_ZC:OFF_ZC + CONV_WIDTH],
                       u, u_m1, u_m2, cw_ref)
    ycat_scr[:, 0:CONV_WIDTH] = y_conv.astype(_BF)
    ycat_scr[:, CONV_WIDTH:CONV_WIDTH + M_WIDTH] = _gated_heads(hn_scr[...], p_scr).astype(_BF)
    ubuf[0:SUBLANES, :] = ubuf[tb:tb + SUBLANES, :]


def _sequence_scratch(tb):
    return [
        pltpu.VMEM((tb, MAIN_WIDTH), _F32),
        pltpu.VMEM((tb, LANES), _F32),
        pltpu.VMEM((tb, M_WIDTH), _F32),
        pltpu.VMEM((tb, D_MODEL), _BF),
        pltpu.VMEM((tb + 2 * SUBLANES, CONV_WIDTH), _F32),
        pltpu.VMEM((M_HEADS, HEAD_DIM, 2 * LANES), _F32),
        pltpu.VMEM((1, LANES), _F32),
    ]


def _front_body(wt_ref, x_ref, wgt_ref, wout_ref, ng_ref, fgain_ref, mhg_ref, cw_ref, big_ref, bfg_ref,
                xs_ref, sconv_ref, ms_ref, ns_ref, cs_ref,
                w3_ref, wg_ref, wo_ref, caug_out_ref, m_out_ref, ut_out_ref,
                ys_ref, sconv_out_ref, cs_out_ref, ns_out_ref, ms_out_ref,
                w3_scr, xn_scr, p3_scr, sc_scr, p_scr, g_scr, hn_scr, ycat_scr, ubuf, caug, m_scr):
    t = pl.program_id(0)
    n_prep = N_TILES // PREP_TILES

    @pl.when(t == 0)
    def _():
        xn_scr[...] = jnp.zeros_like(xn_scr)
        xn_scr[CHUNK - N_META:CHUNK, :] = _rms_rows(x_ref[...], ng_ref[...]).astype(_BF)
        gate_rows = jnp.concatenate([wgt_ref[...], jnp.zeros((LANES - N_GATES, D_MODEL), _F32)], axis=0)
        wg_ref[...] = gate_rows.T.astype(_BF)
        wo_ref[...] = wout_ref[...].astype(_BF)
        sc_scr[...] = jnp.zeros_like(sc_scr)

    @pl.when(t < n_prep)
    def _():
        for j in range(PREP_TILES):
            tile = wt_ref[j * MXU_COLS:(j + 1) * MXU_COLS, :].T.astype(_BF)
            w3_ref[j] = tile
            w3_scr[t * PREP_TILES + j] = tile
            p3_scr[t * PREP_TILES + j] = _dot(xn_scr[...], tile)

    @pl.when(t == n_prep - 1)
    def _():
        for j in range(N_TILES):
            p_scr[:, _tile_cols(j)] = p3_scr[j]
        caug[...] = jnp.zeros_like(caug)
        m_scr[...] = jnp.zeros_like(m_scr)
        ubuf[0:SUBLANES, :] = jnp.zeros((SUBLANES, CONV_WIDTH), _F32)
        _sequence_block(xn_scr[...], None, p_scr, g_scr, hn_scr, ycat_scr, ubuf, caug, m_scr,
                        wg_ref, mhg_ref, cw_ref, big_ref, bfg_ref, tb=CHUNK, n_pad=CHUNK - N_META)
        caug_out_ref[...] = caug[...]
        m_out_ref[...] = m_scr[...]
        ut_out_ref[...] = ubuf[0:SUBLANES, :]

    @pl.when(t >= n_prep)
    def _():
        _sample_block(xs_ref, w3_scr, wg_ref, wo_ref, ng_ref, fgain_ref, mhg_ref, cw_ref, big_ref, bfg_ref,
                      sconv_ref, ms_ref, ns_ref, cs_ref,
                      ys_ref, sconv_out_ref, cs_out_ref, ns_out_ref, ms_out_ref,
                      p_scr, hn_scr, sc_scr, ycat_scr)


def _front_call(w_t, meta_tokens, w_out, small, fgain, xs, state_conv, m_pad, n_state, c_state):
    ng, mhg, cw, big, bfg = small
    n_prep = N_TILES // PREP_TILES
    rows = xs.shape[0]
    nseq = c_state.shape[0]
    n_sample = rows // CHUNK

    def const(shape):
        nd = len(shape)
        return pl.BlockSpec(shape, lambda t: (0,) * nd)

    def sample_spec(*block):
        nd = len(block) - 1
        return pl.BlockSpec(block, lambda t: (jnp.maximum(t - n_prep, 0),) + (0,) * nd)

    def prep_idx(t):
        return jnp.minimum(t, n_prep - 1)

    xs_spec = sample_spec(CHUNK, D_MODEL)
    sconv_spec = sample_spec(SEQ_PER_STEP, CONV_K - 1, CONV_WIDTH)
    cs_spec = sample_spec(SEQ_PER_STEP, M_HEADS, HEAD_DIM, HEAD_DIM)
    ns_spec = sample_spec(SEQ_PER_STEP, M_HEADS, HEAD_DIM)
    ms_spec = sample_spec(SEQ_PER_STEP, LANES)
    in_specs = [
        pl.BlockSpec((PREP_TILES * MXU_COLS, D_MODEL), lambda t: (prep_idx(t), 0)),
        const(meta_tokens.shape),
        pl.BlockSpec((N_GATES, D_MODEL), lambda t: (MAIN_WIDTH // N_GATES, 0)),
        const(w_out.shape), const(ng.shape), const(fgain.shape), const(mhg.shape), const(cw.shape),
        _SMEM_SPEC, _SMEM_SPEC,
        xs_spec, sconv_spec, ms_spec, ns_spec, cs_spec,
    ]
    out_shape = (
        jax.ShapeDtypeStruct((N_TILES, D_MODEL, MXU_COLS), _BF),
        jax.ShapeDtypeStruct((D_MODEL, LANES), _BF),
        jax.ShapeDtypeStruct(w_out.shape, _BF),
        jax.ShapeDtypeStruct((M_HEADS, HEAD_DIM, 2 * LANES), _F32),
        jax.ShapeDtypeStruct((1, LANES), _F32),
        jax.ShapeDtypeStruct((SUBLANES, CONV_WIDTH), _F32),
        jax.ShapeDtypeStruct((rows, D_MODEL), _F32),
        jax.ShapeDtypeStruct(state_conv.shape, _F32),
        jax.ShapeDtypeStruct(c_state.shape, _F32),
        jax.ShapeDtypeStruct(n_state.shape, _F32),
        jax.ShapeDtypeStruct((nseq, LANES), _F32),
    )
    out_specs = (
        pl.BlockSpec((PREP_TILES, D_MODEL, MXU_COLS), lambda t: (prep_idx(t), 0, 0)),
        const(out_shape[1].shape), const(out_shape[2].shape), const(out_shape[3].shape), const(out_shape[4].shape),
        const(out_shape[5].shape),
        xs_spec, sconv_spec, cs_spec, ns_spec, ms_spec,
    )
    scratch = [
        pltpu.VMEM((N_TILES, D_MODEL, MXU_COLS), _BF),
        pltpu.VMEM((CHUNK, D_MODEL), _BF),
        pltpu.VMEM((N_TILES, CHUNK, MXU_COLS), _F32),
        pltpu.VMEM((CHUNK, CONV_WIDTH), _F32),
    ] + _sequence_scratch(CHUNK)
    return pl.pallas_call(
        _front_body,
        grid=(n_prep + n_sample,),
        in_specs=in_specs,
        out_specs=out_specs,
        out_shape=out_shape,
        scratch_shapes=scratch,
        compiler_params=pltpu.CompilerParams(
            dimension_semantics=("arbitrary",), vmem_limit_bytes=VMEM_REQUEST_BYTES),
        name="prep_meta_sample",
    )(w_t, meta_tokens, w_t, w_out, ng, fgain, mhg, cw, big, bfg, xs, state_conv, m_pad, n_state, c_state)


def _prompt_body(x_ref, w3_ref, wg_ref, wo_ref, ng_ref, fgain_ref, mhg_ref, cw_ref, big_ref, bfg_ref,
                 caug0_ref, m0_ref, ut0_ref,
                 y_ref, c_out_ref, n_out_ref, m_out_ref, conv_out_ref,
                 out_scr, p_scr, g_scr, hn_scr, ycat_scr, ubuf, caug, m_scr, *, tb, nt, n_blocks):
    s = pl.program_id(0)
    i = s % nt
    live = s < n_blocks

    @pl.when(s == 0)
    def _():
        out_scr[...] = jnp.zeros_like(out_scr)

    @pl.when(live & (i == 0))
    def _():
        caug[...] = caug0_ref[...]
        m_scr[...] = m0_ref[...]
        ubuf[0:SUBLANES, :] = ut0_ref[...]

    @pl.when(live)
    def _():
        xn = _rms_rows(x_ref[0], ng_ref[...]).astype(_BF)

        def project_tile(t):
            p_scr[:, _tile_cols(t)] = _dot(xn, w3_ref[t])

        def norm_previous(r0, r1):
            def fn():
                y_ref[0, r0:r1, :] = _rms_rows(out_scr[r0:r1, :], fgain_ref[...])
            return fn

        q4 = tb // 4
        side_work = {2 + 2 * j: [norm_previous(j * q4, (j + 1) * q4)] for j in range(4)}
        _sequence_block(xn, project_tile, p_scr, g_scr, hn_scr, ycat_scr, ubuf, caug, m_scr,
                        wg_ref, mhg_ref, cw_ref, big_ref, bfg_ref, tb=tb, n_pad=0, side_work=side_work)
        out_scr[...] = _dot(ycat_scr[...], wo_ref[...]) + x_ref[0]

    @pl.when(s == n_blocks)
    def _():
        y_ref[0] = _rms_rows(out_scr[...], fgain_ref[...])

    @pl.when(live & (i == nt - 1))
    def _():
        m_out_ref[0] = m_scr[...]
        conv_out_ref[0] = ubuf[SUBLANES - (CONV_K - 1):SUBLANES, :]
        for h in range(M_HEADS):
            c_out_ref[0, h] = caug[h, :, 0:HEAD_DIM]
            n_out_ref[0, h:h + 1, :] = caug[h, :, LANES:2 * LANES].T[0:1, :]


def _prompt_call(x, w3, wg, wo, small, fgain, caug0, m0, ut0):
    ng, mhg, cw, big, bfg = small
    nb, t, _ = x.shape
    tb = PROMPT_ROWS
    nt = t // tb
    n_blocks = nb * nt

    def const(shape):
        nd = len(shape)
        return pl.BlockSpec(shape, lambda s: (0,) * nd)

    def block_of(s):
        return jnp.minimum(s, n_blocks - 1)

    def rows_spec(step_to_block):
        return pl.BlockSpec((1, tb, D_MODEL), lambda s: (step_to_block(s) // nt, step_to_block(s) % nt, 0))

    def state_spec(*tail):
        nd = len(tail)
        return pl.BlockSpec((1,) + tail, lambda s: (block_of(s) // nt,) + (0,) * nd)

    in_specs = [
        rows_spec(block_of),
        const(w3.shape), const(wg.shape), const(wo.shape), const(ng.shape), const(fgain.shape),
        const(mhg.shape), const(cw.shape), _SMEM_SPEC, _SMEM_SPEC,
        const(caug0.shape), const(m0.shape), const(ut0.shape),
    ]
    out_shape = (
        jax.ShapeDtypeStruct((nb, t, D_MODEL), _F32),
        jax.ShapeDtypeStruct((nb, M_HEADS, HEAD_DIM, HEAD_DIM), _F32),
        jax.ShapeDtypeStruct((nb, M_HEADS, HEAD_DIM), _F32),
        jax.ShapeDtypeStruct((nb, 1, LANES), _F32),
        jax.ShapeDtypeStruct((nb, CONV_K - 1, CONV_WIDTH), _F32),
    )
    out_specs = (
        rows_spec(lambda s: jnp.maximum(s - 1, 0)),
        state_spec(M_HEADS, HEAD_DIM, HEAD_DIM),
        state_spec(M_HEADS, HEAD_DIM),
        state_spec(1, LANES),
        state_spec(CONV_K - 1, CONV_WIDTH),
    )
    scratch = [
        pltpu.VMEM((tb, D_MODEL), _F32),
    ] + _sequence_scratch(tb)
    return pl.pallas_call(
        functools.partial(_prompt_body, tb=tb, nt=nt, n_blocks=n_blocks),
        grid=(n_blocks + 1,),
        in_specs=in_specs,
        out_specs=out_specs,
        out_shape=out_shape,
        scratch_shapes=scratch,
        compiler_params=pltpu.CompilerParams(
            dimension_semantics=("arbitrary",), vmem_limit_bytes=VMEM_REQUEST_BYTES),
        name="prompt_layer",
    )(x, w3, wg, wo, ng, fgain, mhg, cw, big, bfg, caug0, m0, ut0)


def _sample_block(x_ref, w3_ref, wg_ref, wo_ref, ng_ref, fgain_ref, mhg_ref, cw_ref, big_ref, bfg_ref,
                  sconv_ref, m_ref, n_ref, c_ref,
                  y_ref, conv_out_ref, c_out_ref, n_out_ref, m_out_ref,
                  p_scr, ni_scr, sc_scr, ycat_scr):
    x = x_ref[...]
    xn = _rms_rows(x, ng_ref[...]).astype(_BF)
    g_raw = _dot(xn, wg_ref[...])
    for t in QKV_TILES:
        p_scr[:, _tile_cols(t)] = _dot(xn, w3_ref[t])

    row = lax.broadcasted_iota(jnp.int32, (CHUNK, CHUNK), 0)
    col = lax.broadcasted_iota(jnp.int32, (CHUNK, CHUNK), 1)
    pos = row % DEC_SEQ
    grp = row // DEC_SEQ
    causal = (grp == (col // DEC_SEQ)) & (col <= row)
    grow = lax.broadcasted_iota(jnp.int32, (SEQ_PER_STEP, CHUNK), 0)
    gcol = lax.broadcasted_iota(jnp.int32, (SEQ_PER_STEP, CHUNK), 1)
    pick_last = (gcol == grow * DEC_SEQ + (DEC_SEQ - 1)).astype(_F32)
    seq_sum = (gcol // DEC_SEQ == grow).astype(_BF)
    erow = lax.broadcasted_iota(jnp.int32, (CHUNK, SEQ_PER_STEP), 0)
    ecol = lax.broadcasted_iota(jnp.int32, (CHUNK, SEQ_PER_STEP), 1)
    own_seq = erow // DEC_SEQ == ecol
    ones_blk = jnp.ones((CHUNK, LANES), _BF)

    def rows_from_seqs(v16):
        return jnp.broadcast_to(v16[:, None, :], (SEQ_PER_STEP, DEC_SEQ, LANES)).reshape(CHUNK, LANES)

    def last_row_of_seq(v):
        v3 = v.reshape(SEQ_PER_STEP, DEC_SEQ, LANES)
        return jnp.broadcast_to(v3[:, DEC_SEQ - 1:DEC_SEQ, :], v3.shape).reshape(CHUNK, LANES)

    ig, lf = _gate_columns(g_raw, big_ref, bfg_ref)
    b = _cumsum_rows(lf, pos, DEC_SEQ)
    a = ig - b
    m_prev = rows_from_seqs(m_ref[...])
    big_m = jnp.maximum(m_prev, _cummax_rows(a, pos, DEC_SEQ))
    m_t = b + big_m
    inter = jnp.exp(m_prev - big_m)
    emt = jnp.exp(-m_t)
    m_last = last_row_of_seq(big_m)
    w = jnp.exp(a - m_last)
    decay = jnp.exp(m_prev - m_last)
    m_out_ref[...] = _dot_exact(pick_last, m_t)
    dec16 = _dot_exact(pick_last, decay)
    a_t = a.T

    for t in REST_TILES:
        p_scr[:, _tile_cols(t)] = _dot(xn, w3_ref[t])

    heads = []
    for h in range(M_HEADS):
        hs = h * HEAD_DIM
        q = p_scr[:, OFF_Q + hs:OFF_Q + hs + HEAD_DIM]
        k = p_scr[:, OFF_K + hs:OFF_K + hs + HEAD_DIM] * K_SCALE
        v = p_scr[:, OFF_V + hs:OFF_V + hs + HEAD_DIM]
        q_bf = q.astype(_BF)
        a_b = jnp.broadcast_to(a_t[h:h + 1, :], (CHUNK, CHUNK))
        m_b = jnp.broadcast_to(big_m[:, h:h + 1], (CHUNK, CHUNK))
        d = jnp.exp(jnp.where(causal, a_b - m_b, NEG_INF))
        w_b = jnp.broadcast_to(w[:, h:h + 1], (CHUNK, LANES))
        heads.append(dict(q_bf=q_bf, k=k, v=v, d=d, w_b=w_b, qk=_dot_nt(q_bf, k.astype(_BF))))

    u = p_scr[:, OFF_GC:OFF_GC + CONV_WIDTH] * p_scr[:, OFF_HC:OFF_HC + CONV_WIDTH]
    for g in range(SEQ_PER_STEP):
        sc_scr[g * DEC_SEQ:g * DEC_SEQ + CONV_K - 1, :] = sconv_ref[g]
        conv_out_ref[g] = u[(g + 1) * DEC_SEQ - (CONV_K - 1):(g + 1) * DEC_SEQ, :]
    sc = sc_scr[...]
    posw = lax.broadcasted_iota(jnp.int32, (CHUNK, CONV_WIDTH), 0) % DEC_SEQ
    u_m1 = jnp.where(posw == 0, pltpu.roll(sc, CHUNK - 1, 0), pltpu.roll(u, 1, 0))
    u_m2 = jnp.where(posw < 2, sc, pltpu.roll(u, 2, 0))
    y_conv = _conv_out(p_scr[:, OFF_GB:OFF_GB + CONV_WIDTH], p_scr[:, OFF_ZC:OFF_ZC + CONV_WIDTH],
                       u, u_m1, u_m2, cw_ref)
    ycat_scr[:, 0:CONV_WIDTH] = y_conv.astype(_BF)

    for h in range(M_HEADS):
        hs = h * HEAD_DIM
        st = heads[h]
        wv = st["w_b"] * st["v"]
        k_t_bf = st["k"].T.astype(_BF)
        dec_b = jnp.broadcast_to(decay[:, h:h + 1], (CHUNK, HEAD_DIM))

        for g in range(0, SEQ_PER_STEP, 2):
            pair_rows = slice(g * DEC_SEQ, (g + 2) * DEC_SEQ)
            c0, c1 = c_ref[g, h], c_ref[g + 1, h]
            qc = _dot(st["q_bf"][pair_rows, :], jnp.concatenate([c0.astype(_BF), c1.astype(_BF)], axis=1))
            ni_scr[g * DEC_SEQ:(g + 1) * DEC_SEQ, hs:hs + HEAD_DIM] = qc[0:DEC_SEQ, 0:HEAD_DIM]
            ni_scr[(g + 1) * DEC_SEQ:(g + 2) * DEC_SEQ, hs:hs + HEAD_DIM] = qc[DEC_SEQ:, HEAD_DIM:]
            wv_pair = jnp.concatenate([jnp.where(grp == g, wv, 0.0).astype(_BF),
                                       jnp.where(grp == g + 1, wv, 0.0).astype(_BF)], axis=1)
            upd = _dot(k_t_bf, wv_pair)
            dec0 = jnp.broadcast_to(dec_b[g * DEC_SEQ:g * DEC_SEQ + 1, :], (HEAD_DIM, HEAD_DIM))
            dec1 = jnp.broadcast_to(dec_b[(g + 1) * DEC_SEQ:(g + 1) * DEC_SEQ + 1, :], (HEAD_DIM, HEAD_DIM))
            c_out_ref[g, h] = dec0 * c0 + upd[:, 0:HEAD_DIM]
            c_out_ref[g + 1, h] = dec1 * c1 + upd[:, HEAD_DIM:]

        s = st["qk"] * st["d"]
        sv = _dot(s.astype(_BF), jnp.concatenate([st["v"].astype(_BF), ones_blk], axis=1))
        n_old = n_ref[:, h, :]
        num_inter = ni_scr[:, hs:hs + HEAD_DIM]
        qn = _dot_nt(st["q_bf"], n_old.astype(_BF))
        den_inter = jnp.sum(jnp.where(own_seq, qn, 0.0), axis=-1, keepdims=True)
        inter_b = jnp.broadcast_to(inter[:, h:h + 1], (CHUNK, LANES))
        num = sv[:, :LANES] + inter_b * num_inter
        den = sv[:, LANES:] + inter_b * den_inter
        emt_b = jnp.broadcast_to(emt[:, h:h + 1], (CHUNK, LANES))
        ni_scr[:, hs:hs + HEAD_DIM] = _head_norm(num, den, emt_b, mhg_ref[:, hs:hs + HEAD_DIM])
        n_upd = _dot(seq_sum, (st["w_b"] * st["k"]).astype(_BF))
        dec16_b = jnp.broadcast_to(dec16[:, h:h + 1], (SEQ_PER_STEP, HEAD_DIM))
        n_out_ref[:, h, :] = dec16_b * n_old + n_upd

    ycat_scr[:, CONV_WIDTH:CONV_WIDTH + M_WIDTH] = _gated_heads(ni_scr[...], p_scr).astype(_BF)
    out = _dot(ycat_scr[...], wo_ref[...]) + x
    y_ref[...] = _rms_rows(out, fgain_ref[...])


def kernel(x_prompt, x_sample, state_conv, state_mlstm_c, state_mlstm_n, state_mlstm_m, meta_tokens, norm_gain,
           w_in, conv_w, b_igate, b_fgate, mh_norm_gain, w_out, final_norm_gain):
    nseq, dec_seq, _ = x_sample.shape
    assert dec_seq == DEC_SEQ and meta_tokens.shape[0] == N_META
    assert w_in.shape == (D_MODEL, MAIN_WIDTH + N_GATES) and x_prompt.shape[1] % PROMPT_ROWS == 0
    assert (nseq * DEC_SEQ) % CHUNK == 0

    w_t = w_in.T
    small = (norm_gain.reshape(1, D_MODEL), mh_norm_gain.reshape(1, M_WIDTH), conv_w,
             b_igate.astype(_F32), b_fgate.astype(_F32))
    fgain = final_norm_gain.reshape(1, D_MODEL)

    xs = x_sample.reshape(nseq * DEC_SEQ, D_MODEL)
    m_pad = jnp.pad(state_mlstm_m, ((0, 0), (0, LANES - M_HEADS)))
    w3, wg, wo, caug_meta, m_meta, ut_meta, ys, conv_s, c_s, n_s, m_s = _front_call(
        w_t, meta_tokens.astype(_F32), w_out.astype(_F32), small, fgain, xs, state_conv, m_pad, state_mlstm_n,
        state_mlstm_c)

    y_prompt, c_p, n_p, m_p, conv_p = _prompt_call(x_prompt, w3, wg, wo, small, fgain, caug_meta, m_meta, ut_meta)
    m_pp = m_p[:, 0, :M_HEADS]
    y_sample = ys.reshape(nseq, DEC_SEQ, D_MODEL)
    m_ss = m_s[:, :M_HEADS]

    return (y_prompt, y_sample, conv_p, c_p, n_p, m_pp, conv_s, c_s, n_s, m_ss)
```

```python
import functools

import jax
import jax.numpy as jnp
from jax import lax
from jax.experimental import pallas as pl
from jax.experimental.pallas import tpu as pltpu

D_MODEL = 1024
CONV_WIDTH = 512
M_WIDTH = 512
M_HEADS = 4
HEAD_DIM = 128
N_META = 16
CONV_K = 3
EPS = 1e-6
MAIN_WIDTH = 4 * CONV_WIDTH + 5 * M_WIDTH
N_GATES = 2 * M_HEADS
LANES = 128
SUBLANES = 8
MXU_COLS = 256
N_TILES = MAIN_WIDTH // MXU_COLS
CHUNK = 128
DEC_SEQ = 8
SEQ_PER_STEP = CHUNK // DEC_SEQ
NEG_INF = float("-inf")
K_SCALE = HEAD_DIM ** -0.5

PROMPT_ROWS = 512
PREP_TILES = 3
VMEM_REQUEST_BYTES = 56 * 1024 * 1024
EARLY_TILES = 2
UNIT_LAG = 3

OFF_GB, OFF_GC, OFF_HC, OFF_ZC = 0, 512, 1024, 1536
OFF_Q, OFF_K, OFF_V, OFF_O, OFF_ZM = 2048, 2560, 3072, 3584, 4096
QKV_TILES = tuple(range(OFF_Q // MXU_COLS, OFF_O // MXU_COLS))
REST_TILES = tuple(t for t in range(N_TILES) if t not in QKV_TILES)

_SMEM_SPEC = pl.BlockSpec(memory_space=pltpu.SMEM)
_HI = lax.Precision.HIGHEST
_BF = jnp.bfloat16
_F32 = jnp.float32


def _dot(a, b):
    return jnp.dot(a, b, preferred_element_type=_F32)


def _dot_exact(a, b):
    return jnp.dot(a, b, precision=_HI, preferred_element_type=_F32)


def _dot_nt(a, b):
    return lax.dot_general(a, b, (((1,), (1,)), ((), ())), preferred_element_type=_F32)


def _rms_rows(x, gain_row):
    return x * lax.rsqrt(jnp.mean(x * x, axis=-1, keepdims=True) + EPS) * gain_row


def _sigmoid(x):
    return 1.0 / (1.0 + jnp.exp(-x))


def _silu(x):
    return x * _sigmoid(x)


def _tile_cols(t):
    return slice(t * MXU_COLS, (t + 1) * MXU_COLS)


def _cummax_rows(x, pos, period):
    s = 1
    while s < period:
        shifted = pltpu.roll(x, s, 0)
        x = jnp.maximum(x, jnp.where(pos >= s, shifted, NEG_INF))
        s *= 2
    return x


def _cumsum_rows(x, pos, period):
    s = 1
    while s < period:
        x = x + jnp.where(pos >= s, pltpu.roll(x, s, 0), 0.0)
        s *= 2
    return x


def _bias_row(b_ref):
    lane = lax.broadcasted_iota(jnp.int32, (1, LANES), 1)
    row = jnp.zeros((1, LANES), _F32)
    for h in range(M_HEADS):
        row = jnp.where(lane == h, b_ref[h], row)
    return row


def _gate_columns(g_raw, big_ref, bfg_ref):
    ig = g_raw + _bias_row(big_ref)
    lf = jax.nn.log_sigmoid(pltpu.roll(g_raw, LANES - M_HEADS, 1) + _bias_row(bfg_ref))
    return ig, lf


def _head_norm(num, den, emt_b, gain_row):
    hh = num / jnp.maximum(jnp.abs(den), emt_b)
    return hh * lax.rsqrt(jnp.mean(hh * hh, axis=-1, keepdims=True) + EPS) * gain_row


def _conv_out(gb, zc, u, u_m1, u_m2, cw_ref):
    yc = cw_ref[0:1, :] * u_m2 + cw_ref[1:2, :] * u_m1 + cw_ref[2:3, :] * u
    return gb * yc * _silu(zc)


def _gated_heads(hn, p_scr):
    return hn * _sigmoid(p_scr[:, OFF_O:OFF_O + M_WIDTH]) * _silu(p_scr[:, OFF_ZM:OFF_ZM + M_WIDTH])


def _sequence_block(xn, project_tile, p_scr, g_scr, hn_scr, ycat_scr, ubuf, caug, m_scr,
                    wg_ref, mhg_ref, cw_ref, big_ref, bfg_ref, *, tb, n_pad, side_work=None):
    side_work = side_work or {}
    g_scr[...] = _dot(xn, wg_ref[...])
    late_tiles = []
    if project_tile is not None:
        for t in QKV_TILES + REST_TILES[:EARLY_TILES]:
            project_tile(t)
        late_tiles = list(REST_TILES[EARLY_TILES:])

    row = lax.broadcasted_iota(jnp.int32, (CHUNK, CHUNK), 0)
    col = lax.broadcasted_iota(jnp.int32, (CHUNK, CHUNK), 1)
    causal = col <= row
    ones_blk = jnp.ones((CHUNK, LANES), _BF)

    n_chunks = tb // CHUNK
    gates = []
    m_prev = m_scr[...]
    for c in range(n_chunks):
        rows = slice(c * CHUNK, (c + 1) * CHUNK)
        ig, lf = _gate_columns(g_scr[rows, :], big_ref, bfg_ref)
        if n_pad:
            ig = jnp.where(row >= n_pad, ig, NEG_INF)
            lf = jnp.where(row >= n_pad, lf, 0.0)
        b = _cumsum_rows(lf, row, CHUNK)
        a = ig - b
        big_m = jnp.maximum(m_prev, _cummax_rows(a, row, CHUNK))
        m_t = b + big_m
        m_last = big_m[CHUNK - 1:CHUNK, :]
        gates.append(dict(a_t=a.T, big_m=big_m, inter=jnp.exp(m_prev - big_m), emt=jnp.exp(-m_t),
                          w=jnp.exp(a - m_last), decay=jnp.exp(m_prev - m_last)))
        m_prev = m_t[CHUNK - 1:CHUNK, :]
    m_scr[...] = m_prev

    def stage_a(c, h):
        rows = slice(c * CHUNK, (c + 1) * CHUNK)
        hs = h * HEAD_DIM
        g = gates[c]
        q2 = p_scr[rows, OFF_Q + hs:OFF_Q + hs + 2 * HEAD_DIM]
        k2 = p_scr[rows, OFF_K + hs:OFF_K + hs + 2 * HEAD_DIM] * K_SCALE
        k2_bf = k2.astype(_BF)
        zero = jnp.zeros((CHUNK, HEAD_DIM), _BF)
        k_diag = jnp.concatenate([jnp.concatenate([k2_bf[:, :HEAD_DIM], zero], axis=1),
                                  jnp.concatenate([zero, k2_bf[:, HEAD_DIM:]], axis=1)], axis=0)
        qk2 = _dot_nt(q2.astype(_BF), k_diag)
        out = []
        for j in range(2):
            cols = slice(j * HEAD_DIM, (j + 1) * HEAD_DIM)
            a_b = jnp.broadcast_to(g["a_t"][h + j:h + j + 1, :], (CHUNK, CHUNK))
            m_b = jnp.broadcast_to(g["big_m"][:, h + j:h + j + 1], (CHUNK, CHUNK))
            d = jnp.exp(jnp.where(causal, a_b - m_b, NEG_INF))
            inter_b = jnp.broadcast_to(g["inter"][:, h + j:h + j + 1], (CHUNK, HEAD_DIM))
            out.append(dict(qk=qk2[:, cols], d=d, q_inter=(inter_b * q2[:, cols]).astype(_BF), k=k2[:, cols]))
        return out

    def stage_b(c, h, st):
        rows = slice(c * CHUNK, (c + 1) * CHUNK)
        hs = h * HEAD_DIM
        g = gates[c]
        v = p_scr[rows, OFF_V + hs:OFF_V + hs + HEAD_DIM]
        s = st["qk"] * st["d"]
        lhs = jnp.concatenate([s.astype(_BF), st["q_inter"]], axis=1)
        rhs = jnp.concatenate([jnp.concatenate([v.astype(_BF), ones_blk], axis=1), caug[h].astype(_BF)], axis=0)
        tot = _dot(lhs, rhs)
        emt_b = jnp.broadcast_to(g["emt"][:, h:h + 1], (CHUNK, LANES))
        hn_scr[rows, hs:hs + HEAD_DIM] = _head_norm(tot[:, :LANES], tot[:, LANES:], emt_b,
                                                    mhg_ref[:, hs:hs + HEAD_DIM])
        w_b = jnp.broadcast_to(g["w"][:, h:h + 1], (CHUNK, LANES))
        wv = jnp.concatenate([(w_b * v).astype(_BF), w_b.astype(_BF)], axis=1)
        upd = _dot(st["k"].T.astype(_BF), wv)
        decay_b = jnp.broadcast_to(g["decay"][:, h:h + 1], (HEAD_DIM, 2 * LANES))
        caug[h] = decay_b * caug[h] + upd

    units = [(c, h) for c in range(n_chunks) for h in range(M_HEADS)]
    lag = min(UNIT_LAG, len(units))
    n_slots = len(units) + lag
    assert len(late_tiles) <= n_slots
    assert all(0 <= slot < n_slots for slot in side_work)
    pending = {}
    for t in range(n_slots):
        if t < len(units) and t % 2 == 0:
            pending[t], pending[t + 1] = stage_a(*units[t])
        if t < len(late_tiles):
            project_tile(late_tiles[t])
        for fn in side_work.get(t, ()):
            fn()
        if t >= lag:
            stage_b(*units[t - lag], pending.pop(t - lag))

    u = p_scr[:, OFF_GC:OFF_GC + CONV_WIDTH] * p_scr[:, OFF_HC:OFF_HC + CONV_WIDTH]
    ubuf[SUBLANES:SUBLANES + tb, :] = u
    u_m1 = ubuf[SUBLANES - 1:SUBLANES - 1 + tb, :]
    u_m2 = ubuf[SUBLANES - 2:SUBLANES - 2 + tb, :]
    y_conv = _conv_out(p_scr[:, OFF_GB:OFF_GB + CONV_WIDTH], p_scr[:, OFF_ZC:OFF_ZC + CONV_WIDTH],
                       u, u_m1, u_m2, cw_ref)
    ycat_scr[:, 0:CONV_WIDTH] = y_conv.astype(_BF)
    ycat_scr[:, CONV_WIDTH:CONV_WIDTH + M_WIDTH] = _gated_heads(hn_scr[...], p_scr).astype(_BF)
    ubuf[0:SUBLANES, :] = ubuf[tb:tb + SUBLANES, :]


def _sequence_scratch(tb):
    return [
        pltpu.VMEM((tb, MAIN_WIDTH), _F32),
        pltpu.VMEM((tb, LANES), _F32),
        pltpu.VMEM((tb, M_WIDTH), _F32),
        pltpu.VMEM((tb, D_MODEL), _BF),
        pltpu.VMEM((tb + 2 * SUBLANES, CONV_WIDTH), _F32),
        pltpu.VMEM((M_HEADS, HEAD_DIM, 2 * LANES), _F32),
        pltpu.VMEM((1, LANES), _F32),
    ]


def _front_body(wt_ref, x_ref, wgt_ref, wout_ref, ng_ref, fgain_ref, mhg_ref, cw_ref, big_ref, bfg_ref,
                xs_ref, sconv_ref, ms_ref, ns_ref, cs_ref,
                w3_ref, wg_ref, wo_ref, caug_out_ref, m_out_ref, ut_out_ref,
                ys_ref, sconv_out_ref, cs_out_ref, ns_out_ref, ms_out_ref,
                w3_scr, xn_scr, p3_scr, sc_scr, ms_in_scr, ms_new_scr,
                p_scr, g_scr, hn_scr, ycat_scr, ubuf, caug, m_scr):
    t = pl.program_id(0)
    n_prep = N_TILES // PREP_TILES
    last = pl.num_programs(0) - 1

    @pl.when(t == 0)
    def _():
        xn_scr[...] = jnp.zeros_like(xn_scr)
        xn_scr[CHUNK - N_META:CHUNK, :] = _rms_rows(x_ref[...], ng_ref[...]).astype(_BF)
        gate_rows = jnp.concatenate([wgt_ref[...], jnp.zeros((LANES - N_GATES, D_MODEL), _F32)], axis=0)
        wg_ref[...] = gate_rows.T.astype(_BF)
        wo_ref[...] = wout_ref[...].astype(_BF)
        sc_scr[...] = jnp.zeros_like(sc_scr)
        m_heads = jnp.concatenate([ms_ref[...], jnp.zeros((LANES - M_HEADS, LANES), _F32)], axis=0)
        ms_in_scr[...] = m_heads.T

    @pl.when(t < n_prep)
    def _():
        for j in range(PREP_TILES):
            tile = wt_ref[j * MXU_COLS:(j + 1) * MXU_COLS, :].T.astype(_BF)
            w3_ref[j] = tile
            w3_scr[t * PREP_TILES + j] = tile
            p3_scr[t * PREP_TILES + j] = _dot(xn_scr[...], tile)

    @pl.when(t == n_prep - 1)
    def _():
        for j in range(N_TILES):
            p_scr[:, _tile_cols(j)] = p3_scr[j]
        caug[...] = jnp.zeros_like(caug)
        m_scr[...] = jnp.zeros_like(m_scr)
        ubuf[0:SUBLANES, :] = jnp.zeros((SUBLANES, CONV_WIDTH), _F32)
        _sequence_block(xn_scr[...], None, p_scr, g_scr, hn_scr, ycat_scr, ubuf, caug, m_scr,
                        wg_ref, mhg_ref, cw_ref, big_ref, bfg_ref, tb=CHUNK, n_pad=CHUNK - N_META)
        caug_out_ref[...] = caug[...]
        m_out_ref[...] = m_scr[...]
        ut_out_ref[...] = ubuf[0:SUBLANES, :]

    @pl.when(t >= n_prep)
    def _():
        seqs = pl.ds(pl.multiple_of((t - n_prep) * SEQ_PER_STEP, SEQ_PER_STEP), SEQ_PER_STEP)
        _sample_block(xs_ref, w3_scr, wg_ref, wo_ref, ng_ref, fgain_ref, mhg_ref, cw_ref, big_ref, bfg_ref,
                      sconv_ref, ms_in_scr.at[seqs], ns_ref, cs_ref,
                      ys_ref, sconv_out_ref, cs_out_ref, ns_out_ref, ms_new_scr.at[seqs],
                      p_scr, hn_scr, sc_scr, ycat_scr)

    @pl.when(t == last)
    def _():
        ms_out_ref[...] = ms_new_scr[...].T[0:M_HEADS, :]


def _front_call(w_t, meta_tokens, w_out, small, fgain, xs, state_conv, m_heads, n_state, c_state):
    ng, mhg, cw, big, bfg = small
    n_prep = N_TILES // PREP_TILES
    rows = xs.shape[0]
    nseq = c_state.shape[0]
    n_sample = rows // CHUNK
    assert m_heads.shape == (M_HEADS, LANES) and nseq == LANES

    def const(shape):
        nd = len(shape)
        return pl.BlockSpec(shape, lambda t: (0,) * nd)

    def sample_spec(*block):
        nd = len(block) - 1
        return pl.BlockSpec(block, lambda t: (jnp.maximum(t - n_prep, 0),) + (0,) * nd)

    def prep_idx(t):
        return jnp.minimum(t, n_prep - 1)

    xs_spec = sample_spec(CHUNK, D_MODEL)
    sconv_spec = sample_spec(SEQ_PER_STEP, CONV_K - 1, CONV_WIDTH)
    cs_spec = sample_spec(SEQ_PER_STEP, M_HEADS, HEAD_DIM, HEAD_DIM)
    ns_spec = sample_spec(SEQ_PER_STEP, M_HEADS, HEAD_DIM)
    ms_spec = const(m_heads.shape)
    in_specs = [
        pl.BlockSpec((PREP_TILES * MXU_COLS, D_MODEL), lambda t: (prep_idx(t), 0)),
        const(meta_tokens.shape),
        pl.BlockSpec((N_GATES, D_MODEL), lambda t: (MAIN_WIDTH // N_GATES, 0)),
        const(w_out.shape), const(ng.shape), const(fgain.shape), const(mhg.shape), const(cw.shape),
        _SMEM_SPEC, _SMEM_SPEC,
        xs_spec, sconv_spec, ms_spec, ns_spec, cs_spec,
    ]
    out_shape = (
        jax.ShapeDtypeStruct((N_TILES, D_MODEL, MXU_COLS), _BF),
        jax.ShapeDtypeStruct((D_MODEL, LANES), _BF),
        jax.ShapeDtypeStruct(w_out.shape, _BF),
        jax.ShapeDtypeStruct((M_HEADS, HEAD_DIM, 2 * LANES), _F32),
        jax.ShapeDtypeStruct((1, LANES), _F32),
        jax.ShapeDtypeStruct((SUBLANES, CONV_WIDTH), _F32),
        jax.ShapeDtypeStruct((rows, D_MODEL), _F32),
        jax.ShapeDtypeStruct(state_conv.shape, _F32),
        jax.ShapeDtypeStruct(c_state.shape, _F32),
        jax.ShapeDtypeStruct(n_state.shape, _F32),
        jax.ShapeDtypeStruct(m_heads.shape, _F32),
    )
    out_specs = (
        pl.BlockSpec((PREP_TILES, D_MODEL, MXU_COLS), lambda t: (prep_idx(t), 0, 0)),
        const(out_shape[1].shape), const(out_shape[2].shape), const(out_shape[3].shape), const(out_shape[4].shape),
        const(out_shape[5].shape),
        xs_spec, sconv_spec, cs_spec, ns_spec, ms_spec,
    )
    scratch = [
        pltpu.VMEM((N_TILES, D_MODEL, MXU_COLS), _BF),
        pltpu.VMEM((CHUNK, D_MODEL), _BF),
        pltpu.VMEM((N_TILES, CHUNK, MXU_COLS), _F32),
        pltpu.VMEM((CHUNK, CONV_WIDTH), _F32),
        pltpu.VMEM((LANES, LANES), _F32),
        pltpu.VMEM((LANES, LANES), _F32),
    ] + _sequence_scratch(CHUNK)
    return pl.pallas_call(
        _front_body,
        grid=(n_prep + n_sample,),
        in_specs=in_specs,
        out_specs=out_specs,
        out_shape=out_shape,
        scratch_shapes=scratch,
        compiler_params=pltpu.CompilerParams(
            dimension_semantics=("arbitrary",), vmem_limit_bytes=VMEM_REQUEST_BYTES),
        name="prep_meta_sample",
    )(w_t, meta_tokens, w_t, w_out, ng, fgain, mhg, cw, big, bfg, xs, state_conv, m_heads, n_state, c_state)


def _prompt_body(x_ref, w3_ref, wg_ref, wo_ref, ng_ref, fgain_ref, mhg_ref, cw_ref, big_ref, bfg_ref,
                 caug0_ref, m0_ref, ut0_ref,
                 y_ref, c_out_ref, n_out_ref, m_out_ref, conv_out_ref,
                 out_scr, m_all_scr, p_scr, g_scr, hn_scr, ycat_scr, ubuf, caug, m_scr, *, tb, nt, n_blocks):
    s = pl.program_id(0)
    i = s % nt
    live = s < n_blocks

    @pl.when(s == 0)
    def _():
        out_scr[...] = jnp.zeros_like(out_scr)
        m_all_scr[...] = jnp.zeros_like(m_all_scr)

    @pl.when(live & (i == 0))
    def _():
        caug[...] = caug0_ref[...]
        m_scr[...] = m0_ref[...]
        ubuf[0:SUBLANES, :] = ut0_ref[...]

    @pl.when(live)
    def _():
        xn = _rms_rows(x_ref[0], ng_ref[...]).astype(_BF)

        def project_tile(t):
            p_scr[:, _tile_cols(t)] = _dot(xn, w3_ref[t])

        def norm_previous(r0, r1):
            def fn():
                y_ref[0, r0:r1, :] = _rms_rows(out_scr[r0:r1, :], fgain_ref[...])
            return fn

        q4 = tb // 4
        side_work = {2 + 2 * j: [norm_previous(j * q4, (j + 1) * q4)] for j in range(4)}
        _sequence_block(xn, project_tile, p_scr, g_scr, hn_scr, ycat_scr, ubuf, caug, m_scr,
                        wg_ref, mhg_ref, cw_ref, big_ref, bfg_ref, tb=tb, n_pad=0, side_work=side_work)
        out_scr[...] = _dot(ycat_scr[...], wo_ref[...]) + x_ref[0]

    @pl.when(s == n_blocks)
    def _():
        y_ref[0] = _rms_rows(out_scr[...], fgain_ref[...])
        m_out_ref[...] = m_all_scr[...].T[0:M_HEADS, 0:n_blocks // nt]

    @pl.when(live & (i == nt - 1))
    def _():
        m_all_scr[pl.ds(s // nt, 1), :] = m_scr[...]
        conv_out_ref[0] = ubuf[SUBLANES - (CONV_K - 1):SUBLANES, :]
        for h in range(M_HEADS):
            c_out_ref[0, h] = caug[h, :, 0:HEAD_DIM]
            n_out_ref[0, h:h + 1, :] = caug[h, :, LANES:2 * LANES].T[0:1, :]


def _prompt_call(x, w3, wg, wo, small, fgain, caug0, m0, ut0):
    ng, mhg, cw, big, bfg = small
    nb, t, _ = x.shape
    tb = PROMPT_ROWS
    nt = t // tb
    n_blocks = nb * nt

    def const(shape):
        nd = len(shape)
        return pl.BlockSpec(shape, lambda s: (0,) * nd)

    def block_of(s):
        return jnp.minimum(s, n_blocks - 1)

    def rows_spec(step_to_block):
        return pl.BlockSpec((1, tb, D_MODEL), lambda s: (step_to_block(s) // nt, step_to_block(s) % nt, 0))

    def state_spec(*tail):
        nd = len(tail)
        return pl.BlockSpec((1,) + tail, lambda s: (block_of(s) // nt,) + (0,) * nd)

    in_specs = [
        rows_spec(block_of),
        const(w3.shape), const(wg.shape), const(wo.shape), const(ng.shape), const(fgain.shape),
        const(mhg.shape), const(cw.shape), _SMEM_SPEC, _SMEM_SPEC,
        const(caug0.shape), const(m0.shape), const(ut0.shape),
    ]
    out_shape = (
        jax.ShapeDtypeStruct((nb, t, D_MODEL), _F32),
        jax.ShapeDtypeStruct((nb, M_HEADS, HEAD_DIM, HEAD_DIM), _F32),
        jax.ShapeDtypeStruct((nb, M_HEADS, HEAD_DIM), _F32),
        jax.ShapeDtypeStruct((M_HEADS, nb), _F32),
        jax.ShapeDtypeStruct((nb, CONV_K - 1, CONV_WIDTH), _F32),
    )
    out_specs = (
        rows_spec(lambda s: jnp.maximum(s - 1, 0)),
        state_spec(M_HEADS, HEAD_DIM, HEAD_DIM),
        state_spec(M_HEADS, HEAD_DIM),
        const((M_HEADS, nb)),
        state_spec(CONV_K - 1, CONV_WIDTH),
    )
    assert nb <= LANES
    scratch = [
        pltpu.VMEM((tb, D_MODEL), _F32),
        pltpu.VMEM((LANES, LANES), _F32),
    ] + _sequence_scratch(tb)
    return pl.pallas_call(
        functools.partial(_prompt_body, tb=tb, nt=nt, n_blocks=n_blocks),
        grid=(n_blocks + 1,),
        in_specs=in_specs,
        out_specs=out_specs,
        out_shape=out_shape,
        scratch_shapes=scratch,
        compiler_params=pltpu.CompilerParams(
            dimension_semantics=("arbitrary",), vmem_limit_bytes=VMEM_REQUEST_BYTES),
        name="prompt_layer",
    )(x, w3, wg, wo, ng, fgain, mhg, cw, big, bfg, caug0, m0, ut0)


def _sample_block(x_ref, w3_ref, wg_ref, wo_ref, ng_ref, fgain_ref, mhg_ref, cw_ref, big_ref, bfg_ref,
                  sconv_ref, m_ref, n_ref, c_ref,
                  y_ref, conv_out_ref, c_out_ref, n_out_ref, m_out_ref,
                  p_scr, ni_scr, sc_scr, ycat_scr):
    x = x_ref[...]
    xn = _rms_rows(x, ng_ref[...]).astype(_BF)
    g_raw = _dot(xn, wg_ref[...])
    for t in QKV_TILES:
        p_scr[:, _tile_cols(t)] = _dot(xn, w3_ref[t])

    row = lax.broadcasted_iota(jnp.int32, (CHUNK, CHUNK), 0)
    col = lax.broadcasted_iota(jnp.int32, (CHUNK, CHUNK), 1)
    pos = row % DEC_SEQ
    grp = row // DEC_SEQ
    causal = (grp == (col // DEC_SEQ)) & (col <= row)
    grow = lax.broadcasted_iota(jnp.int32, (SEQ_PER_STEP, CHUNK), 0)
    gcol = lax.broadcasted_iota(jnp.int32, (SEQ_PER_STEP, CHUNK), 1)
    pick_last = (gcol == grow * DEC_SEQ + (DEC_SEQ - 1)).astype(_F32)
    seq_sum = (gcol // DEC_SEQ == grow).astype(_BF)
    erow = lax.broadcasted_iota(jnp.int32, (CHUNK, SEQ_PER_STEP), 0)
    ecol = lax.broadcasted_iota(jnp.int32, (CHUNK, SEQ_PER_STEP), 1)
    own_seq = erow // DEC_SEQ == ecol
    ones_blk = jnp.ones((CHUNK, LANES), _BF)

    def rows_from_seqs(v16):
        return jnp.broadcast_to(v16[:, None, :], (SEQ_PER_STEP, DEC_SEQ, LANES)).reshape(CHUNK, LANES)

    def last_row_of_seq(v):
        v3 = v.reshape(SEQ_PER_STEP, DEC_SEQ, LANES)
        return jnp.broadcast_to(v3[:, DEC_SEQ - 1:DEC_SEQ, :], v3.shape).reshape(CHUNK, LANES)

    ig, lf = _gate_columns(g_raw, big_ref, bfg_ref)
    b = _cumsum_rows(lf, pos, DEC_SEQ)
    a = ig - b
    m_prev = rows_from_seqs(m_ref[...])
    big_m = jnp.maximum(m_prev, _cummax_rows(a, pos, DEC_SEQ))
    m_t = b + big_m
    inter = jnp.exp(m_prev - big_m)
    emt = jnp.exp(-m_t)
    m_last = last_row_of_seq(big_m)
    w = jnp.exp(a - m_last)
    decay = jnp.exp(m_prev - m_last)
    m_out_ref[...] = _dot_exact(pick_last, m_t)
    dec16 = _dot_exact(pick_last, decay)
    a_t = a.T

    for t in REST_TILES:
        p_scr[:, _tile_cols(t)] = _dot(xn, w3_ref[t])

    heads = []
    for h in range(M_HEADS):
        hs = h * HEAD_DIM
        q = p_scr[:, OFF_Q + hs:OFF_Q + hs + HEAD_DIM]
        k = p_scr[:, OFF_K + hs:OFF_K + hs + HEAD_DIM] * K_SCALE
        v = p_scr[:, OFF_V + hs:OFF_V + hs + HEAD_DIM]
        q_bf = q.astype(_BF)
        a_b = jnp.broadcast_to(a_t[h:h + 1, :], (CHUNK, CHUNK))
        m_b = jnp.broadcast_to(big_m[:, h:h + 1], (CHUNK, CHUNK))
        d = jnp.exp(jnp.where(causal, a_b - m_b, NEG_INF))
        w_b = jnp.broadcast_to(w[:, h:h + 1], (CHUNK, LANES))
        heads.append(dict(q_bf=q_bf, k=k, v=v, d=d, w_b=w_b, qk=_dot_nt(q_bf, k.astype(_BF))))

    u = p_scr[:, OFF_GC:OFF_GC + CONV_WIDTH] * p_scr[:, OFF_HC:OFF_HC + CONV_WIDTH]
    for g in range(SEQ_PER_STEP):
        sc_scr[g * DEC_SEQ:g * DEC_SEQ + CONV_K - 1, :] = sconv_ref[g]
        conv_out_ref[g] = u[(g + 1) * DEC_SEQ - (CONV_K - 1):(g + 1) * DEC_SEQ, :]
    sc = sc_scr[...]
    posw = lax.broadcasted_iota(jnp.int32, (CHUNK, CONV_WIDTH), 0) % DEC_SEQ
    u_m1 = jnp.where(posw == 0, pltpu.roll(sc, CHUNK - 1, 0), pltpu.roll(u, 1, 0))
    u_m2 = jnp.where(posw < 2, sc, pltpu.roll(u, 2, 0))
    y_conv = _conv_out(p_scr[:, OFF_GB:OFF_GB + CONV_WIDTH], p_scr[:, OFF_ZC:OFF_ZC + CONV_WIDTH],
                       u, u_m1, u_m2, cw_ref)
    ycat_scr[:, 0:CONV_WIDTH] = y_conv.astype(_BF)

    for h in range(M_HEADS):
        hs = h * HEAD_DIM
        st = heads[h]
        wv = st["w_b"] * st["v"]
        k_t_bf = st["k"].T.astype(_BF)
        dec_b = jnp.broadcast_to(decay[:, h:h + 1], (CHUNK, HEAD_DIM))

        for g in range(0, SEQ_PER_STEP, 2):
            pair_rows = slice(g * DEC_SEQ, (g + 2) * DEC_SEQ)
            c0, c1 = c_ref[g, h], c_ref[g + 1, h]
            qc = _dot(st["q_bf"][pair_rows, :], jnp.concatenate([c0.astype(_BF), c1.astype(_BF)], axis=1))
            ni_scr[g * DEC_SEQ:(g + 1) * DEC_SEQ, hs:hs + HEAD_DIM] = qc[0:DEC_SEQ, 0:HEAD_DIM]
            ni_scr[(g + 1) * DEC_SEQ:(g + 2) * DEC_SEQ, hs:hs + HEAD_DIM] = qc[DEC_SEQ:, HEAD_DIM:]
            wv_pair = jnp.concatenate([jnp.where(grp == g, wv, 0.0).astype(_BF),
                                       jnp.where(grp == g + 1, wv, 0.0).astype(_BF)], axis=1)
            upd = _dot(k_t_bf, wv_pair)
            dec0 = jnp.broadcast_to(dec_b[g * DEC_SEQ:g * DEC_SEQ + 1, :], (HEAD_DIM, HEAD_DIM))
            dec1 = jnp.broadcast_to(dec_b[(g + 1) * DEC_SEQ:(g + 1) * DEC_SEQ + 1, :], (HEAD_DIM, HEAD_DIM))
            c_out_ref[g, h] = dec0 * c0 + upd[:, 0:HEAD_DIM]
            c_out_ref[g + 1, h] = dec1 * c1 + upd[:, HEAD_DIM:]

        s = st["qk"] * st["d"]
        sv = _dot(s.astype(_BF), jnp.concatenate([st["v"].astype(_BF), ones_blk], axis=1))
        n_old = n_ref[:, h, :]
        num_inter = ni_scr[:, hs:hs + HEAD_DIM]
        qn = _dot_nt(st["q_bf"], n_old.astype(_BF))
        den_inter = jnp.sum(jnp.where(own_seq, qn, 0.0), axis=-1, keepdims=True)
        inter_b = jnp.broadcast_to(inter[:, h:h + 1], (CHUNK, LANES))
        num = sv[:, :LANES] + inter_b * num_inter
        den = sv[:, LANES:] + inter_b * den_inter
        emt_b = jnp.broadcast_to(emt[:, h:h + 1], (CHUNK, LANES))
        ni_scr[:, hs:hs + HEAD_DIM] = _head_norm(num, den, emt_b, mhg_ref[:, hs:hs + HEAD_DIM])
        n_upd = _dot(seq_sum, (st["w_b"] * st["k"]).astype(_BF))
        dec16_b = jnp.broadcast_to(dec16[:, h:h + 1], (SEQ_PER_STEP, HEAD_DIM))
        n_out_ref[:, h, :] = dec16_b * n_old + n_upd

    ycat_scr[:, CONV_WIDTH:CONV_WIDTH + M_WIDTH] = _gated_heads(ni_scr[...], p_scr).astype(_BF)
    out = _dot(ycat_scr[...], wo_ref[...]) + x
    y_ref[...] = _rms_rows(out, fgain_ref[...])


def kernel(x_prompt, x_sample, state_conv, state_mlstm_c, state_mlstm_n, state_mlstm_m, meta_tokens, norm_gain,
           w_in, conv_w, b_igate, b_fgate, mh_norm_gain, w_out, final_norm_gain):
    nseq, dec_seq, _ = x_sample.shape
    assert dec_seq == DEC_SEQ and meta_tokens.shape[0] == N_META
    assert w_in.shape == (D_MODEL, MAIN_WIDTH + N_GATES) and x_prompt.shape[1] % PROMPT_ROWS == 0
    assert (nseq * DEC_SEQ) % CHUNK == 0

    w_t = w_in.T
    small = (norm_gain.reshape(1, D_MODEL), mh_norm_gain.reshape(1, M_WIDTH), conv_w,
             b_igate.astype(_F32), b_fgate.astype(_F32))
    fgain = final_norm_gain.reshape(1, D_MODEL)

    xs = x_sample.reshape(nseq * DEC_SEQ, D_MODEL)
    w3, wg, wo, caug_meta, m_meta, ut_meta, ys, conv_s, c_s, n_s, m_s = _front_call(
        w_t, meta_tokens.astype(_F32), w_out.astype(_F32), small, fgain, xs, state_conv, state_mlstm_m.T,
        state_mlstm_n, state_mlstm_c)

    y_prompt, c_p, n_p, m_p, conv_p = _prompt_call(x_prompt, w3, wg, wo, small, fgain, caug_meta, m_meta, ut_meta)
    y_sample = ys.reshape(nseq, DEC_SEQ, D_MODEL)

    return (y_prompt, y_sample, conv_p, c_p, n_p, m_p.T, conv_s, c_s, n_s, m_s.T)
```

```python
import functools

import jax
import jax.numpy as jnp
from jax import lax
from jax.experimental import pallas as pl
from jax.experimental.pallas import tpu as pltpu

D_MODEL = 1024
CONV_WIDTH = 512
M_WIDTH = 512
M_HEADS = 4
HEAD_DIM = 128
N_META = 16
CONV_K = 3
EPS = 1e-6
MAIN_WIDTH = 4 * CONV_WIDTH + 5 * M_WIDTH
N_GATES = 2 * M_HEADS
LANES = 128
SUBLANES = 8
MXU_COLS = 256
N_TILES = MAIN_WIDTH // MXU_COLS
CHUNK = 128
DEC_SEQ = 8
SEQ_PER_STEP = CHUNK // DEC_SEQ
NEG_INF = float("-inf")
K_SCALE = HEAD_DIM ** -0.5

PROMPT_ROWS = 512
PREP_TILES = 3
C_RING_SLOTS = 3
VMEM_REQUEST_BYTES = 56 * 1024 * 1024
FRONT_VMEM_REQUEST_BYTES = 60 * 1024 * 1024
EARLY_TILES = 2
UNIT_LAG = 3

OFF_GB, OFF_GC, OFF_HC, OFF_ZC = 0, 512, 1024, 1536
OFF_Q, OFF_K, OFF_V, OFF_O, OFF_ZM = 2048, 2560, 3072, 3584, 4096
QKV_TILES = tuple(range(OFF_Q // MXU_COLS, OFF_O // MXU_COLS))
REST_TILES = tuple(t for t in range(N_TILES) if t not in QKV_TILES)

_SMEM_SPEC = pl.BlockSpec(memory_space=pltpu.SMEM)
_HI = lax.Precision.HIGHEST
_BF = jnp.bfloat16
_F32 = jnp.float32


def _dot(a, b):
    return jnp.dot(a, b, preferred_element_type=_F32)


def _dot_exact(a, b):
    return jnp.dot(a, b, precision=_HI, preferred_element_type=_F32)


def _dot_nt(a, b):
    return lax.dot_general(a, b, (((1,), (1,)), ((), ())), preferred_element_type=_F32)


def _rms_rows(x, gain_row):
    return x * lax.rsqrt(jnp.mean(x * x, axis=-1, keepdims=True) + EPS) * gain_row


def _sigmoid(x):
    return 1.0 / (1.0 + jnp.exp(-x))


def _silu(x):
    return x * _sigmoid(x)


def _tile_cols(t):
    return slice(t * MXU_COLS, (t + 1) * MXU_COLS)


def _cummax_rows(x, pos, period):
    s = 1
    while s < period:
        shifted = pltpu.roll(x, s, 0)
        x = jnp.maximum(x, jnp.where(pos >= s, shifted, NEG_INF))
        s *= 2
    return x


def _cumsum_rows(x, pos, period):
    s = 1
    while s < period:
        x = x + jnp.where(pos >= s, pltpu.roll(x, s, 0), 0.0)
        s *= 2
    return x


def _bias_row(b_ref):
    lane = lax.broadcasted_iota(jnp.int32, (1, LANES), 1)
    row = jnp.zeros((1, LANES), _F32)
    for h in range(M_HEADS):
        row = jnp.where(lane == h, b_ref[h], row)
    return row


def _gate_columns(g_raw, big_ref, bfg_ref):
    ig = g_raw + _bias_row(big_ref)
    lf = jax.nn.log_sigmoid(pltpu.roll(g_raw, LANES - M_HEADS, 1) + _bias_row(bfg_ref))
    return ig, lf


def _head_norm(num, den, emt_b, gain_row):
    hh = num / jnp.maximum(jnp.abs(den), emt_b)
    return hh * lax.rsqrt(jnp.mean(hh * hh, axis=-1, keepdims=True) + EPS) * gain_row


def _conv_out(gb, zc, u, u_m1, u_m2, cw_ref):
    yc = cw_ref[0:1, :] * u_m2 + cw_ref[1:2, :] * u_m1 + cw_ref[2:3, :] * u
    return gb * yc * _silu(zc)


def _gated_heads(hn, p_scr):
    return hn * _sigmoid(p_scr[:, OFF_O:OFF_O + M_WIDTH]) * _silu(p_scr[:, OFF_ZM:OFF_ZM + M_WIDTH])


def _sequence_block(xn, project_tile, p_scr, g_scr, hn_scr, ycat_scr, ubuf, caug, m_scr,
                    wg_ref, mhg_ref, cw_ref, big_ref, bfg_ref, *, tb, n_pad, side_work=None):
    side_work = side_work or {}
    g_scr[...] = _dot(xn, wg_ref[...])
    late_tiles = []
    if project_tile is not None:
        for t in QKV_TILES + REST_TILES[:EARLY_TILES]:
            project_tile(t)
        late_tiles = list(REST_TILES[EARLY_TILES:])

    row = lax.broadcasted_iota(jnp.int32, (CHUNK, CHUNK), 0)
    col = lax.broadcasted_iota(jnp.int32, (CHUNK, CHUNK), 1)
    causal = col <= row
    ones_blk = jnp.ones((CHUNK, LANES), _BF)

    n_chunks = tb // CHUNK
    gates = []
    m_prev = m_scr[...]
    for c in range(n_chunks):
        rows = slice(c * CHUNK, (c + 1) * CHUNK)
        ig, lf = _gate_columns(g_scr[rows, :], big_ref, bfg_ref)
        if n_pad:
            ig = jnp.where(row >= n_pad, ig, NEG_INF)
            lf = jnp.where(row >= n_pad, lf, 0.0)
        b = _cumsum_rows(lf, row, CHUNK)
        a = ig - b
        big_m = jnp.maximum(m_prev, _cummax_rows(a, row, CHUNK))
        m_t = b + big_m
        m_last = big_m[CHUNK - 1:CHUNK, :]
        gates.append(dict(a_t=a.T, big_m=big_m, inter=jnp.exp(m_prev - big_m), emt=jnp.exp(-m_t),
                          w=jnp.exp(a - m_last), decay=jnp.exp(m_prev - m_last)))
        m_prev = m_t[CHUNK - 1:CHUNK, :]
    m_scr[...] = m_prev

    def stage_a(c, h):
        rows = slice(c * CHUNK, (c + 1) * CHUNK)
        hs = h * HEAD_DIM
        g = gates[c]
        q2 = p_scr[rows, OFF_Q + hs:OFF_Q + hs + 2 * HEAD_DIM]
        k2 = p_scr[rows, OFF_K + hs:OFF_K + hs + 2 * HEAD_DIM] * K_SCALE
        k2_bf = k2.astype(_BF)
        zero = jnp.zeros((CHUNK, HEAD_DIM), _BF)
        k_diag = jnp.concatenate([jnp.concatenate([k2_bf[:, :HEAD_DIM], zero], axis=1),
                                  jnp.concatenate([zero, k2_bf[:, HEAD_DIM:]], axis=1)], axis=0)
        qk2 = _dot_nt(q2.astype(_BF), k_diag)
        out = []
        for j in range(2):
            cols = slice(j * HEAD_DIM, (j + 1) * HEAD_DIM)
            a_b = jnp.broadcast_to(g["a_t"][h + j:h + j + 1, :], (CHUNK, CHUNK))
            m_b = jnp.broadcast_to(g["big_m"][:, h + j:h + j + 1], (CHUNK, CHUNK))
            d = jnp.exp(jnp.where(causal, a_b - m_b, NEG_INF))
            inter_b = jnp.broadcast_to(g["inter"][:, h + j:h + j + 1], (CHUNK, HEAD_DIM))
            out.append(dict(qk=qk2[:, cols], d=d, q_inter=(inter_b * q2[:, cols]).astype(_BF), k=k2[:, cols]))
        return out

    def stage_b(c, h, st):
        rows = slice(c * CHUNK, (c + 1) * CHUNK)
        hs = h * HEAD_DIM
        g = gates[c]
        v = p_scr[rows, OFF_V + hs:OFF_V + hs + HEAD_DIM]
        s = st["qk"] * st["d"]
        lhs = jnp.concatenate([s.astype(_BF), st["q_inter"]], axis=1)
        rhs = jnp.concatenate([jnp.concatenate([v.astype(_BF), ones_blk], axis=1), caug[h].astype(_BF)], axis=0)
        tot = _dot(lhs, rhs)
        emt_b = jnp.broadcast_to(g["emt"][:, h:h + 1], (CHUNK, LANES))
        hn_scr[rows, hs:hs + HEAD_DIM] = _head_norm(tot[:, :LANES], tot[:, LANES:], emt_b,
                                                    mhg_ref[:, hs:hs + HEAD_DIM])
        w_b = jnp.broadcast_to(g["w"][:, h:h + 1], (CHUNK, LANES))
        wv = jnp.concatenate([(w_b * v).astype(_BF), w_b.astype(_BF)], axis=1)
        upd = _dot(st["k"].T.astype(_BF), wv)
        decay_b = jnp.broadcast_to(g["decay"][:, h:h + 1], (HEAD_DIM, 2 * LANES))
        caug[h] = decay_b * caug[h] + upd

    units = [(c, h) for c in range(n_chunks) for h in range(M_HEADS)]
    lag = min(UNIT_LAG, len(units))
    n_slots = len(units) + lag
    assert len(late_tiles) <= n_slots
    assert all(0 <= slot < n_slots for slot in side_work)
    pending = {}
    for t in range(n_slots):
        if t < len(units) and t % 2 == 0:
            pending[t], pending[t + 1] = stage_a(*units[t])
        if t < len(late_tiles):
            project_tile(late_tiles[t])
        for fn in side_work.get(t, ()):
            fn()
        if t >= lag:
            stage_b(*units[t - lag], pending.pop(t - lag))

    u = p_scr[:, OFF_GC:OFF_GC + CONV_WIDTH] * p_scr[:, OFF_HC:OFF_HC + CONV_WIDTH]
    ubuf[SUBLANES:SUBLANES + tb, :] = u
    u_m1 = ubuf[SUBLANES - 1:SUBLANES - 1 + tb, :]
    u_m2 = ubuf[SUBLANES - 2:SUBLANES - 2 + tb, :]
    y_conv = _conv_out(p_scr[:, OFF_GB:OFF_GB + CONV_WIDTH], p_scr[:, OFF_ZC:OFF_ZC + CONV_WIDTH],
                       u, u_m1, u_m2, cw_ref)
    ycat_scr[:, 0:CONV_WIDTH] = y_conv.astype(_BF)
    ycat_scr[:, CONV_WIDTH:CONV_WIDTH + M_WIDTH] = _gated_heads(hn_scr[...], p_scr).astype(_BF)
    ubuf[0:SUBLANES, :] = ubuf[tb:tb + SUBLANES, :]


def _sequence_scratch(tb):
    return [
        pltpu.VMEM((tb, MAIN_WIDTH), _F32),
        pltpu.VMEM((tb, LANES), _F32),
        pltpu.VMEM((tb, M_WIDTH), _F32),
        pltpu.VMEM((tb, D_MODEL), _BF),
        pltpu.VMEM((tb + 2 * SUBLANES, CONV_WIDTH), _F32),
        pltpu.VMEM((M_HEADS, HEAD_DIM, 2 * LANES), _F32),
        pltpu.VMEM((1, LANES), _F32),
    ]


def _front_body(wt_ref, x_ref, wgt_ref, wout_ref, ng_ref, fgain_ref, mhg_ref, cw_ref, big_ref, bfg_ref,
                xs_ref, sconv_ref, ms_ref, ns_ref, cs_ref,
                w3_ref, wg_ref, wo_ref, caug_out_ref, m_out_ref, ut_out_ref,
                ys_ref, sconv_out_ref, cs_out_ref, ns_out_ref, ms_out_ref,
                w3_scr, xn_scr, p3_scr, sc_scr, ms_in_scr, ms_new_scr, c_ring, c_sem,
                p_scr, g_scr, hn_scr, ycat_scr, ubuf, caug, m_scr, *, n_sample):
    t = pl.program_id(0)
    n_prep = N_TILES // PREP_TILES
    last = pl.num_programs(0) - 1

    def c_copy(block, slot):
        rows = pl.ds(pl.multiple_of(block * SEQ_PER_STEP, SEQ_PER_STEP), SEQ_PER_STEP)
        return pltpu.make_async_copy(cs_ref.at[rows], c_ring.at[slot], c_sem.at[slot])

    @pl.when(t == 0)
    def _():
        for b in range(min(C_RING_SLOTS, n_sample)):
            c_copy(b, b).start()
        xn_scr[...] = jnp.zeros_like(xn_scr)
        xn_scr[CHUNK - N_META:CHUNK, :] = _rms_rows(x_ref[...], ng_ref[...]).astype(_BF)
        gate_rows = jnp.concatenate([wgt_ref[...], jnp.zeros((LANES - N_GATES, D_MODEL), _F32)], axis=0)
        wg_ref[...] = gate_rows.T.astype(_BF)
        wo_ref[...] = wout_ref[...].astype(_BF)
        sc_scr[...] = jnp.zeros_like(sc_scr)
        m_heads = jnp.concatenate([ms_ref[...], jnp.zeros((LANES - M_HEADS, LANES), _F32)], axis=0)
        ms_in_scr[...] = m_heads.T

    @pl.when(t < n_prep)
    def _():
        for j in range(PREP_TILES):
            tile = wt_ref[j * MXU_COLS:(j + 1) * MXU_COLS, :].T.astype(_BF)
            w3_ref[j] = tile
            w3_scr[t * PREP_TILES + j] = tile
            p3_scr[t * PREP_TILES + j] = _dot(xn_scr[...], tile)

    @pl.when(t == n_prep - 1)
    def _():
        for j in range(N_TILES):
            p_scr[:, _tile_cols(j)] = p3_scr[j]
        caug[...] = jnp.zeros_like(caug)
        m_scr[...] = jnp.zeros_like(m_scr)
        ubuf[0:SUBLANES, :] = jnp.zeros((SUBLANES, CONV_WIDTH), _F32)
        _sequence_block(xn_scr[...], None, p_scr, g_scr, hn_scr, ycat_scr, ubuf, caug, m_scr,
                        wg_ref, mhg_ref, cw_ref, big_ref, bfg_ref, tb=CHUNK, n_pad=CHUNK - N_META)
        caug_out_ref[...] = caug[...]
        m_out_ref[...] = m_scr[...]
        ut_out_ref[...] = ubuf[0:SUBLANES, :]

    @pl.when(t >= n_prep)
    def _():
        j = t - n_prep
        slot = j % C_RING_SLOTS
        seqs = pl.ds(pl.multiple_of(j * SEQ_PER_STEP, SEQ_PER_STEP), SEQ_PER_STEP)
        c_copy(j, slot).wait()
        _sample_block(xs_ref, w3_scr, wg_ref, wo_ref, ng_ref, fgain_ref, mhg_ref, cw_ref, big_ref, bfg_ref,
                      sconv_ref, ms_in_scr.at[seqs], ns_ref, c_ring.at[slot],
                      ys_ref, sconv_out_ref, cs_out_ref, ns_out_ref, ms_new_scr.at[seqs],
                      p_scr, hn_scr, sc_scr, ycat_scr)

        @pl.when(j + C_RING_SLOTS < n_sample)
        def _():
            c_copy(j + C_RING_SLOTS, slot).start()

    @pl.when(t == last)
    def _():
        ms_out_ref[...] = ms_new_scr[...].T[0:M_HEADS, :]


def _front_call(w_t, meta_tokens, w_out, small, fgain, xs, state_conv, m_heads, n_state, c_state):
    ng, mhg, cw, big, bfg = small
    n_prep = N_TILES // PREP_TILES
    rows = xs.shape[0]
    nseq = c_state.shape[0]
    n_sample = rows // CHUNK
    assert m_heads.shape == (M_HEADS, LANES) and nseq == LANES

    def const(shape):
        nd = len(shape)
        return pl.BlockSpec(shape, lambda t: (0,) * nd)

    def sample_spec(*block):
        nd = len(block) - 1
        return pl.BlockSpec(block, lambda t: (jnp.maximum(t - n_prep, 0),) + (0,) * nd)

    def prep_idx(t):
        return jnp.minimum(t, n_prep - 1)

    xs_spec = sample_spec(CHUNK, D_MODEL)
    sconv_spec = sample_spec(SEQ_PER_STEP, CONV_K - 1, CONV_WIDTH)
    cs_spec = sample_spec(SEQ_PER_STEP, M_HEADS, HEAD_DIM, HEAD_DIM)
    ns_spec = sample_spec(SEQ_PER_STEP, M_HEADS, HEAD_DIM)
    ms_spec = const(m_heads.shape)
    in_specs = [
        pl.BlockSpec((PREP_TILES * MXU_COLS, D_MODEL), lambda t: (prep_idx(t), 0)),
        const(meta_tokens.shape),
        pl.BlockSpec((N_GATES, D_MODEL), lambda t: (MAIN_WIDTH // N_GATES, 0)),
        const(w_out.shape), const(ng.shape), const(fgain.shape), const(mhg.shape), const(cw.shape),
        _SMEM_SPEC, _SMEM_SPEC,
        xs_spec, sconv_spec, ms_spec, ns_spec,
        pl.BlockSpec(memory_space=pl.ANY),
    ]
    out_shape = (
        jax.ShapeDtypeStruct((N_TILES, D_MODEL, MXU_COLS), _BF),
        jax.ShapeDtypeStruct((D_MODEL, LANES), _BF),
        jax.ShapeDtypeStruct(w_out.shape, _BF),
        jax.ShapeDtypeStruct((M_HEADS, HEAD_DIM, 2 * LANES), _F32),
        jax.ShapeDtypeStruct((1, LANES), _F32),
        jax.ShapeDtypeStruct((SUBLANES, CONV_WIDTH), _F32),
        jax.ShapeDtypeStruct((rows, D_MODEL), _F32),
        jax.ShapeDtypeStruct(state_conv.shape, _F32),
        jax.ShapeDtypeStruct(c_state.shape, _F32),
        jax.ShapeDtypeStruct(n_state.shape, _F32),
        jax.ShapeDtypeStruct(m_heads.shape, _F32),
    )
    out_specs = (
        pl.BlockSpec((PREP_TILES, D_MODEL, MXU_COLS), lambda t: (prep_idx(t), 0, 0)),
        const(out_shape[1].shape), const(out_shape[2].shape), const(out_shape[3].shape), const(out_shape[4].shape),
        const(out_shape[5].shape),
        xs_spec, sconv_spec, cs_spec, ns_spec, ms_spec,
    )
    scratch = [
        pltpu.VMEM((N_TILES, D_MODEL, MXU_COLS), _BF),
        pltpu.VMEM((CHUNK, D_MODEL), _BF),
        pltpu.VMEM((N_TILES, CHUNK, MXU_COLS), _F32),
        pltpu.VMEM((CHUNK, CONV_WIDTH), _F32),
        pltpu.VMEM((LANES, LANES), _F32),
        pltpu.VMEM((LANES, LANES), _F32),
        pltpu.VMEM((C_RING_SLOTS, SEQ_PER_STEP, M_HEADS, HEAD_DIM, HEAD_DIM), _F32),
        pltpu.SemaphoreType.DMA((C_RING_SLOTS,)),
    ] + _sequence_scratch(CHUNK)
    return pl.pallas_call(
        functools.partial(_front_body, n_sample=n_sample),
        grid=(n_prep + n_sample,),
        in_specs=in_specs,
        out_specs=out_specs,
        out_shape=out_shape,
        scratch_shapes=scratch,
        compiler_params=pltpu.CompilerParams(
            dimension_semantics=("arbitrary",), vmem_limit_bytes=FRONT_VMEM_REQUEST_BYTES),
        name="prep_meta_sample",
    )(w_t, meta_tokens, w_t, w_out, ng, fgain, mhg, cw, big, bfg, xs, state_conv, m_heads, n_state, c_state)


def _prompt_body(x_ref, w3_ref, wg_ref, wo_ref, ng_ref, fgain_ref, mhg_ref, cw_ref, big_ref, bfg_ref,
                 caug0_ref, m0_ref, ut0_ref,
                 y_ref, c_out_ref, n_out_ref, m_out_ref, conv_out_ref,
                 out_scr, m_all_scr, p_scr, g_scr, hn_scr, ycat_scr, ubuf, caug, m_scr, *, tb, nt, n_blocks):
    s = pl.program_id(0)
    i = s % nt
    live = s < n_blocks

    @pl.when(s == 0)
    def _():
        out_scr[...] = jnp.zeros_like(out_scr)
        m_all_scr[...] = jnp.zeros_like(m_all_scr)

    @pl.when(live & (i == 0))
    def _():
        caug[...] = caug0_ref[...]
        m_scr[...] = m0_ref[...]
        ubuf[0:SUBLANES, :] = ut0_ref[...]

    @pl.when(live)
    def _():
        xn = _rms_rows(x_ref[0], ng_ref[...]).astype(_BF)

        def project_tile(t):
            p_scr[:, _tile_cols(t)] = _dot(xn, w3_ref[t])

        def norm_previous(r0, r1):
            def fn():
                y_ref[0, r0:r1, :] = _rms_rows(out_scr[r0:r1, :], fgain_ref[...])
            return fn

        q4 = tb // 4
        side_work = {2 + 2 * j: [norm_previous(j * q4, (j + 1) * q4)] for j in range(4)}
        _sequence_block(xn, project_tile, p_scr, g_scr, hn_scr, ycat_scr, ubuf, caug, m_scr,
                        wg_ref, mhg_ref, cw_ref, big_ref, bfg_ref, tb=tb, n_pad=0, side_work=side_work)
        out_scr[...] = _dot(ycat_scr[...], wo_ref[...]) + x_ref[0]

    @pl.when(s == n_blocks)
    def _():
        y_ref[0] = _rms_rows(out_scr[...], fgain_ref[...])
        m_out_ref[...] = m_all_scr[...].T[0:M_HEADS, 0:n_blocks // nt]

    @pl.when(live & (i == nt - 1))
    def _():
        m_all_scr[pl.ds(s // nt, 1), :] = m_scr[...]
        conv_out_ref[0] = ubuf[SUBLANES - (CONV_K - 1):SUBLANES, :]
        for h in range(M_HEADS):
            c_out_ref[0, h] = caug[h, :, 0:HEAD_DIM]
            n_out_ref[0, h:h + 1, :] = caug[h, :, LANES:2 * LANES].T[0:1, :]


def _prompt_call(x, w3, wg, wo, small, fgain, caug0, m0, ut0):
    ng, mhg, cw, big, bfg = small
    nb, t, _ = x.shape
    tb = PROMPT_ROWS
    nt = t // tb
    n_blocks = nb * nt

    def const(shape):
        nd = len(shape)
        return pl.BlockSpec(shape, lambda s: (0,) * nd)

    def block_of(s):
        return jnp.minimum(s, n_blocks - 1)

    def rows_spec(step_to_block):
        return pl.BlockSpec((1, tb, D_MODEL), lambda s: (step_to_block(s) // nt, step_to_block(s) % nt, 0))

    def state_spec(*tail):
        nd = len(tail)
        return pl.BlockSpec((1,) + tail, lambda s: (block_of(s) // nt,) + (0,) * nd)

    in_specs = [
        rows_spec(block_of),
        const(w3.shape), const(wg.shape), const(wo.shape), const(ng.shape), const(fgain.shape),
        const(mhg.shape), const(cw.shape), _SMEM_SPEC, _SMEM_SPEC,
        const(caug0.shape), const(m0.shape), const(ut0.shape),
    ]
    out_shape = (
        jax.ShapeDtypeStruct((nb, t, D_MODEL), _F32),
        jax.ShapeDtypeStruct((nb, M_HEADS, HEAD_DIM, HEAD_DIM), _F32),
        jax.ShapeDtypeStruct((nb, M_HEADS, HEAD_DIM), _F32),
        jax.ShapeDtypeStruct((M_HEADS, nb), _F32),
        jax.ShapeDtypeStruct((nb, CONV_K - 1, CONV_WIDTH), _F32),
    )
    out_specs = (
        rows_spec(lambda s: jnp.maximum(s - 1, 0)),
        state_spec(M_HEADS, HEAD_DIM, HEAD_DIM),
        state_spec(M_HEADS, HEAD_DIM),
        const((M_HEADS, nb)),
        state_spec(CONV_K - 1, CONV_WIDTH),
    )
    assert nb <= LANES
    scratch = [
        pltpu.VMEM((tb, D_MODEL), _F32),
        pltpu.VMEM((LANES, LANES), _F32),
    ] + _sequence_scratch(tb)
    return pl.pallas_call(
        functools.partial(_prompt_body, tb=tb, nt=nt, n_blocks=n_blocks),
        grid=(n_blocks + 1,),
        in_specs=in_specs,
        out_specs=out_specs,
        out_shape=out_shape,
        scratch_shapes=scratch,
        compiler_params=pltpu.CompilerParams(
            dimension_semantics=("arbitrary",), vmem_limit_bytes=VMEM_REQUEST_BYTES),
        name="prompt_layer",
    )(x, w3, wg, wo, ng, fgain, mhg, cw, big, bfg, caug0, m0, ut0)


def _sample_block(x_ref, w3_ref, wg_ref, wo_ref, ng_ref, fgain_ref, mhg_ref, cw_ref, big_ref, bfg_ref,
                  sconv_ref, m_ref, n_ref, c_ref,
                  y_ref, conv_out_ref, c_out_ref, n_out_ref, m_out_ref,
                  p_scr, ni_scr, sc_scr, ycat_scr):
    x = x_ref[...]
    xn = _rms_rows(x, ng_ref[...]).astype(_BF)
    g_raw = _dot(xn, wg_ref[...])
    for t in QKV_TILES:
        p_scr[:, _tile_cols(t)] = _dot(xn, w3_ref[t])

    row = lax.broadcasted_iota(jnp.int32, (CHUNK, CHUNK), 0)
    col = lax.broadcasted_iota(jnp.int32, (CHUNK, CHUNK), 1)
    pos = row % DEC_SEQ
    grp = row // DEC_SEQ
    causal = (grp == (col // DEC_SEQ)) & (col <= row)
    grow = lax.broadcasted_iota(jnp.int32, (SEQ_PER_STEP, CHUNK), 0)
    gcol = lax.broadcasted_iota(jnp.int32, (SEQ_PER_STEP, CHUNK), 1)
    pick_last = (gcol == grow * DEC_SEQ + (DEC_SEQ - 1)).astype(_F32)
    seq_sum = (gcol // DEC_SEQ == grow).astype(_BF)
    erow = lax.broadcasted_iota(jnp.int32, (CHUNK, SEQ_PER_STEP), 0)
    ecol = lax.broadcasted_iota(jnp.int32, (CHUNK, SEQ_PER_STEP), 1)
    own_seq = erow // DEC_SEQ == ecol
    ones_blk = jnp.ones((CHUNK, LANES), _BF)

    def rows_from_seqs(v16):
        return jnp.broadcast_to(v16[:, None, :], (SEQ_PER_STEP, DEC_SEQ, LANES)).reshape(CHUNK, LANES)

    def last_row_of_seq(v):
        v3 = v.reshape(SEQ_PER_STEP, DEC_SEQ, LANES)
        return jnp.broadcast_to(v3[:, DEC_SEQ - 1:DEC_SEQ, :], v3.shape).reshape(CHUNK, LANES)

    ig, lf = _gate_columns(g_raw, big_ref, bfg_ref)
    b = _cumsum_rows(lf, pos, DEC_SEQ)
    a = ig - b
    m_prev = rows_from_seqs(m_ref[...])
    big_m = jnp.maximum(m_prev, _cummax_rows(a, pos, DEC_SEQ))
    m_t = b + big_m
    inter = jnp.exp(m_prev - big_m)
    emt = jnp.exp(-m_t)
    m_last = last_row_of_seq(big_m)
    w = jnp.exp(a - m_last)
    decay = jnp.exp(m_prev - m_last)
    m_out_ref[...] = _dot_exact(pick_last, m_t)
    dec16 = _dot_exact(pick_last, decay)
    a_t = a.T

    for t in REST_TILES:
        p_scr[:, _tile_cols(t)] = _dot(xn, w3_ref[t])

    heads = []
    for h in range(M_HEADS):
        hs = h * HEAD_DIM
        q = p_scr[:, OFF_Q + hs:OFF_Q + hs + HEAD_DIM]
        k = p_scr[:, OFF_K + hs:OFF_K + hs + HEAD_DIM] * K_SCALE
        v = p_scr[:, OFF_V + hs:OFF_V + hs + HEAD_DIM]
        q_bf = q.astype(_BF)
        a_b = jnp.broadcast_to(a_t[h:h + 1, :], (CHUNK, CHUNK))
        m_b = jnp.broadcast_to(big_m[:, h:h + 1], (CHUNK, CHUNK))
        d = jnp.exp(jnp.where(causal, a_b - m_b, NEG_INF))
        w_b = jnp.broadcast_to(w[:, h:h + 1], (CHUNK, LANES))
        heads.append(dict(q_bf=q_bf, k=k, v=v, d=d, w_b=w_b, qk=_dot_nt(q_bf, k.astype(_BF))))

    u = p_scr[:, OFF_GC:OFF_GC + CONV_WIDTH] * p_scr[:, OFF_HC:OFF_HC + CONV_WIDTH]
    for g in range(SEQ_PER_STEP):
        sc_scr[g * DEC_SEQ:g * DEC_SEQ + CONV_K - 1, :] = sconv_ref[g]
        conv_out_ref[g] = u[(g + 1) * DEC_SEQ - (CONV_K - 1):(g + 1) * DEC_SEQ, :]
    sc = sc_scr[...]
    posw = lax.broadcasted_iota(jnp.int32, (CHUNK, CONV_WIDTH), 0) % DEC_SEQ
    u_m1 = jnp.where(posw == 0, pltpu.roll(sc, CHUNK - 1, 0), pltpu.roll(u, 1, 0))
    u_m2 = jnp.where(posw < 2, sc, pltpu.roll(u, 2, 0))
    y_conv = _conv_out(p_scr[:, OFF_GB:OFF_GB + CONV_WIDTH], p_scr[:, OFF_ZC:OFF_ZC + CONV_WIDTH],
                       u, u_m1, u_m2, cw_ref)
    ycat_scr[:, 0:CONV_WIDTH] = y_conv.astype(_BF)

    for h in range(M_HEADS):
        hs = h * HEAD_DIM
        st = heads[h]
        wv = st["w_b"] * st["v"]
        k_t_bf = st["k"].T.astype(_BF)
        dec_b = jnp.broadcast_to(decay[:, h:h + 1], (CHUNK, HEAD_DIM))

        for g in range(0, SEQ_PER_STEP, 2):
            pair_rows = slice(g * DEC_SEQ, (g + 2) * DEC_SEQ)
            c0, c1 = c_ref[g, h], c_ref[g + 1, h]
            qc = _dot(st["q_bf"][pair_rows, :], jnp.concatenate([c0.astype(_BF), c1.astype(_BF)], axis=1))
            ni_scr[g * DEC_SEQ:(g + 1) * DEC_SEQ, hs:hs + HEAD_DIM] = qc[0:DEC_SEQ, 0:HEAD_DIM]
            ni_scr[(g + 1) * DEC_SEQ:(g + 2) * DEC_SEQ, hs:hs + HEAD_DIM] = qc[DEC_SEQ:, HEAD_DIM:]
            wv_pair = jnp.concatenate([jnp.where(grp == g, wv, 0.0).astype(_BF),
                                       jnp.where(grp == g + 1, wv, 0.0).astype(_BF)], axis=1)
            upd = _dot(k_t_bf, wv_pair)
            dec0 = jnp.broadcast_to(dec_b[g * DEC_SEQ:g * DEC_SEQ + 1, :], (HEAD_DIM, HEAD_DIM))
            dec1 = jnp.broadcast_to(dec_b[(g + 1) * DEC_SEQ:(g + 1) * DEC_SEQ + 1, :], (HEAD_DIM, HEAD_DIM))
            c_out_ref[g, h] = dec0 * c0 + upd[:, 0:HEAD_DIM]
            c_out_ref[g + 1, h] = dec1 * c1 + upd[:, HEAD_DIM:]

        s = st["qk"] * st["d"]
        sv = _dot(s.astype(_BF), jnp.concatenate([st["v"].astype(_BF), ones_blk], axis=1))
        n_old = n_ref[:, h, :]
        num_inter = ni_scr[:, hs:hs + HEAD_DIM]
        qn = _dot_nt(st["q_bf"], n_old.astype(_BF))
        den_inter = jnp.sum(jnp.where(own_seq, qn, 0.0), axis=-1, keepdims=True)
        inter_b = jnp.broadcast_to(inter[:, h:h + 1], (CHUNK, LANES))
        num = sv[:, :LANES] + inter_b * num_inter
        den = sv[:, LANES:] + inter_b * den_inter
        emt_b = jnp.broadcast_to(emt[:, h:h + 1], (CHUNK, LANES))
        ni_scr[:, hs:hs + HEAD_DIM] = _head_norm(num, den, emt_b, mhg_ref[:, hs:hs + HEAD_DIM])
        n_upd = _dot(seq_sum, (st["w_b"] * st["k"]).astype(_BF))
        dec16_b = jnp.broadcast_to(dec16[:, h:h + 1], (SEQ_PER_STEP, HEAD_DIM))
        n_out_ref[:, h, :] = dec16_b * n_old + n_upd

    ycat_scr[:, CONV_WIDTH:CONV_WIDTH + M_WIDTH] = _gated_heads(ni_scr[...], p_scr).astype(_BF)
    out = _dot(ycat_scr[...], wo_ref[...]) + x
    y_ref[...] = _rms_rows(out, fgain_ref[...])


def kernel(x_prompt, x_sample, state_conv, state_mlstm_c, state_mlstm_n, state_mlstm_m, meta_tokens, norm_gain,
           w_in, conv_w, b_igate, b_fgate, mh_norm_gain, w_out, final_norm_gain):
    nseq, dec_seq, _ = x_sample.shape
    assert dec_seq == DEC_SEQ and meta_tokens.shape[0] == N_META
    assert w_in.shape == (D_MODEL, MAIN_WIDTH + N_GATES) and x_prompt.shape[1] % PROMPT_ROWS == 0
    assert (nseq * DEC_SEQ) % CHUNK == 0

    w_t = w_in.T
    small = (norm_gain.reshape(1, D_MODEL), mh_norm_gain.reshape(1, M_WIDTH), conv_w,
             b_igate.astype(_F32), b_fgate.astype(_F32))
    fgain = final_norm_gain.reshape(1, D_MODEL)

    xs = x_sample.reshape(nseq * DEC_SEQ, D_MODEL)
    w3, wg, wo, caug_meta, m_meta, ut_meta, ys, conv_s, c_s, n_s, m_s = _front_call(
        w_t, meta_tokens.astype(_F32), w_out.astype(_F32), small, fgain, xs, state_conv, state_mlstm_m.T,
        state_mlstm_n, state_mlstm_c)

    y_prompt, c_p, n_p, m_p, conv_p = _prompt_call(x_prompt, w3, wg, wo, small, fgain, caug_meta, m_meta, ut_meta)
    y_sample = ys.reshape(nseq, DEC_SEQ, D_MODEL)

    return (y_prompt, y_sample, conv_p, c_p, n_p, m_p.T, conv_s, c_s, n_s, m_s.T)
```

```python
import functools

import jax
import jax.numpy as jnp
from jax import lax
from jax.experimental import pallas as pl
from jax.experimental.pallas import tpu as pltpu

D_MODEL = 1024
CONV_WIDTH = 512
M_WIDTH = 512
M_HEADS = 4
HEAD_DIM = 128
N_META = 16
CONV_K = 3
EPS = 1e-6
MAIN_WIDTH = 4 * CONV_WIDTH + 5 * M_WIDTH
N_GATES = 2 * M_HEADS
LANES = 128
SUBLANES = 8
MXU_COLS = 256
N_TILES = MAIN_WIDTH // MXU_COLS
CHUNK = 128
DEC_SEQ = 8
SEQ_PER_STEP = CHUNK // DEC_SEQ
NEG_INF = float("-inf")
K_SCALE = HEAD_DIM ** -0.5

PROMPT_ROWS = 512
PREP_TILES = 3
VMEM_REQUEST_BYTES = 56 * 1024 * 1024
EARLY_TILES = 2
UNIT_LAG = 3

OFF_GB, OFF_GC, OFF_HC, OFF_ZC = 0, 512, 1024, 1536
OFF_Q, OFF_K, OFF_V, OFF_O, OFF_ZM = 2048, 2560, 3072, 3584, 4096
QKV_TILES = tuple(range(OFF_Q // MXU_COLS, OFF_O // MXU_COLS))
REST_TILES = tuple(t for t in range(N_TILES) if t not in QKV_TILES)

_SMEM_SPEC = pl.BlockSpec(memory_space=pltpu.SMEM)
_HI = lax.Precision.HIGHEST
_BF = jnp.bfloat16
_F32 = jnp.float32


def _dot(a, b):
    return jnp.dot(a, b, preferred_element_type=_F32)


def _dot_exact(a, b):
    return jnp.dot(a, b, precision=_HI, preferred_element_type=_F32)


def _dot_nt(a, b):
    return lax.dot_general(a, b, (((1,), (1,)), ((), ())), preferred_element_type=_F32)


def _rms_rows(x, gain_row):
    return x * lax.rsqrt(jnp.mean(x * x, axis=-1, keepdims=True) + EPS) * gain_row


def _sigmoid(x):
    return 1.0 / (1.0 + jnp.exp(-x))


def _silu(x):
    return x * _sigmoid(x)


def _tile_cols(t):
    return slice(t * MXU_COLS, (t + 1) * MXU_COLS)


def _cummax_rows(x, pos, period):
    s = 1
    while s < period:
        shifted = pltpu.roll(x, s, 0)
        x = jnp.maximum(x, jnp.where(pos >= s, shifted, NEG_INF))
        s *= 2
    return x


def _cumsum_rows(x, pos, period):
    s = 1
    while s < period:
        x = x + jnp.where(pos >= s, pltpu.roll(x, s, 0), 0.0)
        s *= 2
    return x


def _bias_row(b_ref):
    lane = lax.broadcasted_iota(jnp.int32, (1, LANES), 1)
    row = jnp.zeros((1, LANES), _F32)
    for h in range(M_HEADS):
        row = jnp.where(lane == h, b_ref[h], row)
    return row


def _gate_columns(g_raw, big_ref, bfg_ref):
    ig = g_raw + _bias_row(big_ref)
    lf = jax.nn.log_sigmoid(pltpu.roll(g_raw, LANES - M_HEADS, 1) + _bias_row(bfg_ref))
    return ig, lf


def _head_norm(num, den, emt_b, gain_row):
    hh = num / jnp.maximum(jnp.abs(den), emt_b)
    return hh * lax.rsqrt(jnp.mean(hh * hh, axis=-1, keepdims=True) + EPS) * gain_row


def _conv_out(gb, zc, u, u_m1, u_m2, cw_ref):
    yc = cw_ref[0:1, :] * u_m2 + cw_ref[1:2, :] * u_m1 + cw_ref[2:3, :] * u
    return gb * yc * _silu(zc)


def _gated_heads(hn, p_scr):
    return hn * _sigmoid(p_scr[:, OFF_O:OFF_O + M_WIDTH]) * _silu(p_scr[:, OFF_ZM:OFF_ZM + M_WIDTH])


def _sequence_block(xn, project_tile, p_scr, g_scr, hn_scr, ycat_scr, ubuf, caug, m_scr,
                    wg_ref, mhg_ref, cw_ref, big_ref, bfg_ref, *, tb, n_pad, side_work=None):
    side_work = side_work or {}
    g_scr[...] = _dot(xn, wg_ref[...])
    late_tiles = []
    if project_tile is not None:
        for t in QKV_TILES + REST_TILES[:EARLY_TILES]:
            project_tile(t)
        late_tiles = list(REST_TILES[EARLY_TILES:])

    row = lax.broadcasted_iota(jnp.int32, (CHUNK, CHUNK), 0)
    col = lax.broadcasted_iota(jnp.int32, (CHUNK, CHUNK), 1)
    causal = col <= row
    ones_blk = jnp.ones((CHUNK, LANES), _BF)

    n_chunks = tb // CHUNK
    gates = []
    m_prev = m_scr[...]
    for c in range(n_chunks):
        rows = slice(c * CHUNK, (c + 1) * CHUNK)
        ig, lf = _gate_columns(g_scr[rows, :], big_ref, bfg_ref)
        if n_pad:
            ig = jnp.where(row >= n_pad, ig, NEG_INF)
            lf = jnp.where(row >= n_pad, lf, 0.0)
        b = _cumsum_rows(lf, row, CHUNK)
        a = ig - b
        big_m = jnp.maximum(m_prev, _cummax_rows(a, row, CHUNK))
        m_t = b + big_m
        m_last = big_m[CHUNK - 1:CHUNK, :]
        gates.append(dict(a_t=a.T, big_m=big_m, inter=jnp.exp(m_prev - big_m), emt=jnp.exp(-m_t),
                          w=jnp.exp(a - m_last), decay=jnp.exp(m_prev - m_last)))
        m_prev = m_t[CHUNK - 1:CHUNK, :]
    m_scr[...] = m_prev

    def stage_a(c, h):
        rows = slice(c * CHUNK, (c + 1) * CHUNK)
        hs = h * HEAD_DIM
        g = gates[c]
        q2 = p_scr[rows, OFF_Q + hs:OFF_Q + hs + 2 * HEAD_DIM]
        k2 = p_scr[rows, OFF_K + hs:OFF_K + hs + 2 * HEAD_DIM] * K_SCALE
        k2_bf = k2.astype(_BF)
        zero = jnp.zeros((CHUNK, HEAD_DIM), _BF)
        k_diag = jnp.concatenate([jnp.concatenate([k2_bf[:, :HEAD_DIM], zero], axis=1),
                                  jnp.concatenate([zero, k2_bf[:, HEAD_DIM:]], axis=1)], axis=0)
        qk2 = _dot_nt(q2.astype(_BF), k_diag)
        out = []
        for j in range(2):
            cols = slice(j * HEAD_DIM, (j + 1) * HEAD_DIM)
            a_b = jnp.broadcast_to(g["a_t"][h + j:h + j + 1, :], (CHUNK, CHUNK))
            m_b = jnp.broadcast_to(g["big_m"][:, h + j:h + j + 1], (CHUNK, CHUNK))
            d = jnp.exp(jnp.where(causal, a_b - m_b, NEG_INF))
            inter_b = jnp.broadcast_to(g["inter"][:, h + j:h + j + 1], (CHUNK, HEAD_DIM))
            out.append(dict(qk=qk2[:, cols], d=d, q_inter=(inter_b * q2[:, cols]).astype(_BF), k=k2[:, cols]))
        return out

    def stage_b(c, h, st):
        rows = slice(c * CHUNK, (c + 1) * CHUNK)
        hs = h * HEAD_DIM
        g = gates[c]
        v = p_scr[rows, OFF_V + hs:OFF_V + hs + HEAD_DIM]
        s = st["qk"] * st["d"]
        lhs = jnp.concatenate([s.astype(_BF), st["q_inter"]], axis=1)
        rhs = jnp.concatenate([jnp.concatenate([v.astype(_BF), ones_blk], axis=1), caug[h].astype(_BF)], axis=0)
        tot = _dot(lhs, rhs)
        emt_b = jnp.broadcast_to(g["emt"][:, h:h + 1], (CHUNK, LANES))
        hn_scr[rows, hs:hs + HEAD_DIM] = _head_norm(tot[:, :LANES], tot[:, LANES:], emt_b,
                                                    mhg_ref[:, hs:hs + HEAD_DIM])
        w_b = jnp.broadcast_to(g["w"][:, h:h + 1], (CHUNK, LANES))
        wv = jnp.concatenate([(w_b * v).astype(_BF), w_b.astype(_BF)], axis=1)
        upd = _dot(st["k"].T.astype(_BF), wv)
        decay_b = jnp.broadcast_to(g["decay"][:, h:h + 1], (HEAD_DIM, 2 * LANES))
        caug[h] = decay_b * caug[h] + upd

    units = [(c, h) for c in range(n_chunks) for h in range(M_HEADS)]
    lag = min(UNIT_LAG, len(units))
    n_slots = len(units) + lag
    assert len(late_tiles) <= n_slots
    assert all(0 <= slot < n_slots for slot in side_work)
    pending = {}
    for t in range(n_slots):
        if t < len(units) and t % 2 == 0:
            pending[t], pending[t + 1] = stage_a(*units[t])
        if t < len(late_tiles):
            project_tile(late_tiles[t])
        for fn in side_work.get(t, ()):
            fn()
        if t >= lag:
            stage_b(*units[t - lag], pending.pop(t - lag))

    u = p_scr[:, OFF_GC:OFF_GC + CONV_WIDTH] * p_scr[:, OFF_HC:OFF_HC + CONV_WIDTH]
    ubuf[SUBLANES:SUBLANES + tb, :] = u
    u_m1 = ubuf[SUBLANES - 1:SUBLANES - 1 + tb, :]
    u_m2 = ubuf[SUBLANES - 2:SUBLANES - 2 + tb, :]
    y_conv = _conv_out(p_scr[:, OFF_GB:OFF_GB + CONV_WIDTH], p_scr[:, OFF_ZC:OFF_ZC + CONV_WIDTH],
                       u, u_m1, u_m2, cw_ref)
    ycat_scr[:, 0:CONV_WIDTH] = y_conv.astype(_BF)
    ycat_scr[:, CONV_WIDTH:CONV_WIDTH + M_WIDTH] = _gated_heads(hn_scr[...], p_scr).astype(_BF)
    ubuf[0:SUBLANES, :] = ubuf[tb:tb + SUBLANES, :]


def _sequence_scratch(tb):
    return [
        pltpu.VMEM((tb, MAIN_WIDTH), _F32),
        pltpu.VMEM((tb, LANES), _F32),
        pltpu.VMEM((tb, M_WIDTH), _F32),
        pltpu.VMEM((tb, D_MODEL), _BF),
        pltpu.VMEM((tb + 2 * SUBLANES, CONV_WIDTH), _F32),
        pltpu.VMEM((M_HEADS, HEAD_DIM, 2 * LANES), _F32),
        pltpu.VMEM((1, LANES), _F32),
    ]


def _front_body(wt_ref, x_ref, wgt_ref, wout_ref, ng_ref, fgain_ref, mhg_ref, cw_ref, big_ref, bfg_ref,
                xs_ref, sconv_ref, ms_ref, ns_ref, cs_ref,
                w3_ref, wg_ref, wo_ref, caug_out_ref, m_out_ref, ut_out_ref,
                ys_ref, sconv_out_ref, cs_out_ref, ns_out_ref, ms_out_ref,
                w3_scr, xn_scr, p3_scr, sc_scr, ms_in_scr, ms_new_scr,
                p_scr, g_scr, hn_scr, ycat_scr, ubuf, caug, m_scr):
    t = pl.program_id(0)
    n_prep = N_TILES // PREP_TILES
    last = pl.num_programs(0) - 1

    @pl.when(t == 0)
    def _():
        xn_scr[...] = jnp.zeros_like(xn_scr)
        xn_scr[CHUNK - N_META:CHUNK, :] = _rms_rows(x_ref[...], ng_ref[...]).astype(_BF)
        gate_rows = jnp.concatenate([wgt_ref[...], jnp.zeros((LANES - N_GATES, D_MODEL), _F32)], axis=0)
        wg_ref[...] = gate_rows.T.astype(_BF)
        wo_ref[...] = wout_ref[...].astype(_BF)
        sc_scr[...] = jnp.zeros_like(sc_scr)
        m_heads = jnp.concatenate([ms_ref[...], jnp.zeros((LANES - M_HEADS, LANES), _F32)], axis=0)
        ms_in_scr[...] = m_heads.T

    @pl.when(t < n_prep)
    def _():
        for j in range(PREP_TILES):
            tile = wt_ref[j * MXU_COLS:(j + 1) * MXU_COLS, :].T.astype(_BF)
            w3_ref[j] = tile
            w3_scr[t * PREP_TILES + j] = tile
            p3_scr[t * PREP_TILES + j] = _dot(xn_scr[...], tile)

    @pl.when(t == n_prep - 1)
    def _():
        for j in range(N_TILES):
            p_scr[:, _tile_cols(j)] = p3_scr[j]
        caug[...] = jnp.zeros_like(caug)
        m_scr[...] = jnp.zeros_like(m_scr)
        ubuf[0:SUBLANES, :] = jnp.zeros((SUBLANES, CONV_WIDTH), _F32)
        _sequence_block(xn_scr[...], None, p_scr, g_scr, hn_scr, ycat_scr, ubuf, caug, m_scr,
                        wg_ref, mhg_ref, cw_ref, big_ref, bfg_ref, tb=CHUNK, n_pad=CHUNK - N_META)
        caug_out_ref[...] = caug[...]
        m_out_ref[...] = m_scr[...]
        ut_out_ref[...] = ubuf[0:SUBLANES, :]

    @pl.when(t >= n_prep)
    def _():
        seqs = pl.ds(pl.multiple_of((t - n_prep) * SEQ_PER_STEP, SEQ_PER_STEP), SEQ_PER_STEP)
        _sample_block(xs_ref, w3_scr, wg_ref, wo_ref, ng_ref, fgain_ref, mhg_ref, cw_ref, big_ref, bfg_ref,
                      sconv_ref, ms_in_scr.at[seqs], ns_ref, cs_ref,
                      ys_ref, sconv_out_ref, cs_out_ref, ns_out_ref, ms_new_scr.at[seqs],
                      p_scr, hn_scr, sc_scr, ycat_scr)

    @pl.when(t == last)
    def _():
        ms_out_ref[...] = ms_new_scr[...].T[0:M_HEADS, :]


def _front_call(w_t, meta_tokens, w_out, small, fgain, xs, state_conv, m_heads, n_state, c_state):
    ng, mhg, cw, big, bfg = small
    n_prep = N_TILES // PREP_TILES
    rows = xs.shape[0]
    nseq = c_state.shape[0]
    n_sample = rows // CHUNK
    assert m_heads.shape == (M_HEADS, LANES) and nseq == LANES

    def const(shape):
        nd = len(shape)
        return pl.BlockSpec(shape, lambda t: (0,) * nd)

    def sample_spec(*block):
        nd = len(block) - 1
        return pl.BlockSpec(block, lambda t: (jnp.maximum(t - n_prep, 0),) + (0,) * nd)

    def prep_idx(t):
        return jnp.minimum(t, n_prep - 1)

    xs_spec = sample_spec(CHUNK, D_MODEL)
    sconv_spec = sample_spec(SEQ_PER_STEP, CONV_K - 1, CONV_WIDTH)
    cs_spec = sample_spec(SEQ_PER_STEP, M_HEADS, HEAD_DIM, HEAD_DIM)
    ns_spec = sample_spec(SEQ_PER_STEP, M_HEADS, HEAD_DIM)
    ms_spec = const(m_heads.shape)
    in_specs = [
        pl.BlockSpec((PREP_TILES * MXU_COLS, D_MODEL), lambda t: (prep_idx(t), 0)),
        const(meta_tokens.shape),
        pl.BlockSpec((N_GATES, D_MODEL), lambda t: (MAIN_WIDTH // N_GATES, 0)),
        const(w_out.shape), const(ng.shape), const(fgain.shape), const(mhg.shape), const(cw.shape),
        _SMEM_SPEC, _SMEM_SPEC,
        xs_spec, sconv_spec, ms_spec, ns_spec, cs_spec,
    ]
    out_shape = (
        jax.ShapeDtypeStruct((N_TILES, D_MODEL, MXU_COLS), _BF),
        jax.ShapeDtypeStruct((D_MODEL, LANES), _BF),
        jax.ShapeDtypeStruct(w_out.shape, _BF),
        jax.ShapeDtypeStruct((M_HEADS, HEAD_DIM, 2 * LANES), _F32),
        jax.ShapeDtypeStruct((1, LANES), _F32),
        jax.ShapeDtypeStruct((SUBLANES, CONV_WIDTH), _F32),
        jax.ShapeDtypeStruct((rows, D_MODEL), _F32),
        jax.ShapeDtypeStruct(state_conv.shape, _F32),
        jax.ShapeDtypeStruct(c_state.shape, _F32),
        jax.ShapeDtypeStruct(n_state.shape, _F32),
        jax.ShapeDtypeStruct(m_heads.shape, _F32),
    )
    out_specs = (
        pl.BlockSpec((PREP_TILES, D_MODEL, MXU_COLS), lambda t: (prep_idx(t), 0, 0)),
        const(out_shape[1].shape), const(out_shape[2].shape), const(out_shape[3].shape), const(out_shape[4].shape),
        const(out_shape[5].shape),
        xs_spec, sconv_spec, cs_spec, ns_spec, ms_spec,
    )
    scratch = [
        pltpu.VMEM((N_TILES, D_MODEL, MXU_COLS), _BF),
        pltpu.VMEM((CHUNK, D_MODEL), _BF),
        pltpu.VMEM((N_TILES, CHUNK, MXU_COLS), _F32),
        pltpu.VMEM((CHUNK, CONV_WIDTH), _F32),
        pltpu.VMEM((LANES, LANES), _F32),
        pltpu.VMEM((LANES, LANES), _F32),
    ] + _sequence_scratch(CHUNK)
    return pl.pallas_call(
        _front_body,
        grid=(n_prep + n_sample,),
        in_specs=in_specs,
        out_specs=out_specs,
        out_shape=out_shape,
        scratch_shapes=scratch,
        compiler_params=pltpu.CompilerParams(
            dimension_semantics=("arbitrary",), vmem_limit_bytes=VMEM_REQUEST_BYTES),
        name="prep_meta_sample",
    )(w_t, meta_tokens, w_t, w_out, ng, fgain, mhg, cw, big, bfg, xs, state_conv, m_heads, n_state, c_state)


def _prompt_body(x_ref, w3_ref, wg_ref, wo_ref, ng_ref, fgain_ref, mhg_ref, cw_ref, big_ref, bfg_ref,
                 caug0_ref, m0_ref, ut0_ref,
                 y_ref, c_out_ref, n_out_ref, m_out_ref, conv_out_ref,
                 out_scr, m_all_scr, p_scr, g_scr, hn_scr, ycat_scr, ubuf, caug, m_scr, *, tb, nt, n_blocks):
    s = pl.program_id(0)
    i = s % nt
    live = s < n_blocks

    @pl.when(s == 0)
    def _():
        out_scr[...] = jnp.zeros_like(out_scr)
        m_all_scr[...] = jnp.zeros_like(m_all_scr)

    @pl.when(live & (i == 0))
    def _():
        caug[...] = caug0_ref[...]
        m_scr[...] = m0_ref[...]
        ubuf[0:SUBLANES, :] = ut0_ref[...]

    @pl.when(live)
    def _():
        xn = _rms_rows(x_ref[0], ng_ref[...]).astype(_BF)

        def project_tile(t):
            p_scr[:, _tile_cols(t)] = _dot(xn, w3_ref[t])

        def norm_previous(r0, r1):
            def fn():
                y_ref[0, r0:r1, :] = _rms_rows(out_scr[r0:r1, :], fgain_ref[...])
            return fn

        q4 = tb // 4
        side_work = {2 + 2 * j: [norm_previous(j * q4, (j + 1) * q4)] for j in range(4)}
        _sequence_block(xn, project_tile, p_scr, g_scr, hn_scr, ycat_scr, ubuf, caug, m_scr,
                        wg_ref, mhg_ref, cw_ref, big_ref, bfg_ref, tb=tb, n_pad=0, side_work=side_work)
        out_scr[...] = _dot(ycat_scr[...], wo_ref[...]) + x_ref[0]

    @pl.when(s == n_blocks)
    def _():
        y_ref[0] = _rms_rows(out_scr[...], fgain_ref[...])
        m_out_ref[...] = m_all_scr[...].T[0:M_HEADS, 0:n_blocks // nt]

    @pl.when(live & (i == nt - 1))
    def _():
        m_all_scr[pl.ds(s // nt, 1), :] = m_scr[...]
        conv_out_ref[0] = ubuf[SUBLANES - (CONV_K - 1):SUBLANES, :]
        for h in range(M_HEADS):
            c_out_ref[0, h] = caug[h, :, 0:HEAD_DIM]
            n_out_ref[0, h:h + 1, :] = caug[h, :, LANES:2 * LANES].T[0:1, :]


def _prompt_call(x, w3, wg, wo, small, fgain, caug0, m0, ut0):
    ng, mhg, cw, big, bfg = small
    nb, t, _ = x.shape
    tb = PROMPT_ROWS
    nt = t // tb
    n_blocks = nb * nt

    def const(shape):
        nd = len(shape)
        return pl.BlockSpec(shape, lambda s: (0,) * nd)

    def block_of(s):
        return jnp.minimum(s, n_blocks - 1)

    def rows_spec(step_to_block):
        return pl.BlockSpec((1, tb, D_MODEL), lambda s: (step_to_block(s) // nt, step_to_block(s) % nt, 0))

    def state_spec(*tail):
        nd = len(tail)
        return pl.BlockSpec((1,) + tail, lambda s: (block_of(s) // nt,) + (0,) * nd)

    in_specs = [
        rows_spec(block_of),
        const(w3.shape), const(wg.shape), const(wo.shape), const(ng.shape), const(fgain.shape),
        const(mhg.shape), const(cw.shape), _SMEM_SPEC, _SMEM_SPEC,
        const(caug0.shape), const(m0.shape), const(ut0.shape),
    ]
    out_shape = (
        jax.ShapeDtypeStruct((nb, t, D_MODEL), _F32),
        jax.ShapeDtypeStruct((nb, M_HEADS, HEAD_DIM, HEAD_DIM), _F32),
        jax.ShapeDtypeStruct((nb, M_HEADS, HEAD_DIM), _F32),
        jax.ShapeDtypeStruct((M_HEADS, nb), _F32),
        jax.ShapeDtypeStruct((nb, CONV_K - 1, CONV_WIDTH), _F32),
    )
    out_specs = (
        rows_spec(lambda s: jnp.maximum(s - 1, 0)),
        state_spec(M_HEADS, HEAD_DIM, HEAD_DIM),
        state_spec(M_HEADS, HEAD_DIM),
        const((M_HEADS, nb)),
        state_spec(CONV_K - 1, CONV_WIDTH),
    )
    assert nb <= LANES
    scratch = [
        pltpu.VMEM((tb, D_MODEL), _F32),
        pltpu.VMEM((LANES, LANES), _F32),
    ] + _sequence_scratch(tb)
    return pl.pallas_call(
        functools.partial(_prompt_body, tb=tb, nt=nt, n_blocks=n_blocks),
        grid=(n_blocks + 1,),
        in_specs=in_specs,
        out_specs=out_specs,
        out_shape=out_shape,
        scratch_shapes=scratch,
        compiler_params=pltpu.CompilerParams(
            dimension_semantics=("arbitrary",), vmem_limit_bytes=VMEM_REQUEST_BYTES),
        name="prompt_layer",
    )(x, w3, wg, wo, ng, fgain, mhg, cw, big, bfg, caug0, m0, ut0)


def _sample_block(x_ref, w3_ref, wg_ref, wo_ref, ng_ref, fgain_ref, mhg_ref, cw_ref, big_ref, bfg_ref,
                  sconv_ref, m_ref, n_ref, c_ref,
                  y_ref, conv_out_ref, c_out_ref, n_out_ref, m_out_ref,
                  p_scr, ni_scr, sc_scr, ycat_scr):
    x = x_ref[...]
    xn = _rms_rows(x, ng_ref[...]).astype(_BF)
    g_raw = _dot(xn, wg_ref[...])
    for t in QKV_TILES:
        p_scr[:, _tile_cols(t)] = _dot(xn, w3_ref[t])

    row = lax.broadcasted_iota(jnp.int32, (CHUNK, CHUNK), 0)
    col = lax.broadcasted_iota(jnp.int32, (CHUNK, CHUNK), 1)
    pos = row % DEC_SEQ
    grp = row // DEC_SEQ
    causal = (grp == (col // DEC_SEQ)) & (col <= row)
    grow = lax.broadcasted_iota(jnp.int32, (SEQ_PER_STEP, CHUNK), 0)
    gcol = lax.broadcasted_iota(jnp.int32, (SEQ_PER_STEP, CHUNK), 1)
    pick_last = (gcol == grow * DEC_SEQ + (DEC_SEQ - 1)).astype(_F32)
    seq_sum = (gcol // DEC_SEQ == grow).astype(_BF)
    erow = lax.broadcasted_iota(jnp.int32, (CHUNK, SEQ_PER_STEP), 0)
    ecol = lax.broadcasted_iota(jnp.int32, (CHUNK, SEQ_PER_STEP), 1)
    own_seq = erow // DEC_SEQ == ecol
    ones_blk = jnp.ones((CHUNK, LANES), _BF)

    def rows_from_seqs(v16):
        return jnp.broadcast_to(v16[:, None, :], (SEQ_PER_STEP, DEC_SEQ, LANES)).reshape(CHUNK, LANES)

    def last_row_of_seq(v):
        v3 = v.reshape(SEQ_PER_STEP, DEC_SEQ, LANES)
        return jnp.broadcast_to(v3[:, DEC_SEQ - 1:DEC_SEQ, :], v3.shape).reshape(CHUNK, LANES)

    ig, lf = _gate_columns(g_raw, big_ref, bfg_ref)
    b = _cumsum_rows(lf, pos, DEC_SEQ)
    a = ig - b
    m_prev = rows_from_seqs(m_ref[...])
    big_m = jnp.maximum(m_prev, _cummax_rows(a, pos, DEC_SEQ))
    m_t = b + big_m
    inter = jnp.exp(m_prev - big_m)
    emt = jnp.exp(-m_t)
    m_last = last_row_of_seq(big_m)
    w = jnp.exp(a - m_last)
    decay = jnp.exp(m_prev - m_last)
    m_out_ref[...] = _dot_exact(pick_last, m_t)
    dec16 = _dot_exact(pick_last, decay)
    a_t = a.T

    for t in REST_TILES:
        p_scr[:, _tile_cols(t)] = _dot(xn, w3_ref[t])

    heads = []
    for h in range(M_HEADS):
        hs = h * HEAD_DIM
        q = p_scr[:, OFF_Q + hs:OFF_Q + hs + HEAD_DIM]
        k = p_scr[:, OFF_K + hs:OFF_K + hs + HEAD_DIM] * K_SCALE
        v = p_scr[:, OFF_V + hs:OFF_V + hs + HEAD_DIM]
        q_bf = q.astype(_BF)
        a_b = jnp.broadcast_to(a_t[h:h + 1, :], (CHUNK, CHUNK))
        m_b = jnp.broadcast_to(big_m[:, h:h + 1], (CHUNK, CHUNK))
        d = jnp.exp(jnp.where(causal, a_b - m_b, NEG_INF))
        w_b = jnp.broadcast_to(w[:, h:h + 1], (CHUNK, LANES))
        heads.append(dict(q_bf=q_bf, k=k, v=v, d=d, w_b=w_b, qk=_dot_nt(q_bf, k.astype(_BF))))

    u = p_scr[:, OFF_GC:OFF_GC + CONV_WIDTH] * p_scr[:, OFF_HC:OFF_HC + CONV_WIDTH]
    for g in range(SEQ_PER_STEP):
        sc_scr[g * DEC_SEQ:g * DEC_SEQ + CONV_K - 1, :] = sconv_ref[g]
        conv_out_ref[g] = u[(g + 1) * DEC_SEQ - (CONV_K - 1):(g + 1) * DEC_SEQ, :]
    sc = sc_scr[...]
    posw = lax.broadcasted_iota(jnp.int32, (CHUNK, CONV_WIDTH), 0) % DEC_SEQ
    u_m1 = jnp.where(posw == 0, pltpu.roll(sc, CHUNK - 1, 0), pltpu.roll(u, 1, 0))
    u_m2 = jnp.where(posw < 2, sc, pltpu.roll(u, 2, 0))
    y_conv = _conv_out(p_scr[:, OFF_GB:OFF_GB + CONV_WIDTH], p_scr[:, OFF_ZC:OFF_ZC + CONV_WIDTH],
                       u, u_m1, u_m2, cw_ref)
    ycat_scr[:, 0:CONV_WIDTH] = y_conv.astype(_BF)

    for h in range(M_HEADS):
        hs = h * HEAD_DIM
        st = heads[h]
        wv = st["w_b"] * st["v"]
        k_t_bf = st["k"].T.astype(_BF)
        dec_b = jnp.broadcast_to(decay[:, h:h + 1], (CHUNK, HEAD_DIM))

        for g in range(0, SEQ_PER_STEP, 2):
            pair_rows = slice(g * DEC_SEQ, (g + 2) * DEC_SEQ)
            c0, c1 = c_ref[g, h], c_ref[g + 1, h]
            qc = _dot(st["q_bf"][pair_rows, :], jnp.concatenate([c0.astype(_BF), c1.astype(_BF)], axis=1))
            ni_scr[g * DEC_SEQ:(g + 1) * DEC_SEQ, hs:hs + HEAD_DIM] = qc[0:DEC_SEQ, 0:HEAD_DIM]
            ni_scr[(g + 1) * DEC_SEQ:(g + 2) * DEC_SEQ, hs:hs + HEAD_DIM] = qc[DEC_SEQ:, HEAD_DIM:]
            wv_pair = jnp.concatenate([jnp.where(grp == g, wv, 0.0).astype(_BF),
                                       jnp.where(grp == g + 1, wv, 0.0).astype(_BF)], axis=1)
            upd = _dot(k_t_bf, wv_pair)
            dec0 = jnp.broadcast_to(dec_b[g * DEC_SEQ:g * DEC_SEQ + 1, :], (HEAD_DIM, HEAD_DIM))
            dec1 = jnp.broadcast_to(dec_b[(g + 1) * DEC_SEQ:(g + 1) * DEC_SEQ + 1, :], (HEAD_DIM, HEAD_DIM))
            c_out_ref[g, h] = dec0 * c0 + upd[:, 0:HEAD_DIM]
            c_out_ref[g + 1, h] = dec1 * c1 + upd[:, HEAD_DIM:]

        s = st["qk"] * st["d"]
        sv = _dot(s.astype(_BF), jnp.concatenate([st["v"].astype(_BF), ones_blk], axis=1))
        n_old = n_ref[:, h, :]
        num_inter = ni_scr[:, hs:hs + HEAD_DIM]
        qn = _dot_nt(st["q_bf"], n_old.astype(_BF))
        den_inter = jnp.sum(jnp.where(own_seq, qn, 0.0), axis=-1, keepdims=True)
        inter_b = jnp.broadcast_to(inter[:, h:h + 1], (CHUNK, LANES))
        num = sv[:, :LANES] + inter_b * num_inter
        den = sv[:, LANES:] + inter_b * den_inter
        emt_b = jnp.broadcast_to(emt[:, h:h + 1], (CHUNK, LANES))
        ni_scr[:, hs:hs + HEAD_DIM] = _head_norm(num, den, emt_b, mhg_ref[:, hs:hs + HEAD_DIM])
        n_upd = _dot(seq_sum, (st["w_b"] * st["k"]).astype(_BF))
        dec16_b = jnp.broadcast_to(dec16[:, h:h + 1], (SEQ_PER_STEP, HEAD_DIM))
        n_out_ref[:, h, :] = dec16_b * n_old + n_upd

    ycat_scr[:, CONV_WIDTH:CONV_WIDTH + M_WIDTH] = _gated_heads(ni_scr[...], p_scr).astype(_BF)
    out = _dot(ycat_scr[...], wo_ref[...]) + x
    y_ref[...] = _rms_rows(out, fgain_ref[...])


def kernel(x_prompt, x_sample, state_conv, state_mlstm_c, state_mlstm_n, state_mlstm_m, meta_tokens, norm_gain,
           w_in, conv_w, b_igate, b_fgate, mh_norm_gain, w_out, final_norm_gain):
    nseq, dec_seq, _ = x_sample.shape
    assert dec_seq == DEC_SEQ and meta_tokens.shape[0] == N_META
    assert w_in.shape == (D_MODEL, MAIN_WIDTH + N_GATES) and x_prompt.shape[1] % PROMPT_ROWS == 0
    assert (nseq * DEC_SEQ) % CHUNK == 0

    w_t = w_in.T
    small = (norm_gain.reshape(1, D_MODEL), mh_norm_gain.reshape(1, M_WIDTH), conv_w,
             b_igate.astype(_F32), b_fgate.astype(_F32))
    fgain = final_norm_gain.reshape(1, D_MODEL)

    xs = x_sample.reshape(nseq * DEC_SEQ, D_MODEL)
    w3, wg, wo, caug_meta, m_meta, ut_meta, ys, conv_s, c_s, n_s, m_s = _front_call(
        w_t, meta_tokens.astype(_F32), w_out.astype(_F32), small, fgain, xs, state_conv, state_mlstm_m.T,
        state_mlstm_n, state_mlstm_c)

    y_prompt, c_p, n_p, m_p, conv_p = _prompt_call(x_prompt, w3, wg, wo, small, fgain, caug_meta, m_meta, ut_meta)
    y_sample = ys.reshape(nseq, DEC_SEQ, D_MODEL)

    return (y_prompt, y_sample, conv_p, c_p, n_p, m_p.T, conv_s, c_s, n_s, m_s.T)
```

```python
import functools

import jax
import jax.numpy as jnp
from jax import lax
from jax.experimental import pallas as pl
from jax.experimental.pallas import tpu as pltpu

D_MODEL = 1024
CONV_WIDTH = 512
M_WIDTH = 512
M_HEADS = 4
HEAD_DIM = 128
N_META = 16
CONV_K = 3
EPS = 1e-6
MAIN_WIDTH = 4 * CONV_WIDTH + 5 * M_WIDTH
N_GATES = 2 * M_HEADS
LANES = 128
SUBLANES = 8
MXU_COLS = 256
N_TILES = MAIN_WIDTH // MXU_COLS
CHUNK = 128
DEC_SEQ = 8
SEQ_PER_STEP = CHUNK // DEC_SEQ
NEG_INF = float("-inf")
K_SCALE = HEAD_DIM ** -0.5

PROMPT_ROWS = 512
PREP_TILES = 1
WOUT_ROWS = 64
VMEM_REQUEST_BYTES = 64 * 1024 * 1024
EARLY_TILES = 2
UNIT_LAG = 3

OFF_GB, OFF_GC, OFF_HC, OFF_ZC = 0, 512, 1024, 1536
OFF_Q, OFF_K, OFF_V, OFF_O, OFF_ZM = 2048, 2560, 3072, 3584, 4096
QKV_TILES = tuple(range(OFF_Q // MXU_COLS, OFF_O // MXU_COLS))
REST_TILES = tuple(t for t in range(N_TILES) if t not in QKV_TILES)

_SMEM_SPEC = pl.BlockSpec(memory_space=pltpu.SMEM)
_HI = lax.Precision.HIGHEST
_BF = jnp.bfloat16
_F32 = jnp.float32


def _dot(a, b):
    return jnp.dot(a, b, preferred_element_type=_F32)


def _dot_exact(a, b):
    return jnp.dot(a, b, precision=_HI, preferred_element_type=_F32)


def _dot_nt(a, b):
    return lax.dot_general(a, b, (((1,), (1,)), ((), ())), preferred_element_type=_F32)


def _rms_rows(x, gain_row):
    return x * lax.rsqrt(jnp.mean(x * x, axis=-1, keepdims=True) + EPS) * gain_row


def _sigmoid(x):
    return 1.0 / (1.0 + jnp.exp(-x))


def _silu(x):
    return x * _sigmoid(x)


def _tile_cols(t):
    return slice(t * MXU_COLS, (t + 1) * MXU_COLS)


def _cummax_rows(x, pos, period):
    s = 1
    while s < period:
        shifted = pltpu.roll(x, s, 0)
        x = jnp.maximum(x, jnp.where(pos >= s, shifted, NEG_INF))
        s *= 2
    return x


def _cumsum_rows(x, pos, period):
    s = 1
    while s < period:
        x = x + jnp.where(pos >= s, pltpu.roll(x, s, 0), 0.0)
        s *= 2
    return x


def _bias_row(b_ref):
    lane = lax.broadcasted_iota(jnp.int32, (1, LANES), 1)
    row = jnp.zeros((1, LANES), _F32)
    for h in range(M_HEADS):
        row = jnp.where(lane == h, b_ref[h], row)
    return row


def _gate_columns(g_raw, big_ref, bfg_ref):
    ig = g_raw + _bias_row(big_ref)
    lf = jax.nn.log_sigmoid(pltpu.roll(g_raw, LANES - M_HEADS, 1) + _bias_row(bfg_ref))
    return ig, lf


def _head_norm(num, den, emt_b, gain_row):
    hh = num / jnp.maximum(jnp.abs(den), emt_b)
    return hh * lax.rsqrt(jnp.mean(hh * hh, axis=-1, keepdims=True) + EPS) * gain_row


def _conv_out(gb, zc, u, u_m1, u_m2, cw_ref):
    yc = cw_ref[0:1, :] * u_m2 + cw_ref[1:2, :] * u_m1 + cw_ref[2:3, :] * u
    return gb * yc * _silu(zc)


def _gated_heads(hn, p_scr, tb):
    return hn * _sigmoid(p_scr[0:tb, OFF_O:OFF_O + M_WIDTH]) * _silu(p_scr[0:tb, OFF_ZM:OFF_ZM + M_WIDTH])


def _sequence_block(xn, project_tile, p_scr, g_scr, hn_scr, ycat_scr, ubuf, caug, m_scr,
                    wg_ref, mhg_ref, cw_ref, big_ref, bfg_ref, *, tb, n_pad, side_work=None):
    side_work = side_work or {}
    n_chunks = tb // CHUNK
    units = [(c, h) for c in range(n_chunks) for h in range(M_HEADS)]
    lag = min(UNIT_LAG, len(units))
    n_slots = len(units) + lag
    n_early = max(EARLY_TILES, len(REST_TILES) - n_slots)
    assert all(0 <= slot < n_slots for slot in side_work)

    g_scr[0:tb, :] = _dot(xn, wg_ref[...])
    for t in QKV_TILES + REST_TILES[:n_early]:
        project_tile(t)
    late_tiles = list(REST_TILES[n_early:])

    row = lax.broadcasted_iota(jnp.int32, (CHUNK, CHUNK), 0)
    col = lax.broadcasted_iota(jnp.int32, (CHUNK, CHUNK), 1)
    causal = col <= row
    ones_blk = jnp.ones((CHUNK, LANES), _BF)

    gates = []
    m_prev = m_scr[...]
    for c in range(n_chunks):
        rows = slice(c * CHUNK, (c + 1) * CHUNK)
        ig, lf = _gate_columns(g_scr[rows, :], big_ref, bfg_ref)
        if n_pad:
            ig = jnp.where(row >= n_pad, ig, NEG_INF)
            lf = jnp.where(row >= n_pad, lf, 0.0)
        b = _cumsum_rows(lf, row, CHUNK)
        a = ig - b
        big_m = jnp.maximum(m_prev, _cummax_rows(a, row, CHUNK))
        m_t = b + big_m
        m_last = big_m[CHUNK - 1:CHUNK, :]
        gates.append(dict(a_t=a.T, big_m=big_m, inter=jnp.exp(m_prev - big_m), emt=jnp.exp(-m_t),
                          w=jnp.exp(a - m_last), decay=jnp.exp(m_prev - m_last)))
        m_prev = m_t[CHUNK - 1:CHUNK, :]
    m_scr[...] = m_prev

    def stage_a(c, h):
        rows = slice(c * CHUNK, (c + 1) * CHUNK)
        hs = h * HEAD_DIM
        g = gates[c]
        q2 = p_scr[rows, OFF_Q + hs:OFF_Q + hs + 2 * HEAD_DIM]
        k2 = p_scr[rows, OFF_K + hs:OFF_K + hs + 2 * HEAD_DIM] * K_SCALE
        k2_bf = k2.astype(_BF)
        zero = jnp.zeros((CHUNK, HEAD_DIM), _BF)
        k_diag = jnp.concatenate([jnp.concatenate([k2_bf[:, :HEAD_DIM], zero], axis=1),
                                  jnp.concatenate([zero, k2_bf[:, HEAD_DIM:]], axis=1)], axis=0)
        qk2 = _dot_nt(q2.astype(_BF), k_diag)
        out = []
        for j in range(2):
            cols = slice(j * HEAD_DIM, (j + 1) * HEAD_DIM)
            a_b = jnp.broadcast_to(g["a_t"][h + j:h + j + 1, :], (CHUNK, CHUNK))
            m_b = jnp.broadcast_to(g["big_m"][:, h + j:h + j + 1], (CHUNK, CHUNK))
            d = jnp.exp(jnp.where(causal, a_b - m_b, NEG_INF))
            inter_b = jnp.broadcast_to(g["inter"][:, h + j:h + j + 1], (CHUNK, HEAD_DIM))
            out.append(dict(qk=qk2[:, cols], d=d, q_inter=(inter_b * q2[:, cols]).astype(_BF), k=k2[:, cols]))
        return out

    def stage_b(c, h, st):
        rows = slice(c * CHUNK, (c + 1) * CHUNK)
        hs = h * HEAD_DIM
        g = gates[c]
        v = p_scr[rows, OFF_V + hs:OFF_V + hs + HEAD_DIM]
        s = st["qk"] * st["d"]
        lhs = jnp.concatenate([s.astype(_BF), st["q_inter"]], axis=1)
        rhs = jnp.concatenate([jnp.concatenate([v.astype(_BF), ones_blk], axis=1), caug[h].astype(_BF)], axis=0)
        tot = _dot(lhs, rhs)
        emt_b = jnp.broadcast_to(g["emt"][:, h:h + 1], (CHUNK, LANES))
        hn_scr[rows, hs:hs + HEAD_DIM] = _head_norm(tot[:, :LANES], tot[:, LANES:], emt_b,
                                                    mhg_ref[:, hs:hs + HEAD_DIM])
        w_b = jnp.broadcast_to(g["w"][:, h:h + 1], (CHUNK, LANES))
        wv = jnp.concatenate([(w_b * v).astype(_BF), w_b.astype(_BF)], axis=1)
        upd = _dot(st["k"].T.astype(_BF), wv)
        decay_b = jnp.broadcast_to(g["decay"][:, h:h + 1], (HEAD_DIM, 2 * LANES))
        caug[h] = decay_b * caug[h] + upd

    pending = {}
    for t in range(n_slots):
        if t < len(units) and t % 2 == 0:
            pending[t], pending[t + 1] = stage_a(*units[t])
        if t < len(late_tiles):
            project_tile(late_tiles[t])
        for fn in side_work.get(t, ()):
            fn()
        if t >= lag:
            stage_b(*units[t - lag], pending.pop(t - lag))

    u = p_scr[0:tb, OFF_GC:OFF_GC + CONV_WIDTH] * p_scr[0:tb, OFF_HC:OFF_HC + CONV_WIDTH]
    ubuf[SUBLANES:SUBLANES + tb, :] = u
    u_m1 = ubuf[SUBLANES - 1:SUBLANES - 1 + tb, :]
    u_m2 = ubuf[SUBLANES - 2:SUBLANES - 2 + tb, :]
    y_conv = _conv_out(p_scr[0:tb, OFF_GB:OFF_GB + CONV_WIDTH], p_scr[0:tb, OFF_ZC:OFF_ZC + CONV_WIDTH],
                       u, u_m1, u_m2, cw_ref)
    ycat_scr[0:tb, 0:CONV_WIDTH] = y_conv.astype(_BF)
    ycat_scr[0:tb, CONV_WIDTH:CONV_WIDTH + M_WIDTH] = _gated_heads(hn_scr[0:tb, :], p_scr, tb).astype(_BF)
    ubuf[0:SUBLANES, :] = ubuf[tb:tb + SUBLANES, :]


def _sequence_scratch(tb):
    return [
        pltpu.VMEM((tb, MAIN_WIDTH), _F32),
        pltpu.VMEM((tb, LANES), _F32),
        pltpu.VMEM((tb, M_WIDTH), _F32),
        pltpu.VMEM((tb, D_MODEL), _BF),
        pltpu.VMEM((tb + 2 * SUBLANES, CONV_WIDTH), _F32),
        pltpu.VMEM((M_HEADS, HEAD_DIM, 2 * LANES), _F32),
        pltpu.VMEM((1, LANES), _F32),
    ]


def _prep_phase(t, wt_ref, xm_ref, wgt_ref, wout_ref, ng_ref, mhg_ref, cw_ref, big_ref, bfg_ref,
                w3_scr, wg_scr, wo_scr, xn_scr, caug0_scr, m0_scr, ut0_scr,
                p_scr, g_scr, hn_scr, ycat_scr, ubuf, caug, m_scr, *, n_prep):
    @pl.when(t == 0)
    def _():
        xn_scr[...] = jnp.zeros_like(xn_scr)
        xn_scr[CHUNK - N_META:CHUNK, :] = _rms_rows(xm_ref[...], ng_ref[...]).astype(_BF)
        gate_rows = jnp.concatenate([wgt_ref[...], jnp.zeros((LANES - N_GATES, D_MODEL), _F32)], axis=0)
        wg_scr[...] = gate_rows.T.astype(_BF)

    @pl.when(t < D_MODEL // WOUT_ROWS)
    def _():
        wo_scr[pl.ds(pl.multiple_of(t * WOUT_ROWS, WOUT_ROWS), WOUT_ROWS), :] = wout_ref[...].astype(_BF)

    @pl.when(t < n_prep)
    def _():
        for j in range(PREP_TILES):
            w3_scr[t * PREP_TILES + j] = wt_ref[j * MXU_COLS:(j + 1) * MXU_COLS, :].T.astype(_BF)

    @pl.when(t == n_prep - 1)
    def _():
        xn = xn_scr[...]

        def project_tile(j):
            p_scr[0:CHUNK, _tile_cols(j)] = _dot(xn, w3_scr[j])

        caug[...] = jnp.zeros_like(caug)
        m_scr[...] = jnp.zeros_like(m_scr)
        ubuf[0:SUBLANES, :] = jnp.zeros((SUBLANES, CONV_WIDTH), _F32)
        _sequence_block(xn, project_tile, p_scr, g_scr, hn_scr, ycat_scr, ubuf, caug, m_scr,
                        wg_scr, mhg_ref, cw_ref, big_ref, bfg_ref, tb=CHUNK, n_pad=CHUNK - N_META)
        caug0_scr[...] = caug[...]
        m0_scr[...] = m_scr[...]
        ut0_scr[...] = ubuf[0:SUBLANES, :]


def _sample_phase(j, n_sample, xs_ref, w3_scr, wg_scr, wo_scr, ng_ref, fgain_ref, mhg_ref, cw_ref, big_ref, bfg_ref,
                  sconv_ref, ms_ref, ns_ref, cs_ref, ys_ref, sconv_out_ref, cs_out_ref, ns_out_ref, ms_out_ref,
                  sc_scr, ms_in_scr, ms_new_scr, p_scr, hn_scr, ycat_scr):
    @pl.when(j == 0)
    def _():
        sc_scr[...] = jnp.zeros_like(sc_scr)
        m_heads = jnp.concatenate([ms_ref[...], jnp.zeros((LANES - M_HEADS, LANES), _F32)], axis=0)
        ms_in_scr[...] = m_heads.T

    @pl.when((j >= 0) & (j < n_sample))
    def _():
        seqs = pl.ds(pl.multiple_of(j * SEQ_PER_STEP, SEQ_PER_STEP), SEQ_PER_STEP)
        _sample_block(xs_ref, w3_scr, wg_scr, wo_scr, ng_ref, fgain_ref, mhg_ref, cw_ref, big_ref, bfg_ref,
                      sconv_ref, ms_in_scr.at[seqs], ns_ref, cs_ref,
                      ys_ref, sconv_out_ref, cs_out_ref, ns_out_ref, ms_new_scr.at[seqs],
                      p_scr, hn_scr, sc_scr, ycat_scr)

    @pl.when(j == n_sample - 1)
    def _():
        ms_out_ref[...] = ms_new_scr[...].T[0:M_HEADS, :]


def _layer_body(wt_ref, xm_ref, wgt_ref, wout_ref, ng_ref, fgain_ref, mhg_ref, cw_ref, big_ref, bfg_ref,
                xs_ref, sconv_ref, ms_ref, ns_ref, cs_ref, x_ref,
                ys_ref, sconv_out_ref, cs_out_ref, ns_out_ref, ms_out_ref,
                y_ref, c_out_ref, n_out_ref, m_out_ref, conv_out_ref,
                w3_scr, wg_scr, wo_scr, xn_scr, caug0_scr, m0_scr, ut0_scr, sc_scr, ms_in_scr, ms_new_scr,
                out_scr, m_all_scr, p_scr, g_scr, hn_scr, ycat_scr, ubuf, caug, m_scr,
                *, n_prep, n_sample, tb, nt, n_blocks):
    t = pl.program_id(0)
    _prep_phase(t, wt_ref, xm_ref, wgt_ref, wout_ref, ng_ref, mhg_ref, cw_ref, big_ref, bfg_ref,
                w3_scr, wg_scr, wo_scr, xn_scr, caug0_scr, m0_scr, ut0_scr,
                p_scr, g_scr, hn_scr, ycat_scr, ubuf, caug, m_scr, n_prep=n_prep)
    _sample_phase(t - n_prep, n_sample, xs_ref, w3_scr, wg_scr, wo_scr, ng_ref, fgain_ref, mhg_ref, cw_ref,
                  big_ref, bfg_ref, sconv_ref, ms_ref, ns_ref, cs_ref,
                  ys_ref, sconv_out_ref, cs_out_ref, ns_out_ref, ms_out_ref,
                  sc_scr, ms_in_scr, ms_new_scr, p_scr, hn_scr, ycat_scr)
    _prompt_phase(t - n_prep - n_sample, x_ref, w3_scr, wg_scr, wo_scr, ng_ref, fgain_ref, mhg_ref, cw_ref,
                  big_ref, bfg_ref, caug0_scr, m0_scr, ut0_scr,
                  y_ref, c_out_ref, n_out_ref, m_out_ref, conv_out_ref,
                  out_scr, m_all_scr, p_scr, g_scr, hn_scr, ycat_scr, ubuf, caug, m_scr,
                  tb=tb, nt=nt, n_blocks=n_blocks)


def _layer_call(w_t, meta_tokens, w_out, small, fgain, xs, state_conv, m_heads, n_state, c_state, x):
    ng, mhg, cw, big, bfg = small
    n_prep = N_TILES // PREP_TILES
    n_wo = D_MODEL // WOUT_ROWS
    rows = xs.shape[0]
    nseq = c_state.shape[0]
    n_sample = rows // CHUNK
    nb, t_len, _ = x.shape
    tb = PROMPT_ROWS
    nt = t_len // tb
    n_blocks = nb * nt
    first_prompt = n_prep + n_sample
    assert n_wo <= n_prep and nb <= LANES
    assert m_heads.shape == (M_HEADS, LANES) and nseq == LANES

    def const(shape):
        nd = len(shape)
        return pl.BlockSpec(shape, lambda t: (0,) * nd)

    def sample_spec(*block):
        nd = len(block) - 1
        return pl.BlockSpec(block, lambda t: (jnp.clip(t - n_prep, 0, n_sample - 1),) + (0,) * nd)

    def block_of(t):
        return jnp.clip(t - first_prompt, 0, n_blocks - 1)

    def rows_spec(step_to_block):
        return pl.BlockSpec((1, tb, D_MODEL), lambda t: (step_to_block(t) // nt, step_to_block(t) % nt, 0))

    def state_spec(*tail):
        nd = len(tail)
        return pl.BlockSpec((1,) + tail, lambda t: (block_of(t) // nt,) + (0,) * nd)

    xs_spec = sample_spec(CHUNK, D_MODEL)
    sconv_spec = sample_spec(SEQ_PER_STEP, CONV_K - 1, CONV_WIDTH)
    cs_spec = sample_spec(SEQ_PER_STEP, M_HEADS, HEAD_DIM, HEAD_DIM)
    ns_spec = sample_spec(SEQ_PER_STEP, M_HEADS, HEAD_DIM)
    ms_spec = const(m_heads.shape)
    in_specs = [
        pl.BlockSpec((PREP_TILES * MXU_COLS, D_MODEL), lambda t: (jnp.minimum(t, n_prep - 1), 0)),
        const(meta_tokens.shape),
        pl.BlockSpec((N_GATES, D_MODEL), lambda t: (MAIN_WIDTH // N_GATES, 0)),
        pl.BlockSpec((WOUT_ROWS, D_MODEL), lambda t: (jnp.minimum(t, n_wo - 1), 0)),
        const(ng.shape), const(fgain.shape), const(mhg.shape), const(cw.shape),
        _SMEM_SPEC, _SMEM_SPEC,
        xs_spec, sconv_spec, ms_spec, ns_spec, cs_spec,
        rows_spec(block_of),
    ]
    out_shape = (
        jax.ShapeDtypeStruct((rows, D_MODEL), _F32),
        jax.ShapeDtypeStruct(state_conv.shape, _F32),
        jax.ShapeDtypeStruct(c_state.shape, _F32),
        jax.ShapeDtypeStruct(n_state.shape, _F32),
        jax.ShapeDtypeStruct(m_heads.shape, _F32),
        jax.ShapeDtypeStruct((nb, t_len, D_MODEL), _F32),
        jax.ShapeDtypeStruct((nb, M_HEADS, HEAD_DIM, HEAD_DIM), _F32),
        jax.ShapeDtypeStruct((nb, M_HEADS, HEAD_DIM), _F32),
        jax.ShapeDtypeStruct((M_HEADS, nb), _F32),
        jax.ShapeDtypeStruct((nb, CONV_K - 1, CONV_WIDTH), _F32),
    )
    out_specs = (
        xs_spec, sconv_spec, cs_spec, ns_spec, ms_spec,
        rows_spec(lambda t: block_of(t - 1)),
        state_spec(M_HEADS, HEAD_DIM, HEAD_DIM),
        state_spec(M_HEADS, HEAD_DIM),
        const((M_HEADS, nb)),
        state_spec(CONV_K - 1, CONV_WIDTH),
    )
    scratch = [
        pltpu.VMEM((N_TILES, D_MODEL, MXU_COLS), _BF),
        pltpu.VMEM((D_MODEL, LANES), _BF),
        pltpu.VMEM(w_out.shape, _BF),
        pltpu.VMEM((CHUNK, D_MODEL), _BF),
        pltpu.VMEM((M_HEADS, HEAD_DIM, 2 * LANES), _F32),
        pltpu.VMEM((1, LANES), _F32),
        pltpu.VMEM((SUBLANES, CONV_WIDTH), _F32),
        pltpu.VMEM((CHUNK, CONV_WIDTH), _F32),
        pltpu.VMEM((LANES, LANES), _F32),
        pltpu.VMEM((LANES, LANES), _F32),
        pltpu.VMEM((tb, D_MODEL), _F32),
        pltpu.VMEM((LANES, LANES), _F32),
    ] + _sequence_scratch(tb)
    return pl.pallas_call(
        functools.partial(_layer_body, n_prep=n_prep, n_sample=n_sample, tb=tb, nt=nt, n_blocks=n_blocks),
        grid=(first_prompt + n_blocks + 1,),
        in_specs=in_specs,
        out_specs=out_specs,
        out_shape=out_shape,
        scratch_shapes=scratch,
        compiler_params=pltpu.CompilerParams(
            dimension_semantics=("arbitrary",), vmem_limit_bytes=VMEM_REQUEST_BYTES),
        name="decoder_layer",
    )(w_t, meta_tokens, w_t, w_out, ng, fgain, mhg, cw, big, bfg, xs, state_conv, m_heads, n_state, c_state, x)


def _prompt_phase(s, x_ref, w3_ref, wg_ref, wo_ref, ng_ref, fgain_ref, mhg_ref, cw_ref, big_ref, bfg_ref,
                  caug0_ref, m0_ref, ut0_ref,
                  y_ref, c_out_ref, n_out_ref, m_out_ref, conv_out_ref,
                  out_scr, m_all_scr, p_scr, g_scr, hn_scr, ycat_scr, ubuf, caug, m_scr, *, tb, nt, n_blocks):
    live = (s >= 0) & (s < n_blocks)
    i = jnp.maximum(s, 0) % nt

    @pl.when(s == 0)
    def _():
        out_scr[...] = jnp.zeros_like(out_scr)
        m_all_scr[...] = jnp.zeros_like(m_all_scr)

    @pl.when(live & (i == 0))
    def _():
        caug[...] = caug0_ref[...]
        m_scr[...] = m0_ref[...]
        ubuf[0:SUBLANES, :] = ut0_ref[...]

    @pl.when(live)
    def _():
        xn = _rms_rows(x_ref[0], ng_ref[...]).astype(_BF)

        def project_tile(t):
            p_scr[0:tb, _tile_cols(t)] = _dot(xn, w3_ref[t])

        def norm_previous(r0, r1):
            def fn():
                y_ref[0, r0:r1, :] = _rms_rows(out_scr[r0:r1, :], fgain_ref[...])
            return fn

        q4 = tb // 4
        side_work = {2 + 2 * j: [norm_previous(j * q4, (j + 1) * q4)] for j in range(4)}
        _sequence_block(xn, project_tile, p_scr, g_scr, hn_scr, ycat_scr, ubuf, caug, m_scr,
                        wg_ref, mhg_ref, cw_ref, big_ref, bfg_ref, tb=tb, n_pad=0, side_work=side_work)
        out_scr[...] = _dot(ycat_scr[...], wo_ref[...]) + x_ref[0]

    @pl.when(s == n_blocks)
    def _():
        y_ref[0] = _rms_rows(out_scr[...], fgain_ref[...])
        m_out_ref[...] = m_all_scr[...].T[0:M_HEADS, 0:n_blocks // nt]

    @pl.when(live & (i == nt - 1))
    def _():
        m_all_scr[pl.ds(s // nt, 1), :] = m_scr[...]
        conv_out_ref[0] = ubuf[SUBLANES - (CONV_K - 1):SUBLANES, :]
        for h in range(M_HEADS):
            c_out_ref[0, h] = caug[h, :, 0:HEAD_DIM]
            n_out_ref[0, h:h + 1, :] = caug[h, :, LANES:2 * LANES].T[0:1, :]


def _sample_block(x_ref, w3_ref, wg_ref, wo_ref, ng_ref, fgain_ref, mhg_ref, cw_ref, big_ref, bfg_ref,
                  sconv_ref, m_ref, n_ref, c_ref,
                  y_ref, conv_out_ref, c_out_ref, n_out_ref, m_out_ref,
                  p_scr, ni_scr, sc_scr, ycat_scr):
    x = x_ref[...]
    xn = _rms_rows(x, ng_ref[...]).astype(_BF)
    g_raw = _dot(xn, wg_ref[...])
    for t in QKV_TILES:
        p_scr[0:CHUNK, _tile_cols(t)] = _dot(xn, w3_ref[t])

    row = lax.broadcasted_iota(jnp.int32, (CHUNK, CHUNK), 0)
    col = lax.broadcasted_iota(jnp.int32, (CHUNK, CHUNK), 1)
    pos = row % DEC_SEQ
    grp = row // DEC_SEQ
    causal = (grp == (col // DEC_SEQ)) & (col <= row)
    grow = lax.broadcasted_iota(jnp.int32, (SEQ_PER_STEP, CHUNK), 0)
    gcol = lax.broadcasted_iota(jnp.int32, (SEQ_PER_STEP, CHUNK), 1)
    pick_last = (gcol == grow * DEC_SEQ + (DEC_SEQ - 1)).astype(_F32)
    seq_sum = (gcol // DEC_SEQ == grow).astype(_BF)
    erow = lax.broadcasted_iota(jnp.int32, (CHUNK, SEQ_PER_STEP), 0)
    ecol = lax.broadcasted_iota(jnp.int32, (CHUNK, SEQ_PER_STEP), 1)
    own_seq = erow // DEC_SEQ == ecol
    ones_blk = jnp.ones((CHUNK, LANES), _BF)

    def rows_from_seqs(v16):
        return jnp.broadcast_to(v16[:, None, :], (SEQ_PER_STEP, DEC_SEQ, LANES)).reshape(CHUNK, LANES)

    def last_row_of_seq(v):
        v3 = v.reshape(SEQ_PER_STEP, DEC_SEQ, LANES)
        return jnp.broadcast_to(v3[:, DEC_SEQ - 1:DEC_SEQ, :], v3.shape).reshape(CHUNK, LANES)

    ig, lf = _gate_columns(g_raw, big_ref, bfg_ref)
    b = _cumsum_rows(lf, pos, DEC_SEQ)
    a = ig - b
    m_prev = rows_from_seqs(m_ref[...])
    big_m = jnp.maximum(m_prev, _cummax_rows(a, pos, DEC_SEQ))
    m_t = b + big_m
    inter = jnp.exp(m_prev - big_m)
    emt = jnp.exp(-m_t)
    m_last = last_row_of_seq(big_m)
    w = jnp.exp(a - m_last)
    decay = jnp.exp(m_prev - m_last)
    m_out_ref[...] = _dot_exact(pick_last, m_t)
    dec16 = _dot_exact(pick_last, decay)
    a_t = a.T

    for t in REST_TILES:
        p_scr[0:CHUNK, _tile_cols(t)] = _dot(xn, w3_ref[t])

    heads = []
    for h in range(M_HEADS):
        hs = h * HEAD_DIM
        q = p_scr[0:CHUNK, OFF_Q + hs:OFF_Q + hs + HEAD_DIM]
        k = p_scr[0:CHUNK, OFF_K + hs:OFF_K + hs + HEAD_DIM] * K_SCALE
        v = p_scr[0:CHUNK, OFF_V + hs:OFF_V + hs + HEAD_DIM]
        q_bf = q.astype(_BF)
        a_b = jnp.broadcast_to(a_t[h:h + 1, :], (CHUNK, CHUNK))
        m_b = jnp.broadcast_to(big_m[:, h:h + 1], (CHUNK, CHUNK))
        d = jnp.exp(jnp.where(causal, a_b - m_b, NEG_INF))
        w_b = jnp.broadcast_to(w[:, h:h + 1], (CHUNK, LANES))
        heads.append(dict(q_bf=q_bf, k=k, v=v, d=d, w_b=w_b, qk=_dot_nt(q_bf, k.astype(_BF))))

    u = p_scr[0:CHUNK, OFF_GC:OFF_GC + CONV_WIDTH] * p_scr[0:CHUNK, OFF_HC:OFF_HC + CONV_WIDTH]
    for g in range(SEQ_PER_STEP):
        sc_scr[g * DEC_SEQ:g * DEC_SEQ + CONV_K - 1, :] = sconv_ref[g]
        conv_out_ref[g] = u[(g + 1) * DEC_SEQ - (CONV_K - 1):(g + 1) * DEC_SEQ, :]
    sc = sc_scr[...]
    posw = lax.broadcasted_iota(jnp.int32, (CHUNK, CONV_WIDTH), 0) % DEC_SEQ
    u_m1 = jnp.where(posw == 0, pltpu.roll(sc, CHUNK - 1, 0), pltpu.roll(u, 1, 0))
    u_m2 = jnp.where(posw < 2, sc, pltpu.roll(u, 2, 0))
    y_conv = _conv_out(p_scr[0:CHUNK, OFF_GB:OFF_GB + CONV_WIDTH], p_scr[0:CHUNK, OFF_ZC:OFF_ZC + CONV_WIDTH],
                       u, u_m1, u_m2, cw_ref)
    ycat_scr[0:CHUNK, 0:CONV_WIDTH] = y_conv.astype(_BF)

    for h in range(M_HEADS):
        hs = h * HEAD_DIM
        st = heads[h]
        wv = st["w_b"] * st["v"]
        k_t_bf = st["k"].T.astype(_BF)
        dec_b = jnp.broadcast_to(decay[:, h:h + 1], (CHUNK, HEAD_DIM))

        for g in range(0, SEQ_PER_STEP, 2):
            pair_rows = slice(g * DEC_SEQ, (g + 2) * DEC_SEQ)
            c0, c1 = c_ref[g, h], c_ref[g + 1, h]
            qc = _dot(st["q_bf"][pair_rows, :], jnp.concatenate([c0.astype(_BF), c1.astype(_BF)], axis=1))
            ni_scr[g * DEC_SEQ:(g + 1) * DEC_SEQ, hs:hs + HEAD_DIM] = qc[0:DEC_SEQ, 0:HEAD_DIM]
            ni_scr[(g + 1) * DEC_SEQ:(g + 2) * DEC_SEQ, hs:hs + HEAD_DIM] = qc[DEC_SEQ:, HEAD_DIM:]
            wv_pair = jnp.concatenate([jnp.where(grp == g, wv, 0.0).astype(_BF),
                                       jnp.where(grp == g + 1, wv, 0.0).astype(_BF)], axis=1)
            upd = _dot(k_t_bf, wv_pair)
            dec0 = jnp.broadcast_to(dec_b[g * DEC_SEQ:g * DEC_SEQ + 1, :], (HEAD_DIM, HEAD_DIM))
            dec1 = jnp.broadcast_to(dec_b[(g + 1) * DEC_SEQ:(g + 1) * DEC_SEQ + 1, :], (HEAD_DIM, HEAD_DIM))
            c_out_ref[g, h] = dec0 * c0 + upd[:, 0:HEAD_DIM]
            c_out_ref[g + 1, h] = dec1 * c1 + upd[:, HEAD_DIM:]

        s = st["qk"] * st["d"]
        sv = _dot(s.astype(_BF), jnp.concatenate([st["v"].astype(_BF), ones_blk], axis=1))
        n_old = n_ref[:, h, :]
        num_inter = ni_scr[0:CHUNK, hs:hs + HEAD_DIM]
        qn = _dot_nt(st["q_bf"], n_old.astype(_BF))
        den_inter = jnp.sum(jnp.where(own_seq, qn, 0.0), axis=-1, keepdims=True)
        inter_b = jnp.broadcast_to(inter[:, h:h + 1], (CHUNK, LANES))
        num = sv[:, :LANES] + inter_b * num_inter
        den = sv[:, LANES:] + inter_b * den_inter
        emt_b = jnp.broadcast_to(emt[:, h:h + 1], (CHUNK, LANES))
        ni_scr[0:CHUNK, hs:hs + HEAD_DIM] = _head_norm(num, den, emt_b, mhg_ref[:, hs:hs + HEAD_DIM])
        n_upd = _dot(seq_sum, (st["w_b"] * st["k"]).astype(_BF))
        dec16_b = jnp.broadcast_to(dec16[:, h:h + 1], (SEQ_PER_STEP, HEAD_DIM))
        n_out_ref[:, h, :] = dec16_b * n_old + n_upd

    ycat_scr[0:CHUNK, CONV_WIDTH:CONV_WIDTH + M_WIDTH] = _gated_heads(ni_scr[0:CHUNK, :], p_scr, CHUNK).astype(_BF)
    out = _dot(ycat_scr[0:CHUNK, :], wo_ref[...]) + x
    y_ref[...] = _rms_rows(out, fgain_ref[...])


def kernel(x_prompt, x_sample, state_conv, state_mlstm_c, state_mlstm_n, state_mlstm_m, meta_tokens, norm_gain,
           w_in, conv_w, b_igate, b_fgate, mh_norm_gain, w_out, final_norm_gain):
    nseq, dec_seq, _ = x_sample.shape
    assert dec_seq == DEC_SEQ and meta_tokens.shape[0] == N_META
    assert w_in.shape == (D_MODEL, MAIN_WIDTH + N_GATES) and x_prompt.shape[1] % PROMPT_ROWS == 0
    assert (nseq * DEC_SEQ) % CHUNK == 0

    w_t = w_in.T
    small = (norm_gain.reshape(1, D_MODEL), mh_norm_gain.reshape(1, M_WIDTH), conv_w,
             b_igate.astype(_F32), b_fgate.astype(_F32))
    fgain = final_norm_gain.reshape(1, D_MODEL)

    xs = x_sample.reshape(nseq * DEC_SEQ, D_MODEL)
    ys, conv_s, c_s, n_s, m_s, y_prompt, c_p, n_p, m_p, conv_p = _layer_call(
        w_t, meta_tokens.astype(_F32), w_out.astype(_F32), small, fgain, xs, state_conv, state_mlstm_m.T,
        state_mlstm_n, state_mlstm_c, x_prompt)
    y_sample = ys.reshape(nseq, DEC_SEQ, D_MODEL)

    return (y_prompt, y_sample, conv_p, c_p, n_p, m_p.T, conv_s, c_s, n_s, m_s.T)
```

```python
import functools

import jax
import jax.numpy as jnp
from jax import lax
from jax.experimental import pallas as pl
from jax.experimental.pallas import tpu as pltpu

D_MODEL = 1024
CONV_WIDTH = 512
M_WIDTH = 512
M_HEADS = 4
HEAD_DIM = 128
N_META = 16
CONV_K = 3
EPS = 1e-6
MAIN_WIDTH = 4 * CONV_WIDTH + 5 * M_WIDTH
N_GATES = 2 * M_HEADS
LANES = 128
SUBLANES = 8
MXU_COLS = 256
N_TILES = MAIN_WIDTH // MXU_COLS
CHUNK = 128
DEC_SEQ = 8
SEQ_PER_STEP = CHUNK // DEC_SEQ
NEG_INF = float("-inf")
K_SCALE = HEAD_DIM ** -0.5

PROMPT_ROWS = 512
PREP_TILES = 3
WOUT_ROWS = 256
VMEM_REQUEST_BYTES = 56 * 1024 * 1024
EARLY_TILES = 2
UNIT_LAG = 3

OFF_GB, OFF_GC, OFF_HC, OFF_ZC = 0, 512, 1024, 1536
OFF_Q, OFF_K, OFF_V, OFF_O, OFF_ZM = 2048, 2560, 3072, 3584, 4096
QKV_TILES = tuple(range(OFF_Q // MXU_COLS, OFF_O // MXU_COLS))
REST_TILES = tuple(t for t in range(N_TILES) if t not in QKV_TILES)

_SMEM_SPEC = pl.BlockSpec(memory_space=pltpu.SMEM)
_HI = lax.Precision.HIGHEST
_BF = jnp.bfloat16
_F32 = jnp.float32


def _dot(a, b):
    return jnp.dot(a, b, preferred_element_type=_F32)


def _dot_exact(a, b):
    return jnp.dot(a, b, precision=_HI, preferred_element_type=_F32)


def _dot_nt(a, b):
    return lax.dot_general(a, b, (((1,), (1,)), ((), ())), preferred_element_type=_F32)


def _rms_rows(x, gain_row):
    return x * lax.rsqrt(jnp.mean(x * x, axis=-1, keepdims=True) + EPS) * gain_row


def _sigmoid(x):
    return 1.0 / (1.0 + jnp.exp(-x))


def _silu(x):
    return x * _sigmoid(x)


def _tile_cols(t):
    return slice(t * MXU_COLS, (t + 1) * MXU_COLS)


def _cummax_rows(x, pos, period):
    s = 1
    while s < period:
        shifted = pltpu.roll(x, s, 0)
        x = jnp.maximum(x, jnp.where(pos >= s, shifted, NEG_INF))
        s *= 2
    return x


def _cumsum_rows(x, pos, period):
    s = 1
    while s < period:
        x = x + jnp.where(pos >= s, pltpu.roll(x, s, 0), 0.0)
        s *= 2
    return x


def _bias_row(b_ref):
    lane = lax.broadcasted_iota(jnp.int32, (1, LANES), 1)
    row = jnp.zeros((1, LANES), _F32)
    for h in range(M_HEADS):
        row = jnp.where(lane == h, b_ref[h], row)
    return row


def _gate_columns(g_raw, big_ref, bfg_ref):
    ig = g_raw + _bias_row(big_ref)
    lf = jax.nn.log_sigmoid(pltpu.roll(g_raw, LANES - M_HEADS, 1) + _bias_row(bfg_ref))
    return ig, lf


def _head_norm(num, den, emt_b, gain_row):
    hh = num / jnp.maximum(jnp.abs(den), emt_b)
    return hh * lax.rsqrt(jnp.mean(hh * hh, axis=-1, keepdims=True) + EPS) * gain_row


def _conv_out(gb, zc, u, u_m1, u_m2, cw_ref):
    yc = cw_ref[0:1, :] * u_m2 + cw_ref[1:2, :] * u_m1 + cw_ref[2:3, :] * u
    return gb * yc * _silu(zc)


def _gated_heads(hn, p_scr):
    return hn * _sigmoid(p_scr[:, OFF_O:OFF_O + M_WIDTH]) * _silu(p_scr[:, OFF_ZM:OFF_ZM + M_WIDTH])


def _sequence_block(xn, project_tile, p_scr, g_scr, hn_scr, ycat_scr, ubuf, caug, m_scr,
                    wg_ref, mhg_ref, cw_ref, big_ref, bfg_ref, *, tb, n_pad, side_work=None):
    side_work = side_work or {}
    g_scr[...] = _dot(xn, wg_ref[...])
    late_tiles = []
    if project_tile is not None:
        for t in QKV_TILES + REST_TILES[:EARLY_TILES]:
            project_tile(t)
        late_tiles = list(REST_TILES[EARLY_TILES:])

    row = lax.broadcasted_iota(jnp.int32, (CHUNK, CHUNK), 0)
    col = lax.broadcasted_iota(jnp.int32, (CHUNK, CHUNK), 1)
    causal = col <= row
    ones_blk = jnp.ones((CHUNK, LANES), _BF)

    n_chunks = tb // CHUNK
    gates = []
    m_prev = m_scr[...]
    for c in range(n_chunks):
        rows = slice(c * CHUNK, (c + 1) * CHUNK)
        ig, lf = _gate_columns(g_scr[rows, :], big_ref, bfg_ref)
        if n_pad:
            ig = jnp.where(row >= n_pad, ig, NEG_INF)
            lf = jnp.where(row >= n_pad, lf, 0.0)
        b = _cumsum_rows(lf, row, CHUNK)
        a = ig - b
        big_m = jnp.maximum(m_prev, _cummax_rows(a, row, CHUNK))
        m_t = b + big_m
        m_last = big_m[CHUNK - 1:CHUNK, :]
        gates.append(dict(a_t=a.T, big_m=big_m, inter=jnp.exp(m_prev - big_m), emt=jnp.exp(-m_t),
                          w=jnp.exp(a - m_last), decay=jnp.exp(m_prev - m_last)))
        m_prev = m_t[CHUNK - 1:CHUNK, :]
    m_scr[...] = m_prev

    def stage_a(c, h):
        rows = slice(c * CHUNK, (c + 1) * CHUNK)
        hs = h * HEAD_DIM
        g = gates[c]
        q2 = p_scr[rows, OFF_Q + hs:OFF_Q + hs + 2 * HEAD_DIM]
        k2 = p_scr[rows, OFF_K + hs:OFF_K + hs + 2 * HEAD_DIM] * K_SCALE
        k2_bf = k2.astype(_BF)
        zero = jnp.zeros((CHUNK, HEAD_DIM), _BF)
        k_diag = jnp.concatenate([jnp.concatenate([k2_bf[:, :HEAD_DIM], zero], axis=1),
                                  jnp.concatenate([zero, k2_bf[:, HEAD_DIM:]], axis=1)], axis=0)
        qk2 = _dot_nt(q2.astype(_BF), k_diag)
        out = []
        for j in range(2):
            cols = slice(j * HEAD_DIM, (j + 1) * HEAD_DIM)
            a_b = jnp.broadcast_to(g["a_t"][h + j:h + j + 1, :], (CHUNK, CHUNK))
            m_b = jnp.broadcast_to(g["big_m"][:, h + j:h + j + 1], (CHUNK, CHUNK))
            d = jnp.exp(jnp.where(causal, a_b - m_b, NEG_INF))
            inter_b = jnp.broadcast_to(g["inter"][:, h + j:h + j + 1], (CHUNK, HEAD_DIM))
            out.append(dict(qk=qk2[:, cols], d=d, q_inter=(inter_b * q2[:, cols]).astype(_BF), k=k2[:, cols]))
        return out

    def stage_b(c, h, st):
        rows = slice(c * CHUNK, (c + 1) * CHUNK)
        hs = h * HEAD_DIM
        g = gates[c]
        v = p_scr[rows, OFF_V + hs:OFF_V + hs + HEAD_DIM]
        s = st["qk"] * st["d"]
        lhs = jnp.concatenate([s.astype(_BF), st["q_inter"]], axis=1)
        rhs = jnp.concatenate([jnp.concatenate([v.astype(_BF), ones_blk], axis=1), caug[h].astype(_BF)], axis=0)
        tot = _dot(lhs, rhs)
        emt_b = jnp.broadcast_to(g["emt"][:, h:h + 1], (CHUNK, LANES))
        hn_scr[rows, hs:hs + HEAD_DIM] = _head_norm(tot[:, :LANES], tot[:, LANES:], emt_b,
                                                    mhg_ref[:, hs:hs + HEAD_DIM])
        w_b = jnp.broadcast_to(g["w"][:, h:h + 1], (CHUNK, LANES))
        wv = jnp.concatenate([(w_b * v).astype(_BF), w_b.astype(_BF)], axis=1)
        upd = _dot(st["k"].T.astype(_BF), wv)
        decay_b = jnp.broadcast_to(g["decay"][:, h:h + 1], (HEAD_DIM, 2 * LANES))
        caug[h] = decay_b * caug[h] + upd

    units = [(c, h) for c in range(n_chunks) for h in range(M_HEADS)]
    lag = min(UNIT_LAG, len(units))
    n_slots = len(units) + lag
    assert len(late_tiles) <= n_slots
    assert all(0 <= slot < n_slots for slot in side_work)
    pending = {}
    for t in range(n_slots):
        if t < len(units) and t % 2 == 0:
            pending[t], pending[t + 1] = stage_a(*units[t])
        if t < len(late_tiles):
            project_tile(late_tiles[t])
        for fn in side_work.get(t, ()):
            fn()
        if t >= lag:
            stage_b(*units[t - lag], pending.pop(t - lag))

    u = p_scr[:, OFF_GC:OFF_GC + CONV_WIDTH] * p_scr[:, OFF_HC:OFF_HC + CONV_WIDTH]
    ubuf[SUBLANES:SUBLANES + tb, :] = u
    u_m1 = ubuf[SUBLANES - 1:SUBLANES - 1 + tb, :]
    u_m2 = ubuf[SUBLANES - 2:SUBLANES - 2 + tb, :]
    y_conv = _conv_out(p_scr[:, OFF_GB:OFF_GB + CONV_WIDTH], p_scr[:, OFF_ZC:OFF_ZC + CONV_WIDTH],
                       u, u_m1, u_m2, cw_ref)
    ycat_scr[:, 0:CONV_WIDTH] = y_conv.astype(_BF)
    ycat_scr[:, CONV_WIDTH:CONV_WIDTH + M_WIDTH] = _gated_heads(hn_scr[...], p_scr).astype(_BF)
    ubuf[0:SUBLANES, :] = ubuf[tb:tb + SUBLANES, :]


def _sequence_scratch(tb):
    return [
        pltpu.VMEM((tb, MAIN_WIDTH), _F32),
        pltpu.VMEM((tb, LANES), _F32),
        pltpu.VMEM((tb, M_WIDTH), _F32),
        pltpu.VMEM((tb, D_MODEL), _BF),
        pltpu.VMEM((tb + 2 * SUBLANES, CONV_WIDTH), _F32),
        pltpu.VMEM((M_HEADS, HEAD_DIM, 2 * LANES), _F32),
        pltpu.VMEM((1, LANES), _F32),
    ]


def _front_body(wt_ref, x_ref, wgt_ref, wout_ref, ng_ref, fgain_ref, mhg_ref, cw_ref, big_ref, bfg_ref,
                xs_ref, sconv_ref, ms_ref, ns_ref, cs_ref,
                w3_ref, wg_ref, wo_ref, caug_out_ref, m_out_ref, ut_out_ref,
                ys_ref, sconv_out_ref, cs_out_ref, ns_out_ref, ms_out_ref,
                w3_scr, xn_scr, p3_scr, sc_scr, ms_in_scr, ms_new_scr,
                p_scr, g_scr, hn_scr, ycat_scr, ubuf, caug, m_scr):
    t = pl.program_id(0)
    n_prep = N_TILES // PREP_TILES
    last = pl.num_programs(0) - 1

    @pl.when(t == 0)
    def _():
        xn_scr[...] = jnp.zeros_like(xn_scr)
        xn_scr[CHUNK - N_META:CHUNK, :] = _rms_rows(x_ref[...], ng_ref[...]).astype(_BF)
        gate_rows = jnp.concatenate([wgt_ref[...], jnp.zeros((LANES - N_GATES, D_MODEL), _F32)], axis=0)
        wg_ref[...] = gate_rows.T.astype(_BF)
        sc_scr[...] = jnp.zeros_like(sc_scr)
        m_heads = jnp.concatenate([ms_ref[...], jnp.zeros((LANES - M_HEADS, LANES), _F32)], axis=0)
        ms_in_scr[...] = m_heads.T

    @pl.when(t < D_MODEL // WOUT_ROWS)
    def _():
        wo_ref[pl.ds(pl.multiple_of(t * WOUT_ROWS, WOUT_ROWS), WOUT_ROWS), :] = wout_ref[...].astype(_BF)

    @pl.when(t < n_prep)
    def _():
        for j in range(PREP_TILES):
            tile = wt_ref[j * MXU_COLS:(j + 1) * MXU_COLS, :].T.astype(_BF)
            w3_ref[j] = tile
            w3_scr[t * PREP_TILES + j] = tile
            p3_scr[t * PREP_TILES + j] = _dot(xn_scr[...], tile)

    @pl.when(t == n_prep - 1)
    def _():
        for j in range(N_TILES):
            p_scr[:, _tile_cols(j)] = p3_scr[j]
        caug[...] = jnp.zeros_like(caug)
        m_scr[...] = jnp.zeros_like(m_scr)
        ubuf[0:SUBLANES, :] = jnp.zeros((SUBLANES, CONV_WIDTH), _F32)
        _sequence_block(xn_scr[...], None, p_scr, g_scr, hn_scr, ycat_scr, ubuf, caug, m_scr,
                        wg_ref, mhg_ref, cw_ref, big_ref, bfg_ref, tb=CHUNK, n_pad=CHUNK - N_META)
        caug_out_ref[...] = caug[...]
        m_out_ref[...] = m_scr[...]
        ut_out_ref[...] = ubuf[0:SUBLANES, :]

    @pl.when(t >= n_prep)
    def _():
        seqs = pl.ds(pl.multiple_of((t - n_prep) * SEQ_PER_STEP, SEQ_PER_STEP), SEQ_PER_STEP)
        _sample_block(xs_ref, w3_scr, wg_ref, wo_ref, ng_ref, fgain_ref, mhg_ref, cw_ref, big_ref, bfg_ref,
                      sconv_ref, ms_in_scr.at[seqs], ns_ref, cs_ref,
                      ys_ref, sconv_out_ref, cs_out_ref, ns_out_ref, ms_new_scr.at[seqs],
                      p_scr, hn_scr, sc_scr, ycat_scr)

    @pl.when(t == last)
    def _():
        ms_out_ref[...] = ms_new_scr[...].T[0:M_HEADS, :]


def _front_call(w_t, meta_tokens, w_out, small, fgain, xs, state_conv, m_heads, n_state, c_state):
    ng, mhg, cw, big, bfg = small
    n_prep = N_TILES // PREP_TILES
    rows = xs.shape[0]
    nseq = c_state.shape[0]
    n_sample = rows // CHUNK
    assert m_heads.shape == (M_HEADS, LANES) and nseq == LANES
    assert D_MODEL // WOUT_ROWS <= n_prep

    def const(shape):
        nd = len(shape)
        return pl.BlockSpec(shape, lambda t: (0,) * nd)

    def sample_spec(*block):
        nd = len(block) - 1
        return pl.BlockSpec(block, lambda t: (jnp.maximum(t - n_prep, 0),) + (0,) * nd)

    def prep_idx(t):
        return jnp.minimum(t, n_prep - 1)

    xs_spec = sample_spec(CHUNK, D_MODEL)
    sconv_spec = sample_spec(SEQ_PER_STEP, CONV_K - 1, CONV_WIDTH)
    cs_spec = sample_spec(SEQ_PER_STEP, M_HEADS, HEAD_DIM, HEAD_DIM)
    ns_spec = sample_spec(SEQ_PER_STEP, M_HEADS, HEAD_DIM)
    ms_spec = const(m_heads.shape)
    in_specs = [
        pl.BlockSpec((PREP_TILES * MXU_COLS, D_MODEL), lambda t: (prep_idx(t), 0)),
        const(meta_tokens.shape),
        pl.BlockSpec((N_GATES, D_MODEL), lambda t: (MAIN_WIDTH // N_GATES, 0)),
        pl.BlockSpec((WOUT_ROWS, D_MODEL), lambda t: (jnp.minimum(t, D_MODEL // WOUT_ROWS - 1), 0)),
        const(ng.shape), const(fgain.shape), const(mhg.shape), const(cw.shape),
        _SMEM_SPEC, _SMEM_SPEC,
        xs_spec, sconv_spec, ms_spec, ns_spec, cs_spec,
    ]
    out_shape = (
        jax.ShapeDtypeStruct((N_TILES, D_MODEL, MXU_COLS), _BF),
        jax.ShapeDtypeStruct((D_MODEL, LANES), _BF),
        jax.ShapeDtypeStruct(w_out.shape, _BF),
        jax.ShapeDtypeStruct((M_HEADS, HEAD_DIM, 2 * LANES), _F32),
        jax.ShapeDtypeStruct((1, LANES), _F32),
        jax.ShapeDtypeStruct((SUBLANES, CONV_WIDTH), _F32),
        jax.ShapeDtypeStruct((rows, D_MODEL), _F32),
        jax.ShapeDtypeStruct(state_conv.shape, _F32),
        jax.ShapeDtypeStruct(c_state.shape, _F32),
        jax.ShapeDtypeStruct(n_state.shape, _F32),
        jax.ShapeDtypeStruct(m_heads.shape, _F32),
    )
    out_specs = (
        pl.BlockSpec((PREP_TILES, D_MODEL, MXU_COLS), lambda t: (prep_idx(t), 0, 0)),
        const(out_shape[1].shape), const(out_shape[2].shape), const(out_shape[3].shape), const(out_shape[4].shape),
        const(out_shape[5].shape),
        xs_spec, sconv_spec, cs_spec, ns_spec, ms_spec,
    )
    scratch = [
        pltpu.VMEM((N_TILES, D_MODEL, MXU_COLS), _BF),
        pltpu.VMEM((CHUNK, D_MODEL), _BF),
        pltpu.VMEM((N_TILES, CHUNK, MXU_COLS), _F32),
        pltpu.VMEM((CHUNK, CONV_WIDTH), _F32),
        pltpu.VMEM((LANES, LANES), _F32),
        pltpu.VMEM((LANES, LANES), _F32),
    ] + _sequence_scratch(CHUNK)
    return pl.pallas_call(
        _front_body,
        grid=(n_prep + n_sample,),
        in_specs=in_specs,
        out_specs=out_specs,
        out_shape=out_shape,
        scratch_shapes=scratch,
        compiler_params=pltpu.CompilerParams(
            dimension_semantics=("arbitrary",), vmem_limit_bytes=VMEM_REQUEST_BYTES),
        name="prep_meta_sample",
    )(w_t, meta_tokens, w_t, w_out, ng, fgain, mhg, cw, big, bfg, xs, state_conv, m_heads, n_state, c_state)


def _prompt_body(x_ref, w3_ref, wg_ref, wo_ref, ng_ref, fgain_ref, mhg_ref, cw_ref, big_ref, bfg_ref,
                 caug0_ref, m0_ref, ut0_ref,
                 y_ref, c_out_ref, n_out_ref, m_out_ref, conv_out_ref,
                 out_scr, m_all_scr, p_scr, g_scr, hn_scr, ycat_scr, ubuf, caug, m_scr, *, tb, nt, n_blocks):
    s = pl.program_id(0)
    i = s % nt
    live = s < n_blocks

    @pl.when(s == 0)
    def _():
        out_scr[...] = jnp.zeros_like(out_scr)
        m_all_scr[...] = jnp.zeros_like(m_all_scr)

    @pl.when(live & (i == 0))
    def _():
        caug[...] = caug0_ref[...]
        m_scr[...] = m0_ref[...]
        ubuf[0:SUBLANES, :] = ut0_ref[...]

    @pl.when(live)
    def _():
        xn = _rms_rows(x_ref[0], ng_ref[...]).astype(_BF)

        def project_tile(t):
            p_scr[:, _tile_cols(t)] = _dot(xn, w3_ref[t])

        def norm_previous(r0, r1):
            def fn():
                y_ref[0, r0:r1, :] = _rms_rows(out_scr[r0:r1, :], fgain_ref[...])
            return fn

        q4 = tb // 4
        side_work = {2 + 2 * j: [norm_previous(j * q4, (j + 1) * q4)] for j in range(4)}
        _sequence_block(xn, project_tile, p_scr, g_scr, hn_scr, ycat_scr, ubuf, caug, m_scr,
                        wg_ref, mhg_ref, cw_ref, big_ref, bfg_ref, tb=tb, n_pad=0, side_work=side_work)
        out_scr[...] = _dot(ycat_scr[...], wo_ref[...]) + x_ref[0]

    @pl.when(s == n_blocks)
    def _():
        y_ref[0] = _rms_rows(out_scr[...], fgain_ref[...])
        m_out_ref[...] = m_all_scr[...].T[0:M_HEADS, 0:n_blocks // nt]

    @pl.when(live & (i == nt - 1))
    def _():
        m_all_scr[pl.ds(s // nt, 1), :] = m_scr[...]
        conv_out_ref[0] = ubuf[SUBLANES - (CONV_K - 1):SUBLANES, :]
        for h in range(M_HEADS):
            c_out_ref[0, h] = caug[h, :, 0:HEAD_DIM]
            n_out_ref[0, h:h + 1, :] = caug[h, :, LANES:2 * LANES].T[0:1, :]


def _prompt_call(x, w3, wg, wo, small, fgain, caug0, m0, ut0):
    ng, mhg, cw, big, bfg = small
    nb, t, _ = x.shape
    tb = PROMPT_ROWS
    nt = t // tb
    n_blocks = nb * nt

    def const(shape):
        nd = len(shape)
        return pl.BlockSpec(shape, lambda s: (0,) * nd)

    def block_of(s):
        return jnp.minimum(s, n_blocks - 1)

    def rows_spec(step_to_block):
        return pl.BlockSpec((1, tb, D_MODEL), lambda s: (step_to_block(s) // nt, step_to_block(s) % nt, 0))

    def state_spec(*tail):
        nd = len(tail)
        return pl.BlockSpec((1,) + tail, lambda s: (block_of(s) // nt,) + (0,) * nd)

    in_specs = [
        rows_spec(block_of),
        const(w3.shape), const(wg.shape), const(wo.shape), const(ng.shape), const(fgain.shape),
        const(mhg.shape), const(cw.shape), _SMEM_SPEC, _SMEM_SPEC,
        const(caug0.shape), const(m0.shape), const(ut0.shape),
    ]
    out_shape = (
        jax.ShapeDtypeStruct((nb, t, D_MODEL), _F32),
        jax.ShapeDtypeStruct((nb, M_HEADS, HEAD_DIM, HEAD_DIM), _F32),
        jax.ShapeDtypeStruct((nb, M_HEADS, HEAD_DIM), _F32),
        jax.ShapeDtypeStruct((M_HEADS, nb), _F32),
        jax.ShapeDtypeStruct((nb, CONV_K - 1, CONV_WIDTH), _F32),
    )
    out_specs = (
        rows_spec(lambda s: jnp.maximum(s - 1, 0)),
        state_spec(M_HEADS, HEAD_DIM, HEAD_DIM),
        state_spec(M_HEADS, HEAD_DIM),
        const((M_HEADS, nb)),
        state_spec(CONV_K - 1, CONV_WIDTH),
    )
    assert nb <= LANES
    scratch = [
        pltpu.VMEM((tb, D_MODEL), _F32),
        pltpu.VMEM((LANES, LANES), _F32),
    ] + _sequence_scratch(tb)
    return pl.pallas_call(
        functools.partial(_prompt_body, tb=tb, nt=nt, n_blocks=n_blocks),
        grid=(n_blocks + 1,),
        in_specs=in_specs,
        out_specs=out_specs,
        out_shape=out_shape,
        scratch_shapes=scratch,
        compiler_params=pltpu.CompilerParams(
            dimension_semantics=("arbitrary",), vmem_limit_bytes=VMEM_REQUEST_BYTES),
        name="prompt_layer",
    )(x, w3, wg, wo, ng, fgain, mhg, cw, big, bfg, caug0, m0, ut0)


def _sample_block(x_ref, w3_ref, wg_ref, wo_ref, ng_ref, fgain_ref, mhg_ref, cw_ref, big_ref, bfg_ref,
                  sconv_ref, m_ref, n_ref, c_ref,
                  y_ref, conv_out_ref, c_out_ref, n_out_ref, m_out_ref,
                  p_scr, ni_scr, sc_scr, ycat_scr):
    x = x_ref[...]
    xn = _rms_rows(x, ng_ref[...]).astype(_BF)
    g_raw = _dot(xn, wg_ref[...])
    for t in QKV_TILES:
        p_scr[:, _tile_cols(t)] = _dot(xn, w3_ref[t])

    row = lax.broadcasted_iota(jnp.int32, (CHUNK, CHUNK), 0)
    col = lax.broadcasted_iota(jnp.int32, (CHUNK, CHUNK), 1)
    pos = row % DEC_SEQ
    grp = row // DEC_SEQ
    causal = (grp == (col // DEC_SEQ)) & (col <= row)
    grow = lax.broadcasted_iota(jnp.int32, (SEQ_PER_STEP, CHUNK), 0)
    gcol = lax.broadcasted_iota(jnp.int32, (SEQ_PER_STEP, CHUNK), 1)
    pick_last = (gcol == grow * DEC_SEQ + (DEC_SEQ - 1)).astype(_F32)
    seq_sum = (gcol // DEC_SEQ == grow).astype(_BF)
    erow = lax.broadcasted_iota(jnp.int32, (CHUNK, SEQ_PER_STEP), 0)
    ecol = lax.broadcasted_iota(jnp.int32, (CHUNK, SEQ_PER_STEP), 1)
    own_seq = erow // DEC_SEQ == ecol
    ones_blk = jnp.ones((CHUNK, LANES), _BF)

    def rows_from_seqs(v16):
        return jnp.broadcast_to(v16[:, None, :], (SEQ_PER_STEP, DEC_SEQ, LANES)).reshape(CHUNK, LANES)

    def last_row_of_seq(v):
        v3 = v.reshape(SEQ_PER_STEP, DEC_SEQ, LANES)
        return jnp.broadcast_to(v3[:, DEC_SEQ - 1:DEC_SEQ, :], v3.shape).reshape(CHUNK, LANES)

    ig, lf = _gate_columns(g_raw, big_ref, bfg_ref)
    b = _cumsum_rows(lf, pos, DEC_SEQ)
    a = ig - b
    m_prev = rows_from_seqs(m_ref[...])
    big_m = jnp.maximum(m_prev, _cummax_rows(a, pos, DEC_SEQ))
    m_t = b + big_m
    inter = jnp.exp(m_prev - big_m)
    emt = jnp.exp(-m_t)
    m_last = last_row_of_seq(big_m)
    w = jnp.exp(a - m_last)
    decay = jnp.exp(m_prev - m_last)
    m_out_ref[...] = _dot_exact(pick_last, m_t)
    dec16 = _dot_exact(pick_last, decay)
    a_t = a.T

    for t in REST_TILES:
        p_scr[:, _tile_cols(t)] = _dot(xn, w3_ref[t])

    heads = []
    for h in range(M_HEADS):
        hs = h * HEAD_DIM
        q = p_scr[:, OFF_Q + hs:OFF_Q + hs + HEAD_DIM]
        k = p_scr[:, OFF_K + hs:OFF_K + hs + HEAD_DIM] * K_SCALE
        v = p_scr[:, OFF_V + hs:OFF_V + hs + HEAD_DIM]
        q_bf = q.astype(_BF)
        a_b = jnp.broadcast_to(a_t[h:h + 1, :], (CHUNK, CHUNK))
        m_b = jnp.broadcast_to(big_m[:, h:h + 1], (CHUNK, CHUNK))
        d = jnp.exp(jnp.where(causal, a_b - m_b, NEG_INF))
        w_b = jnp.broadcast_to(w[:, h:h + 1], (CHUNK, LANES))
        heads.append(dict(q_bf=q_bf, k=k, v=v, d=d, w_b=w_b, qk=_dot_nt(q_bf, k.astype(_BF))))

    u = p_scr[:, OFF_GC:OFF_GC + CONV_WIDTH] * p_scr[:, OFF_HC:OFF_HC + CONV_WIDTH]
    for g in range(SEQ_PER_STEP):
        sc_scr[g * DEC_SEQ:g * DEC_SEQ + CONV_K - 1, :] = sconv_ref[g]
        conv_out_ref[g] = u[(g + 1) * DEC_SEQ - (CONV_K - 1):(g + 1) * DEC_SEQ, :]
    sc = sc_scr[...]
    posw = lax.broadcasted_iota(jnp.int32, (CHUNK, CONV_WIDTH), 0) % DEC_SEQ
    u_m1 = jnp.where(posw == 0, pltpu.roll(sc, CHUNK - 1, 0), pltpu.roll(u, 1, 0))
    u_m2 = jnp.where(posw < 2, sc, pltpu.roll(u, 2, 0))
    y_conv = _conv_out(p_scr[:, OFF_GB:OFF_GB + CONV_WIDTH], p_scr[:, OFF_ZC:OFF_ZC + CONV_WIDTH],
                       u, u_m1, u_m2, cw_ref)
    ycat_scr[:, 0:CONV_WIDTH] = y_conv.astype(_BF)

    for h in range(M_HEADS):
        hs = h * HEAD_DIM
        st = heads[h]
        wv = st["w_b"] * st["v"]
        k_t_bf = st["k"].T.astype(_BF)
        dec_b = jnp.broadcast_to(decay[:, h:h + 1], (CHUNK, HEAD_DIM))

        for g in range(0, SEQ_PER_STEP, 2):
            pair_rows = slice(g * DEC_SEQ, (g + 2) * DEC_SEQ)
            c0, c1 = c_ref[g, h], c_ref[g + 1, h]
            qc = _dot(st["q_bf"][pair_rows, :], jnp.concatenate([c0.astype(_BF), c1.astype(_BF)], axis=1))
            ni_scr[g * DEC_SEQ:(g + 1) * DEC_SEQ, hs:hs + HEAD_DIM] = qc[0:DEC_SEQ, 0:HEAD_DIM]
            ni_scr[(g + 1) * DEC_SEQ:(g + 2) * DEC_SEQ, hs:hs + HEAD_DIM] = qc[DEC_SEQ:, HEAD_DIM:]
            wv_pair = jnp.concatenate([jnp.where(grp == g, wv, 0.0).astype(_BF),
                                       jnp.where(grp == g + 1, wv, 0.0).astype(_BF)], axis=1)
            upd = _dot(k_t_bf, wv_pair)
            dec0 = jnp.broadcast_to(dec_b[g * DEC_SEQ:g * DEC_SEQ + 1, :], (HEAD_DIM, HEAD_DIM))
            dec1 = jnp.broadcast_to(dec_b[(g + 1) * DEC_SEQ:(g + 1) * DEC_SEQ + 1, :], (HEAD_DIM, HEAD_DIM))
            c_out_ref[g, h] = dec0 * c0 + upd[:, 0:HEAD_DIM]
            c_out_ref[g + 1, h] = dec1 * c1 + upd[:, HEAD_DIM:]

        s = st["qk"] * st["d"]
        sv = _dot(s.astype(_BF), jnp.concatenate([st["v"].astype(_BF), ones_blk], axis=1))
        n_old = n_ref[:, h, :]
        num_inter = ni_scr[:, hs:hs + HEAD_DIM]
        qn = _dot_nt(st["q_bf"], n_old.astype(_BF))
        den_inter = jnp.sum(jnp.where(own_seq, qn, 0.0), axis=-1, keepdims=True)
        inter_b = jnp.broadcast_to(inter[:, h:h + 1], (CHUNK, LANES))
        num = sv[:, :LANES] + inter_b * num_inter
        den = sv[:, LANES:] + inter_b * den_inter
        emt_b = jnp.broadcast_to(emt[:, h:h + 1], (CHUNK, LANES))
        ni_scr[:, hs:hs + HEAD_DIM] = _head_norm(num, den, emt_b, mhg_ref[:, hs:hs + HEAD_DIM])
        n_upd = _dot(seq_sum, (st["w_b"] * st["k"]).astype(_BF))
        dec16_b = jnp.broadcast_to(dec16[:, h:h + 1], (SEQ_PER_STEP, HEAD_DIM))
        n_out_ref[:, h, :] = dec16_b * n_old + n_upd

    ycat_scr[:, CONV_WIDTH:CONV_WIDTH + M_WIDTH] = _gated_heads(ni_scr[...], p_scr).astype(_BF)
    out = _dot(ycat_scr[...], wo_ref[...]) + x
    y_ref[...] = _rms_rows(out, fgain_ref[...])


def kernel(x_prompt, x_sample, state_conv, state_mlstm_c, state_mlstm_n, state_mlstm_m, meta_tokens, norm_gain,
           w_in, conv_w, b_igate, b_fgate, mh_norm_gain, w_out, final_norm_gain):
    nseq, dec_seq, _ = x_sample.shape
    assert dec_seq == DEC_SEQ and meta_tokens.shape[0] == N_META
    assert w_in.shape == (D_MODEL, MAIN_WIDTH + N_GATES) and x_prompt.shape[1] % PROMPT_ROWS == 0
    assert (nseq * DEC_SEQ) % CHUNK == 0

    w_t = w_in.T
    small = (norm_gain.reshape(1, D_MODEL), mh_norm_gain.reshape(1, M_WIDTH), conv_w,
             b_igate.astype(_F32), b_fgate.astype(_F32))
    fgain = final_norm_gain.reshape(1, D_MODEL)

    xs = x_sample.reshape(nseq * DEC_SEQ, D_MODEL)
    w3, wg, wo, caug_meta, m_meta, ut_meta, ys, conv_s, c_s, n_s, m_s = _front_call(
        w_t, meta_tokens.astype(_F32), w_out.astype(_F32), small, fgain, xs, state_conv, state_mlstm_m.T,
        state_mlstm_n, state_mlstm_c)

    y_prompt, c_p, n_p, m_p, conv_p = _prompt_call(x_prompt, w3, wg, wo, small, fgain, caug_meta, m_meta, ut_meta)
    y_sample = ys.reshape(nseq, DEC_SEQ, D_MODEL)

    return (y_prompt, y_sample, conv_p, c_p, n_p, m_p.T, conv_s, c_s, n_s, m_s.T)
```

```python
import functools

import jax
import jax.numpy as jnp
from jax import lax
from jax.experimental import pallas as pl
from jax.experimental.pallas import tpu as pltpu

D_MODEL = 1024
CONV_WIDTH = 512
M_WIDTH = 512
M_HEADS = 4
HEAD_DIM = 128
N_META = 16
CONV_K = 3
EPS = 1e-6
MAIN_WIDTH = 4 * CONV_WIDTH + 5 * M_WIDTH
N_GATES = 2 * M_HEADS
LANES = 128
SUBLANES = 8
MXU_COLS = 256
N_TILES = MAIN_WIDTH // MXU_COLS
CHUNK = 128
DEC_SEQ = 8
SEQ_PER_STEP = CHUNK // DEC_SEQ
NEG_INF = float("-inf")
K_SCALE = HEAD_DIM ** -0.5

PROMPT_ROWS = 512
PREP_TILES = 3
WOUT_ROWS = 256
VMEM_REQUEST_BYTES = 56 * 1024 * 1024
EARLY_TILES = 2
UNIT_LAG = 3

OFF_GB, OFF_GC, OFF_HC, OFF_ZC = 0, 512, 1024, 1536
OFF_Q, OFF_K, OFF_V, OFF_O, OFF_ZM = 2048, 2560, 3072, 3584, 4096
QKV_TILES = tuple(range(OFF_Q // MXU_COLS, OFF_O // MXU_COLS))
REST_TILES = tuple(t for t in range(N_TILES) if t not in QKV_TILES)

_SMEM_SPEC = pl.BlockSpec(memory_space=pltpu.SMEM)
_HI = lax.Precision.HIGHEST
_BF = jnp.bfloat16
_F32 = jnp.float32


def _dot(a, b):
    return jnp.dot(a, b, preferred_element_type=_F32)


def _dot_exact(a, b):
    return jnp.dot(a, b, precision=_HI, preferred_element_type=_F32)


def _dot_nt(a, b):
    return lax.dot_general(a, b, (((1,), (1,)), ((), ())), preferred_element_type=_F32)


def _rms_rows(x, gain_row):
    return x * lax.rsqrt(jnp.mean(x * x, axis=-1, keepdims=True) + EPS) * gain_row


def _sigmoid(x):
    return 1.0 / (1.0 + jnp.exp(-x))


def _silu(x):
    return x * _sigmoid(x)


def _tile_cols(t):
    return slice(t * MXU_COLS, (t + 1) * MXU_COLS)


def _cummax_rows(x, pos, period):
    s = 1
    while s < period:
        shifted = pltpu.roll(x, s, 0)
        x = jnp.maximum(x, jnp.where(pos >= s, shifted, NEG_INF))
        s *= 2
    return x


def _cumsum_rows(x, pos, period):
    s = 1
    while s < period:
        x = x + jnp.where(pos >= s, pltpu.roll(x, s, 0), 0.0)
        s *= 2
    return x


def _bias_row(b_ref):
    lane = lax.broadcasted_iota(jnp.int32, (1, LANES), 1)
    row = jnp.zeros((1, LANES), _F32)
    for h in range(M_HEADS):
        row = jnp.where(lane == h, b_ref[h], row)
    return row


def _gate_columns(g_raw, big_ref, bfg_ref):
    ig = g_raw + _bias_row(big_ref)
    lf = jax.nn.log_sigmoid(pltpu.roll(g_raw, LANES - M_HEADS, 1) + _bias_row(bfg_ref))
    return ig, lf


def _head_norm(num, den, emt_b, gain_row):
    hh = num / jnp.maximum(jnp.abs(den), emt_b)
    return hh * lax.rsqrt(jnp.mean(hh * hh, axis=-1, keepdims=True) + EPS) * gain_row


def _conv_out(gb, zc, u, u_m1, u_m2, cw_ref):
    yc = cw_ref[0:1, :] * u_m2 + cw_ref[1:2, :] * u_m1 + cw_ref[2:3, :] * u
    return gb * yc * _silu(zc)


def _gated_heads(hn, p_scr):
    return hn * _sigmoid(p_scr[:, OFF_O:OFF_O + M_WIDTH]) * _silu(p_scr[:, OFF_ZM:OFF_ZM + M_WIDTH])


def _sequence_block(xn, project_tile, p_scr, g_scr, hn_scr, ycat_scr, ubuf, caug, m_scr,
                    wg_ref, mhg_ref, cw_ref, big_ref, bfg_ref, *, tb, n_pad, side_work=None):
    side_work = side_work or {}
    g_scr[...] = _dot(xn, wg_ref[...])
    late_tiles = []
    if project_tile is not None:
        for t in QKV_TILES + REST_TILES[:EARLY_TILES]:
            project_tile(t)
        late_tiles = list(REST_TILES[EARLY_TILES:])

    row = lax.broadcasted_iota(jnp.int32, (CHUNK, CHUNK), 0)
    col = lax.broadcasted_iota(jnp.int32, (CHUNK, CHUNK), 1)
    causal = col <= row
    ones_blk = jnp.ones((CHUNK, LANES), _BF)

    n_chunks = tb // CHUNK
    gates = []
    m_prev = m_scr[...]
    for c in range(n_chunks):
        rows = slice(c * CHUNK, (c + 1) * CHUNK)
        ig, lf = _gate_columns(g_scr[rows, :], big_ref, bfg_ref)
        if n_pad:
            ig = jnp.where(row >= n_pad, ig, NEG_INF)
            lf = jnp.where(row >= n_pad, lf, 0.0)
        b = _cumsum_rows(lf, row, CHUNK)
        a = ig - b
        big_m = jnp.maximum(m_prev, _cummax_rows(a, row, CHUNK))
        m_t = b + big_m
        m_last = big_m[CHUNK - 1:CHUNK, :]
        gates.append(dict(a_t=a.T, big_m=big_m, inter=jnp.exp(m_prev - big_m), emt=jnp.exp(-m_t),
                          w=jnp.exp(a - m_last), decay=jnp.exp(m_prev - m_last)))
        m_prev = m_t[CHUNK - 1:CHUNK, :]
    m_scr[...] = m_prev

    def stage_a(c, h):
        rows = slice(c * CHUNK, (c + 1) * CHUNK)
        hs = h * HEAD_DIM
        g = gates[c]
        q2 = p_scr[rows, OFF_Q + hs:OFF_Q + hs + 2 * HEAD_DIM]
        k2 = p_scr[rows, OFF_K + hs:OFF_K + hs + 2 * HEAD_DIM] * K_SCALE
        k2_bf = k2.astype(_BF)
        zero = jnp.zeros((CHUNK, HEAD_DIM), _BF)
        k_diag = jnp.concatenate([jnp.concatenate([k2_bf[:, :HEAD_DIM], zero], axis=1),
                                  jnp.concatenate([zero, k2_bf[:, HEAD_DIM:]], axis=1)], axis=0)
        qk2 = _dot_nt(q2.astype(_BF), k_diag)
        out = []
        for j in range(2):
            cols = slice(j * HEAD_DIM, (j + 1) * HEAD_DIM)
            a_b = jnp.broadcast_to(g["a_t"][h + j:h + j + 1, :], (CHUNK, CHUNK))
            m_b = jnp.broadcast_to(g["big_m"][:, h + j:h + j + 1], (CHUNK, CHUNK))
            d = jnp.exp(jnp.where(causal, a_b - m_b, NEG_INF))
            inter_b = jnp.broadcast_to(g["inter"][:, h + j:h + j + 1], (CHUNK, HEAD_DIM))
            out.append(dict(qk=qk2[:, cols], d=d, q_inter=(inter_b * q2[:, cols]).astype(_BF), k=k2[:, cols]))
        return out

    def stage_b(c, h, st):
        rows = slice(c * CHUNK, (c + 1) * CHUNK)
        hs = h * HEAD_DIM
        g = gates[c]
        v = p_scr[rows, OFF_V + hs:OFF_V + hs + HEAD_DIM]
        s = st["qk"] * st["d"]
        lhs = jnp.concatenate([s.astype(_BF), st["q_inter"]], axis=1)
        rhs = jnp.concatenate([jnp.concatenate([v.astype(_BF), ones_blk], axis=1), caug[h].astype(_BF)], axis=0)
        tot = _dot(lhs, rhs)
        emt_b = jnp.broadcast_to(g["emt"][:, h:h + 1], (CHUNK, LANES))
        hn_scr[rows, hs:hs + HEAD_DIM] = _head_norm(tot[:, :LANES], tot[:, LANES:], emt_b,
                                                    mhg_ref[:, hs:hs + HEAD_DIM])
        w_b = jnp.broadcast_to(g["w"][:, h:h + 1], (CHUNK, LANES))
        wv = jnp.concatenate([(w_b * v).astype(_BF), w_b.astype(_BF)], axis=1)
        upd = _dot(st["k"].T.astype(_BF), wv)
        decay_b = jnp.broadcast_to(g["decay"][:, h:h + 1], (HEAD_DIM, 2 * LANES))
        caug[h] = decay_b * caug[h] + upd

    units = [(c, h) for c in range(n_chunks) for h in range(M_HEADS)]
    lag = min(UNIT_LAG, len(units))
    n_slots = len(units) + lag
    assert len(late_tiles) <= n_slots
    assert all(0 <= slot < n_slots for slot in side_work)
    pending = {}
    for t in range(n_slots):
        if t < len(units) and t % 2 == 0:
            pending[t], pending[t + 1] = stage_a(*units[t])
        if t < len(late_tiles):
            project_tile(late_tiles[t])
        for fn in side_work.get(t, ()):
            fn()
        if t >= lag:
            stage_b(*units[t - lag], pending.pop(t - lag))

    u = p_scr[:, OFF_GC:OFF_GC + CONV_WIDTH] * p_scr[:, OFF_HC:OFF_HC + CONV_WIDTH]
    ubuf[SUBLANES:SUBLANES + tb, :] = u
    u_m1 = ubuf[SUBLANES - 1:SUBLANES - 1 + tb, :]
    u_m2 = ubuf[SUBLANES - 2:SUBLANES - 2 + tb, :]
    y_conv = _conv_out(p_scr[:, OFF_GB:OFF_GB + CONV_WIDTH], p_scr[:, OFF_ZC:OFF_ZC + CONV_WIDTH],
                       u, u_m1, u_m2, cw_ref)
    ycat_scr[:, 0:CONV_WIDTH] = y_conv.astype(_BF)
    ycat_scr[:, CONV_WIDTH:CONV_WIDTH + M_WIDTH] = _gated_heads(hn_scr[...], p_scr).astype(_BF)
    ubuf[0:SUBLANES, :] = ubuf[tb:tb + SUBLANES, :]


def _sequence_scratch(tb):
    return [
        pltpu.VMEM((tb, MAIN_WIDTH), _F32),
        pltpu.VMEM((tb, LANES), _F32),
        pltpu.VMEM((tb, M_WIDTH), _F32),
        pltpu.VMEM((tb, D_MODEL), _BF),
        pltpu.VMEM((tb + 2 * SUBLANES, CONV_WIDTH), _F32),
        pltpu.VMEM((M_HEADS, HEAD_DIM, 2 * LANES), _F32),
        pltpu.VMEM((1, LANES), _F32),
    ]


def _front_body(wt_ref, x_ref, wgt_ref, wout_ref, ng_ref, fgain_ref, mhg_ref, cw_ref, big_ref, bfg_ref,
                xs_ref, sconv_ref, ms_ref, ns_ref, cs_ref,
                w3_ref, wg_ref, wo_ref, caug_out_ref, m_out_ref, ut_out_ref,
                ys_ref, sconv_out_ref, cs_out_ref, ns_out_ref, ms_out_ref,
                w3_scr, xn_scr, p3_scr, sc_scr, ms_in_scr, ms_new_scr,
                p_scr, g_scr, hn_scr, ycat_scr, ubuf, caug, m_scr):
    t = pl.program_id(0)
    n_prep = N_TILES // PREP_TILES
    last = pl.num_programs(0) - 1

    @pl.when(t == 0)
    def _():
        xn_scr[...] = jnp.zeros_like(xn_scr)
        xn_scr[CHUNK - N_META:CHUNK, :] = _rms_rows(x_ref[...], ng_ref[...]).astype(_BF)
        gate_rows = jnp.concatenate([wgt_ref[...], jnp.zeros((LANES - N_GATES, D_MODEL), _F32)], axis=0)
        wg_ref[...] = gate_rows.T.astype(_BF)
        sc_scr[...] = jnp.zeros_like(sc_scr)
        m_heads = jnp.concatenate([ms_ref[...], jnp.zeros((LANES - M_HEADS, LANES), _F32)], axis=0)
        ms_in_scr[...] = m_heads.T

    @pl.when(t < D_MODEL // WOUT_ROWS)
    def _():
        wo_ref[pl.ds(pl.multiple_of(t * WOUT_ROWS, WOUT_ROWS), WOUT_ROWS), :] = wout_ref[...].astype(_BF)

    @pl.when(t < n_prep)
    def _():
        for j in range(PREP_TILES):
            tile = wt_ref[j * MXU_COLS:(j + 1) * MXU_COLS, :].T.astype(_BF)
            w3_ref[j] = tile
            w3_scr[t * PREP_TILES + j] = tile
            p3_scr[t * PREP_TILES + j] = _dot(xn_scr[...], tile)

    @pl.when(t == n_prep - 1)
    def _():
        for j in range(N_TILES):
            p_scr[:, _tile_cols(j)] = p3_scr[j]
        caug[...] = jnp.zeros_like(caug)
        m_scr[...] = jnp.zeros_like(m_scr)
        ubuf[0:SUBLANES, :] = jnp.zeros((SUBLANES, CONV_WIDTH), _F32)
        _sequence_block(xn_scr[...], None, p_scr, g_scr, hn_scr, ycat_scr, ubuf, caug, m_scr,
                        wg_ref, mhg_ref, cw_ref, big_ref, bfg_ref, tb=CHUNK, n_pad=CHUNK - N_META)
        caug_out_ref[...] = caug[...]
        m_out_ref[...] = m_scr[...]
        ut_out_ref[...] = ubuf[0:SUBLANES, :]

    @pl.when(t >= n_prep)
    def _():
        seqs = pl.ds(pl.multiple_of((t - n_prep) * SEQ_PER_STEP, SEQ_PER_STEP), SEQ_PER_STEP)
        _sample_block(xs_ref, w3_scr, wg_ref, wo_ref, ng_ref, fgain_ref, mhg_ref, cw_ref, big_ref, bfg_ref,
                      sconv_ref, ms_in_scr.at[seqs], ns_ref, cs_ref,
                      ys_ref, sconv_out_ref, cs_out_ref, ns_out_ref, ms_new_scr.at[seqs],
                      p_scr, hn_scr, sc_scr, ycat_scr)

    @pl.when(t == last)
    def _():
        ms_out_ref[...] = ms_new_scr[...].T[0:M_HEADS, :]


def _front_call(w_t, meta_tokens, w_out, small, fgain, xs, state_conv, m_heads, n_state, c_state):
    ng, mhg, cw, big, bfg = small
    n_prep = N_TILES // PREP_TILES
    rows = xs.shape[0]
    nseq = c_state.shape[0]
    n_sample = rows // CHUNK
    assert m_heads.shape == (M_HEADS, LANES) and nseq == LANES
    assert D_MODEL // WOUT_ROWS <= n_prep

    def const(shape):
        nd = len(shape)
        return pl.BlockSpec(shape, lambda t: (0,) * nd)

    def sample_spec(*block):
        nd = len(block) - 1
        return pl.BlockSpec(block, lambda t: (jnp.maximum(t - n_prep, 0),) + (0,) * nd)

    def prep_idx(t):
        return jnp.minimum(t, n_prep - 1)

    xs_spec = sample_spec(CHUNK, D_MODEL)
    sconv_spec = sample_spec(SEQ_PER_STEP, CONV_K - 1, CONV_WIDTH)
    cs_spec = sample_spec(SEQ_PER_STEP, M_HEADS, HEAD_DIM, HEAD_DIM)
    ns_spec = sample_spec(SEQ_PER_STEP, M_HEADS, HEAD_DIM)
    ms_spec = const(m_heads.shape)
    in_specs = [
        pl.BlockSpec((PREP_TILES * MXU_COLS, D_MODEL), lambda t: (prep_idx(t), 0)),
        const(meta_tokens.shape),
        pl.BlockSpec((N_GATES, D_MODEL), lambda t: (MAIN_WIDTH // N_GATES, 0)),
        pl.BlockSpec((WOUT_ROWS, D_MODEL), lambda t: (jnp.minimum(t, D_MODEL // WOUT_ROWS - 1), 0)),
        const(ng.shape), const(fgain.shape), const(mhg.shape), const(cw.shape),
        _SMEM_SPEC, _SMEM_SPEC,
        xs_spec, sconv_spec, ms_spec, ns_spec, cs_spec,
    ]
    out_shape = (
        jax.ShapeDtypeStruct((N_TILES, D_MODEL, MXU_COLS), _BF),
        jax.ShapeDtypeStruct((D_MODEL, LANES), _BF),
        jax.ShapeDtypeStruct(w_out.shape, _BF),
        jax.ShapeDtypeStruct((M_HEADS, HEAD_DIM, 2 * LANES), _F32),
        jax.ShapeDtypeStruct((1, LANES), _F32),
        jax.ShapeDtypeStruct((SUBLANES, CONV_WIDTH), _F32),
        jax.ShapeDtypeStruct((rows, D_MODEL), _F32),
        jax.ShapeDtypeStruct(state_conv.shape, _F32),
        jax.ShapeDtypeStruct(c_state.shape, _F32),
        jax.ShapeDtypeStruct(n_state.shape, _F32),
        jax.ShapeDtypeStruct(m_heads.shape, _F32),
    )
    out_specs = (
        pl.BlockSpec((PREP_TILES, D_MODEL, MXU_COLS), lambda t: (prep_idx(t), 0, 0)),
        const(out_shape[1].shape), const(out_shape[2].shape), const(out_shape[3].shape), const(out_shape[4].shape),
        const(out_shape[5].shape),
        xs_spec, sconv_spec, cs_spec, ns_spec, ms_spec,
    )
    scratch = [
        pltpu.VMEM((N_TILES, D_MODEL, MXU_COLS), _BF),
        pltpu.VMEM((CHUNK, D_MODEL), _BF),
        pltpu.VMEM((N_TILES, CHUNK, MXU_COLS), _F32),
        pltpu.VMEM((CHUNK, CONV_WIDTH), _F32),
        pltpu.VMEM((LANES, LANES), _F32),
        pltpu.VMEM((LANES, LANES), _F32),
    ] + _sequence_scratch(CHUNK)
    return pl.pallas_call(
        _front_body,
        grid=(n_prep + n_sample,),
        in_specs=in_specs,
        out_specs=out_specs,
        out_shape=out_shape,
        scratch_shapes=scratch,
        compiler_params=pltpu.CompilerParams(
            dimension_semantics=("arbitrary",), vmem_limit_bytes=VMEM_REQUEST_BYTES),
        name="prep_meta_sample",
    )(w_t, meta_tokens, w_t, w_out, ng, fgain, mhg, cw, big, bfg, xs, state_conv, m_heads, n_state, c_state)


def _prompt_body(x_ref, w3_ref, wg_ref, wo_ref, ng_ref, fgain_ref, mhg_ref, cw_ref, big_ref, bfg_ref,
                 caug0_ref, m0_ref, ut0_ref,
                 y_ref, c_out_ref, n_out_ref, m_out_ref, conv_out_ref,
                 out_scr, m_all_scr, p_scr, g_scr, hn_scr, ycat_scr, ubuf, caug, m_scr, *, tb, nt, n_blocks):
    s = pl.program_id(0)
    i = s % nt
    live = s < n_blocks

    @pl.when(s == 0)
    def _():
        out_scr[...] = jnp.zeros_like(out_scr)
        m_all_scr[...] = jnp.zeros_like(m_all_scr)

    @pl.when(live & (i == 0))
    def _():
        caug[...] = caug0_ref[...]
        m_scr[...] = m0_ref[...]
        ubuf[0:SUBLANES, :] = ut0_ref[...]

    @pl.when(live)
    def _():
        xn = _rms_rows(x_ref[0], ng_ref[...]).astype(_BF)

        def project_tile(t):
            p_scr[:, _tile_cols(t)] = _dot(xn, w3_ref[t])

        def norm_previous(r0, r1):
            def fn():
                y_ref[0, r0:r1, :] = _rms_rows(out_scr[r0:r1, :], fgain_ref[...])
            return fn

        q4 = tb // 4
        side_work = {3 + 2 * j: [norm_previous(j * q4, (j + 1) * q4)] for j in range(4)}
        _sequence_block(xn, project_tile, p_scr, g_scr, hn_scr, ycat_scr, ubuf, caug, m_scr,
                        wg_ref, mhg_ref, cw_ref, big_ref, bfg_ref, tb=tb, n_pad=0, side_work=side_work)
        out_scr[...] = _dot(ycat_scr[...], wo_ref[...]) + x_ref[0]

    @pl.when(s == n_blocks)
    def _():
        y_ref[0] = _rms_rows(out_scr[...], fgain_ref[...])
        m_out_ref[...] = m_all_scr[...].T[0:M_HEADS, 0:n_blocks // nt]

    @pl.when(live & (i == nt - 1))
    def _():
        m_all_scr[pl.ds(s // nt, 1), :] = m_scr[...]
        conv_out_ref[0] = ubuf[SUBLANES - (CONV_K - 1):SUBLANES, :]
        for h in range(M_HEADS):
            c_out_ref[0, h] = caug[h, :, 0:HEAD_DIM]
            n_out_ref[0, h:h + 1, :] = caug[h, :, LANES:2 * LANES].T[0:1, :]


def _prompt_call(x, w3, wg, wo, small, fgain, caug0, m0, ut0):
    ng, mhg, cw, big, bfg = small
    nb, t, _ = x.shape
    tb = PROMPT_ROWS
    nt = t // tb
    n_blocks = nb * nt

    def const(shape):
        nd = len(shape)
        return pl.BlockSpec(shape, lambda s: (0,) * nd)

    def block_of(s):
        return jnp.minimum(s, n_blocks - 1)

    def rows_spec(step_to_block):
        return pl.BlockSpec((1, tb, D_MODEL), lambda s: (step_to_block(s) // nt, step_to_block(s) % nt, 0))

    def state_spec(*tail):
        nd = len(tail)
        return pl.BlockSpec((1,) + tail, lambda s: (block_of(s) // nt,) + (0,) * nd)

    in_specs = [
        rows_spec(block_of),
        const(w3.shape), const(wg.shape), const(wo.shape), const(ng.shape), const(fgain.shape),
        const(mhg.shape), const(cw.shape), _SMEM_SPEC, _SMEM_SPEC,
        const(caug0.shape), const(m0.shape), const(ut0.shape),
    ]
    out_shape = (
        jax.ShapeDtypeStruct((nb, t, D_MODEL), _F32),
        jax.ShapeDtypeStruct((nb, M_HEADS, HEAD_DIM, HEAD_DIM), _F32),
        jax.ShapeDtypeStruct((nb, M_HEADS, HEAD_DIM), _F32),
        jax.ShapeDtypeStruct((M_HEADS, nb), _F32),
        jax.ShapeDtypeStruct((nb, CONV_K - 1, CONV_WIDTH), _F32),
    )
    out_specs = (
        rows_spec(lambda s: jnp.maximum(s - 1, 0)),
        state_spec(M_HEADS, HEAD_DIM, HEAD_DIM),
        state_spec(M_HEADS, HEAD_DIM),
        const((M_HEADS, nb)),
        state_spec(CONV_K - 1, CONV_WIDTH),
    )
    assert nb <= LANES
    scratch = [
        pltpu.VMEM((tb, D_MODEL), _F32),
        pltpu.VMEM((LANES, LANES), _F32),
    ] + _sequence_scratch(tb)
    return pl.pallas_call(
        functools.partial(_prompt_body, tb=tb, nt=nt, n_blocks=n_blocks),
        grid=(n_blocks + 1,),
        in_specs=in_specs,
        out_specs=out_specs,
        out_shape=out_shape,
        scratch_shapes=scratch,
        compiler_params=pltpu.CompilerParams(
            dimension_semantics=("arbitrary",), vmem_limit_bytes=VMEM_REQUEST_BYTES),
        name="prompt_layer",
    )(x, w3, wg, wo, ng, fgain, mhg, cw, big, bfg, caug0, m0, ut0)


def _sample_block(x_ref, w3_ref, wg_ref, wo_ref, ng_ref, fgain_ref, mhg_ref, cw_ref, big_ref, bfg_ref,
                  sconv_ref, m_ref, n_ref, c_ref,
                  y_ref, conv_out_ref, c_out_ref, n_out_ref, m_out_ref,
                  p_scr, ni_scr, sc_scr, ycat_scr):
    x = x_ref[...]
    xn = _rms_rows(x, ng_ref[...]).astype(_BF)
    g_raw = _dot(xn, wg_ref[...])
    for t in QKV_TILES:
        p_scr[:, _tile_cols(t)] = _dot(xn, w3_ref[t])

    row = lax.broadcasted_iota(jnp.int32, (CHUNK, CHUNK), 0)
    col = lax.broadcasted_iota(jnp.int32, (CHUNK, CHUNK), 1)
    pos = row % DEC_SEQ
    grp = row // DEC_SEQ
    causal = (grp == (col // DEC_SEQ)) & (col <= row)
    grow = lax.broadcasted_iota(jnp.int32, (SEQ_PER_STEP, CHUNK), 0)
    gcol = lax.broadcasted_iota(jnp.int32, (SEQ_PER_STEP, CHUNK), 1)
    pick_last = (gcol == grow * DEC_SEQ + (DEC_SEQ - 1)).astype(_F32)
    seq_sum = (gcol // DEC_SEQ == grow).astype(_BF)
    erow = lax.broadcasted_iota(jnp.int32, (CHUNK, SEQ_PER_STEP), 0)
    ecol = lax.broadcasted_iota(jnp.int32, (CHUNK, SEQ_PER_STEP), 1)
    own_seq = erow // DEC_SEQ == ecol
    ones_blk = jnp.ones((CHUNK, LANES), _BF)

    def rows_from_seqs(v16):
        return jnp.broadcast_to(v16[:, None, :], (SEQ_PER_STEP, DEC_SEQ, LANES)).reshape(CHUNK, LANES)

    def last_row_of_seq(v):
        v3 = v.reshape(SEQ_PER_STEP, DEC_SEQ, LANES)
        return jnp.broadcast_to(v3[:, DEC_SEQ - 1:DEC_SEQ, :], v3.shape).reshape(CHUNK, LANES)

    ig, lf = _gate_columns(g_raw, big_ref, bfg_ref)
    b = _cumsum_rows(lf, pos, DEC_SEQ)
    a = ig - b
    m_prev = rows_from_seqs(m_ref[...])
    big_m = jnp.maximum(m_prev, _cummax_rows(a, pos, DEC_SEQ))
    m_t = b + big_m
    inter = jnp.exp(m_prev - big_m)
    emt = jnp.exp(-m_t)
    m_last = last_row_of_seq(big_m)
    w = jnp.exp(a - m_last)
    decay = jnp.exp(m_prev - m_last)
    m_out_ref[...] = _dot_exact(pick_last, m_t)
    dec16 = _dot_exact(pick_last, decay)
    a_t = a.T

    for t in REST_TILES:
        p_scr[:, _tile_cols(t)] = _dot(xn, w3_ref[t])

    heads = []
    for h in range(M_HEADS):
        hs = h * HEAD_DIM
        q = p_scr[:, OFF_Q + hs:OFF_Q + hs + HEAD_DIM]
        k = p_scr[:, OFF_K + hs:OFF_K + hs + HEAD_DIM] * K_SCALE
        v = p_scr[:, OFF_V + hs:OFF_V + hs + HEAD_DIM]
        q_bf = q.astype(_BF)
        a_b = jnp.broadcast_to(a_t[h:h + 1, :], (CHUNK, CHUNK))
        m_b = jnp.broadcast_to(big_m[:, h:h + 1], (CHUNK, CHUNK))
        d = jnp.exp(jnp.where(causal, a_b - m_b, NEG_INF))
        w_b = jnp.broadcast_to(w[:, h:h + 1], (CHUNK, LANES))
        heads.append(dict(q_bf=q_bf, k=k, v=v, d=d, w_b=w_b, qk=_dot_nt(q_bf, k.astype(_BF))))

    u = p_scr[:, OFF_GC:OFF_GC + CONV_WIDTH] * p_scr[:, OFF_HC:OFF_HC + CONV_WIDTH]
    for g in range(SEQ_PER_STEP):
        sc_scr[g * DEC_SEQ:g * DEC_SEQ + CONV_K - 1, :] = sconv_ref[g]
        conv_out_ref[g] = u[(g + 1) * DEC_SEQ - (CONV_K - 1):(g + 1) * DEC_SEQ, :]
    sc = sc_scr[...]
    posw = lax.broadcasted_iota(jnp.int32, (CHUNK, CONV_WIDTH), 0) % DEC_SEQ
    u_m1 = jnp.where(posw == 0, pltpu.roll(sc, CHUNK - 1, 0), pltpu.roll(u, 1, 0))
    u_m2 = jnp.where(posw < 2, sc, pltpu.roll(u, 2, 0))
    y_conv = _conv_out(p_scr[:, OFF_GB:OFF_GB + CONV_WIDTH], p_scr[:, OFF_ZC:OFF_ZC + CONV_WIDTH],
                       u, u_m1, u_m2, cw_ref)
    ycat_scr[:, 0:CONV_WIDTH] = y_conv.astype(_BF)

    for h in range(M_HEADS):
        hs = h * HEAD_DIM
        st = heads[h]
        wv = st["w_b"] * st["v"]
        k_t_bf = st["k"].T.astype(_BF)
        dec_b = jnp.broadcast_to(decay[:, h:h + 1], (CHUNK, HEAD_DIM))

        for g in range(0, SEQ_PER_STEP, 2):
            pair_rows = slice(g * DEC_SEQ, (g + 2) * DEC_SEQ)
            c0, c1 = c_ref[g, h], c_ref[g + 1, h]
            qc = _dot(st["q_bf"][pair_rows, :], jnp.concatenate([c0.astype(_BF), c1.astype(_BF)], axis=1))
            ni_scr[g * DEC_SEQ:(g + 1) * DEC_SEQ, hs:hs + HEAD_DIM] = qc[0:DEC_SEQ, 0:HEAD_DIM]
            ni_scr[(g + 1) * DEC_SEQ:(g + 2) * DEC_SEQ, hs:hs + HEAD_DIM] = qc[DEC_SEQ:, HEAD_DIM:]
            wv_pair = jnp.concatenate([jnp.where(grp == g, wv, 0.0).astype(_BF),
                                       jnp.where(grp == g + 1, wv, 0.0).astype(_BF)], axis=1)
            upd = _dot(k_t_bf, wv_pair)
            dec0 = jnp.broadcast_to(dec_b[g * DEC_SEQ:g * DEC_SEQ + 1, :], (HEAD_DIM, HEAD_DIM))
            dec1 = jnp.broadcast_to(dec_b[(g + 1) * DEC_SEQ:(g + 1) * DEC_SEQ + 1, :], (HEAD_DIM, HEAD_DIM))
            c_out_ref[g, h] = dec0 * c0 + upd[:, 0:HEAD_DIM]
            c_out_ref[g + 1, h] = dec1 * c1 + upd[:, HEAD_DIM:]

        s = st["qk"] * st["d"]
        sv = _dot(s.astype(_BF), jnp.concatenate([st["v"].astype(_BF), ones_blk], axis=1))
        n_old = n_ref[:, h, :]
        num_inter = ni_scr[:, hs:hs + HEAD_DIM]
        qn = _dot_nt(st["q_bf"], n_old.astype(_BF))
        den_inter = jnp.sum(jnp.where(own_seq, qn, 0.0), axis=-1, keepdims=True)
        inter_b = jnp.broadcast_to(inter[:, h:h + 1], (CHUNK, LANES))
        num = sv[:, :LANES] + inter_b * num_inter
        den = sv[:, LANES:] + inter_b * den_inter
        emt_b = jnp.broadcast_to(emt[:, h:h + 1], (CHUNK, LANES))
        ni_scr[:, hs:hs + HEAD_DIM] = _head_norm(num, den, emt_b, mhg_ref[:, hs:hs + HEAD_DIM])
        n_upd = _dot(seq_sum, (st["w_b"] * st["k"]).astype(_BF))
        dec16_b = jnp.broadcast_to(dec16[:, h:h + 1], (SEQ_PER_STEP, HEAD_DIM))
        n_out_ref[:, h, :] = dec16_b * n_old + n_upd

    ycat_scr[:, CONV_WIDTH:CONV_WIDTH + M_WIDTH] = _gated_heads(ni_scr[...], p_scr).astype(_BF)
    out = _dot(ycat_scr[...], wo_ref[...]) + x
    y_ref[...] = _rms_rows(out, fgain_ref[...])


def kernel(x_prompt, x_sample, state_conv, state_mlstm_c, state_mlstm_n, state_mlstm_m, meta_tokens, norm_gain,
           w_in, conv_w, b_igate, b_fgate, mh_norm_gain, w_out, final_norm_gain):
    nseq, dec_seq, _ = x_sample.shape
    assert dec_seq == DEC_SEQ and meta_tokens.shape[0] == N_META
    assert w_in.shape == (D_MODEL, MAIN_WIDTH + N_GATES) and x_prompt.shape[1] % PROMPT_ROWS == 0
    assert (nseq * DEC_SEQ) % CHUNK == 0

    w_t = w_in.T
    small = (norm_gain.reshape(1, D_MODEL), mh_norm_gain.reshape(1, M_WIDTH), conv_w,
             b_igate.astype(_F32), b_fgate.astype(_F32))
    fgain = final_norm_gain.reshape(1, D_MODEL)

    xs = x_sample.reshape(nseq * DEC_SEQ, D_MODEL)
    w3, wg, wo, caug_meta, m_meta, ut_meta, ys, conv_s, c_s, n_s, m_s = _front_call(
        w_t, meta_tokens.astype(_F32), w_out.astype(_F32), small, fgain, xs, state_conv, state_mlstm_m.T,
        state_mlstm_n, state_mlstm_c)

    y_prompt, c_p, n_p, m_p, conv_p = _prompt_call(x_prompt, w3, wg, wo, small, fgain, caug_meta, m_meta, ut_meta)
    y_sample = ys.reshape(nseq, DEC_SEQ, D_MODEL)

    return (y_prompt, y_sample, conv_p, c_p, n_p, m_p.T, conv_s, c_s, n_s, m_s.T)
```

```python
import functools

import jax
import jax.numpy as jnp
from jax import lax
from jax.experimental import pallas as pl
from jax.experimental.pallas import tpu as pltpu

D_MODEL = 1024
CONV_WIDTH = 512
M_WIDTH = 512
M_HEADS = 4
HEAD_DIM = 128
N_META = 16
CONV_K = 3
EPS = 1e-6
MAIN_WIDTH = 4 * CONV_WIDTH + 5 * M_WIDTH
N_GATES = 2 * M_HEADS
LANES = 128
SUBLANES = 8
MXU_COLS = 256
N_TILES = MAIN_WIDTH // MXU_COLS
CHUNK = 128
DEC_SEQ = 8
SEQ_PER_STEP = CHUNK // DEC_SEQ
NEG_INF = float("-inf")
K_SCALE = HEAD_DIM ** -0.5

PROMPT_ROWS = 512
PREP_TILES = 3
WOUT_ROWS = 256
VMEM_REQUEST_BYTES = 56 * 1024 * 1024
EARLY_TILES = 2
UNIT_LAG = 3

OFF_GB, OFF_GC, OFF_HC, OFF_ZC = 0, 512, 1024, 1536
OFF_Q, OFF_K, OFF_V, OFF_O, OFF_ZM = 2048, 2560, 3072, 3584, 4096
QKV_TILES = tuple(range(OFF_Q // MXU_COLS, OFF_O // MXU_COLS))
REST_TILES = tuple(t for t in range(N_TILES) if t not in QKV_TILES)

_SMEM_SPEC = pl.BlockSpec(memory_space=pltpu.SMEM)
_HI = lax.Precision.HIGHEST
_BF = jnp.bfloat16
_F32 = jnp.float32


def _dot(a, b):
    return jnp.dot(a, b, preferred_element_type=_F32)


def _dot_exact(a, b):
    return jnp.dot(a, b, precision=_HI, preferred_element_type=_F32)


def _dot_nt(a, b):
    return lax.dot_general(a, b, (((1,), (1,)), ((), ())), preferred_element_type=_F32)


def _rms_rows(x, gain_row):
    return x * lax.rsqrt(jnp.mean(x * x, axis=-1, keepdims=True) + EPS) * gain_row


def _sigmoid(x):
    return 1.0 / (1.0 + jnp.exp(-x))


def _silu(x):
    return x * _sigmoid(x)


def _tile_cols(t):
    return slice(t * MXU_COLS, (t + 1) * MXU_COLS)


def _cummax_rows(x, pos, period):
    s = 1
    while s < period:
        shifted = pltpu.roll(x, s, 0)
        x = jnp.maximum(x, jnp.where(pos >= s, shifted, NEG_INF))
        s *= 2
    return x


def _cumsum_rows(x, pos, period):
    s = 1
    while s < period:
        x = x + jnp.where(pos >= s, pltpu.roll(x, s, 0), 0.0)
        s *= 2
    return x


def _bias_row(b_ref):
    lane = lax.broadcasted_iota(jnp.int32, (1, LANES), 1)
    row = jnp.zeros((1, LANES), _F32)
    for h in range(M_HEADS):
        row = jnp.where(lane == h, b_ref[h], row)
    return row


def _gate_columns(g_raw, big_ref, bfg_ref):
    ig = g_raw + _bias_row(big_ref)
    lf = jax.nn.log_sigmoid(pltpu.roll(g_raw, LANES - M_HEADS, 1) + _bias_row(bfg_ref))
    return ig, lf


def _head_norm(num, den, emt_b, gain_row):
    hh = num / jnp.maximum(jnp.abs(den), emt_b)
    return hh * lax.rsqrt(jnp.mean(hh * hh, axis=-1, keepdims=True) + EPS) * gain_row


def _conv_out(gb, zc, u, u_m1, u_m2, cw_ref):
    yc = cw_ref[0:1, :] * u_m2 + cw_ref[1:2, :] * u_m1 + cw_ref[2:3, :] * u
    return gb * yc * _silu(zc)


def _gated_heads(hn, p_scr):
    return hn * _sigmoid(p_scr[:, OFF_O:OFF_O + M_WIDTH]) * _silu(p_scr[:, OFF_ZM:OFF_ZM + M_WIDTH])


def _sequence_block(xn, project_tile, p_scr, g_scr, hn_scr, ycat_scr, ubuf, caug, m_scr,
                    wg_ref, mhg_ref, cw_ref, big_ref, bfg_ref, *, tb, n_pad, side_work=None):
    side_work = side_work or {}
    g_scr[...] = _dot(xn, wg_ref[...])
    late_tiles = []
    if project_tile is not None:
        for t in QKV_TILES + REST_TILES[:EARLY_TILES]:
            project_tile(t)
        late_tiles = list(REST_TILES[EARLY_TILES:])

    row = lax.broadcasted_iota(jnp.int32, (CHUNK, CHUNK), 0)
    col = lax.broadcasted_iota(jnp.int32, (CHUNK, CHUNK), 1)
    causal = col <= row
    ones_blk = jnp.ones((CHUNK, LANES), _BF)

    n_chunks = tb // CHUNK
    gates = []
    m_prev = m_scr[...]
    for c in range(n_chunks):
        rows = slice(c * CHUNK, (c + 1) * CHUNK)
        ig, lf = _gate_columns(g_scr[rows, :], big_ref, bfg_ref)
        if n_pad:
            ig = jnp.where(row >= n_pad, ig, NEG_INF)
            lf = jnp.where(row >= n_pad, lf, 0.0)
        b = _cumsum_rows(lf, row, CHUNK)
        a = ig - b
        big_m = jnp.maximum(m_prev, _cummax_rows(a, row, CHUNK))
        m_t = b + big_m
        m_last = big_m[CHUNK - 1:CHUNK, :]
        gates.append(dict(a_t=a.T, big_m=big_m, inter=jnp.exp(m_prev - big_m), emt=jnp.exp(-m_t),
                          w=jnp.exp(a - m_last), decay=jnp.exp(m_prev - m_last)))
        m_prev = m_t[CHUNK - 1:CHUNK, :]
    m_scr[...] = m_prev

    def stage_a(c, h):
        rows = slice(c * CHUNK, (c + 1) * CHUNK)
        hs = h * HEAD_DIM
        g = gates[c]
        q2 = p_scr[rows, OFF_Q + hs:OFF_Q + hs + 2 * HEAD_DIM]
        k2 = p_scr[rows, OFF_K + hs:OFF_K + hs + 2 * HEAD_DIM] * K_SCALE
        k2_bf = k2.astype(_BF)
        zero = jnp.zeros((CHUNK, HEAD_DIM), _BF)
        k_diag = jnp.concatenate([jnp.concatenate([k2_bf[:, :HEAD_DIM], zero], axis=1),
                                  jnp.concatenate([zero, k2_bf[:, HEAD_DIM:]], axis=1)], axis=0)
        qk2 = _dot_nt(q2.astype(_BF), k_diag)
        out = []
        for j in range(2):
            cols = slice(j * HEAD_DIM, (j + 1) * HEAD_DIM)
            a_b = jnp.broadcast_to(g["a_t"][h + j:h + j + 1, :], (CHUNK, CHUNK))
            m_b = jnp.broadcast_to(g["big_m"][:, h + j:h + j + 1], (CHUNK, CHUNK))
            d = jnp.exp(jnp.where(causal, a_b - m_b, NEG_INF))
            inter_b = jnp.broadcast_to(g["inter"][:, h + j:h + j + 1], (CHUNK, HEAD_DIM))
            out.append(dict(qk=qk2[:, cols], d=d, q_inter=(inter_b * q2[:, cols]).astype(_BF), k=k2[:, cols]))
        return out

    def stage_b(c, h, st):
        rows = slice(c * CHUNK, (c + 1) * CHUNK)
        hs = h * HEAD_DIM
        g = gates[c]
        v = p_scr[rows, OFF_V + hs:OFF_V + hs + HEAD_DIM]
        s = st["qk"] * st["d"]
        lhs = jnp.concatenate([s.astype(_BF), st["q_inter"]], axis=1)
        rhs = jnp.concatenate([jnp.concatenate([v.astype(_BF), ones_blk], axis=1), caug[h].astype(_BF)], axis=0)
        tot = _dot(lhs, rhs)
        emt_b = jnp.broadcast_to(g["emt"][:, h:h + 1], (CHUNK, LANES))
        hn_scr[rows, hs:hs + HEAD_DIM] = _head_norm(tot[:, :LANES], tot[:, LANES:], emt_b,
                                                    mhg_ref[:, hs:hs + HEAD_DIM])
        w_b = jnp.broadcast_to(g["w"][:, h:h + 1], (CHUNK, LANES))
        wv = jnp.concatenate([(w_b * v).astype(_BF), w_b.astype(_BF)], axis=1)
        upd = _dot(st["k"].T.astype(_BF), wv)
        decay_b = jnp.broadcast_to(g["decay"][:, h:h + 1], (HEAD_DIM, 2 * LANES))
        caug[h] = decay_b * caug[h] + upd

    units = [(c, h) for c in range(n_chunks) for h in range(M_HEADS)]
    lag = min(UNIT_LAG, len(units))
    n_slots = len(units) + lag
    assert len(late_tiles) <= n_slots
    assert all(0 <= slot < n_slots for slot in side_work)
    pending = {}
    for t in range(n_slots):
        if t < len(units) and t % 2 == 0:
            pending[t], pending[t + 1] = stage_a(*units[t])
        if t < len(late_tiles):
            project_tile(late_tiles[t])
        for fn in side_work.get(t, ()):
            fn()
        if t >= lag:
            stage_b(*units[t - lag], pending.pop(t - lag))

    u = p_scr[:, OFF_GC:OFF_GC + CONV_WIDTH] * p_scr[:, OFF_HC:OFF_HC + CONV_WIDTH]
    ubuf[SUBLANES:SUBLANES + tb, :] = u
    u_m1 = ubuf[SUBLANES - 1:SUBLANES - 1 + tb, :]
    u_m2 = ubuf[SUBLANES - 2:SUBLANES - 2 + tb, :]
    y_conv = _conv_out(p_scr[:, OFF_GB:OFF_GB + CONV_WIDTH], p_scr[:, OFF_ZC:OFF_ZC + CONV_WIDTH],
                       u, u_m1, u_m2, cw_ref)
    ycat_scr[:, 0:CONV_WIDTH] = y_conv.astype(_BF)
    ycat_scr[:, CONV_WIDTH:CONV_WIDTH + M_WIDTH] = _gated_heads(hn_scr[...], p_scr).astype(_BF)
    ubuf[0:SUBLANES, :] = ubuf[tb:tb + SUBLANES, :]


def _sequence_scratch(tb):
    return [
        pltpu.VMEM((tb, MAIN_WIDTH), _F32),
        pltpu.VMEM((tb, LANES), _F32),
        pltpu.VMEM((tb, M_WIDTH), _F32),
        pltpu.VMEM((tb, D_MODEL), _BF),
        pltpu.VMEM((tb + 2 * SUBLANES, CONV_WIDTH), _F32),
        pltpu.VMEM((M_HEADS, HEAD_DIM, 2 * LANES), _F32),
        pltpu.VMEM((1, LANES), _F32),
    ]


def _front_body(wt_ref, x_ref, wgt_ref, wout_ref, ng_ref, fgain_ref, mhg_ref, cw_ref, big_ref, bfg_ref,
                xs_ref, sconv_ref, ms_ref, ns_ref, cs_ref,
                w3_ref, wg_ref, wo_ref, caug_out_ref, m_out_ref, ut_out_ref,
                ys_ref, sconv_out_ref, cs_out_ref, ns_out_ref, ms_out_ref,
                w3_scr, xn_scr, p3_scr, sc_scr, ms_in_scr, ms_new_scr,
                p_scr, g_scr, hn_scr, ycat_scr, ubuf, caug, m_scr):
    t = pl.program_id(0)
    n_prep = N_TILES // PREP_TILES
    last = pl.num_programs(0) - 1

    @pl.when(t == 0)
    def _():
        xn_scr[...] = jnp.zeros_like(xn_scr)
        xn_scr[CHUNK - N_META:CHUNK, :] = _rms_rows(x_ref[...], ng_ref[...]).astype(_BF)
        gate_rows = jnp.concatenate([wgt_ref[...], jnp.zeros((LANES - N_GATES, D_MODEL), _F32)], axis=0)
        wg_ref[...] = gate_rows.T.astype(_BF)
        sc_scr[...] = jnp.zeros_like(sc_scr)
        m_heads = jnp.concatenate([ms_ref[...], jnp.zeros((LANES - M_HEADS, LANES), _F32)], axis=0)
        ms_in_scr[...] = m_heads.T

    @pl.when(t < D_MODEL // WOUT_ROWS)
    def _():
        wo_ref[pl.ds(pl.multiple_of(t * WOUT_ROWS, WOUT_ROWS), WOUT_ROWS), :] = wout_ref[...].astype(_BF)

    @pl.when(t < n_prep)
    def _():
        for j in range(PREP_TILES):
            tile = wt_ref[j * MXU_COLS:(j + 1) * MXU_COLS, :].T.astype(_BF)
            w3_ref[j] = tile
            w3_scr[t * PREP_TILES + j] = tile
            p3_scr[t * PREP_TILES + j] = _dot(xn_scr[...], tile)

    @pl.when(t == n_prep - 1)
    def _():
        for j in range(N_TILES):
            p_scr[:, _tile_cols(j)] = p3_scr[j]
        caug[...] = jnp.zeros_like(caug)
        m_scr[...] = jnp.zeros_like(m_scr)
        ubuf[0:SUBLANES, :] = jnp.zeros((SUBLANES, CONV_WIDTH), _F32)
        _sequence_block(xn_scr[...], None, p_scr, g_scr, hn_scr, ycat_scr, ubuf, caug, m_scr,
                        wg_ref, mhg_ref, cw_ref, big_ref, bfg_ref, tb=CHUNK, n_pad=CHUNK - N_META)
        caug_out_ref[...] = caug[...]
        m_out_ref[...] = m_scr[...]
        ut_out_ref[...] = ubuf[0:SUBLANES, :]

    @pl.when(t >= n_prep)
    def _():
        seqs = pl.ds(pl.multiple_of((t - n_prep) * SEQ_PER_STEP, SEQ_PER_STEP), SEQ_PER_STEP)
        _sample_block(xs_ref, w3_scr, wg_ref, wo_ref, ng_ref, fgain_ref, mhg_ref, cw_ref, big_ref, bfg_ref,
                      sconv_ref, ms_in_scr.at[seqs], ns_ref, cs_ref,
                      ys_ref, sconv_out_ref, cs_out_ref, ns_out_ref, ms_new_scr.at[seqs],
                      p_scr, hn_scr, sc_scr, ycat_scr)

    @pl.when(t == last)
    def _():
        ms_out_ref[...] = ms_new_scr[...].T[0:M_HEADS, :]


def _front_call(w_t, meta_tokens, w_out, small, fgain, xs, state_conv, m_heads, n_state, c_state):
    ng, mhg, cw, big, bfg = small
    n_prep = N_TILES // PREP_TILES
    rows = xs.shape[0]
    nseq = c_state.shape[0]
    n_sample = rows // CHUNK
    assert m_heads.shape == (M_HEADS, LANES) and nseq == LANES
    assert D_MODEL // WOUT_ROWS <= n_prep

    def const(shape):
        nd = len(shape)
        return pl.BlockSpec(shape, lambda t: (0,) * nd)

    def sample_spec(*block):
        nd = len(block) - 1
        return pl.BlockSpec(block, lambda t: (jnp.maximum(t - n_prep, 0),) + (0,) * nd)

    def prep_idx(t):
        return jnp.minimum(t, n_prep - 1)

    xs_spec = sample_spec(CHUNK, D_MODEL)
    sconv_spec = sample_spec(SEQ_PER_STEP, CONV_K - 1, CONV_WIDTH)
    cs_spec = sample_spec(SEQ_PER_STEP, M_HEADS, HEAD_DIM, HEAD_DIM)
    ns_spec = sample_spec(SEQ_PER_STEP, M_HEADS, HEAD_DIM)
    ms_spec = const(m_heads.shape)
    in_specs = [
        pl.BlockSpec((PREP_TILES * MXU_COLS, D_MODEL), lambda t: (prep_idx(t), 0)),
        const(meta_tokens.shape),
        pl.BlockSpec((N_GATES, D_MODEL), lambda t: (MAIN_WIDTH // N_GATES, 0)),
        pl.BlockSpec((WOUT_ROWS, D_MODEL), lambda t: (jnp.minimum(t, D_MODEL // WOUT_ROWS - 1), 0)),
        const(ng.shape), const(fgain.shape), const(mhg.shape), const(cw.shape),
        _SMEM_SPEC, _SMEM_SPEC,
        xs_spec, sconv_spec, ms_spec, ns_spec, cs_spec,
    ]
    out_shape = (
        jax.ShapeDtypeStruct((N_TILES, D_MODEL, MXU_COLS), _BF),
        jax.ShapeDtypeStruct((D_MODEL, LANES), _BF),
        jax.ShapeDtypeStruct(w_out.shape, _BF),
        jax.ShapeDtypeStruct((M_HEADS, HEAD_DIM, 2 * LANES), _F32),
        jax.ShapeDtypeStruct((1, LANES), _F32),
        jax.ShapeDtypeStruct((SUBLANES, CONV_WIDTH), _F32),
        jax.ShapeDtypeStruct((rows, D_MODEL), _F32),
        jax.ShapeDtypeStruct(state_conv.shape, _F32),
        jax.ShapeDtypeStruct(c_state.shape, _F32),
        jax.ShapeDtypeStruct(n_state.shape, _F32),
        jax.ShapeDtypeStruct(m_heads.shape, _F32),
    )
    out_specs = (
        pl.BlockSpec((PREP_TILES, D_MODEL, MXU_COLS), lambda t: (prep_idx(t), 0, 0)),
        const(out_shape[1].shape), const(out_shape[2].shape), const(out_shape[3].shape), const(out_shape[4].shape),
        const(out_shape[5].shape),
        xs_spec, sconv_spec, cs_spec, ns_spec, ms_spec,
    )
    scratch = [
        pltpu.VMEM((N_TILES, D_MODEL, MXU_COLS), _BF),
        pltpu.VMEM((CHUNK, D_MODEL), _BF),
        pltpu.VMEM((N_TILES, CHUNK, MXU_COLS), _F32),
        pltpu.VMEM((CHUNK, CONV_WIDTH), _F32),
        pltpu.VMEM((LANES, LANES), _F32),
        pltpu.VMEM((LANES, LANES), _F32),
    ] + _sequence_scratch(CHUNK)
    return pl.pallas_call(
        _front_body,
        grid=(n_prep + n_sample,),
        in_specs=in_specs,
        out_specs=out_specs,
        out_shape=out_shape,
        scratch_shapes=scratch,
        compiler_params=pltpu.CompilerParams(
            dimension_semantics=("arbitrary",), vmem_limit_bytes=VMEM_REQUEST_BYTES),
        name="prep_meta_sample",
    )(w_t, meta_tokens, w_t, w_out, ng, fgain, mhg, cw, big, bfg, xs, state_conv, m_heads, n_state, c_state)


def _prompt_step(s, x_ref, w3_ref, wg_ref, wo_ref, ng_ref, fgain_ref, mhg_ref, cw_ref, big_ref, bfg_ref,
                 caug0_ref, m0_ref, ut0_ref,
                 y_ref, c_out_ref, n_out_ref, m_out_ref, conv_out_ref,
                 out_scr, m_all_scr, p_scr, g_scr, hn_scr, ycat_scr, ubuf, caug, m_scr, *, tb, nt, n_blocks):
    i = s % nt
    live = s < n_blocks

    @pl.when(s == 0)
    def _():
        out_scr[...] = jnp.zeros_like(out_scr)
        m_all_scr[...] = jnp.zeros_like(m_all_scr)

    @pl.when(live & (i == 0))
    def _():
        caug[...] = caug0_ref[...]
        m_scr[...] = m0_ref[...]
        ubuf[0:SUBLANES, :] = ut0_ref[...]

    @pl.when(live)
    def _():
        xn = _rms_rows(x_ref[0], ng_ref[...]).astype(_BF)

        def project_tile(t):
            p_scr[:, _tile_cols(t)] = _dot(xn, w3_ref[t])

        def norm_previous(r0, r1):
            def fn():
                y_ref[0, r0:r1, :] = _rms_rows(out_scr[r0:r1, :], fgain_ref[...])
            return fn

        q4 = tb // 4
        side_work = {3 + 2 * j: [norm_previous(j * q4, (j + 1) * q4)] for j in range(4)}
        _sequence_block(xn, project_tile, p_scr, g_scr, hn_scr, ycat_scr, ubuf, caug, m_scr,
                        wg_ref, mhg_ref, cw_ref, big_ref, bfg_ref, tb=tb, n_pad=0, side_work=side_work)
        out_scr[...] = _dot(ycat_scr[...], wo_ref[...]) + x_ref[0]

    @pl.when(s == n_blocks)
    def _():
        y_ref[0] = _rms_rows(out_scr[...], fgain_ref[...])
        m_out_ref[...] = m_all_scr[...].T[0:M_HEADS, 0:n_blocks // nt]

    @pl.when(live & (i == nt - 1))
    def _():
        m_all_scr[pl.ds(s // nt, 1), :] = m_scr[...]
        conv_out_ref[0] = ubuf[SUBLANES - (CONV_K - 1):SUBLANES, :]
        for h in range(M_HEADS):
            c_out_ref[0, h] = caug[h, :, 0:HEAD_DIM]
            n_out_ref[0, h:h + 1, :] = caug[h, :, LANES:2 * LANES].T[0:1, :]


def _prompt_call(x, w3, wg, wo, small, fgain, caug0, m0, ut0):
    ng, mhg, cw, big, bfg = small
    nb, t, _ = x.shape
    tb = PROMPT_ROWS
    nt = t // tb
    n_blocks = nb * nt

    def block_of(s):
        return jnp.minimum(s, n_blocks - 1)

    def rows_spec(step_to_block):
        return pl.BlockSpec((1, tb, D_MODEL), lambda s: (step_to_block(s) // nt, step_to_block(s) % nt, 0))

    def state_spec(*tail):
        nd = len(tail)
        return pl.BlockSpec((1,) + tail, lambda s: (block_of(s) // nt,) + (0,) * nd)

    stream_in = [rows_spec(block_of)]
    stream_out = [
        rows_spec(lambda s: jnp.maximum(s - 1, 0)),
        state_spec(M_HEADS, HEAD_DIM, HEAD_DIM),
        state_spec(M_HEADS, HEAD_DIM),
        state_spec(CONV_K - 1, CONV_WIDTH),
    ]
    hbm = pl.BlockSpec(memory_space=pl.ANY)
    vmem = pl.BlockSpec(memory_space=pltpu.VMEM)

    def body(x_hbm, w3_ref, wg_ref, wo_ref, ng_ref, fgain_ref, mhg_ref, cw_ref, big_ref, bfg_ref,
             caug0_ref, m0_ref, ut0_ref, y_hbm, c_hbm, n_hbm, m_out_ref, conv_hbm, step_scr, *work):
        step_scr[0] = 0

        def step(x_ref, y_ref, c_out_ref, n_out_ref, conv_out_ref):
            s = step_scr[0]
            _prompt_step(s, x_ref, w3_ref, wg_ref, wo_ref, ng_ref, fgain_ref, mhg_ref, cw_ref, big_ref, bfg_ref,
                         caug0_ref, m0_ref, ut0_ref, y_ref, c_out_ref, n_out_ref, m_out_ref, conv_out_ref,
                         *work, tb=tb, nt=nt, n_blocks=n_blocks)
            step_scr[0] = s + 1

        pltpu.emit_pipeline(step, grid=(n_blocks + 1,), in_specs=stream_in, out_specs=stream_out)(
            x_hbm, y_hbm, c_hbm, n_hbm, conv_hbm)

    in_specs = [hbm, vmem, vmem, vmem, vmem, vmem, vmem, vmem, _SMEM_SPEC, _SMEM_SPEC, vmem, vmem, vmem]
    out_shape = (
        jax.ShapeDtypeStruct((nb, t, D_MODEL), _F32),
        jax.ShapeDtypeStruct((nb, M_HEADS, HEAD_DIM, HEAD_DIM), _F32),
        jax.ShapeDtypeStruct((nb, M_HEADS, HEAD_DIM), _F32),
        jax.ShapeDtypeStruct((M_HEADS, nb), _F32),
        jax.ShapeDtypeStruct((nb, CONV_K - 1, CONV_WIDTH), _F32),
    )
    out_specs = (hbm, hbm, hbm, vmem, hbm)
    assert nb <= LANES
    scratch = [
        pltpu.SMEM((1,), jnp.int32),
        pltpu.VMEM((tb, D_MODEL), _F32),
        pltpu.VMEM((LANES, LANES), _F32),
    ] + _sequence_scratch(tb)
    return pl.pallas_call(
        body,
        in_specs=in_specs,
        out_specs=out_specs,
        out_shape=out_shape,
        scratch_shapes=scratch,
        compiler_params=pltpu.CompilerParams(vmem_limit_bytes=VMEM_REQUEST_BYTES),
        name="prompt_layer",
    )(x, w3, wg, wo, ng, fgain, mhg, cw, big, bfg, caug0, m0, ut0)


def _sample_block(x_ref, w3_ref, wg_ref, wo_ref, ng_ref, fgain_ref, mhg_ref, cw_ref, big_ref, bfg_ref,
                  sconv_ref, m_ref, n_ref, c_ref,
                  y_ref, conv_out_ref, c_out_ref, n_out_ref, m_out_ref,
                  p_scr, ni_scr, sc_scr, ycat_scr):
    x = x_ref[...]
    xn = _rms_rows(x, ng_ref[...]).astype(_BF)
    g_raw = _dot(xn, wg_ref[...])
    for t in QKV_TILES:
        p_scr[:, _tile_cols(t)] = _dot(xn, w3_ref[t])

    row = lax.broadcasted_iota(jnp.int32, (CHUNK, CHUNK), 0)
    col = lax.broadcasted_iota(jnp.int32, (CHUNK, CHUNK), 1)
    pos = row % DEC_SEQ
    grp = row // DEC_SEQ
    causal = (grp == (col // DEC_SEQ)) & (col <= row)
    grow = lax.broadcasted_iota(jnp.int32, (SEQ_PER_STEP, CHUNK), 0)
    gcol = lax.broadcasted_iota(jnp.int32, (SEQ_PER_STEP, CHUNK), 1)
    pick_last = (gcol == grow * DEC_SEQ + (DEC_SEQ - 1)).astype(_F32)
    seq_sum = (gcol // DEC_SEQ == grow).astype(_BF)
    erow = lax.broadcasted_iota(jnp.int32, (CHUNK, SEQ_PER_STEP), 0)
    ecol = lax.broadcasted_iota(jnp.int32, (CHUNK, SEQ_PER_STEP), 1)
    own_seq = erow // DEC_SEQ == ecol
    ones_blk = jnp.ones((CHUNK, LANES), _BF)

    def rows_from_seqs(v16):
        return jnp.broadcast_to(v16[:, None, :], (SEQ_PER_STEP, DEC_SEQ, LANES)).reshape(CHUNK, LANES)

    def last_row_of_seq(v):
        v3 = v.reshape(SEQ_PER_STEP, DEC_SEQ, LANES)
        return jnp.broadcast_to(v3[:, DEC_SEQ - 1:DEC_SEQ, :], v3.shape).reshape(CHUNK, LANES)

    ig, lf = _gate_columns(g_raw, big_ref, bfg_ref)
    b = _cumsum_rows(lf, pos, DEC_SEQ)
    a = ig - b
    m_prev = rows_from_seqs(m_ref[...])
    big_m = jnp.maximum(m_prev, _cummax_rows(a, pos, DEC_SEQ))
    m_t = b + big_m
    inter = jnp.exp(m_prev - big_m)
    emt = jnp.exp(-m_t)
    m_last = last_row_of_seq(big_m)
    w = jnp.exp(a - m_last)
    decay = jnp.exp(m_prev - m_last)
    m_out_ref[...] = _dot_exact(pick_last, m_t)
    dec16 = _dot_exact(pick_last, decay)
    a_t = a.T

    for t in REST_TILES:
        p_scr[:, _tile_cols(t)] = _dot(xn, w3_ref[t])

    heads = []
    for h in range(M_HEADS):
        hs = h * HEAD_DIM
        q = p_scr[:, OFF_Q + hs:OFF_Q + hs + HEAD_DIM]
        k = p_scr[:, OFF_K + hs:OFF_K + hs + HEAD_DIM] * K_SCALE
        v = p_scr[:, OFF_V + hs:OFF_V + hs + HEAD_DIM]
        q_bf = q.astype(_BF)
        a_b = jnp.broadcast_to(a_t[h:h + 1, :], (CHUNK, CHUNK))
        m_b = jnp.broadcast_to(big_m[:, h:h + 1], (CHUNK, CHUNK))
        d = jnp.exp(jnp.where(causal, a_b - m_b, NEG_INF))
        w_b = jnp.broadcast_to(w[:, h:h + 1], (CHUNK, LANES))
        heads.append(dict(q_bf=q_bf, k=k, v=v, d=d, w_b=w_b, qk=_dot_nt(q_bf, k.astype(_BF))))

    u = p_scr[:, OFF_GC:OFF_GC + CONV_WIDTH] * p_scr[:, OFF_HC:OFF_HC + CONV_WIDTH]
    for g in range(SEQ_PER_STEP):
        sc_scr[g * DEC_SEQ:g * DEC_SEQ + CONV_K - 1, :] = sconv_ref[g]
        conv_out_ref[g] = u[(g + 1) * DEC_SEQ - (CONV_K - 1):(g + 1) * DEC_SEQ, :]
    sc = sc_scr[...]
    posw = lax.broadcasted_iota(jnp.int32, (CHUNK, CONV_WIDTH), 0) % DEC_SEQ
    u_m1 = jnp.where(posw == 0, pltpu.roll(sc, CHUNK - 1, 0), pltpu.roll(u, 1, 0))
    u_m2 = jnp.where(posw < 2, sc, pltpu.roll(u, 2, 0))
    y_conv = _conv_out(p_scr[:, OFF_GB:OFF_GB + CONV_WIDTH], p_scr[:, OFF_ZC:OFF_ZC + CONV_WIDTH],
                       u, u_m1, u_m2, cw_ref)
    ycat_scr[:, 0:CONV_WIDTH] = y_conv.astype(_BF)

    for h in range(M_HEADS):
        hs = h * HEAD_DIM
        st = heads[h]
        wv = st["w_b"] * st["v"]
        k_t_bf = st["k"].T.astype(_BF)
        dec_b = jnp.broadcast_to(decay[:, h:h + 1], (CHUNK, HEAD_DIM))

        for g in range(0, SEQ_PER_STEP, 2):
            pair_rows = slice(g * DEC_SEQ, (g + 2) * DEC_SEQ)
            c0, c1 = c_ref[g, h], c_ref[g + 1, h]
            qc = _dot(st["q_bf"][pair_rows, :], jnp.concatenate([c0.astype(_BF), c1.astype(_BF)], axis=1))
            ni_scr[g * DEC_SEQ:(g + 1) * DEC_SEQ, hs:hs + HEAD_DIM] = qc[0:DEC_SEQ, 0:HEAD_DIM]
            ni_scr[(g + 1) * DEC_SEQ:(g + 2) * DEC_SEQ, hs:hs + HEAD_DIM] = qc[DEC_SEQ:, HEAD_DIM:]
            wv_pair = jnp.concatenate([jnp.where(grp == g, wv, 0.0).astype(_BF),
                                       jnp.where(grp == g + 1, wv, 0.0).astype(_BF)], axis=1)
            upd = _dot(k_t_bf, wv_pair)
            dec0 = jnp.broadcast_to(dec_b[g * DEC_SEQ:g * DEC_SEQ + 1, :], (HEAD_DIM, HEAD_DIM))
            dec1 = jnp.broadcast_to(dec_b[(g + 1) * DEC_SEQ:(g + 1) * DEC_SEQ + 1, :], (HEAD_DIM, HEAD_DIM))
            c_out_ref[g, h] = dec0 * c0 + upd[:, 0:HEAD_DIM]
            c_out_ref[g + 1, h] = dec1 * c1 + upd[:, HEAD_DIM:]

        s = st["qk"] * st["d"]
        sv = _dot(s.astype(_BF), jnp.concatenate([st["v"].astype(_BF), ones_blk], axis=1))
        n_old = n_ref[:, h, :]
        num_inter = ni_scr[:, hs:hs + HEAD_DIM]
        qn = _dot_nt(st["q_bf"], n_old.astype(_BF))
        den_inter = jnp.sum(jnp.where(own_seq, qn, 0.0), axis=-1, keepdims=True)
        inter_b = jnp.broadcast_to(inter[:, h:h + 1], (CHUNK, LANES))
        num = sv[:, :LANES] + inter_b * num_inter
        den = sv[:, LANES:] + inter_b * den_inter
        emt_b = jnp.broadcast_to(emt[:, h:h + 1], (CHUNK, LANES))
        ni_scr[:, hs:hs + HEAD_DIM] = _head_norm(num, den, emt_b, mhg_ref[:, hs:hs + HEAD_DIM])
        n_upd = _dot(seq_sum, (st["w_b"] * st["k"]).astype(_BF))
        dec16_b = jnp.broadcast_to(dec16[:, h:h + 1], (SEQ_PER_STEP, HEAD_DIM))
        n_out_ref[:, h, :] = dec16_b * n_old + n_upd

    ycat_scr[:, CONV_WIDTH:CONV_WIDTH + M_WIDTH] = _gated_heads(ni_scr[...], p_scr).astype(_BF)
    out = _dot(ycat_scr[...], wo_ref[...]) + x
    y_ref[...] = _rms_rows(out, fgain_ref[...])


def kernel(x_prompt, x_sample, state_conv, state_mlstm_c, state_mlstm_n, state_mlstm_m, meta_tokens, norm_gain,
           w_in, conv_w, b_igate, b_fgate, mh_norm_gain, w_out, final_norm_gain):
    nseq, dec_seq, _ = x_sample.shape
    assert dec_seq == DEC_SEQ and meta_tokens.shape[0] == N_META
    assert w_in.shape == (D_MODEL, MAIN_WIDTH + N_GATES) and x_prompt.shape[1] % PROMPT_ROWS == 0
    assert (nseq * DEC_SEQ) % CHUNK == 0

    w_t = w_in.T
    small = (norm_gain.reshape(1, D_MODEL), mh_norm_gain.reshape(1, M_WIDTH), conv_w,
             b_igate.astype(_F32), b_fgate.astype(_F32))
    fgain = final_norm_gain.reshape(1, D_MODEL)

    xs = x_sample.reshape(nseq * DEC_SEQ, D_MODEL)
    w3, wg, wo, caug_meta, m_meta, ut_meta, ys, conv_s, c_s, n_s, m_s = _front_call(
        w_t, meta_tokens.astype(_F32), w_out.astype(_F32), small, fgain, xs, state_conv, state_mlstm_m.T,
        state_mlstm_n, state_mlstm_c)

    y_prompt, c_p, n_p, m_p, conv_p = _prompt_call(x_prompt, w3, wg, wo, small, fgain, caug_meta, m_meta, ut_meta)
    y_sample = ys.reshape(nseq, DEC_SEQ, D_MODEL)

    return (y_prompt, y_sample, conv_p, c_p, n_p, m_p.T, conv_s, c_s, n_s, m_s.T)
```
